```python
import math
import jax, jax.numpy as jnp
from jax import lax
import numpy as np

D_MODEL = 2048
BATCH = 8
SEQ = 4096
DEPTH = 4

PLE_DIM = 256
ATTN_WIDTH = D_MODEL // 2
POOL_WIDTH = D_MODEL - ATTN_WIDTH
HEAD_DIM = 64
N_HEADS = ATTN_WIDTH // HEAD_DIM
N_KV_HEADS = max(1, N_HEADS // 8)
KV_GROUP = N_HEADS // N_KV_HEADS
WINDOW = 128
BLOCK = WINDOW
POOL_WINDOWS = (2, 4, 8, 16)
N_POOL_GROUPS = len(POOL_WINDOWS)
POOL_GROUP_DIM = POOL_WIDTH // N_POOL_GROUPS
REL_BUCKETS = 32
REL_MAX_DIST = 128
LN_EPS = 1e-5
DEEPNORM_ALPHA = (2.0 * DEPTH) ** 0.25
DEEPNORM_BETA = (8.0 * DEPTH) ** -0.25
Q_COLS = N_HEADS * HEAD_DIM
KV_COLS = N_KV_HEADS * HEAD_DIM
SPLIT_SIZES = (Q_COLS, KV_COLS, KV_COLS, ATTN_WIDTH, POOL_WIDTH, POOL_WIDTH)
SPLIT_POINTS = tuple(int(c) for c in np.cumsum(SPLIT_SIZES)[:-1])
IN_COLS = int(sum(SPLIT_SIZES))

kernel_name = "hymba_swa_sink_pool_deepnorm"


def t5_causal_bucket(dist):
    max_exact = REL_BUCKETS // 2
    d = jnp.maximum(dist, 0)
    d_f = jnp.maximum(d, 1).astype(jnp.float32)
    large = max_exact + (jnp.log(d_f / max_exact) / math.log(REL_MAX_DIST / max_exact)
                         * (REL_BUCKETS - max_exact)).astype(jnp.int32)
    large = jnp.minimum(large, REL_BUCKETS - 1)
    return jnp.where(d < max_exact, d, large)


def band_geometry():
    qq = jnp.arange(BLOCK)[:, None]
    kk = jnp.arange(2 * BLOCK)[None, :]
    dist = qq + BLOCK - kk
    in_window = (dist >= 0) & (dist < WINDOW)
    return dist, in_window


def layer_norm(x, gain, bias):
    xf = x.astype(jnp.float32)
    mu = jnp.mean(xf, axis=-1, keepdims=True)
    var = jnp.mean(jnp.square(xf - mu), axis=-1, keepdims=True)
    y = (xf - mu) * lax.rsqrt(var + LN_EPS)
    return (y * gain.astype(jnp.float32) + bias.astype(jnp.float32)).astype(x.dtype)


def banded_sink_attention(q, k, v, sinks, bias_hqk, valid):
    B, S, _ = q.shape
    nblk = S // BLOCK
    qb = q.reshape(B, nblk, BLOCK, N_KV_HEADS, KV_GROUP, HEAD_DIM)
    pad = ((0, 0), (BLOCK, 0), (0, 0))
    kp = jnp.pad(k, pad).reshape(B, nblk + 1, BLOCK, N_KV_HEADS, HEAD_DIM)
    vp = jnp.pad(v, pad).reshape(B, nblk + 1, BLOCK, N_KV_HEADS, HEAD_DIM)
    kb = jnp.concatenate([kp[:, :-1], kp[:, 1:]], axis=2)
    vb = jnp.concatenate([vp[:, :-1], vp[:, 1:]], axis=2)
    scale = 1.0 / math.sqrt(HEAD_DIM)
    scores = jnp.einsum('bnqhgd,bnkhd->bnhgqk', qb, kb).astype(jnp.float32) * scale
    scores = scores + bias_hqk[None, None]
    scores = jnp.where(valid[None, :, None, None], scores, -1e30)
    s_sink = sinks.astype(jnp.float32).reshape(N_KV_HEADS, KV_GROUP)[None, None, :, :, None, None]
    m = jnp.maximum(jnp.max(scores, axis=-1, keepdims=True), s_sink)
    e = jnp.exp(scores - m)
    denom = jnp.sum(e, axis=-1, keepdims=True) + jnp.exp(s_sink - m)
    probs = (e / denom).astype(v.dtype)
    out = jnp.einsum('bnhgqk,bnkhd->bnqhgd', probs, vb)
    return out.reshape(B, S, N_HEADS * HEAD_DIM)


def multiscale_pool(u, w_pool, pool_scale):
    B, S, _ = u.shape
    ug = u.reshape(B, S, N_POOL_GROUPS, POOL_GROUP_DIM).astype(jnp.float32)
    cs = jnp.cumsum(ug, axis=1)
    t = jnp.arange(S)
    means = []
    for g, w in enumerate(POOL_WINDOWS):
        c = cs[:, :, g]
        lagged = jnp.pad(c[:, :S - w], ((0, 0), (w, 0), (0, 0)))
        count = jnp.minimum(t + 1, w).astype(jnp.float32)[None, :, None]
        means.append((c - lagged) / count)
    pooled = jnp.stack(means, axis=2)
    diff = (pooled - ug).astype(u.dtype)
    mixed = jnp.einsum('bsgc,gcd->bsgd', diff, w_pool)
    return mixed.reshape(B, S, POOL_WIDTH) * pool_scale


def hybrid_layer(x, p_i, w_in, b_in, w_out, sinks, w_pool, pool_scale, w_ple, w_gate_ple,
                 ln_gain, ln_bias, bias_hqk, valid):
    h = jnp.einsum('bsd,dc->bsc', x, w_in) + b_in
    q, k, v, g_attn, u_pool, g_pool = jnp.split(h, SPLIT_POINTS, axis=-1)
    a = banded_sink_attention(q, k, v, sinks, bias_hqk, valid) * jax.nn.silu(g_attn)
    b = multiscale_pool(u_pool, w_pool, pool_scale) * jax.nn.silu(g_pool)
    mix = jnp.einsum('bsc,cd->bsd', jnp.concatenate([a, b], axis=-1), w_out)
    ple = jax.nn.sigmoid(jnp.einsum('bsd,de->bse', x, w_gate_ple)) * jnp.einsum('bsp,pd->bsd', p_i, w_ple)
    return layer_norm(DEEPNORM_ALPHA * x + mix + ple, ln_gain, ln_bias)


def _fwd_setup_inputs(seed: int = 0) -> dict:
    key = jax.random.key(seed)
    ks = jax.random.split(key, 14)
    f32 = jnp.float32
    x = jax.random.normal(ks[0], (BATCH, SEQ, D_MODEL), f32)
    p = jax.random.normal(ks[1], (DEPTH, BATCH, SEQ, PLE_DIM), f32)
    w_in = jax.random.normal(ks[2], (DEPTH, D_MODEL, IN_COLS), f32) * D_MODEL ** -0.5
    b_in = jax.random.normal(ks[3], (DEPTH, IN_COLS), f32) * 0.02
    w_out = jax.random.normal(ks[4], (DEPTH, D_MODEL, D_MODEL), f32) * (D_MODEL ** -0.5 * DEEPNORM_BETA)
    attn_sinks = jax.random.normal(ks[5], (DEPTH, N_HEADS), f32) * 0.5
    rel_bias = jax.random.normal(ks[6], (REL_BUCKETS, N_HEADS), f32) * 0.1
    w_pool = jax.random.normal(ks[7], (DEPTH, N_POOL_GROUPS, POOL_GROUP_DIM, POOL_GROUP_DIM), f32) * POOL_GROUP_DIM ** -0.5
    pool_scale = 1.0 + 0.1 * jax.random.normal(ks[8], (DEPTH, POOL_WIDTH), f32)
    w_ple = jax.random.normal(ks[9], (DEPTH, PLE_DIM, D_MODEL), f32) * PLE_DIM ** -0.5
    w_gate_ple = jax.random.normal(ks[10], (DEPTH, D_MODEL, D_MODEL), f32) * D_MODEL ** -0.5
    ln_gain = 1.0 + 0.02 * jax.random.normal(ks[11], (DEPTH, D_MODEL), f32)
    ln_bias = 0.02 * jax.random.normal(ks[12], (DEPTH, D_MODEL), f32)
    return {"x": x, "p": p, "w_in": w_in, "b_in": b_in, "w_out": w_out,
            "attn_sinks": attn_sinks, "rel_bias": rel_bias, "w_pool": w_pool,
            "pool_scale": pool_scale, "w_ple": w_ple, "w_gate_ple": w_gate_ple,
            "ln_gain": ln_gain, "ln_bias": ln_bias}


def _fwd_reference(x, p, w_in, b_in, w_out, attn_sinks, rel_bias, w_pool, pool_scale, w_ple,
              w_gate_ple, ln_gain, ln_bias):
    S = x.shape[1]
    nblk = S // BLOCK
    dist, in_window = band_geometry()
    bias_hqk = jnp.transpose(rel_bias[t5_causal_bucket(dist)], (2, 0, 1)).astype(jnp.float32)
    bias_hqk = bias_hqk.reshape(N_KV_HEADS, KV_GROUP, BLOCK, 2 * BLOCK)
    k_pos = (jnp.arange(nblk) * BLOCK - BLOCK)[:, None, None] + jnp.arange(2 * BLOCK)[None, None, :]
    valid = in_window[None] & (k_pos >= 0)
    for i in range(DEPTH):
        x = hybrid_layer(x, p[i], w_in[i], b_in[i], w_out[i], attn_sinks[i], w_pool[i],
                         pool_scale[i], w_ple[i], w_gate_ple[i], ln_gain[i], ln_bias[i],
                         bias_hqk, valid)
    return x


import jax as _jax
import jax.numpy as _jnp

TWIN_FORMAT = 'train_step'
FWD_PARAMS = ['x', 'p', 'w_in', 'b_in', 'w_out', 'attn_sinks', 'rel_bias', 'w_pool', 'pool_scale', 'w_ple', 'w_gate_ple', 'ln_gain', 'ln_bias']
TWIN_WEIGHTS = ['w_in', 'b_in', 'w_out', 'attn_sinks', 'rel_bias', 'w_pool', 'pool_scale', 'w_ple', 'w_gate_ple', 'ln_gain', 'ln_bias']
TWIN_DIFF_INPUT = 'x'
TWIN_INPUTS = ['x', 'p', 'w_in', 'b_in', 'w_out', 'attn_sinks', 'rel_bias', 'w_pool', 'pool_scale', 'w_ple', 'w_gate_ple', 'ln_gain', 'ln_bias', 'loss_target', 'm_w_in', 'm_b_in', 'm_w_out', 'm_attn_sinks', 'm_rel_bias', 'm_w_pool', 'm_pool_scale', 'm_w_ple', 'm_w_gate_ple', 'm_ln_gain', 'm_ln_bias', 'v_w_in', 'v_b_in', 'v_w_out', 'v_attn_sinks', 'v_rel_bias', 'v_w_pool', 'v_pool_scale', 'v_w_ple', 'v_w_gate_ple', 'v_ln_gain', 'v_ln_bias']
TWIN_OUTPUTS = ['loss', 'grad_x', 'grad_w_in', 'grad_b_in', 'grad_w_out', 'grad_attn_sinks', 'grad_rel_bias', 'grad_w_pool', 'grad_pool_scale', 'grad_w_ple', 'grad_w_gate_ple', 'grad_ln_gain', 'grad_ln_bias', 'delta_w_in', 'delta_b_in', 'delta_w_out', 'delta_attn_sinks', 'delta_rel_bias', 'delta_w_pool', 'delta_pool_scale', 'delta_w_ple', 'delta_w_gate_ple', 'delta_ln_gain', 'delta_ln_bias', 'new_m_w_in', 'new_m_b_in', 'new_m_w_out', 'new_m_attn_sinks', 'new_m_rel_bias', 'new_m_w_pool', 'new_m_pool_scale', 'new_m_w_ple', 'new_m_w_gate_ple', 'new_m_ln_gain', 'new_m_ln_bias', 'new_v_w_in', 'new_v_b_in', 'new_v_w_out', 'new_v_attn_sinks', 'new_v_rel_bias', 'new_v_w_pool', 'new_v_pool_scale', 'new_v_w_ple', 'new_v_w_gate_ple', 'new_v_ln_gain', 'new_v_ln_bias']
TWIN_LEAF_KINDS = {'loss': 'loss', 'grad_x': 'grad_x', 'grad_w_in': 'grad_w', 'grad_b_in': 'grad_w', 'grad_w_out': 'grad_w', 'grad_attn_sinks': 'grad_w', 'grad_rel_bias': 'grad_w', 'grad_w_pool': 'grad_w', 'grad_pool_scale': 'grad_w', 'grad_w_ple': 'grad_w', 'grad_w_gate_ple': 'grad_w', 'grad_ln_gain': 'grad_w', 'grad_ln_bias': 'grad_w', 'delta_w_in': 'delta_w', 'delta_b_in': 'delta_w', 'delta_w_out': 'delta_w', 'delta_attn_sinks': 'delta_w', 'delta_rel_bias': 'delta_w', 'delta_w_pool': 'delta_w', 'delta_pool_scale': 'delta_w', 'delta_w_ple': 'delta_w', 'delta_w_gate_ple': 'delta_w', 'delta_ln_gain': 'delta_w', 'delta_ln_bias': 'delta_w', 'new_m_w_in': 'new_m', 'new_m_b_in': 'new_m', 'new_m_w_out': 'new_m', 'new_m_attn_sinks': 'new_m', 'new_m_rel_bias': 'new_m', 'new_m_w_pool': 'new_m', 'new_m_pool_scale': 'new_m', 'new_m_w_ple': 'new_m', 'new_m_w_gate_ple': 'new_m', 'new_m_ln_gain': 'new_m', 'new_m_ln_bias': 'new_m', 'new_v_w_in': 'new_v', 'new_v_b_in': 'new_v', 'new_v_w_out': 'new_v', 'new_v_attn_sinks': 'new_v', 'new_v_rel_bias': 'new_v', 'new_v_w_pool': 'new_v', 'new_v_pool_scale': 'new_v', 'new_v_w_ple': 'new_v', 'new_v_w_gate_ple': 'new_v', 'new_v_ln_gain': 'new_v', 'new_v_ln_bias': 'new_v'}


def _forward(args):
    return _fwd_reference(*[args[k] for k in FWD_PARAMS])


def _output_shape():
    def fwd():
        inp = _fwd_setup_inputs(0)
        return _fwd_reference(*[inp[k] for k in FWD_PARAMS])
    out = _jax.eval_shape(fwd)
    return out.shape, out.dtype

N_MICROBATCH = 1
ADAM_LR = 0.001
ADAM_B1 = 0.9
ADAM_B2 = 0.999
ADAM_EPS = 1e-08
ADAM_WD = 0.01
ADAM_STEP = 10
PER_EXAMPLE_BATCH_AXIS = {'x': 0, 'p': 1, 'loss_target': 0}
SHARED_INPUTS = []
_WEIGHT_DTYPES = {'w_in': _jnp.float32, 'b_in': _jnp.float32, 'w_out': _jnp.float32, 'attn_sinks': _jnp.float32, 'rel_bias': _jnp.float32, 'w_pool': _jnp.float32, 'pool_scale': _jnp.float32, 'w_ple': _jnp.float32, 'w_gate_ple': _jnp.float32, 'ln_gain': _jnp.float32, 'ln_bias': _jnp.float32}
MOMENT_SCALE = {'w_in': 7.619252e-03, 'b_in': 9.432457e-03, 'w_out': 1.805420e-02, 'attn_sinks': 1.837568e-03, 'rel_bias': 5.516451e-03, 'w_pool': 1.046130e-02, 'pool_scale': 1.028242e-02, 'w_ple': 2.571933e-02, 'w_gate_ple': 1.001403e-02, 'ln_gain': 8.010128e+00, 'ln_bias': 3.623596e-01}


def _to_microbatches(a, axis):
    t = _jnp.moveaxis(a, axis, 0)
    t = t.reshape((N_MICROBATCH, t.shape[0] // N_MICROBATCH) + t.shape[1:])
    return _jnp.moveaxis(t, 1, axis + 1)


def setup_inputs(seed: int = 0) -> dict:
    inp = _fwd_setup_inputs(seed)
    key = _jax.random.fold_in(_jax.random.key(seed), 7919)
    shape, _ = _output_shape()
    out = dict(inp)
    out["loss_target"] = _jax.random.normal(_jax.random.fold_in(key, 0), shape, _jnp.float32)
    for i, name in enumerate(TWIN_WEIGHTS):
        w = inp[name].astype(_jnp.float32)
        if MOMENT_SCALE is None:
            s = _jnp.sqrt(_jnp.mean(_jnp.square(w)) + 1e-30)
        else:
            s = MOMENT_SCALE[name]
        km, kv = _jax.random.split(_jax.random.fold_in(key, i + 1))
        out[name] = w
        out["m_" + name] = s * _jax.random.normal(km, w.shape, _jnp.float32)
        out["v_" + name] = (s * s) * _jax.random.uniform(kv, w.shape, _jnp.float32, 0.5, 1.5)
    if N_MICROBATCH > 1:
        for name, axis in PER_EXAMPLE_BATCH_AXIS.items():
            out[name] = _to_microbatches(out[name], axis)
    return {'x': out['x'], 'p': out['p'], 'w_in': out['w_in'], 'b_in': out['b_in'], 'w_out': out['w_out'], 'attn_sinks': out['attn_sinks'], 'rel_bias': out['rel_bias'], 'w_pool': out['w_pool'], 'pool_scale': out['pool_scale'], 'w_ple': out['w_ple'], 'w_gate_ple': out['w_gate_ple'], 'ln_gain': out['ln_gain'], 'ln_bias': out['ln_bias'], 'loss_target': out['loss_target'], 'm_w_in': out['m_w_in'], 'm_b_in': out['m_b_in'], 'm_w_out': out['m_w_out'], 'm_attn_sinks': out['m_attn_sinks'], 'm_rel_bias': out['m_rel_bias'], 'm_w_pool': out['m_w_pool'], 'm_pool_scale': out['m_pool_scale'], 'm_w_ple': out['m_w_ple'], 'm_w_gate_ple': out['m_w_gate_ple'], 'm_ln_gain': out['m_ln_gain'], 'm_ln_bias': out['m_ln_bias'], 'v_w_in': out['v_w_in'], 'v_b_in': out['v_b_in'], 'v_w_out': out['v_w_out'], 'v_attn_sinks': out['v_attn_sinks'], 'v_rel_bias': out['v_rel_bias'], 'v_w_pool': out['v_w_pool'], 'v_pool_scale': out['v_pool_scale'], 'v_w_ple': out['v_w_ple'], 'v_w_gate_ple': out['v_w_gate_ple'], 'v_ln_gain': out['v_ln_gain'], 'v_ln_bias': out['v_ln_bias']}


def _loss(weights, diff, rest, loss_target):
    with _jax.named_scope("forward"):
        args = {**rest, TWIN_DIFF_INPUT: diff, **{k: w.astype(_WEIGHT_DTYPES[k]) for k, w in weights.items()}}
        y = _forward(args)
    with _jax.named_scope("loss_head"):
        err = _jnp.square(y.astype(_jnp.float32) - loss_target)
        return 0.5 * _jnp.sum(_jnp.mean(err, axis=-1)) if err.ndim else 0.5 * err


def _adamw(w, g, m, v):
    m = ADAM_B1 * m + (1.0 - ADAM_B1) * g
    v = ADAM_B2 * v + (1.0 - ADAM_B2) * _jnp.square(g)
    m_hat = m / (1.0 - ADAM_B1 ** ADAM_STEP)
    v_hat = v / (1.0 - ADAM_B2 ** ADAM_STEP)
    delta = -ADAM_LR * (m_hat / (_jnp.sqrt(v_hat) + ADAM_EPS) + ADAM_WD * w)
    return delta, m, v


def reference(x, p, w_in, b_in, w_out, attn_sinks, rel_bias, w_pool, pool_scale, w_ple, w_gate_ple, ln_gain, ln_bias, loss_target, m_w_in, m_b_in, m_w_out, m_attn_sinks, m_rel_bias, m_w_pool, m_pool_scale, m_w_ple, m_w_gate_ple, m_ln_gain, m_ln_bias, v_w_in, v_b_in, v_w_out, v_attn_sinks, v_rel_bias, v_w_pool, v_pool_scale, v_w_ple, v_w_gate_ple, v_ln_gain, v_ln_bias):
    given = dict(x=x, p=p, w_in=w_in, b_in=b_in, w_out=w_out, attn_sinks=attn_sinks, rel_bias=rel_bias, w_pool=w_pool, pool_scale=pool_scale, w_ple=w_ple, w_gate_ple=w_gate_ple, ln_gain=ln_gain, ln_bias=ln_bias, loss_target=loss_target, m_w_in=m_w_in, m_b_in=m_b_in, m_w_out=m_w_out, m_attn_sinks=m_attn_sinks, m_rel_bias=m_rel_bias, m_w_pool=m_w_pool, m_pool_scale=m_pool_scale, m_w_ple=m_w_ple, m_w_gate_ple=m_w_gate_ple, m_ln_gain=m_ln_gain, m_ln_bias=m_ln_bias, v_w_in=v_w_in, v_b_in=v_b_in, v_w_out=v_w_out, v_attn_sinks=v_attn_sinks, v_rel_bias=v_rel_bias, v_w_pool=v_w_pool, v_pool_scale=v_pool_scale, v_w_ple=v_w_ple, v_w_gate_ple=v_w_gate_ple, v_ln_gain=v_ln_gain, v_ln_bias=v_ln_bias)
    weights = {n: given[n] for n in TWIN_WEIGHTS}
    shared = {n: given[n] for n in SHARED_INPUTS}
    per_example = {n: given[n] for n in ['x', 'p']}
    grad_fn = _jax.value_and_grad(_loss, argnums=(0, 1))

    def one_microbatch(ex, loss_target):
        ex = dict(ex)
        diff = ex.pop(TWIN_DIFF_INPUT)
        return grad_fn(weights, diff, {**shared, **ex}, loss_target)

    if N_MICROBATCH == 1:
        loss, (grad_w, grad_x) = one_microbatch(per_example, given["loss_target"])
    else:
        def body(carry, xs):
            loss_sum, grad_sum = carry
            l_k, (gw_k, gx_k) = one_microbatch(xs[0], xs[1])
            with _jax.named_scope("update"):
                return (loss_sum + l_k, _jax.tree.map(_jnp.add, grad_sum, gw_k)), gx_k

        init = (_jnp.zeros((), _jnp.float32), _jax.tree.map(_jnp.zeros_like, weights))
        (loss, grad_w), grad_x = _jax.lax.scan(body, init, (per_example, given["loss_target"]))
    with _jax.named_scope("update"):
        delta_w, new_m, new_v = {}, {}, {}
        for n in TWIN_WEIGHTS:
            delta_w[n], new_m[n], new_v[n] = _adamw(weights[n], grad_w[n], given["m_" + n], given["v_" + n])
    return (loss, grad_x, *[grad_w[n] for n in TWIN_WEIGHTS], *[delta_w[n] for n in TWIN_WEIGHTS],
            *[new_m[n] for n in TWIN_WEIGHTS], *[new_v[n] for n in TWIN_WEIGHTS])
```

```python
import functools
import math

import jax
import jax.numpy as jnp
from jax import lax
from jax.experimental import pallas as pl
from jax.experimental.pallas import tpu as pltpu

F32 = jnp.float32
BF16 = jnp.bfloat16
I32 = jnp.int32
MESH = pl.DeviceIdType.MESH

HEAD_DIM = 64
WINDOW = 128
KV_RATIO = 8
POOL_WINDOWS = (2, 4, 8, 16)
POOL_HALO = 16
REL_BUCKETS = 32
REL_MAX_DIST = 128
LN_EPS = 1e-5
ADAM_LR, ADAM_B1, ADAM_B2, ADAM_EPS, ADAM_WD, ADAM_STEP = 0.001, 0.9, 0.999, 1e-08, 0.01, 10

LANES = 128
VMEM_LIMIT = 52 * 1024 * 1024
N_CHIPS = 4
N_DEV = 8

D_MODEL = 2048
ATTN_W = 1024
POOL_W = 1024
KV_W = 128
N_HEADS = ATTN_W // HEAD_DIM
N_KV = N_HEADS // KV_RATIO
IN_COLS = 4352
SHARD = IN_COLS // N_CHIPS
SHARD_P = 1152
EXT = N_CHIPS * SHARD_P
HALF_TILE = SHARD_P - SHARD
OFF_Q, OFF_KA, OFF_KB, OFF_V, OFF_GA, OFF_U, OFF_UA, OFF_UB, OFF_GP = 0, 1024, 1152, 1280, 1408, 2432, 3328, 3456, 3584


def _cparams(sem=None, **kw):
    return pltpu.CompilerParams(dimension_semantics=sem, vmem_limit_bytes=VMEM_LIMIT, **kw)


def _pcall(body, **kw):
    return pl.pallas_call(body, **kw)


def _sigmoid(x):
    return 1.0 / (1.0 + jnp.exp(-x))


def _nt(a, b):
    return lax.dot_general(a, b, (((1,), (1,)), ((), ())), preferred_element_type=F32)


def _tn(a, b):
    return lax.dot_general(a, b, (((0,), (0,)), ((), ())), preferred_element_type=F32)


def _nn(a, b):
    return jnp.dot(a, b, preferred_element_type=F32)


def _mesh_pos():
    return lax.axis_index("x"), lax.axis_index("y"), lax.axis_index("c")


def _peer_chips(x, y):
    return [(1 - x, y), (x, 1 - y), (1 - x, 1 - y)]


def _piece_in(w_in):
    L, D, _ = w_in.shape
    tr = 256

    def body(w_ref, o_ref):
        x = w_ref[...]
        z = jnp.zeros((tr, HALF_TILE), F32)
        odd = lax.axis_index("y") == 1
        o_ref[...] = jnp.where(odd, jnp.concatenate([z, x], axis=1), jnp.concatenate([x, z], axis=1)).astype(BF16)

    return _pcall(
        body, name="piece_in", grid=(L, D // tr),
        in_specs=[pl.BlockSpec((None, tr, SHARD), lambda l, r: (l, r, 0))],
        out_specs=pl.BlockSpec((None, tr, SHARD_P), lambda l, r: (l, r, 0)),
        out_shape=jax.ShapeDtypeStruct((L, D, SHARD_P), BF16),
        compiler_params=_cparams(("parallel", "parallel")),
    )(w_in)


def _cast_piece(w, name):
    L, R, C = w.shape
    tr = min(R, 256)

    def body(w_ref, o_ref):
        o_ref[...] = w_ref[...].astype(BF16)

    return _pcall(
        body, name=name, grid=(L, R // tr),
        in_specs=[pl.BlockSpec((None, tr, C), lambda l, r: (l, r, 0))],
        out_specs=pl.BlockSpec((None, tr, C), lambda l, r: (l, r, 0)),
        out_shape=jax.ShapeDtypeStruct((L, R, C), BF16),
        compiler_params=_cparams(("parallel", "parallel")),
    )(w)


def _hbm_spec():
    return pl.BlockSpec(memory_space=pltpu.HBM)


def _allgather(pieces):
    n = len(pieces)

    def body(*refs):
        p, g = refs[:n], refs[n:2 * n]
        send, recv, fsend, frecv, lsem = refs[2 * n:]
        x, y, c = _mesh_pos()
        me = 2 * x + y
        chips = _peer_chips(x, y)
        sibling = (x, y, 1 - c)
        local = [pltpu.make_async_copy(p[t], g[t].at[me], lsem.at[t]) for t in range(n)]
        for cp in local:
            cp.start()

        def ici(t, k, slot):
            return pltpu.make_async_remote_copy(
                src_ref=p[t].at[:, c], dst_ref=g[t].at[slot, :, c], send_sem=send.at[3 * t + k], recv_sem=recv.at[3 * t + k],
                device_id=(*chips[k], c), device_id_type=MESH)

        def fwd(t, k, half):
            slot = 2 * chips[k][0] + chips[k][1]
            return pltpu.make_async_remote_copy(
                src_ref=g[t].at[slot, :, half], dst_ref=g[t].at[slot, :, half], send_sem=fsend.at[3 * t + k],
                recv_sem=frecv.at[3 * t + k], device_id=sibling, device_id_type=MESH)

        sends = [ici(t, k, me) for k in range(3) for t in range(n)]
        for cp in sends:
            cp.start()
        forwards = []
        for k in range(3):
            for t in range(n):
                ici(t, k, 2 * chips[k][0] + chips[k][1]).wait_recv()
                f = fwd(t, k, c)
                f.start()
                forwards.append(f)
        for k in range(3):
            for t in range(n):
                fwd(t, k, 1 - c).wait_recv()
        for cp in sends + forwards:
            cp.wait_send()
        for cp in local:
            cp.wait()

    outs = _pcall(
        body, name="allgather_weights",
        in_specs=[_hbm_spec() for _ in range(n)],
        out_specs=[_hbm_spec() for _ in range(n)],
        out_shape=[jax.ShapeDtypeStruct((N_CHIPS,) + tuple(a.shape), a.dtype) for a in pieces],
        scratch_shapes=[pltpu.SemaphoreType.DMA((3 * n,)), pltpu.SemaphoreType.DMA((3 * n,)),
                        pltpu.SemaphoreType.DMA((3 * n,)), pltpu.SemaphoreType.DMA((3 * n,)),
                        pltpu.SemaphoreType.DMA((n,))],
    )(*pieces)
    return list(outs)


def _rs_pair_exchange(grads):
    n = len(grads)

    def body(*refs):
        g, r = refs[:n], refs[n:2 * n]
        send, recv = refs[2 * n:]
        x, y, c = _mesh_pos()
        cps = [pltpu.make_async_remote_copy(src_ref=g[t].at[:, :, 1 - c], dst_ref=r[t], send_sem=send.at[t], recv_sem=recv.at[t],
                                            device_id=(x, y, 1 - c), device_id_type=MESH) for t in range(n)]
        for cp in cps:
            cp.start()
        for cp in cps:
            cp.wait()

    outs = _pcall(
        body, name="rs_pair_exchange",
        in_specs=[_hbm_spec() for _ in range(n)], out_specs=[_hbm_spec() for _ in range(n)],
        out_shape=[jax.ShapeDtypeStruct((a.shape[0], a.shape[1], a.shape[3], a.shape[4]), F32) for a in grads],
        scratch_shapes=[pltpu.SemaphoreType.DMA((n,)), pltpu.SemaphoreType.DMA((n,))],
    )(*grads)
    return list(outs)


def _rs_pair_add(g5, r4, cidx, name):
    J, L, _, h, C = g5.shape
    th = min(h, 256)

    def body(c_ref, g_ref, r_ref, o32_ref, o16_ref):
        s = g_ref[...] + r_ref[...]
        o32_ref[...] = s
        o16_ref[...] = s.astype(BF16)

    spec4 = pl.BlockSpec((None, None, th, C), lambda j, l, r, c: (j, l, r, 0))
    gs = pltpu.PrefetchScalarGridSpec(
        num_scalar_prefetch=1, grid=(J, L, h // th),
        in_specs=[pl.BlockSpec((None, None, None, th, C), lambda j, l, r, c: (j, l, c[0], r, 0)), spec4],
        out_specs=[spec4, spec4])
    return _pcall(body, name=name, grid_spec=gs,
                  out_shape=[jax.ShapeDtypeStruct((J, L, h, C), F32), jax.ShapeDtypeStruct((J, L, h, C), BF16)],
                  compiler_params=_cparams(("parallel", "parallel", "parallel")))(cidx, g5, r4)


def _rs_chip_exchange(parts16):
    n = len(parts16)

    def body(*refs):
        p, r = refs[:n], refs[n:2 * n]
        send, recv = refs[2 * n:]
        x, y, c = _mesh_pos()
        chips = _peer_chips(x, y)
        cps = []
        for k in range(3):
            slot = 2 * chips[k][0] + chips[k][1]
            for t in range(n):
                cps.append(pltpu.make_async_remote_copy(
                    src_ref=p[t].at[slot], dst_ref=r[t].at[k], send_sem=send.at[3 * t + k], recv_sem=recv.at[3 * t + k],
                    device_id=(*chips[k], c), device_id_type=MESH))
        for cp in cps:
            cp.start()
        for cp in cps:
            cp.wait()

    outs = _pcall(
        body, name="rs_chip_exchange",
        in_specs=[_hbm_spec() for _ in range(n)], out_specs=[_hbm_spec() for _ in range(n)],
        out_shape=[jax.ShapeDtypeStruct((3,) + tuple(a.shape[1:]), BF16) for a in parts16],
        scratch_shapes=[pltpu.SemaphoreType.DMA((3 * n,)), pltpu.SemaphoreType.DMA((3 * n,))],
    )(*parts16)
    return list(outs)


def _rs_chip_add(p32, r3, meidx, name):
    J, L, h, C = p32.shape
    th = min(h, 256)

    def body(me_ref, p_ref, r_ref, o_ref):
        o_ref[...] = ((p_ref[...] + r_ref[0].astype(F32)) + r_ref[1].astype(F32)) + r_ref[2].astype(F32)

    gs = pltpu.PrefetchScalarGridSpec(
        num_scalar_prefetch=1, grid=(L, h // th),
        in_specs=[pl.BlockSpec((None, None, th, C), lambda l, r, me: (me[0], l, r, 0)),
                  pl.BlockSpec((3, None, th, C), lambda l, r, me: (0, l, r, 0))],
        out_specs=pl.BlockSpec((None, th, C), lambda l, r, me: (l, r, 0)))
    return _pcall(body, name=name, grid_spec=gs, out_shape=jax.ShapeDtypeStruct((L, h, C), F32),
                  compiler_params=_cparams(("parallel", "parallel")))(meidx, p32, r3)


def _rs_pair_share(halves):
    n = len(halves)

    def body(*refs):
        hsrc, full = refs[:n], refs[n:2 * n]
        send, recv, lsem = refs[2 * n:]
        x, y, c = _mesh_pos()
        local = [pltpu.make_async_copy(hsrc[t], full[t].at[:, c], lsem.at[t]) for t in range(n)]
        for cp in local:
            cp.start()
        cps = [pltpu.make_async_remote_copy(src_ref=hsrc[t], dst_ref=full[t].at[:, c], send_sem=send.at[t], recv_sem=recv.at[t],
                                            device_id=(x, y, 1 - c), device_id_type=MESH) for t in range(n)]
        for cp in cps:
            cp.start()
        for t in range(n):
            pltpu.make_async_remote_copy(src_ref=hsrc[t], dst_ref=full[t].at[:, 1 - c], send_sem=send.at[t], recv_sem=recv.at[t],
                                         device_id=(x, y, 1 - c), device_id_type=MESH).wait_recv()
        for cp in cps:
            cp.wait_send()
        for cp in local:
            cp.wait()

    outs = _pcall(
        body, name="rs_pair_share",
        in_specs=[_hbm_spec() for _ in range(n)], out_specs=[_hbm_spec() for _ in range(n)],
        out_shape=[jax.ShapeDtypeStruct((a.shape[0], 2, a.shape[1], a.shape[2]), F32) for a in halves],
        scratch_shapes=[pltpu.SemaphoreType.DMA((n,)), pltpu.SemaphoreType.DMA((n,)), pltpu.SemaphoreType.DMA((n,))],
    )(*halves)
    return list(outs)


def _adamw_math(w, g, m, v):
    nm = ADAM_B1 * m + (1.0 - ADAM_B1) * g
    nv = ADAM_B2 * v + (1.0 - ADAM_B2) * (g * g)
    m_hat = nm / (1.0 - ADAM_B1 ** ADAM_STEP)
    v_hat = nv / (1.0 - ADAM_B2 ** ADAM_STEP)
    delta = -ADAM_LR * (m_hat / (jnp.sqrt(v_hat) + ADAM_EPS) + ADAM_WD * w)
    return delta, nm, nv


def _adamw(w, g, m, v, name, unshift=False):
    L, R, C = w.shape
    Cg = g.shape[2]
    tr = min(R, 256)

    def body(w_ref, g_ref, m_ref, v_ref, go_ref, d_ref, nm_ref, nv_ref):
        gv = g_ref[...]
        if unshift:
            odd = lax.axis_index("y") == 1
            gv = jnp.where(odd, pltpu.roll(gv, Cg - HALF_TILE, axis=1), gv)[:, :C]
        d, nm, nv = _adamw_math(w_ref[...], gv, m_ref[...], v_ref[...])
        go_ref[...] = gv
        d_ref[...] = d
        nm_ref[...] = nm
        nv_ref[...] = nv

    wspec = pl.BlockSpec((None, tr, C), lambda l, r: (l, r, 0))
    sds = jax.ShapeDtypeStruct((L, R, C), F32)
    return _pcall(body, name=name, grid=(L, R // tr),
                  in_specs=[wspec, pl.BlockSpec((None, tr, Cg), lambda l, r: (l, r, 0)), wspec, wspec],
                  out_specs=[wspec, wspec, wspec, wspec], out_shape=[sds, sds, sds, sds],
                  compiler_params=_cparams(("parallel", "parallel")))(w, g, m, v)


def _small_allreduce_adamw(gv, wv, mv, vv):
    NR = gv.shape[0]

    def body(g_ref, w_ref, m_ref, v_ref, go_ref, d_ref, nm_ref, nv_ref, gath, send, recv):
        x, y, c = _mesh_pos()
        rank = 4 * x + 2 * y + c
        gath[rank] = g_ref[...]
        cps = []
        for msk in range(1, N_DEV):
            bx, by, bc = (msk >> 2) & 1, (msk >> 1) & 1, msk & 1
            peer = (1 - x if bx else x, 1 - y if by else y, 1 - c if bc else c)
            cps.append(pltpu.make_async_remote_copy(src_ref=g_ref, dst_ref=gath.at[rank], send_sem=send.at[msk - 1],
                                                    recv_sem=recv.at[msk - 1], device_id=peer, device_id_type=MESH))
        for cp in cps:
            cp.start()
        for msk in range(1, N_DEV):
            bx, by, bc = (msk >> 2) & 1, (msk >> 1) & 1, msk & 1
            peer = (1 - x if bx else x, 1 - y if by else y, 1 - c if bc else c)
            prank = 4 * peer[0] + 2 * peer[1] + peer[2]
            pltpu.make_async_remote_copy(src_ref=g_ref, dst_ref=gath.at[prank], send_sem=send.at[msk - 1],
                                         recv_sem=recv.at[msk - 1], device_id=peer, device_id_type=MESH).wait_recv()
        for cp in cps:
            cp.wait_send()
        tot = gath[0]
        for r in range(1, N_DEV):
            tot = tot + gath[r]
        d, nm, nv = _adamw_math(w_ref[...], tot, m_ref[...], v_ref[...])
        go_ref[...] = tot
        d_ref[...] = d
        nm_ref[...] = nm
        nv_ref[...] = nv

    vm = pl.BlockSpec(memory_space=pltpu.VMEM)
    sds = jax.ShapeDtypeStruct((NR, LANES), F32)
    return _pcall(body, name="small_allreduce_adamw", in_specs=[vm, vm, vm, vm], out_specs=[vm, vm, vm, vm],
                  out_shape=[sds, sds, sds, sds],
                  scratch_shapes=[pltpu.VMEM((N_DEV, NR, LANES), F32), pltpu.SemaphoreType.DMA((N_DEV - 1,)),
                                  pltpu.SemaphoreType.DMA((N_DEV - 1,))],
                  compiler_params=pltpu.CompilerParams(vmem_limit_bytes=VMEM_LIMIT))(gv, wv, mv, vv)


def _split3(a):
    h1 = a.astype(BF16)
    r1 = a - h1.astype(F32)
    h2 = r1.astype(BF16)
    h3 = (r1 - h2.astype(F32)).astype(BF16)
    return h1, h2, h3


def _bucket_onehot():
    qq = jnp.arange(WINDOW)[:, None]
    kk = jnp.arange(2 * WINDOW)[None, :]
    dist = qq + WINDOW - kk
    max_exact = REL_BUCKETS // 2
    d = jnp.maximum(dist, 0)
    d_f = jnp.maximum(d, 1).astype(F32)
    large = max_exact + (jnp.log(d_f / max_exact) / math.log(REL_MAX_DIST / max_exact) * (REL_BUCKETS - max_exact)).astype(I32)
    large = jnp.minimum(large, REL_BUCKETS - 1)
    bucket = jnp.where(d < max_exact, d, large)
    inwin = (dist >= 0) & (dist < WINDOW)
    oh = (bucket[None] == jnp.arange(REL_BUCKETS)[:, None, None]) & inwin[None]
    return oh.reshape(REL_BUCKETS, WINDOW * 2 * WINDOW).astype(BF16)


def _bias_fwd(rel_bias_t, onehot):
    H, N = rel_bias_t.shape[0], onehot.shape[1]
    tn = 4096

    def body(t_ref, oh_ref, o_ref):
        h1, h2, h3 = _split3(t_ref[...])
        oh = oh_ref[...]
        o_ref[...] = (_nn(h1, oh) + _nn(h2, oh)) + _nn(h3, oh)

    return _pcall(body, name="bias_fwd", grid=(N // tn,),
                  in_specs=[pl.BlockSpec((H, REL_BUCKETS), lambda i: (0, 0)), pl.BlockSpec((REL_BUCKETS, tn), lambda i: (0, i))],
                  out_specs=pl.BlockSpec((H, tn), lambda i: (0, i)), out_shape=jax.ShapeDtypeStruct((H, N), F32),
                  compiler_params=_cparams(("parallel",)))(rel_bias_t, onehot)


def _bias_bwd(dbias, onehot):
    H, N = dbias.shape
    tn = 4096

    def body(d_ref, oh_ref, o_ref):
        @pl.when(pl.program_id(0) == 0)
        def _():
            o_ref[...] = jnp.zeros_like(o_ref)
        h1, h2, h3 = _split3(d_ref[...])
        oh = oh_ref[...]
        o_ref[...] += (_nt(h1, oh) + _nt(h2, oh)) + _nt(h3, oh)

    return _pcall(body, name="bias_bwd", grid=(N // tn,),
                  in_specs=[pl.BlockSpec((H, tn), lambda i: (0, i)), pl.BlockSpec((REL_BUCKETS, tn), lambda i: (0, i))],
                  out_specs=pl.BlockSpec((H, REL_BUCKETS), lambda i: (0, 0)), out_shape=jax.ShapeDtypeStruct((H, REL_BUCKETS), F32),
                  compiler_params=_cparams(("arbitrary",)))(dbias, onehot)


def _to_bf16(x):
    S, D = x.shape
    tm = min(S, 512)

    def body(x_ref, o_ref):
        o_ref[...] = x_ref[...].astype(BF16)

    return _pcall(body, name="to_bf16", grid=(S // tm,), in_specs=[pl.BlockSpec((tm, D), lambda i: (i, 0))],
                  out_specs=pl.BlockSpec((tm, D), lambda i: (i, 0)), out_shape=jax.ShapeDtypeStruct((S, D), BF16),
                  compiler_params=_cparams(("parallel",)))(x)


def _inproj(xb, g_in, b_ext, l):
    S, D = xb.shape
    tm = min(S, 512)

    def body(x_ref, w_ref, b_ref, q_ref, k_ref, v_ref, ga_ref, u_ref, gp_ref):
        j = pl.program_id(1)
        acc = _nn(x_ref[...], w_ref[...]) + b_ref[...]

        @pl.when(j == 0)
        def _():
            q_ref[...] = acc[:, :1024].astype(BF16)
            k_ref[...] = acc[:, 1024:1152]

        @pl.when(j == 1)
        def _():
            k_ref[...] += acc[:, 0:128]
            v_ref[...] = acc[:, 128:256]
            ga_ref[:, 0:896] = acc[:, 256:1152]

        @pl.when(j == 2)
        def _():
            ga_ref[:, 896:1024] = acc[:, 0:128]
            u_ref[:, 0:896] = acc[:, 128:1024]
            u_ref[:, 896:1024] = acc[:, 1024:1152]

        @pl.when(j == 3)
        def _():
            u_ref[:, 896:1024] += acc[:, 0:128]
            gp_ref[...] = acc[:, 128:1152]

    def ospec(w):
        return pl.BlockSpec((tm, w), lambda i, j: (i, 0))

    return _pcall(
        body, name="inproj", grid=(S // tm, N_CHIPS),
        in_specs=[pl.BlockSpec((tm, D), lambda i, j: (i, 0)),
                  pl.BlockSpec((None, None, D, SHARD_P), lambda i, j: (j, l, 0, 0)),
                  pl.BlockSpec((None, 1, SHARD_P), lambda i, j: (l, 0, j))],
        out_specs=[ospec(ATTN_W), ospec(KV_W), ospec(KV_W), ospec(ATTN_W), ospec(POOL_W), ospec(POOL_W)],
        out_shape=[jax.ShapeDtypeStruct((S, ATTN_W), BF16), jax.ShapeDtypeStruct((S, KV_W), F32), jax.ShapeDtypeStruct((S, KV_W), F32),
                   jax.ShapeDtypeStruct((S, ATTN_W), F32), jax.ShapeDtypeStruct((S, POOL_W), F32), jax.ShapeDtypeStruct((S, POOL_W), F32)],
        compiler_params=_cparams(("parallel", "arbitrary")),
    )(xb, g_in, b_ext)


def _matmul_nn(a, b4, l, name):
    S, K = a.shape
    N = b4.shape[3]
    kq = b4.shape[2]
    tm, tn = min(S, 512), min(N, 1024)

    def body(a_ref, b_ref, o_ref):
        acc = _nn(a_ref[:, 0:kq], b_ref[0])
        for j in range(1, N_CHIPS):
            acc = acc + _nn(a_ref[:, j * kq:(j + 1) * kq], b_ref[j])
        o_ref[...] = acc

    return _pcall(body, name=name, grid=(N // tn, S // tm),
                  in_specs=[pl.BlockSpec((tm, K), lambda n, i: (i, 0)), pl.BlockSpec((N_CHIPS, None, kq, tn), lambda n, i: (0, l, 0, n))],
                  out_specs=pl.BlockSpec((tm, tn), lambda n, i: (i, n)), out_shape=jax.ShapeDtypeStruct((S, N), F32),
                  compiler_params=_cparams(("parallel", "parallel")))(a, b4)


def _band_mask(n):
    qq = lax.broadcasted_iota(I32, (WINDOW, 2 * WINDOW), 0)
    kk = lax.broadcasted_iota(I32, (WINDOW, 2 * WINDOW), 1)
    dist = qq + WINDOW - kk
    return (dist >= 0) & (dist < WINDOW) & ((kk >= WINDOW) | (n > 0))


def _dup_heads(src_ref, dst_ref):
    a = src_ref[...]
    r = pltpu.roll(a, HEAD_DIM, axis=1)
    lo = lax.broadcasted_iota(I32, a.shape, 1) < HEAD_DIM
    dst_ref[0] = jnp.where(lo, a, r).astype(BF16)
    dst_ref[1] = jnp.where(lo, r, a).astype(BF16)


def _kv_block(ref, h, prev, cur):
    return jnp.concatenate([ref[h, pl.ds(prev, WINDOW), :], ref[h, pl.ds(cur, WINDOW), :]], axis=0)


def _attn_fwd(q, k, v, ga, bias, sinks, l):
    S = q.shape[0]
    nblk = S // WINDOW
    scale = 1.0 / math.sqrt(HEAD_DIM)

    def body(sink_ref, q_ref, k_ref, v_ref, ga_ref, bias_ref, o_ref, ca_ref, lse_ref, kd, vd):
        n = pl.program_id(0)

        @pl.when(n == 0)
        def _():
            _dup_heads(k_ref, kd)
            _dup_heads(v_ref, vd)

        cur = pl.multiple_of(n * WINDOW, WINDOW)
        prev = pl.multiple_of(jnp.maximum(n - 1, 0) * WINDOW, WINDOW)
        valid = _band_mask(n)
        lane = lax.broadcasted_iota(I32, (WINDOW, LANES), 1)
        lo = lane < HEAD_DIM
        kb = [_kv_block(kd, h, prev, cur) for h in range(N_KV)]
        vb = [_kv_block(vd, h, prev, cur) for h in range(N_KV)]
        lse_mat = jnp.zeros((WINDOW, LANES), F32)
        for pair in range(N_HEADS // 2):
            sl = slice(LANES * pair, LANES * (pair + 1))
            qp = q_ref[:, sl]
            outs = []
            for par in range(2):
                h = 2 * pair + par
                kvh = h // KV_RATIO
                qm = jnp.where(lo if par == 0 else jnp.logical_not(lo), qp, jnp.zeros_like(qp))
                s = _nt(qm, kb[kvh]) * scale + bias_ref[h]
                s = jnp.where(valid, s, -1e30)
                sink = sink_ref[l, h]
                m = jnp.maximum(jnp.max(s, axis=1, keepdims=True), sink)
                e = jnp.exp(s - m)
                den = jnp.sum(e, axis=1, keepdims=True) + jnp.exp(sink - m)
                p = (e * (1.0 / den)).astype(BF16)
                outs.append(_nn(p, vb[kvh]))
                lse_mat = jnp.where(lane == h, m + jnp.log(den), lse_mat)
            o_pair = jnp.where(lo, outs[0], outs[1])
            o_ref[:, sl] = o_pair
            gav = ga_ref[:, sl]
            ca_ref[:, sl] = (o_pair * (gav * _sigmoid(gav))).astype(BF16)
        lse_ref[...] = lse_mat

    blk = pl.BlockSpec((WINDOW, ATTN_W), lambda n: (n, 0))
    full_kv = pl.BlockSpec((S, KV_W), lambda n: (0, 0))
    return _pcall(
        body, name="attn_fwd", grid=(nblk,),
        in_specs=[pl.BlockSpec(memory_space=pltpu.SMEM), blk, full_kv, full_kv, blk,
                  pl.BlockSpec((N_HEADS, WINDOW, 2 * WINDOW), lambda n: (0, 0, 0))],
        out_specs=[blk, blk, pl.BlockSpec((WINDOW, LANES), lambda n: (n, 0))],
        out_shape=[jax.ShapeDtypeStruct((S, ATTN_W), F32), jax.ShapeDtypeStruct((S, ATTN_W), BF16), jax.ShapeDtypeStruct((S, LANES), F32)],
        scratch_shapes=[pltpu.VMEM((N_KV, S, LANES), BF16), pltpu.VMEM((N_KV, S, LANES), BF16)],
        compiler_params=_cparams(("arbitrary",)),
    )(sinks, q, k, v, ga, bias)


def _pool_diff(u, halo, tile_index, tm):
    gw = POOL_W // len(POOL_WINDOWS)
    xh = jnp.concatenate([halo, u], axis=0)
    sums = []
    s = xh
    for step in (1, 2, 4, 8):
        s = s + pltpu.roll(s, step, axis=0)
        sums.append(s)
    t = tile_index * tm + lax.broadcasted_iota(I32, (tm, gw), 0)
    diffs = []
    for g, w in enumerate(POOL_WINDOWS):
        cols = slice(g * gw, (g + 1) * gw)
        cnt = jnp.minimum(t + 1, w).astype(F32)
        diffs.append(sums[g][POOL_HALO:, cols] / cnt - u[:, cols])
    return diffs


def _pool_weight(wp_ref, g):
    r = wp_ref.shape[1] // len(POOL_WINDOWS)
    return jnp.concatenate([wp_ref[j, g * r:(g + 1) * r, :] for j in range(N_CHIPS)], axis=0)


def _pool_fwd(u, gp, ca, g_pool, ps, l):
    S = u.shape[0]
    tm = min(S, 256)
    hb = tm // POOL_HALO
    gw = POOL_W // len(POOL_WINDOWS)

    def body(u_ref, uh_ref, gp_ref, ca_ref, wp_ref, ps_ref, c_ref):
        i = pl.program_id(0)
        uv = u_ref[...]
        halo = jnp.where(i > 0, uh_ref[...], 0.0)
        diffs = _pool_diff(uv, halo, i, tm)
        c_ref[:, 0:ATTN_W] = ca_ref[...]
        for g in range(len(POOL_WINDOWS)):
            cols = slice(g * gw, (g + 1) * gw)
            mm = _nn(diffs[g].astype(BF16), _pool_weight(wp_ref, g))
            gpv = gp_ref[:, cols]
            b = (mm * ps_ref[:, cols]) * (gpv * _sigmoid(gpv))
            c_ref[:, ATTN_W + g * gw:ATTN_W + (g + 1) * gw] = b.astype(BF16)

    row = pl.BlockSpec((tm, POOL_W), lambda i: (i, 0))
    return _pcall(
        body, name="pool_fwd", grid=(S // tm,),
        in_specs=[row, pl.BlockSpec((POOL_HALO, POOL_W), lambda i: (jnp.maximum(i * hb - 1, 0), 0)), row, row,
                  pl.BlockSpec((N_CHIPS, None, g_pool.shape[2], g_pool.shape[3]), lambda i: (0, l, 0, 0)),
                  pl.BlockSpec((None, 1, POOL_W), lambda i: (l, 0, 0))],
        out_specs=pl.BlockSpec((tm, ATTN_W + POOL_W), lambda i: (i, 0)),
        out_shape=jax.ShapeDtypeStruct((S, ATTN_W + POOL_W), BF16),
        compiler_params=_cparams(("parallel",)),
    )(u, u, gp, ca, g_pool, ps)


def _ple_embed(p_ref, wple_ref):
    pb = p_ref[...].astype(BF16)
    return jnp.concatenate([_nn(pb, wple_ref[j]) for j in range(N_CHIPS)], axis=1)


def _outproj_ln(c, x, gl, p, g_out, g_ple, gain, bias, l, alpha):
    S, D = x.shape
    tm = min(S, 128)
    kq = D // N_CHIPS

    def body(c_ref, x_ref, gl_ref, p_ref, wo_ref, wp_ref, gain_ref, bias_ref, y_ref, yb_ref, xh_ref, rs_ref):
        mix = _nn(c_ref[:, 0:kq], wo_ref[0])
        for j in range(1, N_CHIPS):
            mix = mix + _nn(c_ref[:, j * kq:(j + 1) * kq], wo_ref[j])
        ple = _sigmoid(gl_ref[...]) * _ple_embed(p_ref, wp_ref)
        z = (alpha * x_ref[...] + mix) + ple
        mu = jnp.mean(z, axis=1, keepdims=True)
        zc = z - mu
        var = jnp.mean(zc * zc, axis=1, keepdims=True)
        rstd = lax.rsqrt(var + LN_EPS)
        xhat = zc * rstd
        y = xhat * gain_ref[...] + bias_ref[...]
        y_ref[...] = y
        yb_ref[...] = y.astype(BF16)
        xh_ref[...] = xhat
        rs_ref[...] = rstd

    row = pl.BlockSpec((tm, D), lambda i: (i, 0))
    vec = pl.BlockSpec((None, 1, D), lambda i: (l, 0, 0))
    return _pcall(
        body, name="outproj_ln", grid=(S // tm,),
        in_specs=[row, row, row, pl.BlockSpec((tm, p.shape[1]), lambda i: (i, 0)),
                  pl.BlockSpec((N_CHIPS, None, kq, D), lambda i: (0, l, 0, 0)),
                  pl.BlockSpec((N_CHIPS, None, g_ple.shape[2], g_ple.shape[3]), lambda i: (0, l, 0, 0)), vec, vec],
        out_specs=[row, row, row, pl.BlockSpec((tm, 1), lambda i: (i, 0))],
        out_shape=[jax.ShapeDtypeStruct((S, D), F32), jax.ShapeDtypeStruct((S, D), BF16), jax.ShapeDtypeStruct((S, D), F32),
                   jax.ShapeDtypeStruct((S, 1), F32)],
        compiler_params=_cparams(("parallel",)),
    )(c, x, gl, p, g_out, g_ple, gain, bias)


def _loss_and_grad(y, target):
    S, D = y.shape
    tm = min(S, 512)

    def body(y_ref, t_ref, dy_ref, acc_ref):
        @pl.when(pl.program_id(0) == 0)
        def _():
            acc_ref[...] = jnp.zeros_like(acc_ref)
        d = y_ref[...] - t_ref[...]
        dy_ref[...] = d * (1.0 / D)
        acc_ref[...] += jnp.sum(jnp.mean(d * d, axis=1, keepdims=True), axis=0, keepdims=True)

    row = pl.BlockSpec((tm, D), lambda i: (i, 0))
    return _pcall(body, name="loss", grid=(S // tm,), in_specs=[row, row],
                  out_specs=[row, pl.BlockSpec((8, LANES), lambda i: (0, 0))],
                  out_shape=[jax.ShapeDtypeStruct((S, D), F32), jax.ShapeDtypeStruct((8, LANES), F32)],
                  compiler_params=_cparams(("arbitrary",)))(y, target)


def _ln_bwd(dy, xhat, rstd, gl, p, g_ple, gain, l):
    S, D = dy.shape
    tm = min(S, 256)

    def body(dy_ref, xh_ref, rs_ref, gl_ref, p_ref, wp_ref, gain_ref, dz_ref, dzb_ref, de_ref, dgl_ref, gg_ref, gb_ref):
        @pl.when(pl.program_id(0) == 0)
        def _():
            gg_ref[...] = jnp.zeros_like(gg_ref)
            gb_ref[...] = jnp.zeros_like(gb_ref)
        dyv = dy_ref[...]
        xh = xh_ref[...]
        dxh = dyv * gain_ref[...]
        m1 = jnp.mean(dxh, axis=1, keepdims=True)
        m2 = jnp.mean(dxh * xh, axis=1, keepdims=True)
        dz = rs_ref[...] * ((dxh - m1) - xh * m2)
        gg_ref[...] += jnp.sum(dyv * xh, axis=0, keepdims=True)
        gb_ref[...] += jnp.sum(dyv, axis=0, keepdims=True)
        sg = _sigmoid(gl_ref[...])
        e = _ple_embed(p_ref, wp_ref)
        dz_ref[...] = dz
        dzb_ref[...] = dz.astype(BF16)
        de_ref[...] = (dz * sg).astype(BF16)
        dgl_ref[...] = ((dz * e) * (sg * (1.0 - sg))).astype(BF16)

    row = pl.BlockSpec((tm, D), lambda i: (i, 0))
    vec_in = pl.BlockSpec((None, 1, D), lambda i: (l, 0, 0))
    vec_out = pl.BlockSpec((1, D), lambda i: (0, 0))
    bsd = jax.ShapeDtypeStruct((S, D), BF16)
    return _pcall(
        body, name="ln_bwd", grid=(S // tm,),
        in_specs=[row, row, pl.BlockSpec((tm, 1), lambda i: (i, 0)), row, pl.BlockSpec((tm, p.shape[1]), lambda i: (i, 0)),
                  pl.BlockSpec((N_CHIPS, None, g_ple.shape[2], g_ple.shape[3]), lambda i: (0, l, 0, 0)), vec_in],
        out_specs=[row, row, row, row, vec_out, vec_out],
        out_shape=[jax.ShapeDtypeStruct((S, D), F32), bsd, bsd, bsd, jax.ShapeDtypeStruct((1, D), F32), jax.ShapeDtypeStruct((1, D), F32)],
        compiler_params=_cparams(("arbitrary",)),
    )(dy, xhat, rstd, gl, p, g_ple, gain)


def _matmul_nt(a, b4, l, name, add, add_scale):
    S = a.shape[0]
    KS, _, N, tk = b4.shape
    tm, tn = min(S, 512), min(N, 1024)

    def body(a_ref, b_ref, add_ref, o_ref):
        k = pl.program_id(2)

        @pl.when(k == 0)
        def _():
            o_ref[...] = add_scale * add_ref[...]
        o_ref[...] += _nt(a_ref[...], b_ref[...])

    return _pcall(body, name=name, grid=(S // tm, N // tn, KS),
                  in_specs=[pl.BlockSpec((tm, tk), lambda i, n, k: (i, k)),
                            pl.BlockSpec((None, None, tn, tk), lambda i, n, k: (k, l, n, 0)),
                            pl.BlockSpec((tm, tn), lambda i, n, k: (i, n))],
                  out_specs=pl.BlockSpec((tm, tn), lambda i, n, k: (i, n)), out_shape=jax.ShapeDtypeStruct((S, N), F32),
                  compiler_params=_cparams(("parallel", "parallel", "arbitrary")))(a, b4, add)


def _matmul_nt_rows(a, b4, l, name, add=None, add_scale=1.0):
    S, K = a.shape
    nq = b4.shape[2]
    tm = min(S, 512)
    out_spec = pl.BlockSpec((tm, nq), lambda j, i: (i, j))
    in_specs = [pl.BlockSpec((tm, K), lambda j, i: (i, 0)), pl.BlockSpec((None, None, nq, K), lambda j, i: (j, l, 0, 0))]
    if add is None:
        def body(a_ref, b_ref, o_ref):
            o_ref[...] = _nt(a_ref[...], b_ref[...])
        args = (a, b4)
    else:
        def body(a_ref, b_ref, add_ref, o_ref):
            o_ref[...] = _nt(a_ref[...], b_ref[...]) + add_scale * add_ref[...]
        in_specs.append(out_spec)
        args = (a, b4, add)

    return _pcall(body, name=name, grid=(N_CHIPS, S // tm), in_specs=in_specs,
                  out_specs=out_spec, out_shape=jax.ShapeDtypeStruct((S, N_CHIPS * nq), F32),
                  compiler_params=_cparams(("parallel", "parallel")))(*args)


def _matmul_tn(a, b, acc, l, name, by_rows):
    S = a.shape[0]
    J, L, R, C = acc.shape
    ts = min(S, 512)

    def body(a_ref, b_ref, acc_in_ref, o_ref):
        del acc_in_ref

        @pl.when(pl.program_id(1) == 0)
        def _():
            o_ref[...] = jnp.zeros_like(o_ref)
        o_ref[...] += _tn(a_ref[...], b_ref[...])

    if by_rows:
        a_spec = pl.BlockSpec((ts, R), lambda j, s: (s, j))
        b_spec = pl.BlockSpec((ts, C), lambda j, s: (s, 0))
    else:
        a_spec = pl.BlockSpec((ts, R), lambda j, s: (s, 0))
        b_spec = pl.BlockSpec((ts, C), lambda j, s: (s, j))
    return _pcall(body, name=name, grid=(J, S // ts),
                  in_specs=[a_spec, b_spec, pl.BlockSpec(memory_space=pl.ANY)],
                  out_specs=pl.BlockSpec((None, None, R, C), lambda j, s: (j, l, 0, 0)),
                  out_shape=jax.ShapeDtypeStruct(acc.shape, F32), input_output_aliases={2: 0},
                  compiler_params=_cparams(("parallel", "arbitrary")))(a, b, acc)


def _pool_bwd(u, gp, dc, g_pool, ps, gw_acc, l):
    S = u.shape[0]
    tm = min(S, 256)
    hb = tm // POOL_HALO
    ngrp = len(POOL_WINDOWS)
    gw = POOL_W // ngrp
    rr = gw // N_CHIPS

    def body(u_ref, uh_ref, gp_ref, dc_ref, wp_ref, ps_ref, acc_in_ref, dd_ref, dgp_ref, gps_ref, gwp_ref):
        del acc_in_ref
        i = pl.program_id(0)

        @pl.when(i == 0)
        def _():
            gps_ref[...] = jnp.zeros_like(gps_ref)
            gwp_ref[...] = jnp.zeros_like(gwp_ref)
        uv = u_ref[...]
        halo = jnp.where(i > 0, uh_ref[...], 0.0)
        diffs = _pool_diff(uv, halo, i, tm)
        for g in range(ngrp):
            cols = slice(g * gw, (g + 1) * gw)
            w = _pool_weight(wp_ref, g)
            db = diffs[g].astype(BF16)
            mm = _nn(db, w)
            gpv = gp_ref[:, cols]
            sg = _sigmoid(gpv)
            si = gpv * sg
            dsi = sg * (1.0 + gpv * (1.0 - sg))
            dcb = dc_ref[:, cols]
            psv = ps_ref[:, cols]
            d_mm = (dcb * si) * psv
            gps_ref[:, cols] += jnp.sum((dcb * si) * mm, axis=0, keepdims=True)
            dgp_ref[:, cols] = (dcb * (mm * psv)) * dsi
            d_mmb = d_mm.astype(BF16)
            dd_ref[:, cols] = _nt(d_mmb, w)
            gwt = _tn(db, d_mmb)
            for j in range(N_CHIPS):
                gwp_ref[j, g * rr:(g + 1) * rr, :] += gwt[j * rr:(j + 1) * rr, :]

    row = pl.BlockSpec((tm, POOL_W), lambda i: (i, 0))
    return _pcall(
        body, name="pool_bwd", grid=(S // tm,),
        in_specs=[row, pl.BlockSpec((POOL_HALO, POOL_W), lambda i: (jnp.maximum(i * hb - 1, 0), 0)), row,
                  pl.BlockSpec((tm, POOL_W), lambda i: (i, 1)),
                  pl.BlockSpec((N_CHIPS, None, g_pool.shape[2], g_pool.shape[3]), lambda i: (0, l, 0, 0)),
                  pl.BlockSpec((None, 1, POOL_W), lambda i: (l, 0, 0)), pl.BlockSpec(memory_space=pl.ANY)],
        out_specs=[row, row, pl.BlockSpec((1, POOL_W), lambda i: (0, 0)),
                   pl.BlockSpec((N_CHIPS, None, gw_acc.shape[2], gw_acc.shape[3]), lambda i: (0, l, 0, 0))],
        out_shape=[jax.ShapeDtypeStruct((S, POOL_W), F32), jax.ShapeDtypeStruct((S, POOL_W), F32),
                   jax.ShapeDtypeStruct((1, POOL_W), F32), jax.ShapeDtypeStruct(gw_acc.shape, F32)],
        input_output_aliases={6: 3},
        compiler_params=_cparams(("arbitrary",)),
    )(u, u, gp, dc, g_pool, ps, gw_acc)


def _pool_window_t(dd, halo_next, tile_index, tm):
    gw = POOL_W // len(POOL_WINDOWS)
    n = tm + POOL_HALO
    t = tile_index * tm + lax.broadcasted_iota(I32, (n, gw), 0)
    xh = jnp.concatenate([dd, halo_next], axis=0)
    outs = []
    for g, w in enumerate(POOL_WINDOWS):
        cols = slice(g * gw, (g + 1) * gw)
        cnt = jnp.minimum(t + 1, w).astype(F32)
        s = xh[:, cols] / cnt
        step = 1
        while step < w:
            s = s + pltpu.roll(s, n - step, axis=0)
            step *= 2
        outs.append(s[:tm] - dd[:, cols])
    return outs


def _assemble_dh(dq, dk, dv, dga, dd, dgp):
    S = dq.shape[0]
    tm = min(S, 256)
    hb = tm // POOL_HALO
    nt = S // tm
    gw = POOL_W // len(POOL_WINDOWS)

    def body(dq_ref, dk_ref, dv_ref, dga_ref, dd_ref, ddn_ref, dgp_ref, dh_ref, gb_ref):
        i = pl.program_id(0)

        @pl.when(i == 0)
        def _():
            gb_ref[...] = jnp.zeros_like(gb_ref)
        halo = jnp.where(i < nt - 1, ddn_ref[...], 0.0)
        du = jnp.concatenate(_pool_window_t(dd_ref[...], halo, i, tm), axis=1)
        dkv = dk_ref[...]
        dgav = dga_ref[...]
        parts = [(OFF_Q, dq_ref[...]), (OFF_KA, dkv), (OFF_KB, dkv), (OFF_V, dv_ref[...]), (OFF_GA, dgav),
                 (OFF_U, du[:, 0:896]), (OFF_UA, du[:, 896:1024]), (OFF_UB, du[:, 896:1024]), (OFF_GP, dgp_ref[...])]
        for off, val in parts:
            w = val.shape[1]
            dh_ref[:, off:off + w] = val.astype(BF16)
            gb_ref[:, off:off + w] += jnp.sum(val, axis=0, keepdims=True)

    def row(w):
        return pl.BlockSpec((tm, w), lambda i: (i, 0))

    return _pcall(
        body, name="assemble_dh", grid=(nt,),
        in_specs=[row(ATTN_W), row(KV_W), row(KV_W), row(ATTN_W), row(POOL_W),
                  pl.BlockSpec((POOL_HALO, POOL_W), lambda i: (jnp.minimum((i + 1) * hb, S // POOL_HALO - 1), 0)), row(POOL_W)],
        out_specs=[row(EXT), pl.BlockSpec((1, EXT), lambda i: (0, 0))],
        out_shape=[jax.ShapeDtypeStruct((S, EXT), BF16), jax.ShapeDtypeStruct((1, EXT), F32)],
        compiler_params=_cparams(("arbitrary",)),
    )(dq, dk, dv, dga, dd, dd, dgp)


def _attn_bwd(q, k, v, ga, o, dc, lse, bias, sinks, dbias_in, l):
    S = q.shape[0]
    nblk = S // WINDOW
    scale = 1.0 / math.sqrt(HEAD_DIM)

    def body(sink_ref, q_ref, k_ref, v_ref, ga_ref, o_ref, dc_ref, lse_ref, bias_ref, dbin_ref,
             dq_ref, dga_ref, dk_ref, dv_ref, db_ref, ds_ref, kd, vd):
        n = pl.program_id(0)

        @pl.when(n == 0)
        def _():
            _dup_heads(k_ref, kd)
            _dup_heads(v_ref, vd)
            dk_ref[...] = jnp.zeros_like(dk_ref)
            dv_ref[...] = jnp.zeros_like(dv_ref)
            db_ref[...] = dbin_ref[...]
            ds_ref[...] = jnp.zeros_like(ds_ref)

        cur = pl.multiple_of(n * WINDOW, WINDOW)
        prev = pl.multiple_of(jnp.maximum(n - 1, 0) * WINDOW, WINDOW)
        valid = _band_mask(n)
        lane = lax.broadcasted_iota(I32, (WINDOW, LANES), 1)
        lane8 = lax.broadcasted_iota(I32, (8, LANES), 1)
        lo = lane < HEAD_DIM
        lo2 = lax.broadcasted_iota(I32, (2 * WINDOW, LANES), 1) < HEAD_DIM
        kb = [_kv_block(kd, h, prev, cur) for h in range(N_KV)]
        vb = [_kv_block(vd, h, prev, cur) for h in range(N_KV)]
        lse_t = lse_ref[...]
        dkacc = [jnp.zeros((2 * WINDOW, LANES), F32) for _ in range(N_KV)]
        dvacc = [jnp.zeros((2 * WINDOW, LANES), F32) for _ in range(N_KV)]
        dsk = jnp.zeros((8, LANES), F32)
        for pair in range(N_HEADS // 2):
            sl = slice(LANES * pair, LANES * (pair + 1))
            qp = q_ref[:, sl]
            op = o_ref[:, sl]
            dcp = dc_ref[:, sl]
            gav = ga_ref[:, sl]
            sg = _sigmoid(gav)
            d_o = dcp * (gav * sg)
            dga_ref[:, sl] = (dcp * op) * (sg * (1.0 + gav * (1.0 - sg)))
            prod = d_o * op
            dqs = []
            for par in range(2):
                h = 2 * pair + par
                kvh = h // KV_RATIO
                msk = lo if par == 0 else jnp.logical_not(lo)
                qm = jnp.where(msk, qp, jnp.zeros_like(qp))
                dom = jnp.where(msk, d_o, 0.0).astype(BF16)
                delta = jnp.sum(jnp.where(msk, prod, 0.0), axis=1, keepdims=True)
                lse_h = jnp.sum(jnp.where(lane == h, lse_t, 0.0), axis=1, keepdims=True)
                s = _nt(qm, kb[kvh]) * scale + bias_ref[h]
                s = jnp.where(valid, s, -1e30)
                p = jnp.exp(s - lse_h)
                dp = _nt(dom, vb[kvh])
                dsc = p * (dp - delta)
                db_ref[h] += dsc
                psink = jnp.exp(sink_ref[l, h] - lse_h)
                dsk = dsk + jnp.where(lane8 == h, -jnp.sum(psink * delta, axis=0, keepdims=True), 0.0)
                dsb = dsc.astype(BF16)
                dqs.append(_nn(dsb, kb[kvh]) * scale)
                dkacc[kvh] = dkacc[kvh] + _tn(dsb, qm)
                dvacc[kvh] = dvacc[kvh] + _tn(p.astype(BF16), dom)
            dq_ref[:, sl] = jnp.where(lo, dqs[0], dqs[1])
        tk = [a + pltpu.roll(a, HEAD_DIM, axis=1) for a in dkacc]
        tv = [a + pltpu.roll(a, HEAD_DIM, axis=1) for a in dvacc]
        dkb = jnp.where(lo2, tk[0], tk[1]) * scale
        dvb = jnp.where(lo2, tv[0], tv[1])
        dk_ref[pl.ds(prev, WINDOW), :] += dkb[:WINDOW]
        dk_ref[pl.ds(cur, WINDOW), :] += dkb[WINDOW:]
        dv_ref[pl.ds(prev, WINDOW), :] += dvb[:WINDOW]
        dv_ref[pl.ds(cur, WINDOW), :] += dvb[WINDOW:]
        ds_ref[...] += dsk

    blk = pl.BlockSpec((WINDOW, ATTN_W), lambda n: (n, 0))
    full_kv = pl.BlockSpec((S, KV_W), lambda n: (0, 0))
    full_b = pl.BlockSpec((N_HEADS, WINDOW, 2 * WINDOW), lambda n: (0, 0, 0))
    return _pcall(
        body, name="attn_bwd", grid=(nblk,),
        in_specs=[pl.BlockSpec(memory_space=pltpu.SMEM), blk, full_kv, full_kv, blk, blk, blk,
                  pl.BlockSpec((WINDOW, LANES), lambda n: (n, 0)), full_b, full_b],
        out_specs=[blk, blk, full_kv, full_kv, full_b, pl.BlockSpec((8, LANES), lambda n: (0, 0))],
        out_shape=[jax.ShapeDtypeStruct((S, ATTN_W), F32), jax.ShapeDtypeStruct((S, ATTN_W), F32), jax.ShapeDtypeStruct((S, KV_W), F32),
                   jax.ShapeDtypeStruct((S, KV_W), F32), jax.ShapeDtypeStruct((N_HEADS, WINDOW, 2 * WINDOW), F32),
                   jax.ShapeDtypeStruct((8, LANES), F32)],
        scratch_shapes=[pltpu.VMEM((N_KV, S, LANES), BF16), pltpu.VMEM((N_KV, S, LANES), BF16)],
        compiler_params=_cparams(("arbitrary",)),
    )(sinks, q, k, v, ga, o, dc, lse, bias, dbias_in)


def _pack_small(arrs):
    flat = []
    for a in arrs:
        v = a.reshape(-1)
        flat.append(jnp.pad(v, (0, (-v.shape[0]) % LANES)))
    v = jnp.concatenate(flat)
    v = jnp.pad(v, (0, (-v.shape[0]) % (8 * LANES)))
    return v.reshape(-1, LANES)


def _unpack_small(packed, shapes):
    v = packed.reshape(-1)
    outs, off = [], 0
    for shp in shapes:
        n = math.prod(shp)
        outs.append(v[off:off + n].reshape(shp))
        off += n + (-n) % LANES
    return outs


def _bias_to_ext(b):
    L = b.shape[0]
    z = jnp.zeros((L, HALF_TILE), b.dtype)
    parts = []
    for j in range(N_CHIPS):
        seg = b[:, j * SHARD:(j + 1) * SHARD]
        parts += [z, seg] if j % 2 else [seg, z]
    return jnp.concatenate(parts, axis=1).reshape(L, 1, EXT)


def _bias_from_ext(g):
    parts = []
    for j in range(N_CHIPS):
        o = j * SHARD_P + (HALF_TILE if j % 2 else 0)
        parts.append(g[:, o:o + SHARD])
    return jnp.concatenate(parts, axis=1)


def kernel(x, p, w_in, b_in, w_out, attn_sinks, rel_bias, w_pool, pool_scale, w_ple, w_gate_ple, ln_gain, ln_bias, loss_target, m_w_in, m_b_in, m_w_out, m_attn_sinks, m_rel_bias, m_w_pool, m_pool_scale, m_w_ple, m_w_gate_ple, m_ln_gain, m_ln_bias, v_w_in, v_b_in, v_w_out, v_attn_sinks, v_rel_bias, v_w_pool, v_pool_scale, v_w_ple, v_w_gate_ple, v_ln_gain, v_ln_bias):
    L = w_in.shape[0]
    S, D = x.shape[1], x.shape[2]
    assert D == D_MODEL and w_in.shape[2] == SHARD and S % WINDOW == 0
    alpha = (2.0 * L) ** 0.25
    ple_dim = p.shape[3]
    xc, yc, cc = _mesh_pos()
    cidx = jnp.reshape(cc, (1,)).astype(I32)
    meidx = jnp.reshape(2 * xc + yc, (1,)).astype(I32)

    def halves(a):
        return a.reshape(a.shape[0], 2, a.shape[1] // 2, a.shape[2])

    def unhalves(a):
        return a.reshape(a.shape[:-3] + (2 * a.shape[-2], a.shape[-1]))

    w_pool2 = w_pool.reshape(L, w_pool.shape[1] * w_pool.shape[2], w_pool.shape[3])
    pieces = [_piece_in(w_in), _cast_piece(w_out, "piece_out"), _cast_piece(w_gate_ple, "piece_gate"),
              _cast_piece(w_ple, "piece_ple"), _cast_piece(w_pool2, "piece_pool")]
    g_in, g_out, g_gate, g_ple, g_pool = [unhalves(a) for a in _allgather([halves(a) for a in pieces])]

    b_ext = _bias_to_ext(b_in)
    ps3 = pool_scale.reshape(L, 1, POOL_W)
    gain3 = ln_gain.reshape(L, 1, D)
    bias3 = ln_bias.reshape(L, 1, D)
    onehot = _bucket_onehot()
    bias_hqk = _bias_fwd(rel_bias.T, onehot).reshape(N_HEADS, WINDOW, 2 * WINDOW)

    xs = x[0]
    xb = _to_bf16(xs)
    saved = []
    for l in range(L):
        q, k, v, ga, u, gp = _inproj(xb, g_in, b_ext, l)
        gl = _matmul_nn(xb, g_gate, l, "gate_logits")
        o, ca, lse = _attn_fwd(q, k, v, ga, bias_hqk, attn_sinks, l)
        c = _pool_fwd(u, gp, ca, g_pool, ps3, l)
        y, yb, xhat, rstd = _outproj_ln(c, xs, gl, p[l, 0], g_out, g_ple, gain3, bias3, l, alpha)
        saved.append(dict(xb=xb, q=q, k=k, v=v, ga=ga, u=u, gp=gp, gl=gl, o=o, lse=lse, c=c, xhat=xhat, rstd=rstd))
        xs, xb = y, yb

    dy, loss_acc = _loss_and_grad(xs, loss_target[0])
    loss = lax.psum(0.5 * loss_acc[0, 0], ("x", "y", "c"))

    gi_acc = lax.empty((N_CHIPS, L, D, SHARD_P), F32)
    go_acc = lax.empty((N_CHIPS, L, D // N_CHIPS, D), F32)
    gg_acc = lax.empty((N_CHIPS, L, D // N_CHIPS, D), F32)
    gp_acc = lax.empty((N_CHIPS, L, ple_dim, D // N_CHIPS), F32)
    gw_acc = lax.empty((N_CHIPS, L, g_pool.shape[2], g_pool.shape[3]), F32)
    dbias = jnp.zeros((N_HEADS, WINDOW, 2 * WINDOW), F32)
    small = [None] * L
    for l in reversed(range(L)):
        sv = saved[l]
        pl_l = p[l, 0]
        dz, dzb, d_e, d_gl, ggain, gbias = _ln_bwd(dy, sv["xhat"], sv["rstd"], sv["gl"], pl_l, g_ple, gain3, l)
        dc = _matmul_nt_rows(dzb, g_out, l, "d_mix_in")
        go_acc = _matmul_tn(sv["c"], dzb, go_acc, l, "grad_w_out", by_rows=True)
        gp_acc = _matmul_tn(_to_bf16(pl_l), d_e, gp_acc, l, "grad_w_ple", by_rows=False)
        gg_acc = _matmul_tn(sv["xb"], d_gl, gg_acc, l, "grad_w_gate", by_rows=True)
        dx1 = _matmul_nt_rows(d_gl, g_gate, l, "d_x_gate", dz, alpha)
        dd, dgp, gps, gw_acc = _pool_bwd(sv["u"], sv["gp"], dc, g_pool, ps3, gw_acc, l)
        dq, dga, dk, dv, dbias, dsink = _attn_bwd(sv["q"], sv["k"], sv["v"], sv["ga"], sv["o"], dc, sv["lse"], bias_hqk,
                                                  attn_sinks, dbias, l)
        dh, gbe = _assemble_dh(dq, dk, dv, dga, dd, dgp)
        gi_acc = _matmul_tn(sv["xb"], dh, gi_acc, l, "grad_w_in", by_rows=False)
        dy = _matmul_nt(dh, g_in, l, "d_x", dx1, 1.0)
        small[l] = dict(b_in=_bias_from_ext(gbe)[0], sinks=dsink[0, :N_HEADS], ps=gps[0], gain=ggain[0], bias=gbias[0])
    grad_x = dy[None]

    grads = [gi_acc, go_acc, gg_acc, gp_acc, gw_acc]
    g5 = [a.reshape(a.shape[0], a.shape[1], 2, a.shape[2] // 2, a.shape[3]) for a in grads]
    names = ["w_in", "w_out", "w_gate", "w_ple", "w_pool"]
    recv_a = _rs_pair_exchange(g5)
    parts = [_rs_pair_add(a, r, cidx, "rs_pair_add_" + nm) for a, r, nm in zip(g5, recv_a, names)]
    recv_b = _rs_chip_exchange([pt[1] for pt in parts])
    res = [_rs_chip_add(pt[0], r, meidx, "rs_chip_add_" + nm) for pt, r, nm in zip(parts, recv_b, names)]
    full = [unhalves(a) for a in _rs_pair_share(res)]

    def pool4(a):
        return a.reshape(w_pool.shape)

    gw_in, dw_in, nm_in, nv_in = _adamw(w_in, full[0], m_w_in, v_w_in, "adamw_w_in", unshift=True)
    gw_out, dw_out, nm_out, nv_out = _adamw(w_out, full[1], m_w_out, v_w_out, "adamw_w_out")
    gw_gate, dw_gate, nm_gate, nv_gate = _adamw(w_gate_ple, full[2], m_w_gate_ple, v_w_gate_ple, "adamw_w_gate")
    gw_ple, dw_ple, nm_ple, nv_ple = _adamw(w_ple, full[3], m_w_ple, v_w_ple, "adamw_w_ple")
    pool_out = _adamw(w_pool2, full[4], m_w_pool.reshape(w_pool2.shape), v_w_pool.reshape(w_pool2.shape), "adamw_w_pool")
    gw_pool, dw_pool, nm_pool, nv_pool = [pool4(a) for a in pool_out]

    g_rel = _bias_bwd(dbias.reshape(N_HEADS, -1), onehot).T
    small_shapes = [b_in.shape, attn_sinks.shape, rel_bias.shape, pool_scale.shape, ln_gain.shape, ln_bias.shape]
    g_small = [jnp.stack([small[l]["b_in"] for l in range(L)]), jnp.stack([small[l]["sinks"] for l in range(L)]), g_rel,
               jnp.stack([small[l]["ps"] for l in range(L)]), jnp.stack([small[l]["gain"] for l in range(L)]),
               jnp.stack([small[l]["bias"] for l in range(L)])]
    packed = _small_allreduce_adamw(
        _pack_small(g_small),
        _pack_small([b_in, attn_sinks, rel_bias, pool_scale, ln_gain, ln_bias]),
        _pack_small([m_b_in, m_attn_sinks, m_rel_bias, m_pool_scale, m_ln_gain, m_ln_bias]),
        _pack_small([v_b_in, v_attn_sinks, v_rel_bias, v_pool_scale, v_ln_gain, v_ln_bias]))
    sg, sd, sm, sv_ = [_unpack_small(a, small_shapes) for a in packed]

    def order(big, sm_):
        return (big[0], sm_[0], big[1], sm_[1], sm_[2], big[2], sm_[3], big[3], big[4], sm_[4], sm_[5])

    return (loss, grad_x,
            *order((gw_in, gw_out, gw_pool, gw_ple, gw_gate), sg),
            *order((dw_in, dw_out, dw_pool, dw_ple, dw_gate), sd),
            *order((nm_in, nm_out, nm_pool, nm_ple, nm_gate), sm),
            *order((nv_in, nv_out, nv_pool, nv_ple, nv_gate), sv_))
```

```python
import functools
import math

import jax
import jax.numpy as jnp
from jax import lax
from jax.experimental import pallas as pl
from jax.experimental.pallas import tpu as pltpu

F32 = jnp.float32
BF16 = jnp.bfloat16
I32 = jnp.int32
MESH = pl.DeviceIdType.MESH

HEAD_DIM = 64
WINDOW = 128
KV_RATIO = 8
POOL_WINDOWS = (2, 4, 8, 16)
POOL_HALO = 16
REL_BUCKETS = 32
REL_MAX_DIST = 128
LN_EPS = 1e-5
ADAM_LR, ADAM_B1, ADAM_B2, ADAM_EPS, ADAM_WD, ADAM_STEP = 0.001, 0.9, 0.999, 1e-08, 0.01, 10

LANES = 128
VMEM_LIMIT = 52 * 1024 * 1024
N_CHIPS = 4
N_DEV = 8

D_MODEL = 2048
ATTN_W = 1024
POOL_W = 1024
KV_W = 128
N_HEADS = ATTN_W // HEAD_DIM
N_KV = N_HEADS // KV_RATIO
IN_COLS = 4352
SHARD = IN_COLS // N_CHIPS
SHARD_P = 1152
EXT = N_CHIPS * SHARD_P
HALF_TILE = SHARD_P - SHARD
OFF_Q, OFF_KA, OFF_KB, OFF_V, OFF_GA, OFF_U, OFF_UA, OFF_UB, OFF_GP = 0, 1024, 1152, 1280, 1408, 2432, 3328, 3456, 3584
WEIGHTS = ("in", "out", "gate", "ple", "pool")


def _cparams(sem=None):
    return pltpu.CompilerParams(dimension_semantics=sem, vmem_limit_bytes=VMEM_LIMIT)


def _pcall(body, **kw):
    return pl.pallas_call(body, **kw)


def _sigmoid(x):
    return 1.0 / (1.0 + jnp.exp(-x))


def _nt(a, b):
    return lax.dot_general(a, b, (((1,), (1,)), ((), ())), preferred_element_type=F32)


def _tn(a, b):
    return lax.dot_general(a, b, (((0,), (0,)), ((), ())), preferred_element_type=F32)


def _nn(a, b):
    return jnp.dot(a, b, preferred_element_type=F32)


def _mesh_pos():
    return lax.axis_index("x"), lax.axis_index("y"), lax.axis_index("c")


def _peer_chips(x, y):
    return [(1 - x, y), (x, 1 - y), (1 - x, 1 - y)]


def _row_tile(rows, cap=256):
    t = min(rows, cap)
    while rows % t or t % 16:
        t -= 1
    return t


def _hbm_spec():
    return pl.BlockSpec(memory_space=pltpu.HBM)


def _halves(a):
    return a.reshape(a.shape[:-2] + (2, a.shape[-2] // 2, a.shape[-1]))


def _unhalves(a):
    return a.reshape(a.shape[:-3] + (2 * a.shape[-2], a.shape[-1]))


def _remote(src, dst, send, recv, i, device):
    return pltpu.make_async_remote_copy(src_ref=src, dst_ref=dst, send_sem=send.at[i], recv_sem=recv.at[i],
                                        device_id=device, device_id_type=MESH)


class _Job:
    def __init__(self):
        self.keys, self.parts, self.n = [], [], 0

    def add(self, fn, keys, n):
        self.parts.append((fn, len(self.keys), len(keys), self.n))
        self.keys += list(keys)
        self.n += n
        return self

    def build(self, refs, send, recv):
        out = []
        for fn, i0, nb, base in self.parts:
            out += fn(refs[i0:i0 + nb], send, recv, base)
        return out


def _ag_ici(refs, send, recv, base):
    x, y, c = _mesh_pos()
    me = 2 * x + y
    out = []
    for t, g in enumerate(refs):
        for k, chip in enumerate(_peer_chips(x, y)):
            i = base + 3 * t + k
            dev = (*chip, c)
            out.append((_remote(g.at[me, c], g.at[me, c], send, recv, i, dev),
                        _remote(g.at[me, c], g.at[2 * chip[0] + chip[1], c], send, recv, i, dev)))
    return out


def _ag_fwd(refs, send, recv, base):
    x, y, c = _mesh_pos()
    out = []
    for t, g in enumerate(refs):
        for k, chip in enumerate(_peer_chips(x, y)):
            i = base + 3 * t + k
            slot = 2 * chip[0] + chip[1]
            dev = (x, y, 1 - c)
            out.append((_remote(g.at[slot, c], g.at[slot, c], send, recv, i, dev),
                        _remote(g.at[slot, c], g.at[slot, 1 - c], send, recv, i, dev)))
    return out


def _rs_pair(refs, send, recv, base):
    x, y, c = _mesh_pos()
    n = len(refs) // 2
    out = []
    for t in range(n):
        cp = _remote(refs[t].at[:, 1 - c], refs[n + t], send, recv, base + t, (x, y, 1 - c))
        out.append((cp, cp))
    return out


def _rs_ici(refs, send, recv, base):
    x, y, c = _mesh_pos()
    n = len(refs) // 2
    out = []
    for t in range(n):
        for k, chip in enumerate(_peer_chips(x, y)):
            cp = _remote(refs[t].at[2 * chip[0] + chip[1]], refs[n + t].at[k], send, recv, base + 3 * t + k, (*chip, c))
            out.append((cp, cp))
    return out


def _rs_share(refs, send, recv, base):
    x, y, c = _mesh_pos()
    out = []
    for t, f in enumerate(refs):
        dev = (x, y, 1 - c)
        out.append((_remote(f.at[:, c], f.at[:, c], send, recv, base + t, dev),
                    _remote(f.at[:, c], f.at[:, 1 - c], send, recv, base + t, dev)))
    return out


def _call(body, *, name, grid, in_specs, out_specs, out_shape, args, scratch_shapes=(), sem=None, job=None, store=None):
    in_specs, out_specs, out_shape, scratch_shapes = list(in_specs), list(out_specs), list(out_shape), list(scratch_shapes)
    if job is None or job.n == 0:
        return list(_pcall(body, name=name, grid=grid, in_specs=in_specs, out_specs=out_specs, out_shape=out_shape,
                           scratch_shapes=scratch_shapes, compiler_params=_cparams(sem))(*args))
    bufs = [store[k] for k in job.keys]
    nb, n_in, n_out, n_sc = len(bufs), len(args), len(out_shape), len(scratch_shapes)

    def wrapped(*refs):
        ins = refs[:n_in]
        outs = refs[n_in + nb:n_in + nb + n_out]
        cb = refs[n_in + nb + n_out:n_in + 2 * nb + n_out]
        scratch = refs[n_in + 2 * nb + n_out:n_in + 2 * nb + n_out + n_sc]
        send, recv = refs[-2:]
        ids = [pl.program_id(a) for a in range(len(grid))]
        first = functools.reduce(jnp.logical_and, [i == 0 for i in ids])
        last = functools.reduce(jnp.logical_and, [i == g - 1 for i, g in zip(ids, grid)])

        @pl.when(first)
        def _():
            for s, _r in job.build(cb, send, recv):
                s.start()

        body(*ins, *outs, *scratch)

        @pl.when(last)
        def _():
            pairs = job.build(cb, send, recv)
            for _s, r in pairs:
                r.wait_recv()
            for s, _r in pairs:
                s.wait_send()

    res = _pcall(
        wrapped, name=name, grid=grid, in_specs=in_specs + [_hbm_spec()] * nb, out_specs=out_specs + [_hbm_spec()] * nb,
        out_shape=out_shape + [jax.ShapeDtypeStruct(b.shape, b.dtype) for b in bufs],
        scratch_shapes=scratch_shapes + [pltpu.SemaphoreType.DMA((job.n,)), pltpu.SemaphoreType.DMA((job.n,))],
        input_output_aliases={n_in + i: n_out + i for i in range(nb)},
        compiler_params=_cparams(("arbitrary",) * len(grid)))(*args, *bufs)
    for k, v in zip(job.keys, res[n_out:]):
        store[k] = v
    return list(res[:n_out])


def _run_job(name, job, store):
    bufs = [store[k] for k in job.keys]
    nb = len(bufs)

    def body(*refs):
        send, recv = refs[-2:]
        pairs = job.build(refs[nb:2 * nb], send, recv)
        for s, _r in pairs:
            s.start()
        for _s, r in pairs:
            r.wait_recv()
        for s, _r in pairs:
            s.wait_send()

    res = _pcall(body, name=name, in_specs=[_hbm_spec()] * nb, out_specs=[_hbm_spec()] * nb,
                 out_shape=[jax.ShapeDtypeStruct(b.shape, b.dtype) for b in bufs],
                 scratch_shapes=[pltpu.SemaphoreType.DMA((job.n,)), pltpu.SemaphoreType.DMA((job.n,))],
                 input_output_aliases={i: i for i in range(nb)})(*bufs)
    for k, v in zip(job.keys, res):
        store[k] = v


def _allgather_now(keys, store):
    bufs = [store[k] for k in keys]
    nb = len(bufs)

    def body(*refs):
        send, recv = refs[-2:]
        g = refs[nb:2 * nb]
        ici = _ag_ici(g, send, recv, 0)
        fwd = _ag_fwd(g, send, recv, 3 * nb)
        for s, _r in ici:
            s.start()
        for (_s, r), (fs, _fr) in zip(ici, fwd):
            r.wait_recv()
            fs.start()
        for _fs, fr in fwd:
            fr.wait_recv()
        for s, _r in ici + fwd:
            s.wait_send()

    res = _pcall(body, name="allgather_first_layer", in_specs=[_hbm_spec()] * nb, out_specs=[_hbm_spec()] * nb,
                 out_shape=[jax.ShapeDtypeStruct(b.shape, b.dtype) for b in bufs],
                 scratch_shapes=[pltpu.SemaphoreType.DMA((6 * nb,)), pltpu.SemaphoreType.DMA((6 * nb,))],
                 input_output_aliases={i: i for i in range(nb)})(*bufs)
    for k, v in zip(keys, res):
        store[k] = v


def _piece(w, l, idx, name):
    _, R, C = w.shape
    tr = _row_tile(R)

    def body(s_ref, w_ref, o_ref):
        del s_ref
        o_ref[...] = w_ref[...].astype(BF16)

    gs = pltpu.PrefetchScalarGridSpec(
        num_scalar_prefetch=1, grid=(R // tr,),
        in_specs=[pl.BlockSpec((None, tr, C), lambda r, s: (l, r, 0))],
        out_specs=pl.BlockSpec((None, tr, C), lambda r, s: (s[0], r, 0)))
    return _pcall(body, name=name, grid_spec=gs, out_shape=jax.ShapeDtypeStruct((N_CHIPS, R, C), BF16),
                  compiler_params=_cparams(("parallel",)))(idx, w)


def _piece_in(w_in_t, l, idx):
    _, R, D = w_in_t.shape
    tr = HALF_TILE
    nsrc = R // tr

    def body(s_ref, w_ref, o_ref):
        src = pl.program_id(0) - s_ref[1]
        ok = jnp.logical_and(src >= 0, src < nsrc)
        o_ref[...] = jnp.where(ok, w_ref[...], 0.0).astype(BF16)

    gs = pltpu.PrefetchScalarGridSpec(
        num_scalar_prefetch=1, grid=(SHARD_P // tr,),
        in_specs=[pl.BlockSpec((None, tr, D), lambda r, s: (l, jnp.clip(r - s[1], 0, nsrc - 1), 0))],
        out_specs=pl.BlockSpec((None, tr, D), lambda r, s: (s[0], r, 0)))
    return _pcall(body, name="piece_in", grid_spec=gs, out_shape=jax.ShapeDtypeStruct((N_CHIPS, SHARD_P, D), BF16),
                  compiler_params=_cparams(("parallel",)))(idx, w_in_t)


def _rs_pair_add(g5, r4, idx, name):
    J, _, h, C = g5.shape
    th = _row_tile(h)

    def body(s_ref, g_ref, r_ref, o32_ref, o16_ref):
        del s_ref
        s = g_ref[...] + r_ref[...]
        o32_ref[...] = s
        o16_ref[...] = s.astype(BF16)

    spec = pl.BlockSpec((None, th, C), lambda j, r, s: (j, r, 0))
    gs = pltpu.PrefetchScalarGridSpec(
        num_scalar_prefetch=1, grid=(J, h // th),
        in_specs=[pl.BlockSpec((None, None, th, C), lambda j, r, s: (j, s[2], r, 0)), spec],
        out_specs=[spec, spec])
    return _pcall(body, name=name, grid_spec=gs,
                  out_shape=[jax.ShapeDtypeStruct((J, h, C), F32), jax.ShapeDtypeStruct((J, h, C), BF16)],
                  compiler_params=_cparams(("parallel", "parallel")))(idx, g5, r4)


def _rs_chip_add(p32, r3, full, l, idx, name):
    _, h, C = p32.shape
    th = _row_tile(h)

    def body(s_ref, p_ref, r_ref, f_ref, o_ref):
        del s_ref, f_ref
        o_ref[...] = ((p_ref[...] + r_ref[0].astype(F32)) + r_ref[1].astype(F32)) + r_ref[2].astype(F32)

    gs = pltpu.PrefetchScalarGridSpec(
        num_scalar_prefetch=1, grid=(h // th,),
        in_specs=[pl.BlockSpec((None, th, C), lambda r, s: (s[0], r, 0)),
                  pl.BlockSpec((3, th, C), lambda r, s: (0, r, 0)),
                  pl.BlockSpec(memory_space=pl.ANY)],
        out_specs=pl.BlockSpec((None, None, th, C), lambda r, s: (l, s[2], r, 0)))
    return _pcall(body, name=name, grid_spec=gs, out_shape=jax.ShapeDtypeStruct(full.shape, F32),
                  input_output_aliases={3: 0}, compiler_params=_cparams(("parallel",)))(idx, p32, r3, full)


def _adamw_math(w, g, m, v):
    nm = ADAM_B1 * m + (1.0 - ADAM_B1) * g
    nv = ADAM_B2 * v + (1.0 - ADAM_B2) * (g * g)
    m_hat = nm / (1.0 - ADAM_B1 ** ADAM_STEP)
    v_hat = nv / (1.0 - ADAM_B2 ** ADAM_STEP)
    delta = -ADAM_LR * (m_hat / (jnp.sqrt(v_hat) + ADAM_EPS) + ADAM_WD * w)
    return delta, nm, nv


def _adamw(w, g, m, v, idx, name, tr=None):
    L, R, C = w.shape
    Rg = g.shape[1]
    tr = tr or _row_tile(R)
    shift = (Rg - R) // tr
    assert (Rg - R) % tr == 0

    def body(s_ref, w_ref, g_ref, m_ref, v_ref, go_ref, d_ref, nm_ref, nv_ref):
        del s_ref
        gv = g_ref[...]
        d, nm, nv = _adamw_math(w_ref[...], gv, m_ref[...], v_ref[...])
        go_ref[...] = gv
        d_ref[...] = d
        nm_ref[...] = nm
        nv_ref[...] = nv

    wspec = pl.BlockSpec((None, tr, C), lambda l, r, s: (l, r, 0))
    gs = pltpu.PrefetchScalarGridSpec(
        num_scalar_prefetch=1, grid=(L, R // tr),
        in_specs=[wspec, pl.BlockSpec((None, tr, C), lambda l, r, s: (l, r + shift * s[1], 0)), wspec, wspec],
        out_specs=[wspec, wspec, wspec, wspec])
    sds = jax.ShapeDtypeStruct((L, R, C), F32)
    return _pcall(body, name=name, grid_spec=gs, out_shape=[sds, sds, sds, sds],
                  compiler_params=_cparams(("parallel", "parallel")))(idx, w, g, m, v)


def _small_allreduce_adamw(gv, wv, mv, vv):
    NR = gv.shape[0]

    def body(g_ref, w_ref, m_ref, v_ref, go_ref, d_ref, nm_ref, nv_ref, gath, send, recv):
        x, y, c = _mesh_pos()
        rank = 4 * x + 2 * y + c
        gath[rank] = g_ref[...]
        cps = []
        for msk in range(1, N_DEV):
            bx, by, bc = (msk >> 2) & 1, (msk >> 1) & 1, msk & 1
            peer = (1 - x if bx else x, 1 - y if by else y, 1 - c if bc else c)
            cps.append(pltpu.make_async_remote_copy(src_ref=g_ref, dst_ref=gath.at[rank], send_sem=send.at[msk - 1],
                                                    recv_sem=recv.at[msk - 1], device_id=peer, device_id_type=MESH))
        for cp in cps:
            cp.start()
        for msk in range(1, N_DEV):
            bx, by, bc = (msk >> 2) & 1, (msk >> 1) & 1, msk & 1
            peer = (1 - x if bx else x, 1 - y if by else y, 1 - c if bc else c)
            prank = 4 * peer[0] + 2 * peer[1] + peer[2]
            pltpu.make_async_remote_copy(src_ref=g_ref, dst_ref=gath.at[prank], send_sem=send.at[msk - 1],
                                         recv_sem=recv.at[msk - 1], device_id=peer, device_id_type=MESH).wait_recv()
        for cp in cps:
            cp.wait_send()
        tot = gath[0]
        for r in range(1, N_DEV):
            tot = tot + gath[r]
        d, nm, nv = _adamw_math(w_ref[...], tot, m_ref[...], v_ref[...])
        go_ref[...] = tot
        d_ref[...] = d
        nm_ref[...] = nm
        nv_ref[...] = nv

    vm = pl.BlockSpec(memory_space=pltpu.VMEM)
    sds = jax.ShapeDtypeStruct((NR, LANES), F32)
    return _pcall(body, name="small_allreduce_adamw", in_specs=[vm, vm, vm, vm], out_specs=[vm, vm, vm, vm],
                  out_shape=[sds, sds, sds, sds],
                  scratch_shapes=[pltpu.VMEM((N_DEV, NR, LANES), F32), pltpu.SemaphoreType.DMA((N_DEV - 1,)),
                                  pltpu.SemaphoreType.DMA((N_DEV - 1,))],
                  compiler_params=pltpu.CompilerParams(vmem_limit_bytes=VMEM_LIMIT))(gv, wv, mv, vv)


def _split3(a):
    h1 = a.astype(BF16)
    r1 = a - h1.astype(F32)
    h2 = r1.astype(BF16)
    h3 = (r1 - h2.astype(F32)).astype(BF16)
    return h1, h2, h3


def _bucket_onehot():
    qq = jnp.arange(WINDOW)[:, None]
    kk = jnp.arange(2 * WINDOW)[None, :]
    dist = qq + WINDOW - kk
    max_exact = REL_BUCKETS // 2
    d = jnp.maximum(dist, 0)
    d_f = jnp.maximum(d, 1).astype(F32)
    large = max_exact + (jnp.log(d_f / max_exact) / math.log(REL_MAX_DIST / max_exact) * (REL_BUCKETS - max_exact)).astype(I32)
    large = jnp.minimum(large, REL_BUCKETS - 1)
    bucket = jnp.where(d < max_exact, d, large)
    inwin = (dist >= 0) & (dist < WINDOW)
    oh = (bucket[None] == jnp.arange(REL_BUCKETS)[:, None, None]) & inwin[None]
    return oh.reshape(REL_BUCKETS, WINDOW * 2 * WINDOW).astype(BF16)


def _bias_fwd(rel_bias_t, onehot):
    H, N = rel_bias_t.shape[0], onehot.shape[1]
    tn = 4096

    def body(t_ref, oh_ref, o_ref):
        h1, h2, h3 = _split3(t_ref[...])
        oh = oh_ref[...]
        o_ref[...] = (_nn(h1, oh) + _nn(h2, oh)) + _nn(h3, oh)

    return _pcall(body, name="bias_fwd", grid=(N // tn,),
                  in_specs=[pl.BlockSpec((H, REL_BUCKETS), lambda i: (0, 0)), pl.BlockSpec((REL_BUCKETS, tn), lambda i: (0, i))],
                  out_specs=pl.BlockSpec((H, tn), lambda i: (0, i)), out_shape=jax.ShapeDtypeStruct((H, N), F32),
                  compiler_params=_cparams(("parallel",)))(rel_bias_t, onehot)


def _bias_bwd(dbias, onehot):
    H, N = dbias.shape
    tn = 4096

    def body(d_ref, oh_ref, o_ref):
        @pl.when(pl.program_id(0) == 0)
        def _():
            o_ref[...] = jnp.zeros_like(o_ref)
        h1, h2, h3 = _split3(d_ref[...])
        oh = oh_ref[...]
        o_ref[...] += (_nt(h1, oh) + _nt(h2, oh)) + _nt(h3, oh)

    return _pcall(body, name="bias_bwd", grid=(N // tn,),
                  in_specs=[pl.BlockSpec((H, tn), lambda i: (0, i)), pl.BlockSpec((REL_BUCKETS, tn), lambda i: (0, i))],
                  out_specs=pl.BlockSpec((H, REL_BUCKETS), lambda i: (0, 0)), out_shape=jax.ShapeDtypeStruct((H, REL_BUCKETS), F32),
                  compiler_params=_cparams(("arbitrary",)))(dbias, onehot)


def _to_bf16(x):
    S, D = x.shape
    tm = min(S, 512)

    def body(x_ref, o_ref):
        o_ref[...] = x_ref[...].astype(BF16)

    return _pcall(body, name="to_bf16", grid=(S // tm,), in_specs=[pl.BlockSpec((tm, D), lambda i: (i, 0))],
                  out_specs=pl.BlockSpec((tm, D), lambda i: (i, 0)), out_shape=jax.ShapeDtypeStruct((S, D), BF16),
                  compiler_params=_cparams(("parallel",)))(x)


def _inproj(xb, w_t, b_ext, l, job, store):
    S, D = xb.shape
    tm = min(S, 512)

    def body(x_ref, w_ref, b_ref, q_ref, k_ref, v_ref, ga_ref, u_ref, gp_ref):
        j = pl.program_id(1)
        acc = _nt(x_ref[...], w_ref[...]) + b_ref[...]

        @pl.when(j == 0)
        def _():
            q_ref[...] = acc[:, :1024].astype(BF16)
            k_ref[...] = acc[:, 1024:1152]

        @pl.when(j == 1)
        def _():
            k_ref[...] += acc[:, 0:128]
            v_ref[...] = acc[:, 128:256]
            ga_ref[:, 0:896] = acc[:, 256:1152]

        @pl.when(j == 2)
        def _():
            ga_ref[:, 896:1024] = acc[:, 0:128]
            u_ref[:, 0:896] = acc[:, 128:1024]
            u_ref[:, 896:1024] = acc[:, 1024:1152]

        @pl.when(j == 3)
        def _():
            u_ref[:, 896:1024] += acc[:, 0:128]
            gp_ref[...] = acc[:, 128:1152]

    def ospec(w):
        return pl.BlockSpec((tm, w), lambda i, j: (i, 0))

    return _call(
        body, name="inproj", grid=(S // tm, N_CHIPS),
        in_specs=[pl.BlockSpec((tm, D), lambda i, j: (i, 0)),
                  pl.BlockSpec((None, SHARD_P, D), lambda i, j: (j, 0, 0)),
                  pl.BlockSpec((None, 1, SHARD_P), lambda i, j: (l, 0, j))],
        out_specs=[ospec(ATTN_W), ospec(KV_W), ospec(KV_W), ospec(ATTN_W), ospec(POOL_W), ospec(POOL_W)],
        out_shape=[jax.ShapeDtypeStruct((S, ATTN_W), BF16), jax.ShapeDtypeStruct((S, KV_W), F32), jax.ShapeDtypeStruct((S, KV_W), F32),
                   jax.ShapeDtypeStruct((S, ATTN_W), F32), jax.ShapeDtypeStruct((S, POOL_W), F32), jax.ShapeDtypeStruct((S, POOL_W), F32)],
        args=(xb, w_t, b_ext), sem=("parallel", "arbitrary"), job=job, store=store)


def _matmul_nn(a, b4, name, job, store):
    S, K = a.shape
    N = b4.shape[2]
    kq = b4.shape[1]
    tm, tn = min(S, 512), min(N, 1024)

    def body(a_ref, b_ref, o_ref):
        acc = _nn(a_ref[:, 0:kq], b_ref[0])
        for j in range(1, N_CHIPS):
            acc = acc + _nn(a_ref[:, j * kq:(j + 1) * kq], b_ref[j])
        o_ref[...] = acc

    return _call(body, name=name, grid=(N // tn, S // tm),
                 in_specs=[pl.BlockSpec((tm, K), lambda n, i: (i, 0)), pl.BlockSpec((N_CHIPS, kq, tn), lambda n, i: (0, 0, n))],
                 out_specs=[pl.BlockSpec((tm, tn), lambda n, i: (i, n))], out_shape=[jax.ShapeDtypeStruct((S, N), F32)],
                 args=(a, b4), sem=("parallel", "parallel"), job=job, store=store)[0]


def _band_mask(n):
    qq = lax.broadcasted_iota(I32, (WINDOW, 2 * WINDOW), 0)
    kk = lax.broadcasted_iota(I32, (WINDOW, 2 * WINDOW), 1)
    dist = qq + WINDOW - kk
    return (dist >= 0) & (dist < WINDOW) & ((kk >= WINDOW) | (n > 0))


def _dup_heads(src_ref, dst_ref):
    a = src_ref[...]
    r = pltpu.roll(a, HEAD_DIM, axis=1)
    lo = lax.broadcasted_iota(I32, a.shape, 1) < HEAD_DIM
    dst_ref[0] = jnp.where(lo, a, r).astype(BF16)
    dst_ref[1] = jnp.where(lo, r, a).astype(BF16)


def _kv_block(ref, h, prev, cur):
    return jnp.concatenate([ref[h, pl.ds(prev, WINDOW), :], ref[h, pl.ds(cur, WINDOW), :]], axis=0)


def _attn_fwd(q, k, v, ga, bias, sinks, l, job, store):
    S = q.shape[0]
    nblk = S // WINDOW
    scale = 1.0 / math.sqrt(HEAD_DIM)

    def body(sink_ref, q_ref, k_ref, v_ref, ga_ref, bias_ref, o_ref, ca_ref, lse_ref, kd, vd):
        n = pl.program_id(0)

        @pl.when(n == 0)
        def _():
            _dup_heads(k_ref, kd)
            _dup_heads(v_ref, vd)

        cur = pl.multiple_of(n * WINDOW, WINDOW)
        prev = pl.multiple_of(jnp.maximum(n - 1, 0) * WINDOW, WINDOW)
        valid = _band_mask(n)
        lane = lax.broadcasted_iota(I32, (WINDOW, LANES), 1)
        lo = lane < HEAD_DIM
        kb = [_kv_block(kd, h, prev, cur) for h in range(N_KV)]
        vb = [_kv_block(vd, h, prev, cur) for h in range(N_KV)]
        lse_mat = jnp.zeros((WINDOW, LANES), F32)
        for pair in range(N_HEADS // 2):
            sl = slice(LANES * pair, LANES * (pair + 1))
            qp = q_ref[:, sl]
            outs = []
            for par in range(2):
                h = 2 * pair + par
                kvh = h // KV_RATIO
                qm = jnp.where(lo if par == 0 else jnp.logical_not(lo), qp, jnp.zeros_like(qp))
                s = _nt(qm, kb[kvh]) * scale + bias_ref[h]
                s = jnp.where(valid, s, -1e30)
                sink = sink_ref[l, h]
                m = jnp.maximum(jnp.max(s, axis=1, keepdims=True), sink)
                e = jnp.exp(s - m)
                den = jnp.sum(e, axis=1, keepdims=True) + jnp.exp(sink - m)
                p = (e * (1.0 / den)).astype(BF16)
                outs.append(_nn(p, vb[kvh]))
                lse_mat = jnp.where(lane == h, m + jnp.log(den), lse_mat)
            o_pair = jnp.where(lo, outs[0], outs[1])
            o_ref[:, sl] = o_pair
            gav = ga_ref[:, sl]
            ca_ref[:, sl] = (o_pair * (gav * _sigmoid(gav))).astype(BF16)
        lse_ref[...] = lse_mat

    blk = pl.BlockSpec((WINDOW, ATTN_W), lambda n: (n, 0))
    full_kv = pl.BlockSpec((S, KV_W), lambda n: (0, 0))
    return _call(
        body, name="attn_fwd", grid=(nblk,),
        in_specs=[pl.BlockSpec(memory_space=pltpu.SMEM), blk, full_kv, full_kv, blk,
                  pl.BlockSpec((N_HEADS, WINDOW, 2 * WINDOW), lambda n: (0, 0, 0))],
        out_specs=[blk, blk, pl.BlockSpec((WINDOW, LANES), lambda n: (n, 0))],
        out_shape=[jax.ShapeDtypeStruct((S, ATTN_W), F32), jax.ShapeDtypeStruct((S, ATTN_W), BF16), jax.ShapeDtypeStruct((S, LANES), F32)],
        scratch_shapes=[pltpu.VMEM((N_KV, S, LANES), BF16), pltpu.VMEM((N_KV, S, LANES), BF16)],
        args=(sinks, q, k, v, ga, bias), sem=("arbitrary",), job=job, store=store)


def _pool_diff(u, halo, tile_index, tm):
    gw = POOL_W // len(POOL_WINDOWS)
    xh = jnp.concatenate([halo, u], axis=0)
    sums = []
    s = xh
    for step in (1, 2, 4, 8):
        s = s + pltpu.roll(s, step, axis=0)
        sums.append(s)
    t = tile_index * tm + lax.broadcasted_iota(I32, (tm, gw), 0)
    diffs = []
    for g, w in enumerate(POOL_WINDOWS):
        cols = slice(g * gw, (g + 1) * gw)
        cnt = jnp.minimum(t + 1, w).astype(F32)
        diffs.append(sums[g][POOL_HALO:, cols] / cnt - u[:, cols])
    return diffs


def _pool_weight(wp_ref, g):
    r = wp_ref.shape[1] // len(POOL_WINDOWS)
    return jnp.concatenate([wp_ref[j, g * r:(g + 1) * r, :] for j in range(N_CHIPS)], axis=0)


def _pool_fwd(u, gp, ca, w_pool, ps, l, job, store):
    S = u.shape[0]
    tm = min(S, 256)
    hb = tm // POOL_HALO
    gw = POOL_W // len(POOL_WINDOWS)

    def body(u_ref, uh_ref, gp_ref, ca_ref, wp_ref, ps_ref, c_ref):
        i = pl.program_id(0)
        uv = u_ref[...]
        halo = jnp.where(i > 0, uh_ref[...], 0.0)
        diffs = _pool_diff(uv, halo, i, tm)
        c_ref[:, 0:ATTN_W] = ca_ref[...]
        for g in range(len(POOL_WINDOWS)):
            cols = slice(g * gw, (g + 1) * gw)
            mm = _nn(diffs[g].astype(BF16), _pool_weight(wp_ref, g))
            gpv = gp_ref[:, cols]
            b = (mm * ps_ref[:, cols]) * (gpv * _sigmoid(gpv))
            c_ref[:, ATTN_W + g * gw:ATTN_W + (g + 1) * gw] = b.astype(BF16)

    row = pl.BlockSpec((tm, POOL_W), lambda i: (i, 0))
    return _call(
        body, name="pool_fwd", grid=(S // tm,),
        in_specs=[row, pl.BlockSpec((POOL_HALO, POOL_W), lambda i: (jnp.maximum(i * hb - 1, 0), 0)), row, row,
                  pl.BlockSpec(w_pool.shape, lambda i: (0, 0, 0)),
                  pl.BlockSpec((None, 1, POOL_W), lambda i: (l, 0, 0))],
        out_specs=[pl.BlockSpec((tm, ATTN_W + POOL_W), lambda i: (i, 0))],
        out_shape=[jax.ShapeDtypeStruct((S, ATTN_W + POOL_W), BF16)],
        args=(u, u, gp, ca, w_pool, ps), sem=("parallel",), job=job, store=store)[0]


def _ple_embed(p_ref, wple_ref):
    pb = p_ref[...].astype(BF16)
    return jnp.concatenate([_nn(pb, wple_ref[j]) for j in range(N_CHIPS)], axis=1)


def _outproj_ln(c, x, gl, p, w_out, w_ple, gain, bias, l, alpha, job, store):
    S, D = x.shape
    tm = min(S, 128)
    kq = D // N_CHIPS

    def body(c_ref, x_ref, gl_ref, p_ref, wo_ref, wp_ref, gain_ref, bias_ref, y_ref, yb_ref, xh_ref, rs_ref):
        mix = _nn(c_ref[:, 0:kq], wo_ref[0])
        for j in range(1, N_CHIPS):
            mix = mix + _nn(c_ref[:, j * kq:(j + 1) * kq], wo_ref[j])
        ple = _sigmoid(gl_ref[...]) * _ple_embed(p_ref, wp_ref)
        z = (alpha * x_ref[...] + mix) + ple
        mu = jnp.mean(z, axis=1, keepdims=True)
        zc = z - mu
        var = jnp.mean(zc * zc, axis=1, keepdims=True)
        rstd = lax.rsqrt(var + LN_EPS)
        xhat = zc * rstd
        y = xhat * gain_ref[...] + bias_ref[...]
        y_ref[...] = y
        yb_ref[...] = y.astype(BF16)
        xh_ref[...] = xhat
        rs_ref[...] = rstd

    row = pl.BlockSpec((tm, D), lambda i: (i, 0))
    vec = pl.BlockSpec((None, 1, D), lambda i: (l, 0, 0))
    return _call(
        body, name="outproj_ln", grid=(S // tm,),
        in_specs=[row, row, row, pl.BlockSpec((tm, p.shape[1]), lambda i: (i, 0)),
                  pl.BlockSpec(w_out.shape, lambda i: (0, 0, 0)), pl.BlockSpec(w_ple.shape, lambda i: (0, 0, 0)), vec, vec],
        out_specs=[row, row, row, pl.BlockSpec((tm, 1), lambda i: (i, 0))],
        out_shape=[jax.ShapeDtypeStruct((S, D), F32), jax.ShapeDtypeStruct((S, D), BF16), jax.ShapeDtypeStruct((S, D), F32),
                   jax.ShapeDtypeStruct((S, 1), F32)],
        args=(c, x, gl, p, w_out, w_ple, gain, bias), sem=("parallel",), job=job, store=store)


def _loss_and_grad(y, target):
    S, D = y.shape
    tm = min(S, 512)

    def body(y_ref, t_ref, dy_ref, acc_ref):
        @pl.when(pl.program_id(0) == 0)
        def _():
            acc_ref[...] = jnp.zeros_like(acc_ref)
        d = y_ref[...] - t_ref[...]
        dy_ref[...] = d * (1.0 / D)
        acc_ref[...] += jnp.sum(jnp.mean(d * d, axis=1, keepdims=True), axis=0, keepdims=True)

    row = pl.BlockSpec((tm, D), lambda i: (i, 0))
    return _pcall(body, name="loss", grid=(S // tm,), in_specs=[row, row],
                  out_specs=[row, pl.BlockSpec((8, LANES), lambda i: (0, 0))],
                  out_shape=[jax.ShapeDtypeStruct((S, D), F32), jax.ShapeDtypeStruct((8, LANES), F32)],
                  compiler_params=_cparams(("arbitrary",)))(y, target)


def _ln_bwd(dy, xhat, rstd, gl, p, w_ple, gain, l, job, store):
    S, D = dy.shape
    tm = min(S, 256)

    def body(dy_ref, xh_ref, rs_ref, gl_ref, p_ref, wp_ref, gain_ref, dz_ref, dzb_ref, de_ref, dgl_ref, gg_ref, gb_ref):
        @pl.when(pl.program_id(0) == 0)
        def _():
            gg_ref[...] = jnp.zeros_like(gg_ref)
            gb_ref[...] = jnp.zeros_like(gb_ref)
        dyv = dy_ref[...]
        xh = xh_ref[...]
        dxh = dyv * gain_ref[...]
        m1 = jnp.mean(dxh, axis=1, keepdims=True)
        m2 = jnp.mean(dxh * xh, axis=1, keepdims=True)
        dz = rs_ref[...] * ((dxh - m1) - xh * m2)
        gg_ref[...] += jnp.sum(dyv * xh, axis=0, keepdims=True)
        gb_ref[...] += jnp.sum(dyv, axis=0, keepdims=True)
        sg = _sigmoid(gl_ref[...])
        e = _ple_embed(p_ref, wp_ref)
        dz_ref[...] = dz
        dzb_ref[...] = dz.astype(BF16)
        de_ref[...] = (dz * sg).astype(BF16)
        dgl_ref[...] = ((dz * e) * (sg * (1.0 - sg))).astype(BF16)

    row = pl.BlockSpec((tm, D), lambda i: (i, 0))
    vec_in = pl.BlockSpec((None, 1, D), lambda i: (l, 0, 0))
    vec_out = pl.BlockSpec((1, D), lambda i: (0, 0))
    bsd = jax.ShapeDtypeStruct((S, D), BF16)
    return _call(
        body, name="ln_bwd", grid=(S // tm,),
        in_specs=[row, row, pl.BlockSpec((tm, 1), lambda i: (i, 0)), row, pl.BlockSpec((tm, p.shape[1]), lambda i: (i, 0)),
                  pl.BlockSpec(w_ple.shape, lambda i: (0, 0, 0)), vec_in],
        out_specs=[row, row, row, row, vec_out, vec_out],
        out_shape=[jax.ShapeDtypeStruct((S, D), F32), bsd, bsd, bsd, jax.ShapeDtypeStruct((1, D), F32), jax.ShapeDtypeStruct((1, D), F32)],
        args=(dy, xhat, rstd, gl, p, w_ple, gain), sem=("arbitrary",), job=job, store=store)


def _matmul_nn_acc(a, b4, name, add, add_scale, job, store):
    S = a.shape[0]
    KS, tk, N = b4.shape
    tm, tn = min(S, 512), min(N, 1024)

    def body(a_ref, b_ref, add_ref, o_ref):
        k = pl.program_id(2)

        @pl.when(k == 0)
        def _():
            o_ref[...] = add_scale * add_ref[...]
        o_ref[...] += _nn(a_ref[...], b_ref[...])

    return _call(body, name=name, grid=(S // tm, N // tn, KS),
                 in_specs=[pl.BlockSpec((tm, tk), lambda i, n, k: (i, k)),
                           pl.BlockSpec((None, tk, tn), lambda i, n, k: (k, 0, n)),
                           pl.BlockSpec((tm, tn), lambda i, n, k: (i, n))],
                 out_specs=[pl.BlockSpec((tm, tn), lambda i, n, k: (i, n))], out_shape=[jax.ShapeDtypeStruct((S, N), F32)],
                 args=(a, b4, add), sem=("parallel", "parallel", "arbitrary"), job=job, store=store)[0]


def _matmul_nt_rows(a, b4, name, add=None, add_scale=1.0):
    S, K = a.shape
    nq = b4.shape[1]
    tm = min(S, 512)
    out_spec = pl.BlockSpec((tm, nq), lambda j, i: (i, j))
    in_specs = [pl.BlockSpec((tm, K), lambda j, i: (i, 0)), pl.BlockSpec((None, nq, K), lambda j, i: (j, 0, 0))]
    if add is None:
        def body(a_ref, b_ref, o_ref):
            o_ref[...] = _nt(a_ref[...], b_ref[...])
        args = (a, b4)
    else:
        def body(a_ref, b_ref, add_ref, o_ref):
            o_ref[...] = _nt(a_ref[...], b_ref[...]) + add_scale * add_ref[...]
        in_specs.append(out_spec)
        args = (a, b4, add)

    return _pcall(body, name=name, grid=(N_CHIPS, S // tm), in_specs=in_specs,
                  out_specs=out_spec, out_shape=jax.ShapeDtypeStruct((S, N_CHIPS * nq), F32),
                  compiler_params=_cparams(("parallel", "parallel")))(*args)


def _matmul_tn(a, b, R, C, name, by_rows, job=None, store=None):
    S = a.shape[0]
    ts = min(S, 512)

    def body(a_ref, b_ref, o_ref):
        @pl.when(pl.program_id(1) == 0)
        def _():
            o_ref[...] = jnp.zeros_like(o_ref)
        o_ref[...] += _tn(a_ref[...], b_ref[...])

    if by_rows:
        a_spec = pl.BlockSpec((ts, R), lambda j, s: (s, j))
        b_spec = pl.BlockSpec((ts, C), lambda j, s: (s, 0))
    else:
        a_spec = pl.BlockSpec((ts, R), lambda j, s: (s, 0))
        b_spec = pl.BlockSpec((ts, C), lambda j, s: (s, j))
    return _call(body, name=name, grid=(N_CHIPS, S // ts), in_specs=[a_spec, b_spec],
                 out_specs=[pl.BlockSpec((None, R, C), lambda j, s: (j, 0, 0))],
                 out_shape=[jax.ShapeDtypeStruct((N_CHIPS, R, C), F32)],
                 args=(a, b), sem=("parallel", "arbitrary"), job=job, store=store)[0]


def _pool_bwd(u, gp, dc, w_pool, ps, l):
    S = u.shape[0]
    tm = min(S, 256)
    hb = tm // POOL_HALO
    ngrp = len(POOL_WINDOWS)
    gw = POOL_W // ngrp
    rr = gw // N_CHIPS

    def body(u_ref, uh_ref, gp_ref, dc_ref, wp_ref, ps_ref, dd_ref, dgp_ref, gps_ref, gwp_ref):
        i = pl.program_id(0)

        @pl.when(i == 0)
        def _():
            gps_ref[...] = jnp.zeros_like(gps_ref)
            gwp_ref[...] = jnp.zeros_like(gwp_ref)
        uv = u_ref[...]
        halo = jnp.where(i > 0, uh_ref[...], 0.0)
        diffs = _pool_diff(uv, halo, i, tm)
        for g in range(ngrp):
            cols = slice(g * gw, (g + 1) * gw)
            w = _pool_weight(wp_ref, g)
            db = diffs[g].astype(BF16)
            mm = _nn(db, w)
            gpv = gp_ref[:, cols]
            sg = _sigmoid(gpv)
            si = gpv * sg
            dsi = sg * (1.0 + gpv * (1.0 - sg))
            dcb = dc_ref[:, cols]
            psv = ps_ref[:, cols]
            d_mm = (dcb * si) * psv
            gps_ref[:, cols] += jnp.sum((dcb * si) * mm, axis=0, keepdims=True)
            dgp_ref[:, cols] = (dcb * (mm * psv)) * dsi
            d_mmb = d_mm.astype(BF16)
            dd_ref[:, cols] = _nt(d_mmb, w)
            gwt = _tn(db, d_mmb)
            for j in range(N_CHIPS):
                gwp_ref[j, g * rr:(g + 1) * rr, :] += gwt[j * rr:(j + 1) * rr, :]

    row = pl.BlockSpec((tm, POOL_W), lambda i: (i, 0))
    wspec = pl.BlockSpec(w_pool.shape, lambda i: (0, 0, 0))
    return _pcall(
        body, name="pool_bwd", grid=(S // tm,),
        in_specs=[row, pl.BlockSpec((POOL_HALO, POOL_W), lambda i: (jnp.maximum(i * hb - 1, 0), 0)), row,
                  pl.BlockSpec((tm, POOL_W), lambda i: (i, 1)), wspec, pl.BlockSpec((None, 1, POOL_W), lambda i: (l, 0, 0))],
        out_specs=[row, row, pl.BlockSpec((1, POOL_W), lambda i: (0, 0)), wspec],
        out_shape=[jax.ShapeDtypeStruct((S, POOL_W), F32), jax.ShapeDtypeStruct((S, POOL_W), F32),
                   jax.ShapeDtypeStruct((1, POOL_W), F32), jax.ShapeDtypeStruct(w_pool.shape, F32)],
        compiler_params=_cparams(("arbitrary",)),
    )(u, u, gp, dc, w_pool, ps)


def _pool_window_t(dd, halo_next, tile_index, tm):
    gw = POOL_W // len(POOL_WINDOWS)
    n = tm + POOL_HALO
    t = tile_index * tm + lax.broadcasted_iota(I32, (n, gw), 0)
    xh = jnp.concatenate([dd, halo_next], axis=0)
    outs = []
    for g, w in enumerate(POOL_WINDOWS):
        cols = slice(g * gw, (g + 1) * gw)
        cnt = jnp.minimum(t + 1, w).astype(F32)
        s = xh[:, cols] / cnt
        step = 1
        while step < w:
            s = s + pltpu.roll(s, n - step, axis=0)
            step *= 2
        outs.append(s[:tm] - dd[:, cols])
    return outs


def _assemble_dh(dq, dk, dv, dga, dd, dgp):
    S = dq.shape[0]
    tm = min(S, 256)
    hb = tm // POOL_HALO
    nt = S // tm

    def body(dq_ref, dk_ref, dv_ref, dga_ref, dd_ref, ddn_ref, dgp_ref, dh_ref, gb_ref):
        i = pl.program_id(0)

        @pl.when(i == 0)
        def _():
            gb_ref[...] = jnp.zeros_like(gb_ref)
        halo = jnp.where(i < nt - 1, ddn_ref[...], 0.0)
        du = jnp.concatenate(_pool_window_t(dd_ref[...], halo, i, tm), axis=1)
        dkv = dk_ref[...]
        dgav = dga_ref[...]
        parts = [(OFF_Q, dq_ref[...]), (OFF_KA, dkv), (OFF_KB, dkv), (OFF_V, dv_ref[...]), (OFF_GA, dgav),
                 (OFF_U, du[:, 0:896]), (OFF_UA, du[:, 896:1024]), (OFF_UB, du[:, 896:1024]), (OFF_GP, dgp_ref[...])]
        for off, val in parts:
            w = val.shape[1]
            dh_ref[:, off:off + w] = val.astype(BF16)
            gb_ref[:, off:off + w] += jnp.sum(val, axis=0, keepdims=True)

    def row(w):
        return pl.BlockSpec((tm, w), lambda i: (i, 0))

    return _pcall(
        body, name="assemble_dh", grid=(nt,),
        in_specs=[row(ATTN_W), row(KV_W), row(KV_W), row(ATTN_W), row(POOL_W),
                  pl.BlockSpec((POOL_HALO, POOL_W), lambda i: (jnp.minimum((i + 1) * hb, S // POOL_HALO - 1), 0)), row(POOL_W)],
        out_specs=[row(EXT), pl.BlockSpec((1, EXT), lambda i: (0, 0))],
        out_shape=[jax.ShapeDtypeStruct((S, EXT), BF16), jax.ShapeDtypeStruct((1, EXT), F32)],
        compiler_params=_cparams(("arbitrary",)),
    )(dq, dk, dv, dga, dd, dd, dgp)


def _attn_bwd(q, k, v, ga, o, dc, lse, bias, sinks, dbias_in, l, job, store):
    S = q.shape[0]
    nblk = S // WINDOW
    scale = 1.0 / math.sqrt(HEAD_DIM)

    def body(sink_ref, q_ref, k_ref, v_ref, ga_ref, o_ref, dc_ref, lse_ref, bias_ref, dbin_ref,
             dq_ref, dga_ref, dk_ref, dv_ref, db_ref, ds_ref, kd, vd):
        n = pl.program_id(0)

        @pl.when(n == 0)
        def _():
            _dup_heads(k_ref, kd)
            _dup_heads(v_ref, vd)
            dk_ref[...] = jnp.zeros_like(dk_ref)
            dv_ref[...] = jnp.zeros_like(dv_ref)
            db_ref[...] = dbin_ref[...]
            ds_ref[...] = jnp.zeros_like(ds_ref)

        cur = pl.multiple_of(n * WINDOW, WINDOW)
        prev = pl.multiple_of(jnp.maximum(n - 1, 0) * WINDOW, WINDOW)
        valid = _band_mask(n)
        lane = lax.broadcasted_iota(I32, (WINDOW, LANES), 1)
        lane8 = lax.broadcasted_iota(I32, (8, LANES), 1)
        lo = lane < HEAD_DIM
        lo2 = lax.broadcasted_iota(I32, (2 * WINDOW, LANES), 1) < HEAD_DIM
        kb = [_kv_block(kd, h, prev, cur) for h in range(N_KV)]
        vb = [_kv_block(vd, h, prev, cur) for h in range(N_KV)]
        lse_t = lse_ref[...]
        dkacc = [jnp.zeros((2 * WINDOW, LANES), F32) for _ in range(N_KV)]
        dvacc = [jnp.zeros((2 * WINDOW, LANES), F32) for _ in range(N_KV)]
        dsk = jnp.zeros((8, LANES), F32)
        for pair in range(N_HEADS // 2):
            sl = slice(LANES * pair, LANES * (pair + 1))
            qp = q_ref[:, sl]
            op = o_ref[:, sl]
            dcp = dc_ref[:, sl]
            gav = ga_ref[:, sl]
            sg = _sigmoid(gav)
            d_o = dcp * (gav * sg)
            dga_ref[:, sl] = (dcp * op) * (sg * (1.0 + gav * (1.0 - sg)))
            prod = d_o * op
            dqs = []
            for par in range(2):
                h = 2 * pair + par
                kvh = h // KV_RATIO
                msk = lo if par == 0 else jnp.logical_not(lo)
                qm = jnp.where(msk, qp, jnp.zeros_like(qp))
                dom = jnp.where(msk, d_o, 0.0).astype(BF16)
                delta = jnp.sum(jnp.where(msk, prod, 0.0), axis=1, keepdims=True)
                lse_h = jnp.sum(jnp.where(lane == h, lse_t, 0.0), axis=1, keepdims=True)
                s = _nt(qm, kb[kvh]) * scale + bias_ref[h]
                s = jnp.where(valid, s, -1e30)
                p = jnp.exp(s - lse_h)
                dp = _nt(dom, vb[kvh])
                dsc = p * (dp - delta)
                db_ref[h] += dsc
                psink = jnp.exp(sink_ref[l, h] - lse_h)
                dsk = dsk + jnp.where(lane8 == h, -jnp.sum(psink * delta, axis=0, keepdims=True), 0.0)
                dsb = dsc.astype(BF16)
                dqs.append(_nn(dsb, kb[kvh]) * scale)
                dkacc[kvh] = dkacc[kvh] + _tn(dsb, qm)
                dvacc[kvh] = dvacc[kvh] + _tn(p.astype(BF16), dom)
            dq_ref[:, sl] = jnp.where(lo, dqs[0], dqs[1])
        tk = [a + pltpu.roll(a, HEAD_DIM, axis=1) for a in dkacc]
        tv = [a + pltpu.roll(a, HEAD_DIM, axis=1) for a in dvacc]
        dkb = jnp.where(lo2, tk[0], tk[1]) * scale
        dvb = jnp.where(lo2, tv[0], tv[1])
        dk_ref[pl.ds(prev, WINDOW), :] += dkb[:WINDOW]
        dk_ref[pl.ds(cur, WINDOW), :] += dkb[WINDOW:]
        dv_ref[pl.ds(prev, WINDOW), :] += dvb[:WINDOW]
        dv_ref[pl.ds(cur, WINDOW), :] += dvb[WINDOW:]
        ds_ref[...] += dsk

    blk = pl.BlockSpec((WINDOW, ATTN_W), lambda n: (n, 0))
    full_kv = pl.BlockSpec((S, KV_W), lambda n: (0, 0))
    full_b = pl.BlockSpec((N_HEADS, WINDOW, 2 * WINDOW), lambda n: (0, 0, 0))
    return _call(
        body, name="attn_bwd", grid=(nblk,),
        in_specs=[pl.BlockSpec(memory_space=pltpu.SMEM), blk, full_kv, full_kv, blk, blk, blk,
                  pl.BlockSpec((WINDOW, LANES), lambda n: (n, 0)), full_b, full_b],
        out_specs=[blk, blk, full_kv, full_kv, full_b, pl.BlockSpec((8, LANES), lambda n: (0, 0))],
        out_shape=[jax.ShapeDtypeStruct((S, ATTN_W), F32), jax.ShapeDtypeStruct((S, ATTN_W), F32), jax.ShapeDtypeStruct((S, KV_W), F32),
                   jax.ShapeDtypeStruct((S, KV_W), F32), jax.ShapeDtypeStruct((N_HEADS, WINDOW, 2 * WINDOW), F32),
                   jax.ShapeDtypeStruct((8, LANES), F32)],
        scratch_shapes=[pltpu.VMEM((N_KV, S, LANES), BF16), pltpu.VMEM((N_KV, S, LANES), BF16)],
        args=(sinks, q, k, v, ga, o, dc, lse, bias, dbias_in), sem=("arbitrary",), job=job, store=store)


def _pack_small(arrs):
    flat = []
    for a in arrs:
        v = a.reshape(-1)
        flat.append(jnp.pad(v, (0, (-v.shape[0]) % LANES)))
    v = jnp.concatenate(flat)
    v = jnp.pad(v, (0, (-v.shape[0]) % (8 * LANES)))
    return v.reshape(-1, LANES)


def _unpack_small(packed, shapes):
    v = packed.reshape(-1)
    outs, off = [], 0
    for shp in shapes:
        n = math.prod(shp)
        outs.append(v[off:off + n].reshape(shp))
        off += n + (-n) % LANES
    return outs


def _bias_to_ext(b):
    L = b.shape[0]
    z = jnp.zeros((L, HALF_TILE), b.dtype)
    parts = []
    for j in range(N_CHIPS):
        seg = b[:, j * SHARD:(j + 1) * SHARD]
        parts += [z, seg] if j % 2 else [seg, z]
    return jnp.concatenate(parts, axis=1).reshape(L, 1, EXT)


def _bias_from_ext(g):
    parts = []
    for j in range(N_CHIPS):
        o = j * SHARD_P + (HALF_TILE if j % 2 else 0)
        parts.append(g[:, o:o + SHARD])
    return jnp.concatenate(parts, axis=1)


def kernel(x, p, w_in, b_in, w_out, attn_sinks, rel_bias, w_pool, pool_scale, w_ple, w_gate_ple, ln_gain, ln_bias, loss_target, m_w_in, m_b_in, m_w_out, m_attn_sinks, m_rel_bias, m_w_pool, m_pool_scale, m_w_ple, m_w_gate_ple, m_ln_gain, m_ln_bias, v_w_in, v_b_in, v_w_out, v_attn_sinks, v_rel_bias, v_w_pool, v_pool_scale, v_w_ple, v_w_gate_ple, v_ln_gain, v_ln_bias):
    L = w_in.shape[0]
    S, D = x.shape[1], x.shape[2]
    assert D == D_MODEL and w_in.shape[2] == SHARD and S % WINDOW == 0
    alpha = (2.0 * L) ** 0.25
    xc, yc, cc = _mesh_pos()
    idx = jnp.stack([2 * xc + yc, yc, cc]).astype(I32)
    store = {}

    def wkeys(l, names):
        return [("w", l, t) for t in names]

    def weight(l, t):
        return _unhalves(store["w", l, t])

    w_in_t = jnp.swapaxes(w_in, 1, 2)
    w_pool2 = w_pool.reshape(L, w_pool.shape[1] * w_pool.shape[2], w_pool.shape[3])
    for l in range(L):
        store["w", l, "in"] = _halves(_piece_in(w_in_t, l, idx))
        store["w", l, "out"] = _halves(_piece(w_out, l, idx, "piece_out"))
        store["w", l, "gate"] = _halves(_piece(w_gate_ple, l, idx, "piece_gate"))
        store["w", l, "ple"] = _halves(_piece(w_ple, l, idx, "piece_ple"))
        store["w", l, "pool"] = _halves(_piece(w_pool2, l, idx, "piece_pool"))
    _allgather_now(wkeys(0, WEIGHTS), store)

    b_ext = _bias_to_ext(b_in)
    ps3 = pool_scale.reshape(L, 1, POOL_W)
    gain3 = ln_gain.reshape(L, 1, D)
    bias3 = ln_bias.reshape(L, 1, D)
    onehot = _bucket_onehot()
    bias_hqk = _bias_fwd(rel_bias.T, onehot).reshape(N_HEADS, WINDOW, 2 * WINDOW)

    xs = x[0]
    xb = _to_bf16(xs)
    saved = []
    for l in range(L):
        nxt = l + 1 < L
        job = _Job()
        if nxt:
            job.add(_ag_ici, wkeys(l + 1, ["in"]), 3)
        if l >= 1:
            job.add(_ag_fwd, wkeys(l, ["gate"]), 3)
        q, k, v, ga, u, gp = _inproj(xb, weight(l, "in"), b_ext, l, job, store)
        job = _Job().add(_ag_fwd, wkeys(l + 1, ["in"]), 3) if nxt else None
        gl = _matmul_nn(xb, weight(l, "gate"), "gate_logits", job, store)
        job = _Job().add(_ag_ici, wkeys(l + 1, ["out", "ple", "pool"]), 9) if nxt else None
        o, ca, lse = _attn_fwd(q, k, v, ga, bias_hqk, attn_sinks, l, job, store)
        job = _Job().add(_ag_fwd, wkeys(l + 1, ["out", "ple", "pool"]), 9) if nxt else None
        c = _pool_fwd(u, gp, ca, weight(l, "pool"), ps3, l, job, store)
        job = _Job().add(_ag_ici, wkeys(l + 1, ["gate"]), 3) if nxt else None
        y, yb, xhat, rstd = _outproj_ln(c, xs, gl, p[l, 0], weight(l, "out"), weight(l, "ple"), gain3, bias3, l, alpha, job, store)
        saved.append(dict(xb=xb, q=q, k=k, v=v, ga=ga, u=u, gp=gp, gl=gl, o=o, lse=lse, c=c, xhat=xhat, rstd=rstd))
        xs, xb = y, yb

    dy, loss_acc = _loss_and_grad(xs, loss_target[0])
    loss = lax.psum(0.5 * loss_acc[0, 0], ("x", "y", "c"))

    shapes = {t: store["w", 0, t].shape for t in WEIGHTS}
    for t in WEIGHTS:
        store["full", t] = lax.empty((L,) + shapes[t][1:], F32)

    def rs_keys(kind, l, names):
        return [(kind, l, t) for t in names]

    def rs_pair_job(l):
        for t in WEIGHTS:
            store["ra", l, t] = lax.empty((N_CHIPS,) + shapes[t][2:], F32)
        return _Job().add(_rs_pair, rs_keys("g", l, WEIGHTS) + rs_keys("ra", l, WEIGHTS), len(WEIGHTS))

    def rs_pair_add(l):
        for t in WEIGHTS:
            p32, p16 = _rs_pair_add(store["g", l, t], store["ra", l, t], idx, "rs_pair_add_" + t)
            store["p32", l, t], store["p16", l, t] = p32, p16
            store["rb", l, t] = lax.empty((3,) + shapes[t][2:], BF16)

    def rs_ici_job(l, names):
        return _Job().add(_rs_ici, rs_keys("p16", l, names) + rs_keys("rb", l, names), 3 * len(names))

    def rs_chip_add(l):
        for t in WEIGHTS:
            store["full", t] = _rs_chip_add(store["p32", l, t], store["rb", l, t], store["full", t], l, idx, "rs_chip_add_" + t)

    dbias = jnp.zeros((N_HEADS, WINDOW, 2 * WINDOW), F32)
    small = [None] * L
    rest = ["out", "gate", "ple", "pool"]
    for l in reversed(range(L)):
        sv = saved[l]
        pl_l = p[l, 0]
        up = l + 1 if l + 1 < L else None
        job = rs_pair_job(up) if up is not None else None
        dz, dzb, d_e, d_gl, ggain, gbias = _ln_bwd(dy, sv["xhat"], sv["rstd"], sv["gl"], pl_l, weight(l, "ple"), gain3, l, job, store)
        if up is not None:
            rs_pair_add(up)
        dc = _matmul_nt_rows(dzb, weight(l, "out"), "d_mix_in")
        g_out = _matmul_tn(sv["c"], dzb, D // N_CHIPS, D, "grad_w_out", by_rows=True)
        g_ple = _matmul_tn(_to_bf16(pl_l), d_e, pl_l.shape[1], D // N_CHIPS, "grad_w_ple", by_rows=False)
        g_gate = _matmul_tn(sv["xb"], d_gl, D // N_CHIPS, D, "grad_w_gate", by_rows=True)
        dx1 = _matmul_nt_rows(d_gl, weight(l, "gate"), "d_x_gate", dz, alpha)
        dd, dgp, gps, g_pool = _pool_bwd(sv["u"], sv["gp"], dc, weight(l, "pool"), ps3, l)
        job = rs_ici_job(up, ["in"]) if up is not None else None
        dq, dga, dk, dv, dbias, dsink = _attn_bwd(sv["q"], sv["k"], sv["v"], sv["ga"], sv["o"], dc, sv["lse"], bias_hqk,
                                                  attn_sinks, dbias, l, job, store)
        dh, gbe = _assemble_dh(dq, dk, dv, dga, dd, dgp)
        job = rs_ici_job(up, rest) if up is not None else None
        g_in = _matmul_tn(dh, sv["xb"], SHARD_P, D, "grad_w_in", by_rows=True, job=job, store=store)
        dy = _matmul_nn_acc(dh, weight(l, "in"), "d_x", dx1, 1.0, None, None)
        if up is not None:
            rs_chip_add(up)
        for t, g in zip(WEIGHTS, (g_in, g_out, g_gate, g_ple, g_pool)):
            store["g", l, t] = _halves(g)
        small[l] = dict(b_in=_bias_from_ext(gbe)[0], sinks=dsink[0, :N_HEADS], ps=gps[0], gain=ggain[0], bias=gbias[0])
    grad_x = dy[None]

    _run_job("rs_pair_exchange", rs_pair_job(0), store)
    rs_pair_add(0)
    _run_job("rs_chip_exchange", rs_ici_job(0, WEIGHTS), store)
    rs_chip_add(0)
    _run_job("rs_pair_share", _Job().add(_rs_share, [("full", t) for t in WEIGHTS], len(WEIGHTS)), store)
    full = {t: _unhalves(store["full", t]) for t in WEIGHTS}

    def t_back(a):
        return jnp.swapaxes(a, 1, 2)

    def pool4(a):
        return a.reshape(w_pool.shape)

    r_in = _adamw(w_in_t, full["in"], jnp.swapaxes(m_w_in, 1, 2), jnp.swapaxes(v_w_in, 1, 2), idx, "adamw_w_in", tr=HALF_TILE)
    gw_in, dw_in, nm_in, nv_in = [t_back(a) for a in r_in]
    gw_out, dw_out, nm_out, nv_out = _adamw(w_out, full["out"], m_w_out, v_w_out, idx, "adamw_w_out")
    gw_gate, dw_gate, nm_gate, nv_gate = _adamw(w_gate_ple, full["gate"], m_w_gate_ple, v_w_gate_ple, idx, "adamw_w_gate")
    gw_ple, dw_ple, nm_ple, nv_ple = _adamw(w_ple, full["ple"], m_w_ple, v_w_ple, idx, "adamw_w_ple")
    r_pool = _adamw(w_pool2, full["pool"], m_w_pool.reshape(w_pool2.shape), v_w_pool.reshape(w_pool2.shape), idx, "adamw_w_pool")
    gw_pool, dw_pool, nm_pool, nv_pool = [pool4(a) for a in r_pool]

    g_rel = _bias_bwd(dbias.reshape(N_HEADS, -1), onehot).T
    small_shapes = [b_in.shape, attn_sinks.shape, rel_bias.shape, pool_scale.shape, ln_gain.shape, ln_bias.shape]
    g_small = [jnp.stack([small[l]["b_in"] for l in range(L)]), jnp.stack([small[l]["sinks"] for l in range(L)]), g_rel,
               jnp.stack([small[l]["ps"] for l in range(L)]), jnp.stack([small[l]["gain"] for l in range(L)]),
               jnp.stack([small[l]["bias"] for l in range(L)])]
    packed = _small_allreduce_adamw(
        _pack_small(g_small),
        _pack_small([b_in, attn_sinks, rel_bias, pool_scale, ln_gain, ln_bias]),
        _pack_small([m_b_in, m_attn_sinks, m_rel_bias, m_pool_scale, m_ln_gain, m_ln_bias]),
        _pack_small([v_b_in, v_attn_sinks, v_rel_bias, v_pool_scale, v_ln_gain, v_ln_bias]))
    sg, sd, sm, sv_ = [_unpack_small(a, small_shapes) for a in packed]

    def order(big, sm_):
        return (big[0], sm_[0], big[1], sm_[1], sm_[2], big[2], sm_[3], big[3], big[4], sm_[4], sm_[5])

    return (loss, grad_x,
            *order((gw_in, gw_out, gw_pool, gw_ple, gw_gate), sg),
            *order((dw_in, dw_out, dw_pool, dw_ple, dw_gate), sd),
            *order((nm_in, nm_out, nm_pool, nm_ple, nm_gate), sm),
            *order((nv_in, nv_out, nv_pool, nv_ple, nv_gate), sv_))
```

```python
import functools
import math

import jax
import jax.numpy as jnp
from jax import lax
from jax.experimental import pallas as pl
from jax.experimental.pallas import tpu as pltpu

F32 = jnp.float32
BF16 = jnp.bfloat16
I32 = jnp.int32
MESH = pl.DeviceIdType.MESH

HEAD_DIM = 64
QK_SCALE = HEAD_DIM ** -0.5
WINDOW = 128
KV_RATIO = 8
POOL_WINDOWS = (2, 4, 8, 16)
POOL_HALO = 16
REL_BUCKETS = 32
REL_MAX_DIST = 128
LN_EPS = 1e-5
ADAM_LR, ADAM_B1, ADAM_B2, ADAM_EPS, ADAM_WD, ADAM_STEP = 0.001, 0.9, 0.999, 1e-08, 0.01, 10

LANES = 128
VMEM_LIMIT = 52 * 1024 * 1024
N_CHIPS = 4
N_DEV = 8

D_MODEL = 2048
ATTN_W = 1024
POOL_W = 1024
KV_W = 128
N_HEADS = ATTN_W // HEAD_DIM
N_KV = N_HEADS // KV_RATIO
IN_COLS = 4352
SHARD = IN_COLS // N_CHIPS
SHARD_P = 1152
EXT = N_CHIPS * SHARD_P
HALF_TILE = SHARD_P - SHARD
OFF_Q, OFF_KA, OFF_KB, OFF_V, OFF_GA, OFF_U, OFF_UA, OFF_UB, OFF_GP = 0, 1024, 1152, 1280, 1408, 2432, 3328, 3456, 3584
WEIGHTS = ("in", "out", "gate", "ple", "pool")


def _cparams(sem=None):
    return pltpu.CompilerParams(dimension_semantics=sem, vmem_limit_bytes=VMEM_LIMIT)


def _pcall(body, **kw):
    return pl.pallas_call(body, **kw)


def _sigmoid(x):
    return 1.0 / (1.0 + jnp.exp(-x))


def _nt(a, b):
    return lax.dot_general(a, b, (((1,), (1,)), ((), ())), preferred_element_type=F32)


def _tn(a, b):
    return lax.dot_general(a, b, (((0,), (0,)), ((), ())), preferred_element_type=F32)


def _nn(a, b):
    return jnp.dot(a, b, preferred_element_type=F32)


def _mesh_pos():
    return lax.axis_index("x"), lax.axis_index("y"), lax.axis_index("c")


def _peer_chips(x, y):
    return [(1 - x, y), (x, 1 - y), (1 - x, 1 - y)]


def _row_tile(rows, cap=256):
    t = min(rows, cap)
    while rows % t or t % 16:
        t -= 1
    return t


def _hbm_spec():
    return pl.BlockSpec(memory_space=pltpu.HBM)


def _halves(a):
    return a.reshape(a.shape[:-2] + (2, a.shape[-2] // 2, a.shape[-1]))


def _unhalves(a):
    return a.reshape(a.shape[:-3] + (2 * a.shape[-2], a.shape[-1]))


class _remote:
    def __init__(self, src, dst, send, recv, i, device):
        self.args = dict(src_ref=src, dst_ref=dst, send_sem=send.at[i], recv_sem=recv.at[i], device_id=device, device_id_type=MESH)

    def start(self):
        pltpu.make_async_remote_copy(**self.args).start()

    def wait_recv(self):
        pltpu.make_async_remote_copy(**self.args).wait_recv()

    def wait_send(self):
        pltpu.make_async_remote_copy(**self.args).wait_send()


class _Job:
    def __init__(self):
        self.keys, self.parts, self.n = [], [], 0

    def add(self, fn, keys, n):
        self.parts.append((fn, len(self.keys), len(keys), self.n))
        self.keys += list(keys)
        self.n += n
        return self

    def build(self, refs, send, recv):
        out = []
        for fn, i0, nb, base in self.parts:
            out += fn(refs[i0:i0 + nb], send, recv, base)
        return out


def _ag_ici(refs, send, recv, base):
    x, y, c = _mesh_pos()
    me = 2 * x + y
    out = []
    for t, g in enumerate(refs):
        for k, chip in enumerate(_peer_chips(x, y)):
            i = base + 3 * t + k
            dev = (*chip, c)
            out.append((_remote(g.at[me, c], g.at[me, c], send, recv, i, dev),
                        _remote(g.at[me, c], g.at[2 * chip[0] + chip[1], c], send, recv, i, dev)))
    return out


def _ag_fwd(refs, send, recv, base):
    x, y, c = _mesh_pos()
    out = []
    for t, g in enumerate(refs):
        for k, chip in enumerate(_peer_chips(x, y)):
            i = base + 3 * t + k
            slot = 2 * chip[0] + chip[1]
            dev = (x, y, 1 - c)
            out.append((_remote(g.at[slot, c], g.at[slot, c], send, recv, i, dev),
                        _remote(g.at[slot, c], g.at[slot, 1 - c], send, recv, i, dev)))
    return out


def _rs_pair(refs, send, recv, base):
    x, y, c = _mesh_pos()
    n = len(refs) // 2
    out = []
    for t in range(n):
        cp = _remote(refs[t].at[:, 1 - c], refs[n + t], send, recv, base + t, (x, y, 1 - c))
        out.append((cp, cp))
    return out


def _rs_ici(refs, send, recv, base):
    x, y, c = _mesh_pos()
    n = len(refs) // 2
    out = []
    for t in range(n):
        for k, chip in enumerate(_peer_chips(x, y)):
            cp = _remote(refs[t].at[2 * chip[0] + chip[1]], refs[n + t].at[k], send, recv, base + 3 * t + k, (*chip, c))
            out.append((cp, cp))
    return out


def _rs_share(refs, send, recv, base, layer):
    x, y, c = _mesh_pos()
    out = []
    for t, f in enumerate(refs):
        dev = (x, y, 1 - c)
        out.append((_remote(f.at[layer, c], f.at[layer, c], send, recv, base + t, dev),
                    _remote(f.at[layer, c], f.at[layer, 1 - c], send, recv, base + t, dev)))
    return out


def _call(body, *, name, grid, in_specs, out_specs, out_shape, args, scratch_shapes=(), sem=None, job=None, store=None):
    in_specs, out_specs, out_shape, scratch_shapes = list(in_specs), list(out_specs), list(out_shape), list(scratch_shapes)
    if job is None or job.n == 0:
        return list(_pcall(body, name=name, grid=grid, in_specs=in_specs, out_specs=out_specs, out_shape=out_shape,
                           scratch_shapes=scratch_shapes, compiler_params=_cparams(sem))(*args))
    bufs = [store[k] for k in job.keys]
    nb, n_in, n_out, n_sc = len(bufs), len(args), len(out_shape), len(scratch_shapes)

    def wrapped(*refs):
        ins = refs[:n_in]
        outs = refs[n_in + nb:n_in + nb + n_out]
        cb = refs[n_in + nb + n_out:n_in + 2 * nb + n_out]
        scratch = refs[n_in + 2 * nb + n_out:n_in + 2 * nb + n_out + n_sc]
        send, recv = refs[-2:]
        ids = [pl.program_id(a) for a in range(len(grid))]
        first = functools.reduce(jnp.logical_and, [i == 0 for i in ids])
        last = functools.reduce(jnp.logical_and, [i == g - 1 for i, g in zip(ids, grid)])

        @pl.when(first)
        def _():
            for s, _r in job.build(cb, send, recv):
                s.start()

        body(*ins, *outs, *scratch)

        @pl.when(last)
        def _():
            pairs = job.build(cb, send, recv)
            for _s, r in pairs:
                r.wait_recv()
            for s, _r in pairs:
                s.wait_send()

    res = _pcall(
        wrapped, name=name, grid=grid, in_specs=in_specs + [_hbm_spec()] * nb, out_specs=out_specs + [_hbm_spec()] * nb,
        out_shape=out_shape + [jax.ShapeDtypeStruct(b.shape, b.dtype) for b in bufs],
        scratch_shapes=scratch_shapes + [pltpu.SemaphoreType.DMA((job.n,)), pltpu.SemaphoreType.DMA((job.n,))],
        input_output_aliases={n_in + i: n_out + i for i in range(nb)},
        compiler_params=_cparams(("arbitrary",) * len(grid)))(*args, *bufs)
    for k, v in zip(job.keys, res[n_out:]):
        store[k] = v
    return list(res[:n_out])


def _run_job(name, job, store):
    bufs = [store[k] for k in job.keys]
    nb = len(bufs)

    def body(*refs):
        send, recv = refs[-2:]
        pairs = job.build(refs[nb:2 * nb], send, recv)
        for s, _r in pairs:
            s.start()
        for _s, r in pairs:
            r.wait_recv()
        for s, _r in pairs:
            s.wait_send()

    res = _pcall(body, name=name, in_specs=[_hbm_spec()] * nb, out_specs=[_hbm_spec()] * nb,
                 out_shape=[jax.ShapeDtypeStruct(b.shape, b.dtype) for b in bufs],
                 scratch_shapes=[pltpu.SemaphoreType.DMA((job.n,)), pltpu.SemaphoreType.DMA((job.n,))],
                 input_output_aliases={i: i for i in range(nb)})(*bufs)
    for k, v in zip(job.keys, res):
        store[k] = v


def _allgather_now(keys, store):
    bufs = [store[k] for k in keys]
    nb = len(bufs)

    def body(*refs):
        send, recv = refs[-2:]
        g = refs[nb:2 * nb]
        ici = _ag_ici(g, send, recv, 0)
        fwd = _ag_fwd(g, send, recv, 3 * nb)
        for s, _r in ici:
            s.start()
        for (_s, r), (fs, _fr) in zip(ici, fwd):
            r.wait_recv()
            fs.start()
        for _fs, fr in fwd:
            fr.wait_recv()
        for s, _r in ici + fwd:
            s.wait_send()

    res = _pcall(body, name="allgather_first_layer", in_specs=[_hbm_spec()] * nb, out_specs=[_hbm_spec()] * nb,
                 out_shape=[jax.ShapeDtypeStruct(b.shape, b.dtype) for b in bufs],
                 scratch_shapes=[pltpu.SemaphoreType.DMA((6 * nb,)), pltpu.SemaphoreType.DMA((6 * nb,))],
                 input_output_aliases={i: i for i in range(nb)})(*bufs)
    for k, v in zip(keys, res):
        store[k] = v


def _piece(w, l, idx, name):
    _, R, C = w.shape
    tr = _row_tile(R)

    def body(s_ref, w_ref, o_ref):
        del s_ref
        o_ref[...] = w_ref[...].astype(BF16)

    gs = pltpu.PrefetchScalarGridSpec(
        num_scalar_prefetch=1, grid=(R // tr,),
        in_specs=[pl.BlockSpec((None, tr, C), lambda r, s: (l, r, 0))],
        out_specs=pl.BlockSpec((None, tr, C), lambda r, s: (s[0], r, 0)))
    return _pcall(body, name=name, grid_spec=gs, out_shape=jax.ShapeDtypeStruct((N_CHIPS, R, C), BF16),
                  compiler_params=_cparams(("parallel",)))(idx, w)


def _piece_in(w_in_t, l, idx):
    _, R, D = w_in_t.shape
    tr = HALF_TILE
    nsrc = R // tr

    def body(s_ref, w_ref, o_ref):
        src = pl.program_id(0) - s_ref[1]
        ok = jnp.logical_and(src >= 0, src < nsrc)
        o_ref[...] = jnp.where(ok, w_ref[...], 0.0).astype(BF16)

    gs = pltpu.PrefetchScalarGridSpec(
        num_scalar_prefetch=1, grid=(SHARD_P // tr,),
        in_specs=[pl.BlockSpec((None, tr, D), lambda r, s: (l, jnp.clip(r - s[1], 0, nsrc - 1), 0))],
        out_specs=pl.BlockSpec((None, tr, D), lambda r, s: (s[0], r, 0)))
    return _pcall(body, name="piece_in", grid_spec=gs, out_shape=jax.ShapeDtypeStruct((N_CHIPS, SHARD_P, D), BF16),
                  compiler_params=_cparams(("parallel",)))(idx, w_in_t)


def _rs_pair_add(g5, r4, idx, name):
    J, _, h, C = g5.shape
    th = _row_tile(h)

    def body(s_ref, g_ref, r_ref, o32_ref, o16_ref):
        del s_ref
        s = g_ref[...] + r_ref[...]
        o32_ref[...] = s
        o16_ref[...] = s.astype(BF16)

    spec = pl.BlockSpec((None, th, C), lambda j, r, s: (j, r, 0))
    gs = pltpu.PrefetchScalarGridSpec(
        num_scalar_prefetch=1, grid=(J, h // th),
        in_specs=[pl.BlockSpec((None, None, th, C), lambda j, r, s: (j, s[2], r, 0)), spec],
        out_specs=[spec, spec])
    return _pcall(body, name=name, grid_spec=gs,
                  out_shape=[jax.ShapeDtypeStruct((J, h, C), F32), jax.ShapeDtypeStruct((J, h, C), BF16)],
                  compiler_params=_cparams(("parallel", "parallel")))(idx, g5, r4)


def _rs_chip_add(p32, r3, full, l, idx, name):
    _, h, C = p32.shape
    th = _row_tile(h)

    def body(s_ref, p_ref, r_ref, f_ref, o_ref):
        del s_ref, f_ref
        o_ref[...] = ((p_ref[...] + r_ref[0].astype(F32)) + r_ref[1].astype(F32)) + r_ref[2].astype(F32)

    gs = pltpu.PrefetchScalarGridSpec(
        num_scalar_prefetch=1, grid=(h // th,),
        in_specs=[pl.BlockSpec((None, th, C), lambda r, s: (s[0], r, 0)),
                  pl.BlockSpec((3, th, C), lambda r, s: (0, r, 0)),
                  pl.BlockSpec(memory_space=pl.ANY)],
        out_specs=pl.BlockSpec((None, None, th, C), lambda r, s: (l, s[2], r, 0)))
    return _pcall(body, name=name, grid_spec=gs, out_shape=jax.ShapeDtypeStruct(full.shape, F32),
                  input_output_aliases={3: 0}, compiler_params=_cparams(("parallel",)))(idx, p32, r3, full)


def _adamw_math(w, g, m, v):
    nm = ADAM_B1 * m + (1.0 - ADAM_B1) * g
    nv = ADAM_B2 * v + (1.0 - ADAM_B2) * (g * g)
    m_hat = nm / (1.0 - ADAM_B1 ** ADAM_STEP)
    v_hat = nv / (1.0 - ADAM_B2 ** ADAM_STEP)
    delta = -ADAM_LR * (m_hat / (jnp.sqrt(v_hat) + ADAM_EPS) + ADAM_WD * w)
    return delta, nm, nv


def _adamw(w, g, m, v, idx, name, tr=None):
    L, R, C = w.shape
    Rg = g.shape[1]
    tr = tr or _row_tile(R)
    shift = (Rg - R) // tr
    assert (Rg - R) % tr == 0

    def body(s_ref, w_ref, g_ref, m_ref, v_ref, go_ref, d_ref, nm_ref, nv_ref):
        del s_ref
        gv = g_ref[...]
        d, nm, nv = _adamw_math(w_ref[...], gv, m_ref[...], v_ref[...])
        go_ref[...] = gv
        d_ref[...] = d
        nm_ref[...] = nm
        nv_ref[...] = nv

    wspec = pl.BlockSpec((None, tr, C), lambda l, r, s: (l, r, 0))
    gs = pltpu.PrefetchScalarGridSpec(
        num_scalar_prefetch=1, grid=(L, R // tr),
        in_specs=[wspec, pl.BlockSpec((None, tr, C), lambda l, r, s: (l, r + shift * s[1], 0)), wspec, wspec],
        out_specs=[wspec, wspec, wspec, wspec])
    sds = jax.ShapeDtypeStruct((L, R, C), F32)
    return _pcall(body, name=name, grid_spec=gs, out_shape=[sds, sds, sds, sds],
                  compiler_params=_cparams(("parallel", "parallel")))(idx, w, g, m, v)


def _small_allreduce_adamw(gv, wv, mv, vv):
    NR = gv.shape[0]

    def body(g_ref, w_ref, m_ref, v_ref, go_ref, d_ref, nm_ref, nv_ref, gath, send, recv):
        x, y, c = _mesh_pos()
        rank = 4 * x + 2 * y + c
        gath[rank] = g_ref[...]
        cps = []
        for msk in range(1, N_DEV):
            bx, by, bc = (msk >> 2) & 1, (msk >> 1) & 1, msk & 1
            peer = (1 - x if bx else x, 1 - y if by else y, 1 - c if bc else c)
            cps.append(pltpu.make_async_remote_copy(src_ref=g_ref, dst_ref=gath.at[rank], send_sem=send.at[msk - 1],
                                                    recv_sem=recv.at[msk - 1], device_id=peer, device_id_type=MESH))
        for cp in cps:
            cp.start()
        for msk in range(1, N_DEV):
            bx, by, bc = (msk >> 2) & 1, (msk >> 1) & 1, msk & 1
            peer = (1 - x if bx else x, 1 - y if by else y, 1 - c if bc else c)
            prank = 4 * peer[0] + 2 * peer[1] + peer[2]
            pltpu.make_async_remote_copy(src_ref=g_ref, dst_ref=gath.at[prank], send_sem=send.at[msk - 1],
                                         recv_sem=recv.at[msk - 1], device_id=peer, device_id_type=MESH).wait_recv()
        for cp in cps:
            cp.wait_send()
        tot = gath[0]
        for r in range(1, N_DEV):
            tot = tot + gath[r]
        d, nm, nv = _adamw_math(w_ref[...], tot, m_ref[...], v_ref[...])
        go_ref[...] = tot
        d_ref[...] = d
        nm_ref[...] = nm
        nv_ref[...] = nv

    vm = pl.BlockSpec(memory_space=pltpu.VMEM)
    sds = jax.ShapeDtypeStruct((NR, LANES), F32)
    return _pcall(body, name="small_allreduce_adamw", in_specs=[vm, vm, vm, vm], out_specs=[vm, vm, vm, vm],
                  out_shape=[sds, sds, sds, sds],
                  scratch_shapes=[pltpu.VMEM((N_DEV, NR, LANES), F32), pltpu.SemaphoreType.DMA((N_DEV - 1,)),
                                  pltpu.SemaphoreType.DMA((N_DEV - 1,))],
                  compiler_params=pltpu.CompilerParams(vmem_limit_bytes=VMEM_LIMIT))(gv, wv, mv, vv)


def _split3(a):
    h1 = a.astype(BF16)
    r1 = a - h1.astype(F32)
    h2 = r1.astype(BF16)
    h3 = (r1 - h2.astype(F32)).astype(BF16)
    return h1, h2, h3


def _bucket_onehot():
    qq = jnp.arange(WINDOW)[:, None]
    kk = jnp.arange(2 * WINDOW)[None, :]
    dist = qq + WINDOW - kk
    max_exact = REL_BUCKETS // 2
    d = jnp.maximum(dist, 0)
    d_f = jnp.maximum(d, 1).astype(F32)
    large = max_exact + (jnp.log(d_f / max_exact) / math.log(REL_MAX_DIST / max_exact) * (REL_BUCKETS - max_exact)).astype(I32)
    large = jnp.minimum(large, REL_BUCKETS - 1)
    bucket = jnp.where(d < max_exact, d, large)
    inwin = (dist >= 0) & (dist < WINDOW)
    oh = (bucket[None] == jnp.arange(REL_BUCKETS)[:, None, None]) & inwin[None]
    return oh.reshape(REL_BUCKETS, WINDOW * 2 * WINDOW).astype(BF16)


def _bias_fwd(rel_bias_t, onehot):
    H, N = rel_bias_t.shape[0], onehot.shape[1]
    tn = 4096

    def body(t_ref, oh_ref, o_ref):
        h1, h2, h3 = _split3(t_ref[...])
        oh = oh_ref[...]
        o_ref[...] = (_nn(h1, oh) + _nn(h2, oh)) + _nn(h3, oh)

    return _pcall(body, name="bias_fwd", grid=(N // tn,),
                  in_specs=[pl.BlockSpec((H, REL_BUCKETS), lambda i: (0, 0)), pl.BlockSpec((REL_BUCKETS, tn), lambda i: (0, i))],
                  out_specs=pl.BlockSpec((H, tn), lambda i: (0, i)), out_shape=jax.ShapeDtypeStruct((H, N), F32),
                  compiler_params=_cparams(("parallel",)))(rel_bias_t, onehot)


def _bias_bwd(dbias, onehot):
    H, N = dbias.shape
    tn = 4096

    def body(d_ref, oh_ref, o_ref):
        @pl.when(pl.program_id(0) == 0)
        def _():
            o_ref[...] = jnp.zeros_like(o_ref)
        h1, h2, h3 = _split3(d_ref[...])
        oh = oh_ref[...]
        o_ref[...] += (_nt(h1, oh) + _nt(h2, oh)) + _nt(h3, oh)

    return _pcall(body, name="bias_bwd", grid=(N // tn,),
                  in_specs=[pl.BlockSpec((H, tn), lambda i: (0, i)), pl.BlockSpec((REL_BUCKETS, tn), lambda i: (0, i))],
                  out_specs=pl.BlockSpec((H, REL_BUCKETS), lambda i: (0, 0)), out_shape=jax.ShapeDtypeStruct((H, REL_BUCKETS), F32),
                  compiler_params=_cparams(("arbitrary",)))(dbias, onehot)


def _to_bf16(x):
    S, D = x.shape
    tm = min(S, 512)

    def body(x_ref, o_ref):
        o_ref[...] = x_ref[...].astype(BF16)

    return _pcall(body, name="to_bf16", grid=(S // tm,), in_specs=[pl.BlockSpec((tm, D), lambda i: (i, 0))],
                  out_specs=pl.BlockSpec((tm, D), lambda i: (i, 0)), out_shape=jax.ShapeDtypeStruct((S, D), BF16),
                  compiler_params=_cparams(("parallel",)))(x)


def _inproj(xb, w_t, b_ext, l, job, store):
    S, D = xb.shape
    tm = min(S, 512)

    def body(x_ref, w_ref, b_ref, q_ref, k_ref, v_ref, ga_ref, u_ref, gp_ref):
        j = pl.program_id(1)
        acc = _nt(x_ref[...], w_ref[...]) + b_ref[...]

        @pl.when(j == 0)
        def _():
            q_ref[...] = (acc[:, :1024] * QK_SCALE).astype(BF16)
            k_ref[...] = acc[:, 1024:1152]

        @pl.when(j == 1)
        def _():
            k_ref[...] += acc[:, 0:128]
            v_ref[...] = acc[:, 128:256]
            ga_ref[:, 0:896] = acc[:, 256:1152]

        @pl.when(j == 2)
        def _():
            ga_ref[:, 896:1024] = acc[:, 0:128]
            u_ref[:, 0:896] = acc[:, 128:1024]
            u_ref[:, 896:1024] = acc[:, 1024:1152]

        @pl.when(j == 3)
        def _():
            u_ref[:, 896:1024] += acc[:, 0:128]
            gp_ref[...] = acc[:, 128:1152]

    def ospec(w):
        return pl.BlockSpec((tm, w), lambda i, j: (i, 0))

    return _call(
        body, name="inproj", grid=(S // tm, N_CHIPS),
        in_specs=[pl.BlockSpec((tm, D), lambda i, j: (i, 0)),
                  pl.BlockSpec((None, SHARD_P, D), lambda i, j: (j, 0, 0)),
                  pl.BlockSpec((None, 1, SHARD_P), lambda i, j: (l, 0, j))],
        out_specs=[ospec(ATTN_W), ospec(KV_W), ospec(KV_W), ospec(ATTN_W), ospec(POOL_W), ospec(POOL_W)],
        out_shape=[jax.ShapeDtypeStruct((S, ATTN_W), BF16), jax.ShapeDtypeStruct((S, KV_W), F32), jax.ShapeDtypeStruct((S, KV_W), F32),
                   jax.ShapeDtypeStruct((S, ATTN_W), F32), jax.ShapeDtypeStruct((S, POOL_W), F32), jax.ShapeDtypeStruct((S, POOL_W), F32)],
        args=(xb, w_t, b_ext), sem=("parallel", "arbitrary"), job=job, store=store)


def _matmul_nn(a, b4, name, job, store):
    S, K = a.shape
    N = b4.shape[2]
    kq = b4.shape[1]
    tm, tn = min(S, 512), min(N, 1024)

    def body(a_ref, b_ref, o_ref):
        acc = _nn(a_ref[:, 0:kq], b_ref[0])
        for j in range(1, N_CHIPS):
            acc = acc + _nn(a_ref[:, j * kq:(j + 1) * kq], b_ref[j])
        o_ref[...] = acc

    return _call(body, name=name, grid=(N // tn, S // tm),
                 in_specs=[pl.BlockSpec((tm, K), lambda n, i: (i, 0)), pl.BlockSpec((N_CHIPS, kq, tn), lambda n, i: (0, 0, n))],
                 out_specs=[pl.BlockSpec((tm, tn), lambda n, i: (i, n))], out_shape=[jax.ShapeDtypeStruct((S, N), F32)],
                 args=(a, b4), sem=("parallel", "parallel"), job=job, store=store)[0]


def _masked_bias(bias):
    qq = jnp.arange(WINDOW)[:, None]
    kk = jnp.arange(2 * WINDOW)[None, :]
    dist = qq + WINDOW - kk
    inwin = (dist >= 0) & (dist < WINDOW)
    return jnp.stack([jnp.where(inwin & (kk >= WINDOW), bias, -1e30), jnp.where(inwin, bias, -1e30)])


def _dup_heads(src_ref, dst_ref):
    a = src_ref[...]
    r = pltpu.roll(a, HEAD_DIM, axis=1)
    lo = lax.broadcasted_iota(I32, a.shape, 1) < HEAD_DIM
    dst_ref[0] = jnp.where(lo, a, r).astype(BF16)
    dst_ref[1] = jnp.where(lo, r, a).astype(BF16)


def _kv_block(ref, h, prev, cur):
    return jnp.concatenate([ref[h, pl.ds(prev, WINDOW), :], ref[h, pl.ds(cur, WINDOW), :]], axis=0)


def _attn_fwd(q, k, v, ga, bias, sinks, l, job, store):
    S = q.shape[0]
    nblk = S // WINDOW

    def body(sink_ref, q_ref, k_ref, v_ref, ga_ref, bias_ref, o_ref, ca_ref, lse_ref, kd, vd):
        n = pl.program_id(0)

        @pl.when(n == 0)
        def _():
            _dup_heads(k_ref, kd)
            _dup_heads(v_ref, vd)

        cur = pl.multiple_of(n * WINDOW, WINDOW)
        prev = pl.multiple_of(jnp.maximum(n - 1, 0) * WINDOW, WINDOW)
        lane = lax.broadcasted_iota(I32, (WINDOW, LANES), 1)
        lo = lane < HEAD_DIM
        kb = [_kv_block(kd, h, prev, cur) for h in range(N_KV)]
        vb = [_kv_block(vd, h, prev, cur) for h in range(N_KV)]
        lse_mat = jnp.zeros((WINDOW, LANES), F32)
        for pair in range(N_HEADS // 2):
            sl = slice(LANES * pair, LANES * (pair + 1))
            qp = q_ref[:, sl]
            outs = []
            for par in range(2):
                h = 2 * pair + par
                kvh = h // KV_RATIO
                qm = jnp.where(lo if par == 0 else jnp.logical_not(lo), qp, jnp.zeros_like(qp))
                s = _nt(qm, kb[kvh]) + bias_ref[h]
                sink = sink_ref[l, h]
                m = jnp.maximum(jnp.max(s, axis=1, keepdims=True), sink)
                e = jnp.exp(s - m)
                den = jnp.sum(e, axis=1, keepdims=True) + jnp.exp(sink - m)
                p = (e * (1.0 / den)).astype(BF16)
                outs.append(_nn(p, vb[kvh]))
                lse_mat = jnp.where(lane == h, m + jnp.log(den), lse_mat)
            o_pair = jnp.where(lo, outs[0], outs[1])
            o_ref[:, sl] = o_pair
            gav = ga_ref[:, sl]
            ca_ref[:, sl] = (o_pair * (gav * _sigmoid(gav))).astype(BF16)
        lse_ref[...] = lse_mat

    blk = pl.BlockSpec((WINDOW, ATTN_W), lambda n: (n, 0))
    full_kv = pl.BlockSpec((S, KV_W), lambda n: (0, 0))
    return _call(
        body, name="attn_fwd", grid=(nblk,),
        in_specs=[pl.BlockSpec(memory_space=pltpu.SMEM), blk, full_kv, full_kv, blk,
                  pl.BlockSpec((None, N_HEADS, WINDOW, 2 * WINDOW), lambda n: (jnp.minimum(n, 1), 0, 0, 0))],
        out_specs=[blk, blk, pl.BlockSpec((WINDOW, LANES), lambda n: (n, 0))],
        out_shape=[jax.ShapeDtypeStruct((S, ATTN_W), F32), jax.ShapeDtypeStruct((S, ATTN_W), BF16), jax.ShapeDtypeStruct((S, LANES), F32)],
        scratch_shapes=[pltpu.VMEM((N_KV, S, LANES), BF16), pltpu.VMEM((N_KV, S, LANES), BF16)],
        args=(sinks, q, k, v, ga, bias), sem=("arbitrary",), job=job, store=store)


def _pool_diff(u, halo, tile_index, tm):
    gw = POOL_W // len(POOL_WINDOWS)
    xh = jnp.concatenate([halo, u], axis=0)
    sums = []
    s = xh
    for step in (1, 2, 4, 8):
        s = s + pltpu.roll(s, step, axis=0)
        sums.append(s)
    t = tile_index * tm + lax.broadcasted_iota(I32, (tm, gw), 0)
    diffs = []
    for g, w in enumerate(POOL_WINDOWS):
        cols = slice(g * gw, (g + 1) * gw)
        cnt = jnp.minimum(t + 1, w).astype(F32)
        diffs.append(sums[g][POOL_HALO:, cols] / cnt - u[:, cols])
    return diffs


def _pool_weight(wp_ref, g):
    r = wp_ref.shape[1] // len(POOL_WINDOWS)
    return jnp.concatenate([wp_ref[j, g * r:(g + 1) * r, :] for j in range(N_CHIPS)], axis=0)


def _pool_fwd(u, gp, ca, w_pool, ps, l, job, store):
    S = u.shape[0]
    tm = min(S, 256)
    hb = tm // POOL_HALO
    gw = POOL_W // len(POOL_WINDOWS)

    def body(u_ref, uh_ref, gp_ref, ca_ref, wp_ref, ps_ref, c_ref):
        i = pl.program_id(0)
        uv = u_ref[...]
        halo = jnp.where(i > 0, uh_ref[...], 0.0)
        diffs = _pool_diff(uv, halo, i, tm)
        c_ref[:, 0:ATTN_W] = ca_ref[...]
        for g in range(len(POOL_WINDOWS)):
            cols = slice(g * gw, (g + 1) * gw)
            mm = _nn(diffs[g].astype(BF16), _pool_weight(wp_ref, g))
            gpv = gp_ref[:, cols]
            b = (mm * ps_ref[:, cols]) * (gpv * _sigmoid(gpv))
            c_ref[:, ATTN_W + g * gw:ATTN_W + (g + 1) * gw] = b.astype(BF16)

    row = pl.BlockSpec((tm, POOL_W), lambda i: (i, 0))
    return _call(
        body, name="pool_fwd", grid=(S // tm,),
        in_specs=[row, pl.BlockSpec((POOL_HALO, POOL_W), lambda i: (jnp.maximum(i * hb - 1, 0), 0)), row, row,
                  pl.BlockSpec(w_pool.shape, lambda i: (0, 0, 0)),
                  pl.BlockSpec((None, 1, POOL_W), lambda i: (l, 0, 0))],
        out_specs=[pl.BlockSpec((tm, ATTN_W + POOL_W), lambda i: (i, 0))],
        out_shape=[jax.ShapeDtypeStruct((S, ATTN_W + POOL_W), BF16)],
        args=(u, u, gp, ca, w_pool, ps), sem=("parallel",), job=job, store=store)[0]


def _ple_embed(p_ref, wple_ref):
    pb = p_ref[...].astype(BF16)
    return jnp.concatenate([_nn(pb, wple_ref[j]) for j in range(N_CHIPS)], axis=1)


def _outproj_ln(c, x, gl, p, w_out, w_ple, gain, bias, l, alpha, job, store):
    S, D = x.shape
    tm = min(S, 128)
    kq = D // N_CHIPS

    def body(c_ref, x_ref, gl_ref, p_ref, wo_ref, wp_ref, gain_ref, bias_ref, y_ref, yb_ref, xh_ref, rs_ref):
        mix = _nn(c_ref[:, 0:kq], wo_ref[0])
        for j in range(1, N_CHIPS):
            mix = mix + _nn(c_ref[:, j * kq:(j + 1) * kq], wo_ref[j])
        ple = _sigmoid(gl_ref[...]) * _ple_embed(p_ref, wp_ref)
        z = (alpha * x_ref[...] + mix) + ple
        mu = jnp.mean(z, axis=1, keepdims=True)
        zc = z - mu
        var = jnp.mean(zc * zc, axis=1, keepdims=True)
        rstd = lax.rsqrt(var + LN_EPS)
        xhat = zc * rstd
        y = xhat * gain_ref[...] + bias_ref[...]
        y_ref[...] = y
        yb_ref[...] = y.astype(BF16)
        xh_ref[...] = xhat
        rs_ref[...] = rstd

    row = pl.BlockSpec((tm, D), lambda i: (i, 0))
    vec = pl.BlockSpec((None, 1, D), lambda i: (l, 0, 0))
    return _call(
        body, name="outproj_ln", grid=(S // tm,),
        in_specs=[row, row, row, pl.BlockSpec((tm, p.shape[1]), lambda i: (i, 0)),
                  pl.BlockSpec(w_out.shape, lambda i: (0, 0, 0)), pl.BlockSpec(w_ple.shape, lambda i: (0, 0, 0)), vec, vec],
        out_specs=[row, row, row, pl.BlockSpec((tm, 1), lambda i: (i, 0))],
        out_shape=[jax.ShapeDtypeStruct((S, D), F32), jax.ShapeDtypeStruct((S, D), BF16), jax.ShapeDtypeStruct((S, D), F32),
                   jax.ShapeDtypeStruct((S, 1), F32)],
        args=(c, x, gl, p, w_out, w_ple, gain, bias), sem=("parallel",), job=job, store=store)


def _loss_and_grad(y, target):
    S, D = y.shape
    tm = min(S, 512)

    def body(y_ref, t_ref, dy_ref, acc_ref):
        @pl.when(pl.program_id(0) == 0)
        def _():
            acc_ref[...] = jnp.zeros_like(acc_ref)
        d = y_ref[...] - t_ref[...]
        dy_ref[...] = d * (1.0 / D)
        acc_ref[...] += jnp.sum(jnp.mean(d * d, axis=1, keepdims=True), axis=0, keepdims=True)

    row = pl.BlockSpec((tm, D), lambda i: (i, 0))
    return _pcall(body, name="loss", grid=(S // tm,), in_specs=[row, row],
                  out_specs=[row, pl.BlockSpec((8, LANES), lambda i: (0, 0))],
                  out_shape=[jax.ShapeDtypeStruct((S, D), F32), jax.ShapeDtypeStruct((8, LANES), F32)],
                  compiler_params=_cparams(("arbitrary",)))(y, target)


def _ln_bwd(dy, xhat, rstd, gl, p, w_ple, gain, l, job, store):
    S, D = dy.shape
    tm = min(S, 256)

    def body(dy_ref, xh_ref, rs_ref, gl_ref, p_ref, wp_ref, gain_ref, dz_ref, dzb_ref, de_ref, dgl_ref, gg_ref, gb_ref):
        @pl.when(pl.program_id(0) == 0)
        def _():
            gg_ref[...] = jnp.zeros_like(gg_ref)
            gb_ref[...] = jnp.zeros_like(gb_ref)
        dyv = dy_ref[...]
        xh = xh_ref[...]
        dxh = dyv * gain_ref[...]
        m1 = jnp.mean(dxh, axis=1, keepdims=True)
        m2 = jnp.mean(dxh * xh, axis=1, keepdims=True)
        dz = rs_ref[...] * ((dxh - m1) - xh * m2)
        gg_ref[...] += jnp.sum(dyv * xh, axis=0, keepdims=True)
        gb_ref[...] += jnp.sum(dyv, axis=0, keepdims=True)
        sg = _sigmoid(gl_ref[...])
        e = _ple_embed(p_ref, wp_ref)
        dz_ref[...] = dz
        dzb_ref[...] = dz.astype(BF16)
        de_ref[...] = (dz * sg).astype(BF16)
        dgl_ref[...] = ((dz * e) * (sg * (1.0 - sg))).astype(BF16)

    row = pl.BlockSpec((tm, D), lambda i: (i, 0))
    vec_in = pl.BlockSpec((None, 1, D), lambda i: (l, 0, 0))
    vec_out = pl.BlockSpec((1, D), lambda i: (0, 0))
    bsd = jax.ShapeDtypeStruct((S, D), BF16)
    return _call(
        body, name="ln_bwd", grid=(S // tm,),
        in_specs=[row, row, pl.BlockSpec((tm, 1), lambda i: (i, 0)), row, pl.BlockSpec((tm, p.shape[1]), lambda i: (i, 0)),
                  pl.BlockSpec(w_ple.shape, lambda i: (0, 0, 0)), vec_in],
        out_specs=[row, row, row, row, vec_out, vec_out],
        out_shape=[jax.ShapeDtypeStruct((S, D), F32), bsd, bsd, bsd, jax.ShapeDtypeStruct((1, D), F32), jax.ShapeDtypeStruct((1, D), F32)],
        args=(dy, xhat, rstd, gl, p, w_ple, gain), sem=("arbitrary",), job=job, store=store)


def _matmul_nn_acc(a, b4, name, add, add_scale, job, store):
    S = a.shape[0]
    KS, tk, N = b4.shape
    tm, tn = min(S, 512), min(N, 512)

    def body(a_ref, b_ref, add_ref, o_ref):
        acc = add_scale * add_ref[...]
        for k in range(KS):
            acc = acc + _nn(a_ref[:, k * tk:(k + 1) * tk], b_ref[k])
        o_ref[...] = acc

    return _call(body, name=name, grid=(S // tm, N // tn),
                 in_specs=[pl.BlockSpec((tm, KS * tk), lambda i, n: (i, 0)),
                           pl.BlockSpec((KS, tk, tn), lambda i, n: (0, 0, n)),
                           pl.BlockSpec((tm, tn), lambda i, n: (i, n))],
                 out_specs=[pl.BlockSpec((tm, tn), lambda i, n: (i, n))], out_shape=[jax.ShapeDtypeStruct((S, N), F32)],
                 args=(a, b4, add), sem=("parallel", "parallel"), job=job, store=store)[0]


def _matmul_nt_rows(a, b4, name, add=None, add_scale=1.0, job=None, store=None):
    S, K = a.shape
    nq = b4.shape[1]
    tm = min(S, 512)
    out_spec = pl.BlockSpec((tm, nq), lambda j, i: (i, j))
    in_specs = [pl.BlockSpec((tm, K), lambda j, i: (i, 0)), pl.BlockSpec((None, nq, K), lambda j, i: (j, 0, 0))]
    if add is None:
        def body(a_ref, b_ref, o_ref):
            o_ref[...] = _nt(a_ref[...], b_ref[...])
        args = (a, b4)
    else:
        def body(a_ref, b_ref, add_ref, o_ref):
            o_ref[...] = _nt(a_ref[...], b_ref[...]) + add_scale * add_ref[...]
        in_specs.append(out_spec)
        args = (a, b4, add)

    return _call(body, name=name, grid=(N_CHIPS, S // tm), in_specs=in_specs,
                 out_specs=[out_spec], out_shape=[jax.ShapeDtypeStruct((S, N_CHIPS * nq), F32)],
                 args=args, sem=("parallel", "parallel"), job=job, store=store)[0]


def _matmul_tn(a, b, R, C, name, by_rows, job=None, store=None):
    S = a.shape[0]
    ts = min(S, 512)

    def body(a_ref, b_ref, o_ref):
        @pl.when(pl.program_id(1) == 0)
        def _():
            o_ref[...] = jnp.zeros_like(o_ref)
        o_ref[...] += _tn(a_ref[...], b_ref[...])

    if by_rows:
        a_spec = pl.BlockSpec((ts, R), lambda j, s: (s, j))
        b_spec = pl.BlockSpec((ts, C), lambda j, s: (s, 0))
    else:
        a_spec = pl.BlockSpec((ts, R), lambda j, s: (s, 0))
        b_spec = pl.BlockSpec((ts, C), lambda j, s: (s, j))
    return _call(body, name=name, grid=(N_CHIPS, S // ts), in_specs=[a_spec, b_spec],
                 out_specs=[pl.BlockSpec((None, R, C), lambda j, s: (j, 0, 0))],
                 out_shape=[jax.ShapeDtypeStruct((N_CHIPS, R, C), F32)],
                 args=(a, b), sem=("parallel", "arbitrary"), job=job, store=store)[0]


def _pool_bwd(u, gp, dc, w_pool, ps, l):
    S = u.shape[0]
    tm = min(S, 256)
    hb = tm // POOL_HALO
    ngrp = len(POOL_WINDOWS)
    gw = POOL_W // ngrp
    rr = gw // N_CHIPS

    def body(u_ref, uh_ref, gp_ref, dc_ref, wp_ref, ps_ref, dd_ref, dgp_ref, gps_ref, gwp_ref):
        i = pl.program_id(0)

        @pl.when(i == 0)
        def _():
            gps_ref[...] = jnp.zeros_like(gps_ref)
            gwp_ref[...] = jnp.zeros_like(gwp_ref)
        uv = u_ref[...]
        halo = jnp.where(i > 0, uh_ref[...], 0.0)
        diffs = _pool_diff(uv, halo, i, tm)
        for g in range(ngrp):
            cols = slice(g * gw, (g + 1) * gw)
            w = _pool_weight(wp_ref, g)
            db = diffs[g].astype(BF16)
            mm = _nn(db, w)
            gpv = gp_ref[:, cols]
            sg = _sigmoid(gpv)
            si = gpv * sg
            dsi = sg * (1.0 + gpv * (1.0 - sg))
            dcb = dc_ref[:, cols]
            psv = ps_ref[:, cols]
            d_mm = (dcb * si) * psv
            gps_ref[:, cols] += jnp.sum((dcb * si) * mm, axis=0, keepdims=True)
            dgp_ref[:, cols] = (dcb * (mm * psv)) * dsi
            d_mmb = d_mm.astype(BF16)
            dd_ref[:, cols] = _nt(d_mmb, w)
            gwt = _tn(db, d_mmb)
            for j in range(N_CHIPS):
                gwp_ref[j, g * rr:(g + 1) * rr, :] += gwt[j * rr:(j + 1) * rr, :]

    row = pl.BlockSpec((tm, POOL_W), lambda i: (i, 0))
    wspec = pl.BlockSpec(w_pool.shape, lambda i: (0, 0, 0))
    return _pcall(
        body, name="pool_bwd", grid=(S // tm,),
        in_specs=[row, pl.BlockSpec((POOL_HALO, POOL_W), lambda i: (jnp.maximum(i * hb - 1, 0), 0)), row,
                  pl.BlockSpec((tm, POOL_W), lambda i: (i, 1)), wspec, pl.BlockSpec((None, 1, POOL_W), lambda i: (l, 0, 0))],
        out_specs=[row, row, pl.BlockSpec((1, POOL_W), lambda i: (0, 0)), wspec],
        out_shape=[jax.ShapeDtypeStruct((S, POOL_W), F32), jax.ShapeDtypeStruct((S, POOL_W), F32),
                   jax.ShapeDtypeStruct((1, POOL_W), F32), jax.ShapeDtypeStruct(w_pool.shape, F32)],
        compiler_params=_cparams(("arbitrary",)),
    )(u, u, gp, dc, w_pool, ps)


def _pool_window_t(dd, halo_next, tile_index, tm):
    gw = POOL_W // len(POOL_WINDOWS)
    n = tm + POOL_HALO
    t = tile_index * tm + lax.broadcasted_iota(I32, (n, gw), 0)
    xh = jnp.concatenate([dd, halo_next], axis=0)
    outs = []
    for g, w in enumerate(POOL_WINDOWS):
        cols = slice(g * gw, (g + 1) * gw)
        cnt = jnp.minimum(t + 1, w).astype(F32)
        s = xh[:, cols] / cnt
        step = 1
        while step < w:
            s = s + pltpu.roll(s, n - step, axis=0)
            step *= 2
        outs.append(s[:tm] - dd[:, cols])
    return outs


def _assemble_dh(dq, dk, dv, dga, dd, dgp):
    S = dq.shape[0]
    tm = min(S, 256)
    hb = tm // POOL_HALO
    nt = S // tm

    def body(dq_ref, dk_ref, dv_ref, dga_ref, dd_ref, ddn_ref, dgp_ref, dh_ref, gb_ref):
        i = pl.program_id(0)

        @pl.when(i == 0)
        def _():
            gb_ref[...] = jnp.zeros_like(gb_ref)
        halo = jnp.where(i < nt - 1, ddn_ref[...], 0.0)
        du = jnp.concatenate(_pool_window_t(dd_ref[...], halo, i, tm), axis=1)
        dkv = dk_ref[...]
        dgav = dga_ref[...]
        parts = [(OFF_Q, dq_ref[...]), (OFF_KA, dkv), (OFF_KB, dkv), (OFF_V, dv_ref[...]), (OFF_GA, dgav),
                 (OFF_U, du[:, 0:896]), (OFF_UA, du[:, 896:1024]), (OFF_UB, du[:, 896:1024]), (OFF_GP, dgp_ref[...])]
        for off, val in parts:
            w = val.shape[1]
            dh_ref[:, off:off + w] = val.astype(BF16)
            gb_ref[:, off:off + w] += jnp.sum(val, axis=0, keepdims=True)

    def row(w):
        return pl.BlockSpec((tm, w), lambda i: (i, 0))

    return _pcall(
        body, name="assemble_dh", grid=(nt,),
        in_specs=[row(ATTN_W), row(KV_W), row(KV_W), row(ATTN_W), row(POOL_W),
                  pl.BlockSpec((POOL_HALO, POOL_W), lambda i: (jnp.minimum((i + 1) * hb, S // POOL_HALO - 1), 0)), row(POOL_W)],
        out_specs=[row(EXT), pl.BlockSpec((1, EXT), lambda i: (0, 0))],
        out_shape=[jax.ShapeDtypeStruct((S, EXT), BF16), jax.ShapeDtypeStruct((1, EXT), F32)],
        compiler_params=_cparams(("arbitrary",)),
    )(dq, dk, dv, dga, dd, dd, dgp)


def _attn_bwd(q, k, v, ga, o, dc, lse, bias, sinks, dbias_in, l, job, store):
    S = q.shape[0]
    nblk = S // WINDOW

    def body(sink_ref, q_ref, k_ref, v_ref, ga_ref, o_ref, dc_ref, lse_ref, bias_ref, dbin_ref,
             dq_ref, dga_ref, dk_ref, dv_ref, db_ref, ds_ref, kd, vd):
        n = pl.program_id(0)

        @pl.when(n == 0)
        def _():
            _dup_heads(k_ref, kd)
            _dup_heads(v_ref, vd)
            dk_ref[...] = jnp.zeros_like(dk_ref)
            dv_ref[...] = jnp.zeros_like(dv_ref)
            db_ref[...] = dbin_ref[...]
            ds_ref[...] = jnp.zeros_like(ds_ref)

        cur = pl.multiple_of(n * WINDOW, WINDOW)
        prev = pl.multiple_of(jnp.maximum(n - 1, 0) * WINDOW, WINDOW)
        lane = lax.broadcasted_iota(I32, (WINDOW, LANES), 1)
        lane8 = lax.broadcasted_iota(I32, (8, LANES), 1)
        lo = lane < HEAD_DIM
        kb = [_kv_block(kd, h, prev, cur) for h in range(N_KV)]
        vb = [_kv_block(vd, h, prev, cur) for h in range(N_KV)]
        lse_t = lse_ref[...]
        dk_t = [jnp.zeros((LANES, 2 * WINDOW), F32) for _ in range(N_KV)]
        dv_t = [jnp.zeros((LANES, 2 * WINDOW), F32) for _ in range(N_KV)]
        dsk = jnp.zeros((8, LANES), F32)
        for pair in range(N_HEADS // 2):
            sl = slice(LANES * pair, LANES * (pair + 1))
            qp = q_ref[:, sl]
            op = o_ref[:, sl]
            dcp = dc_ref[:, sl]
            gav = ga_ref[:, sl]
            sg = _sigmoid(gav)
            d_o = dcp * (gav * sg)
            dga_ref[:, sl] = (dcp * op) * (sg * (1.0 + gav * (1.0 - sg)))
            prod = d_o * op
            dqs = []
            for par in range(2):
                h = 2 * pair + par
                kvh = h // KV_RATIO
                msk = lo if par == 0 else jnp.logical_not(lo)
                qm = jnp.where(msk, qp, jnp.zeros_like(qp))
                dom = jnp.where(msk, d_o, 0.0).astype(BF16)
                delta = jnp.sum(jnp.where(msk, prod, 0.0), axis=1, keepdims=True)
                lse_h = jnp.sum(jnp.where(lane == h, lse_t, 0.0), axis=1, keepdims=True)
                s = _nt(qm, kb[kvh]) + bias_ref[h]
                p = jnp.exp(s - lse_h)
                dp = _nt(dom, vb[kvh])
                dsc = p * (dp - delta)
                db_ref[h] += dsc
                psink = jnp.exp(sink_ref[l, h] - lse_h)
                dsk = dsk + jnp.where(lane8 == h, -jnp.sum(psink * delta, axis=0, keepdims=True), 0.0)
                dsb = dsc.astype(BF16)
                dqs.append(_nn(dsb, kb[kvh]) * QK_SCALE)
                dk_t[kvh] = dk_t[kvh] + _tn(qm, dsb)
                dv_t[kvh] = dv_t[kvh] + _tn(dom, p.astype(BF16))
            dq_ref[:, sl] = jnp.where(lo, dqs[0], dqs[1])

        def untranspose(acc):
            return jnp.concatenate([a[:HEAD_DIM] + a[HEAD_DIM:] for a in acc], axis=0).T

        dkb = untranspose(dk_t)
        dvb = untranspose(dv_t)
        dk_ref[pl.ds(prev, WINDOW), :] += dkb[:WINDOW]
        dk_ref[pl.ds(cur, WINDOW), :] += dkb[WINDOW:]
        dv_ref[pl.ds(prev, WINDOW), :] += dvb[:WINDOW]
        dv_ref[pl.ds(cur, WINDOW), :] += dvb[WINDOW:]
        ds_ref[...] += dsk

    blk = pl.BlockSpec((WINDOW, ATTN_W), lambda n: (n, 0))
    full_kv = pl.BlockSpec((S, KV_W), lambda n: (0, 0))
    full_b = pl.BlockSpec((N_HEADS, WINDOW, 2 * WINDOW), lambda n: (0, 0, 0))
    return _call(
        body, name="attn_bwd", grid=(nblk,),
        in_specs=[pl.BlockSpec(memory_space=pltpu.SMEM), blk, full_kv, full_kv, blk, blk, blk,
                  pl.BlockSpec((WINDOW, LANES), lambda n: (n, 0)),
                  pl.BlockSpec((None, N_HEADS, WINDOW, 2 * WINDOW), lambda n: (jnp.minimum(n, 1), 0, 0, 0)), full_b],
        out_specs=[blk, blk, full_kv, full_kv, full_b, pl.BlockSpec((8, LANES), lambda n: (0, 0))],
        out_shape=[jax.ShapeDtypeStruct((S, ATTN_W), F32), jax.ShapeDtypeStruct((S, ATTN_W), F32), jax.ShapeDtypeStruct((S, KV_W), F32),
                   jax.ShapeDtypeStruct((S, KV_W), F32), jax.ShapeDtypeStruct((N_HEADS, WINDOW, 2 * WINDOW), F32),
                   jax.ShapeDtypeStruct((8, LANES), F32)],
        scratch_shapes=[pltpu.VMEM((N_KV, S, LANES), BF16), pltpu.VMEM((N_KV, S, LANES), BF16)],
        args=(sinks, q, k, v, ga, o, dc, lse, bias, dbias_in), sem=("arbitrary",), job=job, store=store)


def _pack_small(arrs):
    flat = []
    for a in arrs:
        v = a.reshape(-1)
        flat.append(jnp.pad(v, (0, (-v.shape[0]) % LANES)))
    v = jnp.concatenate(flat)
    v = jnp.pad(v, (0, (-v.shape[0]) % (8 * LANES)))
    return v.reshape(-1, LANES)


def _unpack_small(packed, shapes):
    v = packed.reshape(-1)
    outs, off = [], 0
    for shp in shapes:
        n = math.prod(shp)
        outs.append(v[off:off + n].reshape(shp))
        off += n + (-n) % LANES
    return outs


def _bias_to_ext(b):
    L = b.shape[0]
    z = jnp.zeros((L, HALF_TILE), b.dtype)
    parts = []
    for j in range(N_CHIPS):
        seg = b[:, j * SHARD:(j + 1) * SHARD]
        parts += [z, seg] if j % 2 else [seg, z]
    return jnp.concatenate(parts, axis=1).reshape(L, 1, EXT)


def _bias_from_ext(g):
    parts = []
    for j in range(N_CHIPS):
        o = j * SHARD_P + (HALF_TILE if j % 2 else 0)
        parts.append(g[:, o:o + SHARD])
    return jnp.concatenate(parts, axis=1)


def kernel(x, p, w_in, b_in, w_out, attn_sinks, rel_bias, w_pool, pool_scale, w_ple, w_gate_ple, ln_gain, ln_bias, loss_target, m_w_in, m_b_in, m_w_out, m_attn_sinks, m_rel_bias, m_w_pool, m_pool_scale, m_w_ple, m_w_gate_ple, m_ln_gain, m_ln_bias, v_w_in, v_b_in, v_w_out, v_attn_sinks, v_rel_bias, v_w_pool, v_pool_scale, v_w_ple, v_w_gate_ple, v_ln_gain, v_ln_bias):
    L = w_in.shape[0]
    S, D = x.shape[1], x.shape[2]
    assert D == D_MODEL and w_in.shape[2] == SHARD and S % WINDOW == 0
    alpha = (2.0 * L) ** 0.25
    xc, yc, cc = _mesh_pos()
    idx = jnp.stack([2 * xc + yc, yc, cc]).astype(I32)
    store = {}

    def wkeys(l, names):
        return [("w", l, t) for t in names]

    def weight(l, t):
        return _unhalves(store["w", l, t])

    w_in_t = jnp.swapaxes(w_in, 1, 2)
    w_pool2 = w_pool.reshape(L, w_pool.shape[1] * w_pool.shape[2], w_pool.shape[3])
    for l in range(L):
        store["w", l, "in"] = _halves(_piece_in(w_in_t, l, idx))
        store["w", l, "out"] = _halves(_piece(w_out, l, idx, "piece_out"))
        store["w", l, "gate"] = _halves(_piece(w_gate_ple, l, idx, "piece_gate"))
        store["w", l, "ple"] = _halves(_piece(w_ple, l, idx, "piece_ple"))
        store["w", l, "pool"] = _halves(_piece(w_pool2, l, idx, "piece_pool"))
    _allgather_now(wkeys(0, WEIGHTS), store)

    b_ext = _bias_to_ext(b_in)
    ps3 = pool_scale.reshape(L, 1, POOL_W)
    gain3 = ln_gain.reshape(L, 1, D)
    bias3 = ln_bias.reshape(L, 1, D)
    onehot = _bucket_onehot()
    bias_hqk = _masked_bias(_bias_fwd(rel_bias.T, onehot).reshape(N_HEADS, WINDOW, 2 * WINDOW))

    xs = x[0]
    xb = _to_bf16(xs)
    saved = []
    for l in range(L):
        nxt = l + 1 < L
        job = _Job()
        if nxt:
            job.add(_ag_ici, wkeys(l + 1, ["in"]), 3)
        if l >= 1:
            job.add(_ag_fwd, wkeys(l, ["gate"]), 3)
        q, k, v, ga, u, gp = _inproj(xb, weight(l, "in"), b_ext, l, job, store)
        job = _Job().add(_ag_fwd, wkeys(l + 1, ["in"]), 3) if nxt else None
        gl = _matmul_nn(xb, weight(l, "gate"), "gate_logits", job, store)
        job = _Job().add(_ag_ici, wkeys(l + 1, ["out", "ple", "pool"]), 9) if nxt else None
        o, ca, lse = _attn_fwd(q, k, v, ga, bias_hqk, attn_sinks, l, job, store)
        job = _Job().add(_ag_fwd, wkeys(l + 1, ["out", "ple", "pool"]), 9) if nxt else None
        c = _pool_fwd(u, gp, ca, weight(l, "pool"), ps3, l, job, store)
        job = _Job().add(_ag_ici, wkeys(l + 1, ["gate"]), 3) if nxt else None
        y, yb, xhat, rstd = _outproj_ln(c, xs, gl, p[l, 0], weight(l, "out"), weight(l, "ple"), gain3, bias3, l, alpha, job, store)
        saved.append(dict(xb=xb, q=q, k=k, v=v, ga=ga, u=u, gp=gp, gl=gl, o=o, lse=lse, c=c, xhat=xhat, rstd=rstd))
        xs, xb = y, yb

    dy, loss_acc = _loss_and_grad(xs, loss_target[0])
    loss = lax.psum(0.5 * loss_acc[0, 0], ("x", "y", "c"))

    shapes = {t: store["w", 0, t].shape for t in WEIGHTS}
    for t in WEIGHTS:
        store["full", t] = lax.empty((L,) + shapes[t][1:], F32)

    def rs_keys(kind, l, names):
        return [(kind, l, t) for t in names]

    def rs_pair_job(l, names):
        for t in names:
            store["ra", l, t] = lax.empty((N_CHIPS,) + shapes[t][2:], F32)
        return _Job().add(_rs_pair, rs_keys("g", l, names) + rs_keys("ra", l, names), len(names))

    def rs_pair_add(l, names):
        for t in names:
            p32, p16 = _rs_pair_add(store["g", l, t], store["ra", l, t], idx, "rs_pair_add_" + t)
            store["p32", l, t], store["p16", l, t] = p32, p16
            store["rb", l, t] = lax.empty((3,) + shapes[t][2:], BF16)

    def rs_ici_job(l, names):
        return _Job().add(_rs_ici, rs_keys("p16", l, names) + rs_keys("rb", l, names), 3 * len(names))

    def rs_chip_add(l, names):
        for t in names:
            store["full", t] = _rs_chip_add(store["p32", l, t], store["rb", l, t], store["full", t], l, idx, "rs_chip_add_" + t)

    def rs_share_job(l):
        return _Job().add(functools.partial(_rs_share, layer=l), [("full", t) for t in WEIGHTS], len(WEIGHTS))

    dbias = jnp.zeros((N_HEADS, WINDOW, 2 * WINDOW), F32)
    small = [None] * L
    rest = ["out", "gate", "ple", "pool"]
    for l in reversed(range(L)):
        sv = saved[l]
        pl_l = p[l, 0]
        job = rs_share_job(l + 1) if l + 1 < L else None
        dz, dzb, d_e, d_gl, ggain, gbias = _ln_bwd(dy, sv["xhat"], sv["rstd"], sv["gl"], pl_l, weight(l, "ple"), gain3, l, job, store)
        dc = _matmul_nt_rows(dzb, weight(l, "out"), "d_mix_in")
        g_out = _matmul_tn(sv["c"], dzb, D // N_CHIPS, D, "grad_w_out", by_rows=True)
        g_ple = _matmul_tn(_to_bf16(pl_l), d_e, pl_l.shape[1], D // N_CHIPS, "grad_w_ple", by_rows=False)
        g_gate = _matmul_tn(sv["xb"], d_gl, D // N_CHIPS, D, "grad_w_gate", by_rows=True)
        dd, dgp, gps, g_pool = _pool_bwd(sv["u"], sv["gp"], dc, weight(l, "pool"), ps3, l)
        for t, g in zip(rest, (g_out, g_gate, g_ple, g_pool)):
            store["g", l, t] = _halves(g)
        dq, dga, dk, dv, dbias, dsink = _attn_bwd(sv["q"], sv["k"], sv["v"], sv["ga"], sv["o"], dc, sv["lse"], bias_hqk,
                                                  attn_sinks, dbias, l, rs_pair_job(l, rest), store)
        rs_pair_add(l, rest)
        dh, gbe = _assemble_dh(dq, dk, dv, dga, dd, dgp)
        g_in = _matmul_tn(dh, sv["xb"], SHARD_P, D, "grad_w_in", by_rows=True, job=rs_ici_job(l, rest), store=store)
        rs_chip_add(l, rest)
        store["g", l, "in"] = _halves(g_in)
        dx1 = _matmul_nt_rows(d_gl, weight(l, "gate"), "d_x_gate", dz, alpha, job=rs_pair_job(l, ["in"]), store=store)
        rs_pair_add(l, ["in"])
        dy = _matmul_nn_acc(dh, weight(l, "in"), "d_x", dx1, 1.0, rs_ici_job(l, ["in"]), store)
        rs_chip_add(l, ["in"])
        small[l] = dict(b_in=_bias_from_ext(gbe)[0], sinks=dsink[0, :N_HEADS], ps=gps[0], gain=ggain[0], bias=gbias[0])
    grad_x = dy[None]

    _run_job("rs_pair_share", rs_share_job(0), store)
    full = {t: _unhalves(store["full", t]) for t in WEIGHTS}

    def t_back(a):
        return jnp.swapaxes(a, 1, 2)

    def pool4(a):
        return a.reshape(w_pool.shape)

    r_in = _adamw(w_in_t, full["in"], jnp.swapaxes(m_w_in, 1, 2), jnp.swapaxes(v_w_in, 1, 2), idx, "adamw_w_in", tr=HALF_TILE)
    gw_in, dw_in, nm_in, nv_in = [t_back(a) for a in r_in]
    gw_out, dw_out, nm_out, nv_out = _adamw(w_out, full["out"], m_w_out, v_w_out, idx, "adamw_w_out")
    gw_gate, dw_gate, nm_gate, nv_gate = _adamw(w_gate_ple, full["gate"], m_w_gate_ple, v_w_gate_ple, idx, "adamw_w_gate")
    gw_ple, dw_ple, nm_ple, nv_ple = _adamw(w_ple, full["ple"], m_w_ple, v_w_ple, idx, "adamw_w_ple")
    r_pool = _adamw(w_pool2, full["pool"], m_w_pool.reshape(w_pool2.shape), v_w_pool.reshape(w_pool2.shape), idx, "adamw_w_pool")
    gw_pool, dw_pool, nm_pool, nv_pool = [pool4(a) for a in r_pool]

    g_rel = _bias_bwd(dbias.reshape(N_HEADS, -1), onehot).T
    small_shapes = [b_in.shape, attn_sinks.shape, rel_bias.shape, pool_scale.shape, ln_gain.shape, ln_bias.shape]
    g_small = [jnp.stack([small[l]["b_in"] for l in range(L)]), jnp.stack([small[l]["sinks"] for l in range(L)]), g_rel,
               jnp.stack([small[l]["ps"] for l in range(L)]), jnp.stack([small[l]["gain"] for l in range(L)]),
               jnp.stack([small[l]["bias"] for l in range(L)])]
    packed = _small_allreduce_adamw(
        _pack_small(g_small),
        _pack_small([b_in, attn_sinks, rel_bias, pool_scale, ln_gain, ln_bias]),
        _pack_small([m_b_in, m_attn_sinks, m_rel_bias, m_pool_scale, m_ln_gain, m_ln_bias]),
        _pack_small([v_b_in, v_attn_sinks, v_rel_bias, v_pool_scale, v_ln_gain, v_ln_bias]))
    sg, sd, sm, sv_ = [_unpack_small(a, small_shapes) for a in packed]

    def order(big, sm_):
        return (big[0], sm_[0], big[1], sm_[1], sm_[2], big[2], sm_[3], big[3], big[4], sm_[4], sm_[5])

    return (loss, grad_x,
            *order((gw_in, gw_out, gw_pool, gw_ple, gw_gate), sg),
            *order((dw_in, dw_out, dw_pool, dw_ple, dw_gate), sd),
            *order((nm_in, nm_out, nm_pool, nm_ple, nm_gate), sm),
            *order((nv_in, nv_out, nv_pool, nv_ple, nv_gate), sv_))
```

```python
import functools
import math

import jax
import jax.numpy as jnp
from jax import lax
from jax.experimental import pallas as pl
from jax.experimental.pallas import tpu as pltpu

F32 = jnp.float32
BF16 = jnp.bfloat16
I32 = jnp.int32
MESH = pl.DeviceIdType.MESH

HEAD_DIM = 64
QK_SCALE = HEAD_DIM ** -0.5
WINDOW = 128
KV_RATIO = 8
POOL_WINDOWS = (2, 4, 8, 16)
POOL_HALO = 16
REL_BUCKETS = 32
REL_MAX_DIST = 128
LN_EPS = 1e-5
ADAM_LR, ADAM_B1, ADAM_B2, ADAM_EPS, ADAM_WD, ADAM_STEP = 0.001, 0.9, 0.999, 1e-08, 0.01, 10

LANES = 128
VMEM_LIMIT = 52 * 1024 * 1024
N_CHIPS = 4
N_DEV = 8

D_MODEL = 2048
ATTN_W = 1024
POOL_W = 1024
KV_W = 128
N_HEADS = ATTN_W // HEAD_DIM
N_KV = N_HEADS // KV_RATIO
IN_COLS = 4352
SHARD = IN_COLS // N_CHIPS
SHARD_P = 1152
EXT = N_CHIPS * SHARD_P
HALF_TILE = SHARD_P - SHARD
OFF_Q, OFF_KA, OFF_KB, OFF_V, OFF_GA, OFF_U, OFF_UA, OFF_UB, OFF_GP = 0, 1024, 1152, 1280, 1408, 2432, 3328, 3456, 3584
WEIGHTS = ("in", "out", "gate", "ple", "pool")


def _cparams(sem=None):
    return pltpu.CompilerParams(dimension_semantics=sem, vmem_limit_bytes=VMEM_LIMIT)


def _pcall(body, **kw):
    return pl.pallas_call(body, **kw)


def _sigmoid(x):
    return 1.0 / (1.0 + jnp.exp(-x))


def _nt(a, b):
    return lax.dot_general(a, b, (((1,), (1,)), ((), ())), preferred_element_type=F32)


def _tn(a, b):
    return lax.dot_general(a, b, (((0,), (0,)), ((), ())), preferred_element_type=F32)


def _nn(a, b):
    return jnp.dot(a, b, preferred_element_type=F32)


def _mesh_pos():
    return lax.axis_index("x"), lax.axis_index("y"), lax.axis_index("c")


def _peer_chips(x, y):
    return [(1 - x, y), (x, 1 - y), (1 - x, 1 - y)]


def _row_tile(rows, cap=256):
    t = min(rows, cap)
    while rows % t or t % 16:
        t -= 1
    return t


def _hbm_spec():
    return pl.BlockSpec(memory_space=pltpu.HBM)


def _halves(a):
    return a.reshape(a.shape[:-2] + (2, a.shape[-2] // 2, a.shape[-1]))


def _unhalves(a):
    return a.reshape(a.shape[:-3] + (2 * a.shape[-2], a.shape[-1]))


class _remote:
    def __init__(self, src, dst, send, recv, i, device):
        self.args = dict(src_ref=src, dst_ref=dst, send_sem=send.at[i], recv_sem=recv.at[i], device_id=device, device_id_type=MESH)

    def start(self):
        pltpu.make_async_remote_copy(**self.args).start()

    def wait_recv(self):
        pltpu.make_async_remote_copy(**self.args).wait_recv()

    def wait_send(self):
        pltpu.make_async_remote_copy(**self.args).wait_send()


class _Job:
    def __init__(self):
        self.keys, self.parts, self.n = [], [], 0

    def add(self, fn, keys, n):
        self.parts.append((fn, len(self.keys), len(keys), self.n))
        self.keys += list(keys)
        self.n += n
        return self

    def build(self, refs, send, recv):
        out = []
        for fn, i0, nb, base in self.parts:
            out += fn(refs[i0:i0 + nb], send, recv, base)
        return out


def _ag_ici(refs, send, recv, base):
    x, y, c = _mesh_pos()
    me = 2 * x + y
    out = []
    for t, g in enumerate(refs):
        for k, chip in enumerate(_peer_chips(x, y)):
            i = base + 3 * t + k
            dev = (*chip, c)
            out.append((_remote(g.at[me, c], g.at[me, c], send, recv, i, dev),
                        _remote(g.at[me, c], g.at[2 * chip[0] + chip[1], c], send, recv, i, dev)))
    return out


def _ag_fwd(refs, send, recv, base):
    x, y, c = _mesh_pos()
    out = []
    for t, g in enumerate(refs):
        for k, chip in enumerate(_peer_chips(x, y)):
            i = base + 3 * t + k
            slot = 2 * chip[0] + chip[1]
            dev = (x, y, 1 - c)
            out.append((_remote(g.at[slot, c], g.at[slot, c], send, recv, i, dev),
                        _remote(g.at[slot, c], g.at[slot, 1 - c], send, recv, i, dev)))
    return out


def _rs_pair(refs, send, recv, base):
    x, y, c = _mesh_pos()
    n = len(refs) // 2
    out = []
    for t in range(n):
        cp = _remote(refs[t].at[:, 1 - c], refs[n + t], send, recv, base + t, (x, y, 1 - c))
        out.append((cp, cp))
    return out


def _rs_ici(refs, send, recv, base):
    x, y, c = _mesh_pos()
    n = len(refs) // 2
    out = []
    for t in range(n):
        for k, chip in enumerate(_peer_chips(x, y)):
            cp = _remote(refs[t].at[2 * chip[0] + chip[1]], refs[n + t].at[k], send, recv, base + 3 * t + k, (*chip, c))
            out.append((cp, cp))
    return out


def _rs_share(refs, send, recv, base, layer):
    x, y, c = _mesh_pos()
    out = []
    for t, f in enumerate(refs):
        dev = (x, y, 1 - c)
        out.append((_remote(f.at[layer, c], f.at[layer, c], send, recv, base + t, dev),
                    _remote(f.at[layer, c], f.at[layer, 1 - c], send, recv, base + t, dev)))
    return out


def _call(body, *, name, grid, in_specs, out_specs, out_shape, args, scratch_shapes=(), sem=None, job=None, store=None):
    in_specs, out_specs, out_shape, scratch_shapes = list(in_specs), list(out_specs), list(out_shape), list(scratch_shapes)
    if job is None or job.n == 0:
        return list(_pcall(body, name=name, grid=grid, in_specs=in_specs, out_specs=out_specs, out_shape=out_shape,
                           scratch_shapes=scratch_shapes, compiler_params=_cparams(sem))(*args))
    bufs = [store[k] for k in job.keys]
    nb, n_in, n_out, n_sc = len(bufs), len(args), len(out_shape), len(scratch_shapes)

    def wrapped(*refs):
        ins = refs[:n_in]
        outs = refs[n_in + nb:n_in + nb + n_out]
        cb = refs[n_in + nb + n_out:n_in + 2 * nb + n_out]
        scratch = refs[n_in + 2 * nb + n_out:n_in + 2 * nb + n_out + n_sc]
        send, recv = refs[-2:]
        ids = [pl.program_id(a) for a in range(len(grid))]
        first = functools.reduce(jnp.logical_and, [i == 0 for i in ids])
        last = functools.reduce(jnp.logical_and, [i == g - 1 for i, g in zip(ids, grid)])

        @pl.when(first)
        def _():
            for s, _r in job.build(cb, send, recv):
                s.start()

        body(*ins, *outs, *scratch)

        @pl.when(last)
        def _():
            pairs = job.build(cb, send, recv)
            for _s, r in pairs:
                r.wait_recv()
            for s, _r in pairs:
                s.wait_send()

    res = _pcall(
        wrapped, name=name, grid=grid, in_specs=in_specs + [_hbm_spec()] * nb, out_specs=out_specs + [_hbm_spec()] * nb,
        out_shape=out_shape + [jax.ShapeDtypeStruct(b.shape, b.dtype) for b in bufs],
        scratch_shapes=scratch_shapes + [pltpu.SemaphoreType.DMA((job.n,)), pltpu.SemaphoreType.DMA((job.n,))],
        input_output_aliases={n_in + i: n_out + i for i in range(nb)},
        compiler_params=_cparams(("arbitrary",) * len(grid)))(*args, *bufs)
    for k, v in zip(job.keys, res[n_out:]):
        store[k] = v
    return list(res[:n_out])


def _run_job(name, job, store):
    bufs = [store[k] for k in job.keys]
    nb = len(bufs)

    def body(*refs):
        send, recv = refs[-2:]
        pairs = job.build(refs[nb:2 * nb], send, recv)
        for s, _r in pairs:
            s.start()
        for _s, r in pairs:
            r.wait_recv()
        for s, _r in pairs:
            s.wait_send()

    res = _pcall(body, name=name, in_specs=[_hbm_spec()] * nb, out_specs=[_hbm_spec()] * nb,
                 out_shape=[jax.ShapeDtypeStruct(b.shape, b.dtype) for b in bufs],
                 scratch_shapes=[pltpu.SemaphoreType.DMA((job.n,)), pltpu.SemaphoreType.DMA((job.n,))],
                 input_output_aliases={i: i for i in range(nb)})(*bufs)
    for k, v in zip(job.keys, res):
        store[k] = v


def _allgather_now(keys, store):
    bufs = [store[k] for k in keys]
    nb = len(bufs)

    def body(*refs):
        send, recv = refs[-2:]
        g = refs[nb:2 * nb]
        ici = _ag_ici(g, send, recv, 0)
        fwd = _ag_fwd(g, send, recv, 3 * nb)
        for s, _r in ici:
            s.start()
        for (_s, r), (fs, _fr) in zip(ici, fwd):
            r.wait_recv()
            fs.start()
        for _fs, fr in fwd:
            fr.wait_recv()
        for s, _r in ici + fwd:
            s.wait_send()

    res = _pcall(body, name="allgather_first_layer", in_specs=[_hbm_spec()] * nb, out_specs=[_hbm_spec()] * nb,
                 out_shape=[jax.ShapeDtypeStruct(b.shape, b.dtype) for b in bufs],
                 scratch_shapes=[pltpu.SemaphoreType.DMA((6 * nb,)), pltpu.SemaphoreType.DMA((6 * nb,))],
                 input_output_aliases={i: i for i in range(nb)})(*bufs)
    for k, v in zip(keys, res):
        store[k] = v


def _piece(w, l, idx, name):
    _, R, C = w.shape
    tr = _row_tile(R)

    def body(s_ref, w_ref, o_ref):
        del s_ref
        o_ref[...] = w_ref[...].astype(BF16)

    gs = pltpu.PrefetchScalarGridSpec(
        num_scalar_prefetch=1, grid=(R // tr,),
        in_specs=[pl.BlockSpec((None, tr, C), lambda r, s: (l, r, 0))],
        out_specs=pl.BlockSpec((None, tr, C), lambda r, s: (s[0], r, 0)))
    return _pcall(body, name=name, grid_spec=gs, out_shape=jax.ShapeDtypeStruct((N_CHIPS, R, C), BF16),
                  compiler_params=_cparams(("parallel",)))(idx, w)


def _piece_in(w_in_t, l, idx):
    _, R, D = w_in_t.shape
    tr = HALF_TILE
    nsrc = R // tr

    def body(s_ref, w_ref, o_ref):
        src = pl.program_id(0) - s_ref[1]
        ok = jnp.logical_and(src >= 0, src < nsrc)
        o_ref[...] = jnp.where(ok, w_ref[...], 0.0).astype(BF16)

    gs = pltpu.PrefetchScalarGridSpec(
        num_scalar_prefetch=1, grid=(SHARD_P // tr,),
        in_specs=[pl.BlockSpec((None, tr, D), lambda r, s: (l, jnp.clip(r - s[1], 0, nsrc - 1), 0))],
        out_specs=pl.BlockSpec((None, tr, D), lambda r, s: (s[0], r, 0)))
    return _pcall(body, name="piece_in", grid_spec=gs, out_shape=jax.ShapeDtypeStruct((N_CHIPS, SHARD_P, D), BF16),
                  compiler_params=_cparams(("parallel",)))(idx, w_in_t)


def _rs_pair_add(g5, r4, idx, name):
    J, _, h, C = g5.shape
    th = _row_tile(h)

    def body(s_ref, g_ref, r_ref, o32_ref, o16_ref):
        s = g_ref[...] + r_ref[...]
        o16_ref[...] = s.astype(BF16)

        @pl.when(pl.program_id(1) == s_ref[0])
        def _():
            o32_ref[...] = s

    spec = pl.BlockSpec((None, th, C), lambda r, j, s: (j, r, 0))
    gs = pltpu.PrefetchScalarGridSpec(
        num_scalar_prefetch=1, grid=(h // th, J),
        in_specs=[pl.BlockSpec((None, None, th, C), lambda r, j, s: (j, s[2], r, 0)), spec],
        out_specs=[pl.BlockSpec((th, C), lambda r, j, s: (r, 0)), spec])
    return _pcall(body, name=name, grid_spec=gs,
                  out_shape=[jax.ShapeDtypeStruct((h, C), F32), jax.ShapeDtypeStruct((J, h, C), BF16)],
                  compiler_params=_cparams(("parallel", "arbitrary")))(idx, g5, r4)


def _rs_chip_add(p32, r3, full, l, idx, name):
    h, C = p32.shape
    th = _row_tile(h)

    def body(s_ref, p_ref, r_ref, f_ref, o_ref):
        del s_ref, f_ref
        o_ref[...] = ((p_ref[...] + r_ref[0].astype(F32)) + r_ref[1].astype(F32)) + r_ref[2].astype(F32)

    gs = pltpu.PrefetchScalarGridSpec(
        num_scalar_prefetch=1, grid=(h // th,),
        in_specs=[pl.BlockSpec((th, C), lambda r, s: (r, 0)),
                  pl.BlockSpec((3, th, C), lambda r, s: (0, r, 0)),
                  pl.BlockSpec(memory_space=pl.ANY)],
        out_specs=pl.BlockSpec((None, None, th, C), lambda r, s: (l, s[2], r, 0)))
    return _pcall(body, name=name, grid_spec=gs, out_shape=jax.ShapeDtypeStruct(full.shape, F32),
                  input_output_aliases={3: 0}, compiler_params=_cparams(("parallel",)))(idx, p32, r3, full)


def _adamw_math(w, g, m, v):
    nm = ADAM_B1 * m + (1.0 - ADAM_B1) * g
    nv = ADAM_B2 * v + (1.0 - ADAM_B2) * (g * g)
    m_hat = nm / (1.0 - ADAM_B1 ** ADAM_STEP)
    v_hat = nv / (1.0 - ADAM_B2 ** ADAM_STEP)
    delta = -ADAM_LR * (m_hat / (jnp.sqrt(v_hat) + ADAM_EPS) + ADAM_WD * w)
    return delta, nm, nv


def _adamw(w, g, m, v, idx, name, tr=None):
    L, R, C = w.shape
    Rg = g.shape[1]
    tr = tr or _row_tile(R)
    shift = (Rg - R) // tr
    assert (Rg - R) % tr == 0

    def body(s_ref, w_ref, g_ref, m_ref, v_ref, go_ref, d_ref, nm_ref, nv_ref):
        del s_ref
        gv = g_ref[...]
        d, nm, nv = _adamw_math(w_ref[...], gv, m_ref[...], v_ref[...])
        go_ref[...] = gv
        d_ref[...] = d
        nm_ref[...] = nm
        nv_ref[...] = nv

    wspec = pl.BlockSpec((None, tr, C), lambda l, r, s: (l, r, 0))
    gs = pltpu.PrefetchScalarGridSpec(
        num_scalar_prefetch=1, grid=(L, R // tr),
        in_specs=[wspec, pl.BlockSpec((None, tr, C), lambda l, r, s: (l, r + shift * s[1], 0)), wspec, wspec],
        out_specs=[wspec, wspec, wspec, wspec])
    sds = jax.ShapeDtypeStruct((L, R, C), F32)
    return _pcall(body, name=name, grid_spec=gs, out_shape=[sds, sds, sds, sds],
                  compiler_params=_cparams(("parallel", "parallel")))(idx, w, g, m, v)


def _small_allreduce_adamw(gv, wv, mv, vv):
    NR = gv.shape[0]

    def body(g_ref, w_ref, m_ref, v_ref, go_ref, d_ref, nm_ref, nv_ref, gath, send, recv):
        x, y, c = _mesh_pos()
        rank = 4 * x + 2 * y + c
        gath[rank] = g_ref[...]
        cps = []
        for msk in range(1, N_DEV):
            bx, by, bc = (msk >> 2) & 1, (msk >> 1) & 1, msk & 1
            peer = (1 - x if bx else x, 1 - y if by else y, 1 - c if bc else c)
            cps.append(pltpu.make_async_remote_copy(src_ref=g_ref, dst_ref=gath.at[rank], send_sem=send.at[msk - 1],
                                                    recv_sem=recv.at[msk - 1], device_id=peer, device_id_type=MESH))
        for cp in cps:
            cp.start()
        for msk in range(1, N_DEV):
            bx, by, bc = (msk >> 2) & 1, (msk >> 1) & 1, msk & 1
            peer = (1 - x if bx else x, 1 - y if by else y, 1 - c if bc else c)
            prank = 4 * peer[0] + 2 * peer[1] + peer[2]
            pltpu.make_async_remote_copy(src_ref=g_ref, dst_ref=gath.at[prank], send_sem=send.at[msk - 1],
                                         recv_sem=recv.at[msk - 1], device_id=peer, device_id_type=MESH).wait_recv()
        for cp in cps:
            cp.wait_send()
        tot = gath[0]
        for r in range(1, N_DEV):
            tot = tot + gath[r]
        d, nm, nv = _adamw_math(w_ref[...], tot, m_ref[...], v_ref[...])
        go_ref[...] = tot
        d_ref[...] = d
        nm_ref[...] = nm
        nv_ref[...] = nv

    vm = pl.BlockSpec(memory_space=pltpu.VMEM)
    sds = jax.ShapeDtypeStruct((NR, LANES), F32)
    return _pcall(body, name="small_allreduce_adamw", in_specs=[vm, vm, vm, vm], out_specs=[vm, vm, vm, vm],
                  out_shape=[sds, sds, sds, sds],
                  scratch_shapes=[pltpu.VMEM((N_DEV, NR, LANES), F32), pltpu.SemaphoreType.DMA((N_DEV - 1,)),
                                  pltpu.SemaphoreType.DMA((N_DEV - 1,))],
                  compiler_params=pltpu.CompilerParams(vmem_limit_bytes=VMEM_LIMIT))(gv, wv, mv, vv)


def _split3(a):
    h1 = a.astype(BF16)
    r1 = a - h1.astype(F32)
    h2 = r1.astype(BF16)
    h3 = (r1 - h2.astype(F32)).astype(BF16)
    return h1, h2, h3


def _folded_dist():
    r = jnp.arange(WINDOW)[:, None]
    j = jnp.arange(WINDOW)[None, :]
    return jnp.where(j > r, r + WINDOW - j, r - j)


def _bucket_onehot():
    d = _folded_dist()
    max_exact = REL_BUCKETS // 2
    d_f = jnp.maximum(d, 1).astype(F32)
    large = max_exact + (jnp.log(d_f / max_exact) / math.log(REL_MAX_DIST / max_exact) * (REL_BUCKETS - max_exact)).astype(I32)
    large = jnp.minimum(large, REL_BUCKETS - 1)
    bucket = jnp.where(d < max_exact, d, large)
    oh = bucket[None] == jnp.arange(REL_BUCKETS)[:, None, None]
    return oh.reshape(REL_BUCKETS, WINDOW * WINDOW).astype(BF16)


def _bias_fwd(rel_bias_t, onehot):
    H, N = rel_bias_t.shape[0], onehot.shape[1]
    tn = 4096

    def body(t_ref, oh_ref, o_ref):
        h1, h2, h3 = _split3(t_ref[...])
        oh = oh_ref[...]
        o_ref[...] = (_nn(h1, oh) + _nn(h2, oh)) + _nn(h3, oh)

    return _pcall(body, name="bias_fwd", grid=(N // tn,),
                  in_specs=[pl.BlockSpec((H, REL_BUCKETS), lambda i: (0, 0)), pl.BlockSpec((REL_BUCKETS, tn), lambda i: (0, i))],
                  out_specs=pl.BlockSpec((H, tn), lambda i: (0, i)), out_shape=jax.ShapeDtypeStruct((H, N), F32),
                  compiler_params=_cparams(("parallel",)))(rel_bias_t, onehot)


def _bias_bwd(dbias, onehot):
    H, N = dbias.shape
    tn = 4096

    def body(d_ref, oh_ref, o_ref):
        @pl.when(pl.program_id(0) == 0)
        def _():
            o_ref[...] = jnp.zeros_like(o_ref)
        h1, h2, h3 = _split3(d_ref[...])
        oh = oh_ref[...]
        o_ref[...] += (_nt(h1, oh) + _nt(h2, oh)) + _nt(h3, oh)

    return _pcall(body, name="bias_bwd", grid=(N // tn,),
                  in_specs=[pl.BlockSpec((H, tn), lambda i: (0, i)), pl.BlockSpec((REL_BUCKETS, tn), lambda i: (0, i))],
                  out_specs=pl.BlockSpec((H, REL_BUCKETS), lambda i: (0, 0)), out_shape=jax.ShapeDtypeStruct((H, REL_BUCKETS), F32),
                  compiler_params=_cparams(("arbitrary",)))(dbias, onehot)


def _to_bf16(x):
    S, D = x.shape
    tm = min(S, 512)

    def body(x_ref, o_ref):
        o_ref[...] = x_ref[...].astype(BF16)

    return _pcall(body, name="to_bf16", grid=(S // tm,), in_specs=[pl.BlockSpec((tm, D), lambda i: (i, 0))],
                  out_specs=pl.BlockSpec((tm, D), lambda i: (i, 0)), out_shape=jax.ShapeDtypeStruct((S, D), BF16),
                  compiler_params=_cparams(("parallel",)))(x)


def _inproj(xb, w_t, b_ext, l, job, store):
    S, D = xb.shape
    tm = min(S, 512)

    def body(x_ref, w_ref, b_ref, q_ref, k_ref, v_ref, ga_ref, u_ref, gp_ref):
        j = pl.program_id(1)
        acc = _nt(x_ref[...], w_ref[...]) + b_ref[...]

        @pl.when(j == 0)
        def _():
            q_ref[...] = (acc[:, :1024] * QK_SCALE).astype(BF16)
            k_ref[...] = acc[:, 1024:1152]

        @pl.when(j == 1)
        def _():
            k_ref[...] += acc[:, 0:128]
            v_ref[...] = acc[:, 128:256]
            ga_ref[:, 0:896] = acc[:, 256:1152]

        @pl.when(j == 2)
        def _():
            ga_ref[:, 896:1024] = acc[:, 0:128]
            u_ref[:, 0:896] = acc[:, 128:1024]
            u_ref[:, 896:1024] = acc[:, 1024:1152]

        @pl.when(j == 3)
        def _():
            u_ref[:, 896:1024] += acc[:, 0:128]
            gp_ref[...] = acc[:, 128:1152]

    def ospec(w):
        return pl.BlockSpec((tm, w), lambda i, j: (i, 0))

    return _call(
        body, name="inproj", grid=(S // tm, N_CHIPS),
        in_specs=[pl.BlockSpec((tm, D), lambda i, j: (i, 0)),
                  pl.BlockSpec((None, SHARD_P, D), lambda i, j: (j, 0, 0)),
                  pl.BlockSpec((None, 1, SHARD_P), lambda i, j: (l, 0, j))],
        out_specs=[ospec(ATTN_W), ospec(KV_W), ospec(KV_W), ospec(ATTN_W), ospec(POOL_W), ospec(POOL_W)],
        out_shape=[jax.ShapeDtypeStruct((S, ATTN_W), BF16), jax.ShapeDtypeStruct((S, KV_W), F32), jax.ShapeDtypeStruct((S, KV_W), F32),
                   jax.ShapeDtypeStruct((S, ATTN_W), F32), jax.ShapeDtypeStruct((S, POOL_W), F32), jax.ShapeDtypeStruct((S, POOL_W), F32)],
        args=(xb, w_t, b_ext), sem=("parallel", "arbitrary"), job=job, store=store)


def _matmul_nn(a, b4, name, job, store):
    S, K = a.shape
    N = b4.shape[2]
    kq = b4.shape[1]
    tm, tn = min(S, 512), min(N, 1024)

    def body(a_ref, b_ref, o_ref):
        acc = _nn(a_ref[:, 0:kq], b_ref[0])
        for j in range(1, N_CHIPS):
            acc = acc + _nn(a_ref[:, j * kq:(j + 1) * kq], b_ref[j])
        o_ref[...] = acc

    return _call(body, name=name, grid=(N // tn, S // tm),
                 in_specs=[pl.BlockSpec((tm, K), lambda n, i: (i, 0)), pl.BlockSpec((N_CHIPS, kq, tn), lambda n, i: (0, 0, n))],
                 out_specs=[pl.BlockSpec((tm, tn), lambda n, i: (i, n))], out_shape=[jax.ShapeDtypeStruct((S, N), F32)],
                 args=(a, b4), sem=("parallel", "parallel"), job=job, store=store)[0]


def _masked_bias(bias):
    r = jnp.arange(WINDOW)[:, None]
    j = jnp.arange(WINDOW)[None, :]
    return jnp.stack([jnp.where(j > r, -1e30, bias), bias])


def _fold(full, tri):
    return jnp.where(tri, full[:, :WINDOW], full[:, WINDOW:])


def _unfold(folded, tri):
    return jnp.concatenate([jnp.where(tri, folded, 0.0).astype(BF16), jnp.where(tri, 0.0, folded).astype(BF16)], axis=1)


def _dup_heads(src_ref, dst_ref):
    a = src_ref[...]
    r = pltpu.roll(a, HEAD_DIM, axis=1)
    lo = lax.broadcasted_iota(I32, a.shape, 1) < HEAD_DIM
    dst_ref[0] = jnp.where(lo, a, r).astype(BF16)
    dst_ref[1] = jnp.where(lo, r, a).astype(BF16)


def _kv_block(ref, h, prev, cur):
    return jnp.concatenate([ref[h, pl.ds(prev, WINDOW), :], ref[h, pl.ds(cur, WINDOW), :]], axis=0)


def _rows(i):
    return slice(i * WINDOW, (i + 1) * WINDOW)


def _attn_fwd(q, k, v, ga, bias, sinks, l, job, store):
    S = q.shape[0]
    nblk = S // WINDOW

    def body(sink_ref, q_ref, k_ref, v_ref, ga_ref, bias_ref, o_ref, ca_ref, lse_ref, kd, vd, qs, ss, ps, os_):
        n = pl.program_id(0)

        @pl.when(n == 0)
        def _():
            _dup_heads(k_ref, kd)
            _dup_heads(v_ref, vd)

        cur = pl.multiple_of(n * WINDOW, WINDOW)
        prev = pl.multiple_of(jnp.maximum(n - 1, 0) * WINDOW, WINDOW)
        lane = lax.broadcasted_iota(I32, (WINDOW, LANES), 1)
        lo = lane < HEAD_DIM
        tri = lane > lax.broadcasted_iota(I32, (WINDOW, LANES), 0)
        lse_mat = jnp.zeros((WINDOW, LANES), F32)
        for g in range(N_KV):
            for i in range(KV_RATIO):
                h = g * KV_RATIO + i
                qp = q_ref[:, LANES * (h // 2):LANES * (h // 2 + 1)]
                qs[_rows(i), :] = jnp.where(lo if h % 2 == 0 else jnp.logical_not(lo), qp, jnp.zeros_like(qp))
            ss[...] = _nt(qs[...], _kv_block(kd, g, prev, cur))
            for i in range(KV_RATIO):
                h = g * KV_RATIO + i
                s = _fold(ss[_rows(i), :], tri) + bias_ref[h]
                sink = sink_ref[l, h]
                m = jnp.maximum(jnp.max(s, axis=1, keepdims=True), sink)
                e = jnp.exp(s - m)
                den = jnp.sum(e, axis=1, keepdims=True) + jnp.exp(sink - m)
                ps[_rows(i), :] = _unfold(e * (1.0 / den), tri)
                lse_mat = jnp.where(lane == h, m + jnp.log(den), lse_mat)
            os_[...] = _nn(ps[...], _kv_block(vd, g, prev, cur))
            for j in range(KV_RATIO // 2):
                pair = g * (KV_RATIO // 2) + j
                sl = slice(LANES * pair, LANES * (pair + 1))
                o_pair = jnp.where(lo, os_[_rows(2 * j), :], os_[_rows(2 * j + 1), :])
                o_ref[:, sl] = o_pair
                gav = ga_ref[:, sl]
                ca_ref[:, sl] = (o_pair * (gav * _sigmoid(gav))).astype(BF16)
        lse_ref[...] = lse_mat

    blk = pl.BlockSpec((WINDOW, ATTN_W), lambda n: (n, 0))
    full_kv = pl.BlockSpec((S, KV_W), lambda n: (0, 0))
    stack = KV_RATIO * WINDOW
    return _call(
        body, name="attn_fwd", grid=(nblk,),
        in_specs=[pl.BlockSpec(memory_space=pltpu.SMEM), blk, full_kv, full_kv, blk,
                  pl.BlockSpec((None, N_HEADS, WINDOW, WINDOW), lambda n: (jnp.minimum(n, 1), 0, 0, 0))],
        out_specs=[blk, blk, pl.BlockSpec((WINDOW, LANES), lambda n: (n, 0))],
        out_shape=[jax.ShapeDtypeStruct((S, ATTN_W), F32), jax.ShapeDtypeStruct((S, ATTN_W), BF16), jax.ShapeDtypeStruct((S, LANES), F32)],
        scratch_shapes=[pltpu.VMEM((N_KV, S, LANES), BF16), pltpu.VMEM((N_KV, S, LANES), BF16),
                        pltpu.VMEM((stack, LANES), BF16), pltpu.VMEM((stack, 2 * WINDOW), F32),
                        pltpu.VMEM((stack, 2 * WINDOW), BF16), pltpu.VMEM((stack, LANES), F32)],
        args=(sinks, q, k, v, ga, bias), sem=("arbitrary",), job=job, store=store)


def _pool_diff(u, halo, tile_index, tm):
    gw = POOL_W // len(POOL_WINDOWS)
    xh = jnp.concatenate([halo, u], axis=0)
    sums = []
    s = xh
    for step in (1, 2, 4, 8):
        s = s + pltpu.roll(s, step, axis=0)
        sums.append(s)
    t = tile_index * tm + lax.broadcasted_iota(I32, (tm, gw), 0)
    diffs = []
    for g, w in enumerate(POOL_WINDOWS):
        cols = slice(g * gw, (g + 1) * gw)
        cnt = jnp.minimum(t + 1, w).astype(F32)
        diffs.append(sums[g][POOL_HALO:, cols] / cnt - u[:, cols])
    return diffs


def _pool_weight(wp_ref, g):
    r = wp_ref.shape[1] // len(POOL_WINDOWS)
    return jnp.concatenate([wp_ref[j, g * r:(g + 1) * r, :] for j in range(N_CHIPS)], axis=0)


def _pool_fwd(u, gp, ca, w_pool, ps, l, job, store):
    S = u.shape[0]
    tm = min(S, 256)
    hb = tm // POOL_HALO
    gw = POOL_W // len(POOL_WINDOWS)

    def body(u_ref, uh_ref, gp_ref, ca_ref, wp_ref, ps_ref, c_ref):
        i = pl.program_id(0)
        uv = u_ref[...]
        halo = jnp.where(i > 0, uh_ref[...], 0.0)
        diffs = _pool_diff(uv, halo, i, tm)
        c_ref[:, 0:ATTN_W] = ca_ref[...]
        for g in range(len(POOL_WINDOWS)):
            cols = slice(g * gw, (g + 1) * gw)
            mm = _nn(diffs[g].astype(BF16), _pool_weight(wp_ref, g))
            gpv = gp_ref[:, cols]
            b = (mm * ps_ref[:, cols]) * (gpv * _sigmoid(gpv))
            c_ref[:, ATTN_W + g * gw:ATTN_W + (g + 1) * gw] = b.astype(BF16)

    row = pl.BlockSpec((tm, POOL_W), lambda i: (i, 0))
    return _call(
        body, name="pool_fwd", grid=(S // tm,),
        in_specs=[row, pl.BlockSpec((POOL_HALO, POOL_W), lambda i: (jnp.maximum(i * hb - 1, 0), 0)), row, row,
                  pl.BlockSpec(w_pool.shape, lambda i: (0, 0, 0)),
                  pl.BlockSpec((None, 1, POOL_W), lambda i: (l, 0, 0))],
        out_specs=[pl.BlockSpec((tm, ATTN_W + POOL_W), lambda i: (i, 0))],
        out_shape=[jax.ShapeDtypeStruct((S, ATTN_W + POOL_W), BF16)],
        args=(u, u, gp, ca, w_pool, ps), sem=("parallel",), job=job, store=store)[0]


def _ple_embed(p_ref, wple_ref):
    pb = p_ref[...].astype(BF16)
    return jnp.concatenate([_nn(pb, wple_ref[j]) for j in range(N_CHIPS)], axis=1)


def _outproj_ln(c, x, gl, p, w_out, w_ple, gain, bias, l, alpha, job, store):
    S, D = x.shape
    tm = min(S, 256)
    kq = D // N_CHIPS

    def body(c_ref, x_ref, gl_ref, p_ref, wo_ref, wp_ref, gain_ref, bias_ref, y_ref, yb_ref, xh_ref, rs_ref):
        mix = _nn(c_ref[:, 0:kq], wo_ref[0])
        for j in range(1, N_CHIPS):
            mix = mix + _nn(c_ref[:, j * kq:(j + 1) * kq], wo_ref[j])
        ple = _sigmoid(gl_ref[...]) * _ple_embed(p_ref, wp_ref)
        z = (alpha * x_ref[...] + mix) + ple
        mu = jnp.mean(z, axis=1, keepdims=True)
        zc = z - mu
        var = jnp.mean(zc * zc, axis=1, keepdims=True)
        rstd = lax.rsqrt(var + LN_EPS)
        xhat = zc * rstd
        y = xhat * gain_ref[...] + bias_ref[...]
        y_ref[...] = y
        yb_ref[...] = y.astype(BF16)
        xh_ref[...] = xhat
        rs_ref[...] = rstd

    row = pl.BlockSpec((tm, D), lambda i: (i, 0))
    vec = pl.BlockSpec((None, 1, D), lambda i: (l, 0, 0))
    return _call(
        body, name="outproj_ln", grid=(S // tm,),
        in_specs=[row, row, row, pl.BlockSpec((tm, p.shape[1]), lambda i: (i, 0)),
                  pl.BlockSpec(w_out.shape, lambda i: (0, 0, 0)), pl.BlockSpec(w_ple.shape, lambda i: (0, 0, 0)), vec, vec],
        out_specs=[row, row, row, pl.BlockSpec((tm, 1), lambda i: (i, 0))],
        out_shape=[jax.ShapeDtypeStruct((S, D), F32), jax.ShapeDtypeStruct((S, D), BF16), jax.ShapeDtypeStruct((S, D), F32),
                   jax.ShapeDtypeStruct((S, 1), F32)],
        args=(c, x, gl, p, w_out, w_ple, gain, bias), sem=("parallel",), job=job, store=store)


def _loss_and_grad(y, target):
    S, D = y.shape
    tm = min(S, 512)

    def body(y_ref, t_ref, dy_ref, acc_ref):
        @pl.when(pl.program_id(0) == 0)
        def _():
            acc_ref[...] = jnp.zeros_like(acc_ref)
        d = y_ref[...] - t_ref[...]
        dy_ref[...] = d * (1.0 / D)
        acc_ref[...] += jnp.sum(jnp.mean(d * d, axis=1, keepdims=True), axis=0, keepdims=True)

    row = pl.BlockSpec((tm, D), lambda i: (i, 0))
    return _pcall(body, name="loss", grid=(S // tm,), in_specs=[row, row],
                  out_specs=[row, pl.BlockSpec((8, LANES), lambda i: (0, 0))],
                  out_shape=[jax.ShapeDtypeStruct((S, D), F32), jax.ShapeDtypeStruct((8, LANES), F32)],
                  compiler_params=_cparams(("arbitrary",)))(y, target)


def _ln_bwd(dy, xhat, rstd, gl, p, w_ple, gain, l, job, store):
    S, D = dy.shape
    tm = min(S, 256)

    def body(dy_ref, xh_ref, rs_ref, gl_ref, p_ref, wp_ref, gain_ref, dz_ref, dzb_ref, de_ref, dgl_ref, gg_ref, gb_ref):
        @pl.when(pl.program_id(0) == 0)
        def _():
            gg_ref[...] = jnp.zeros_like(gg_ref)
            gb_ref[...] = jnp.zeros_like(gb_ref)
        dyv = dy_ref[...]
        xh = xh_ref[...]
        dxh = dyv * gain_ref[...]
        m1 = jnp.mean(dxh, axis=1, keepdims=True)
        m2 = jnp.mean(dxh * xh, axis=1, keepdims=True)
        dz = rs_ref[...] * ((dxh - m1) - xh * m2)
        gg_ref[...] += jnp.sum(dyv * xh, axis=0, keepdims=True)
        gb_ref[...] += jnp.sum(dyv, axis=0, keepdims=True)
        sg = _sigmoid(gl_ref[...])
        e = _ple_embed(p_ref, wp_ref)
        dz_ref[...] = dz
        dzb_ref[...] = dz.astype(BF16)
        de_ref[...] = (dz * sg).astype(BF16)
        dgl_ref[...] = ((dz * e) * (sg * (1.0 - sg))).astype(BF16)

    row = pl.BlockSpec((tm, D), lambda i: (i, 0))
    vec_in = pl.BlockSpec((None, 1, D), lambda i: (l, 0, 0))
    vec_out = pl.BlockSpec((1, D), lambda i: (0, 0))
    bsd = jax.ShapeDtypeStruct((S, D), BF16)
    return _call(
        body, name="ln_bwd", grid=(S // tm,),
        in_specs=[row, row, pl.BlockSpec((tm, 1), lambda i: (i, 0)), row, pl.BlockSpec((tm, p.shape[1]), lambda i: (i, 0)),
                  pl.BlockSpec(w_ple.shape, lambda i: (0, 0, 0)), vec_in],
        out_specs=[row, row, row, row, vec_out, vec_out],
        out_shape=[jax.ShapeDtypeStruct((S, D), F32), bsd, bsd, bsd, jax.ShapeDtypeStruct((1, D), F32), jax.ShapeDtypeStruct((1, D), F32)],
        args=(dy, xhat, rstd, gl, p, w_ple, gain), sem=("arbitrary",), job=job, store=store)


def _matmul_nn_acc(a, b4, name, add, add_scale, job, store):
    S = a.shape[0]
    KS, tk, N = b4.shape
    tm, tn = min(S, 512), min(N, 512)

    def body(a_ref, b_ref, add_ref, o_ref):
        acc = add_scale * add_ref[...]
        for k in range(KS):
            acc = acc + _nn(a_ref[:, k * tk:(k + 1) * tk], b_ref[k])
        o_ref[...] = acc

    return _call(body, name=name, grid=(S // tm, N // tn),
                 in_specs=[pl.BlockSpec((tm, KS * tk), lambda i, n: (i, 0)),
                           pl.BlockSpec((KS, tk, tn), lambda i, n: (0, 0, n)),
                           pl.BlockSpec((tm, tn), lambda i, n: (i, n))],
                 out_specs=[pl.BlockSpec((tm, tn), lambda i, n: (i, n))], out_shape=[jax.ShapeDtypeStruct((S, N), F32)],
                 args=(a, b4, add), sem=("parallel", "parallel"), job=job, store=store)[0]


def _matmul_nt_rows(a, b4, name, add=None, add_scale=1.0, job=None, store=None):
    S, K = a.shape
    nq = b4.shape[1]
    tm = min(S, 512)
    out_spec = pl.BlockSpec((tm, nq), lambda j, i: (i, j))
    in_specs = [pl.BlockSpec((tm, K), lambda j, i: (i, 0)), pl.BlockSpec((None, nq, K), lambda j, i: (j, 0, 0))]
    if add is None:
        def body(a_ref, b_ref, o_ref):
            o_ref[...] = _nt(a_ref[...], b_ref[...])
        args = (a, b4)
    else:
        def body(a_ref, b_ref, add_ref, o_ref):
            o_ref[...] = _nt(a_ref[...], b_ref[...]) + add_scale * add_ref[...]
        in_specs.append(out_spec)
        args = (a, b4, add)

    return _call(body, name=name, grid=(N_CHIPS, S // tm), in_specs=in_specs,
                 out_specs=[out_spec], out_shape=[jax.ShapeDtypeStruct((S, N_CHIPS * nq), F32)],
                 args=args, sem=("parallel", "parallel"), job=job, store=store)[0]


def _matmul_tn(a, b, R, C, name, by_rows, job=None, store=None):
    S = a.shape[0]
    ts = min(S, 512)

    def body(a_ref, b_ref, o_ref):
        @pl.when(pl.program_id(1) == 0)
        def _():
            o_ref[...] = jnp.zeros_like(o_ref)
        o_ref[...] += _tn(a_ref[...], b_ref[...])

    if by_rows:
        a_spec = pl.BlockSpec((ts, R), lambda j, s: (s, j))
        b_spec = pl.BlockSpec((ts, C), lambda j, s: (s, 0))
    else:
        a_spec = pl.BlockSpec((ts, R), lambda j, s: (s, 0))
        b_spec = pl.BlockSpec((ts, C), lambda j, s: (s, j))
    return _call(body, name=name, grid=(N_CHIPS, S // ts), in_specs=[a_spec, b_spec],
                 out_specs=[pl.BlockSpec((None, R, C), lambda j, s: (j, 0, 0))],
                 out_shape=[jax.ShapeDtypeStruct((N_CHIPS, R, C), F32)],
                 args=(a, b), sem=("parallel", "arbitrary"), job=job, store=store)[0]


def _pool_bwd(u, gp, dc, w_pool, ps, l):
    S = u.shape[0]
    tm = min(S, 256)
    hb = tm // POOL_HALO
    ngrp = len(POOL_WINDOWS)
    gw = POOL_W // ngrp
    rr = gw // N_CHIPS

    def body(u_ref, uh_ref, gp_ref, dc_ref, wp_ref, ps_ref, dd_ref, dgp_ref, gps_ref, gwp_ref):
        i = pl.program_id(0)

        @pl.when(i == 0)
        def _():
            gps_ref[...] = jnp.zeros_like(gps_ref)
            gwp_ref[...] = jnp.zeros_like(gwp_ref)
        uv = u_ref[...]
        halo = jnp.where(i > 0, uh_ref[...], 0.0)
        diffs = _pool_diff(uv, halo, i, tm)
        for g in range(ngrp):
            cols = slice(g * gw, (g + 1) * gw)
            w = _pool_weight(wp_ref, g)
            db = diffs[g].astype(BF16)
            mm = _nn(db, w)
            gpv = gp_ref[:, cols]
            sg = _sigmoid(gpv)
            si = gpv * sg
            dsi = sg * (1.0 + gpv * (1.0 - sg))
            dcb = dc_ref[:, cols]
            psv = ps_ref[:, cols]
            d_mm = (dcb * si) * psv
            gps_ref[:, cols] += jnp.sum((dcb * si) * mm, axis=0, keepdims=True)
            dgp_ref[:, cols] = (dcb * (mm * psv)) * dsi
            d_mmb = d_mm.astype(BF16)
            dd_ref[:, cols] = _nt(d_mmb, w)
            gwt = _tn(db, d_mmb)
            for j in range(N_CHIPS):
                gwp_ref[j, g * rr:(g + 1) * rr, :] += gwt[j * rr:(j + 1) * rr, :]

    row = pl.BlockSpec((tm, POOL_W), lambda i: (i, 0))
    wspec = pl.BlockSpec(w_pool.shape, lambda i: (0, 0, 0))
    return _pcall(
        body, name="pool_bwd", grid=(S // tm,),
        in_specs=[row, pl.BlockSpec((POOL_HALO, POOL_W), lambda i: (jnp.maximum(i * hb - 1, 0), 0)), row,
                  pl.BlockSpec((tm, POOL_W), lambda i: (i, 1)), wspec, pl.BlockSpec((None, 1, POOL_W), lambda i: (l, 0, 0))],
        out_specs=[row, row, pl.BlockSpec((1, POOL_W), lambda i: (0, 0)), wspec],
        out_shape=[jax.ShapeDtypeStruct((S, POOL_W), F32), jax.ShapeDtypeStruct((S, POOL_W), F32),
                   jax.ShapeDtypeStruct((1, POOL_W), F32), jax.ShapeDtypeStruct(w_pool.shape, F32)],
        compiler_params=_cparams(("arbitrary",)),
    )(u, u, gp, dc, w_pool, ps)


def _pool_window_t(dd, halo_next, tile_index, tm):
    gw = POOL_W // len(POOL_WINDOWS)
    n = tm + POOL_HALO
    t = tile_index * tm + lax.broadcasted_iota(I32, (n, gw), 0)
    xh = jnp.concatenate([dd, halo_next], axis=0)
    outs = []
    for g, w in enumerate(POOL_WINDOWS):
        cols = slice(g * gw, (g + 1) * gw)
        cnt = jnp.minimum(t + 1, w).astype(F32)
        s = xh[:, cols] / cnt
        step = 1
        while step < w:
            s = s + pltpu.roll(s, n - step, axis=0)
            step *= 2
        outs.append(s[:tm] - dd[:, cols])
    return outs


def _assemble_dh(dq, dk, dv, dga, dd, dgp):
    S = dq.shape[0]
    tm = min(S, 256)
    hb = tm // POOL_HALO
    nt = S // tm

    def body(dq_ref, dk_ref, dv_ref, dga_ref, dd_ref, ddn_ref, dgp_ref, dh_ref, gb_ref):
        i = pl.program_id(0)

        @pl.when(i == 0)
        def _():
            gb_ref[...] = jnp.zeros_like(gb_ref)
        halo = jnp.where(i < nt - 1, ddn_ref[...], 0.0)
        du = jnp.concatenate(_pool_window_t(dd_ref[...], halo, i, tm), axis=1)
        dkv = dk_ref[...]
        dgav = dga_ref[...]
        parts = [(OFF_Q, dq_ref[...]), (OFF_KA, dkv), (OFF_KB, dkv), (OFF_V, dv_ref[...]), (OFF_GA, dgav),
                 (OFF_U, du[:, 0:896]), (OFF_UA, du[:, 896:1024]), (OFF_UB, du[:, 896:1024]), (OFF_GP, dgp_ref[...])]
        for off, val in parts:
            w = val.shape[1]
            dh_ref[:, off:off + w] = val.astype(BF16)
            gb_ref[:, off:off + w] += jnp.sum(val, axis=0, keepdims=True)

    def row(w):
        return pl.BlockSpec((tm, w), lambda i: (i, 0))

    return _pcall(
        body, name="assemble_dh", grid=(nt,),
        in_specs=[row(ATTN_W), row(KV_W), row(KV_W), row(ATTN_W), row(POOL_W),
                  pl.BlockSpec((POOL_HALO, POOL_W), lambda i: (jnp.minimum((i + 1) * hb, S // POOL_HALO - 1), 0)), row(POOL_W)],
        out_specs=[row(EXT), pl.BlockSpec((1, EXT), lambda i: (0, 0))],
        out_shape=[jax.ShapeDtypeStruct((S, EXT), BF16), jax.ShapeDtypeStruct((1, EXT), F32)],
        compiler_params=_cparams(("arbitrary",)),
    )(dq, dk, dv, dga, dd, dd, dgp)


def _attn_bwd(q, k, v, ga, o, dc, lse, bias, sinks, dbias_in, l, job, store):
    S = q.shape[0]
    nblk = S // WINDOW

    def body(sink_ref, q_ref, k_ref, v_ref, ga_ref, o_ref, dc_ref, lse_ref, bias_ref, dbin_ref,
             dq_ref, dga_ref, dk_ref, dv_ref, db_ref, ds_ref, kd, vd, qs, dos, dls, ss, dps, dss, pss, dqs):
        n = pl.program_id(0)

        @pl.when(n == 0)
        def _():
            _dup_heads(k_ref, kd)
            _dup_heads(v_ref, vd)
            dk_ref[...] = jnp.zeros_like(dk_ref)
            dv_ref[...] = jnp.zeros_like(dv_ref)
            db_ref[...] = dbin_ref[...]
            ds_ref[...] = jnp.zeros_like(ds_ref)

        cur = pl.multiple_of(n * WINDOW, WINDOW)
        prev = pl.multiple_of(jnp.maximum(n - 1, 0) * WINDOW, WINDOW)
        lane = lax.broadcasted_iota(I32, (WINDOW, LANES), 1)
        lane8 = lax.broadcasted_iota(I32, (8, LANES), 1)
        lo = lane < HEAD_DIM
        tri = lane > lax.broadcasted_iota(I32, (WINDOW, LANES), 0)
        lse_t = lse_ref[...]
        dk_t, dv_t = [], []
        dsk = jnp.zeros((8, LANES), F32)
        for g in range(N_KV):
            kb = _kv_block(kd, g, prev, cur)
            vb = _kv_block(vd, g, prev, cur)
            for j in range(KV_RATIO // 2):
                pair = g * (KV_RATIO // 2) + j
                sl = slice(LANES * pair, LANES * (pair + 1))
                qp = q_ref[:, sl]
                op = o_ref[:, sl]
                dcp = dc_ref[:, sl]
                gav = ga_ref[:, sl]
                sg = _sigmoid(gav)
                d_o = dcp * (gav * sg)
                dga_ref[:, sl] = (dcp * op) * (sg * (1.0 + gav * (1.0 - sg)))
                prod = d_o * op
                for par in range(2):
                    msk = lo if par == 0 else jnp.logical_not(lo)
                    rows = _rows(2 * j + par)
                    qs[rows, :] = jnp.where(msk, qp, jnp.zeros_like(qp))
                    dos[rows, :] = jnp.where(msk, d_o, 0.0).astype(BF16)
                    delta = jnp.sum(jnp.where(msk, prod, 0.0), axis=1, keepdims=True)
                    dls[rows, :] = jnp.broadcast_to(delta, (WINDOW, LANES))
            ss[...] = _nt(qs[...], kb)
            dps[...] = _nt(dos[...], vb)
            for i in range(KV_RATIO):
                h = g * KV_RATIO + i
                rows = _rows(i)
                lse_h = jnp.sum(jnp.where(lane == h, lse_t, 0.0), axis=1, keepdims=True)
                p = jnp.exp(_fold(ss[rows, :], tri) + bias_ref[h] - lse_h)
                delta = dls[rows, :]
                dsc = p * (_fold(dps[rows, :], tri) - delta)
                db_ref[h] += dsc
                psink = jnp.exp(sink_ref[l, h] - lse_h)
                dsk = dsk + jnp.where(lane8 == h, -jnp.sum(psink * delta, axis=0, keepdims=True), 0.0)
                dss[rows, :] = _unfold(dsc, tri)
                pss[rows, :] = _unfold(p, tri)
            dqs[...] = _nn(dss[...], kb) * QK_SCALE
            for j in range(KV_RATIO // 2):
                pair = g * (KV_RATIO // 2) + j
                dq_ref[:, LANES * pair:LANES * (pair + 1)] = jnp.where(lo, dqs[_rows(2 * j), :], dqs[_rows(2 * j + 1), :])
            dk_t.append(_tn(qs[...], dss[...]))
            dv_t.append(_tn(dos[...], pss[...]))

        def untranspose(acc):
            return jnp.concatenate([a[:HEAD_DIM] + a[HEAD_DIM:] for a in acc], axis=0).T

        dkb = untranspose(dk_t)
        dvb = untranspose(dv_t)
        dk_ref[pl.ds(prev, WINDOW), :] += dkb[:WINDOW]
        dk_ref[pl.ds(cur, WINDOW), :] += dkb[WINDOW:]
        dv_ref[pl.ds(prev, WINDOW), :] += dvb[:WINDOW]
        dv_ref[pl.ds(cur, WINDOW), :] += dvb[WINDOW:]
        ds_ref[...] += dsk

    blk = pl.BlockSpec((WINDOW, ATTN_W), lambda n: (n, 0))
    full_kv = pl.BlockSpec((S, KV_W), lambda n: (0, 0))
    full_b = pl.BlockSpec((N_HEADS, WINDOW, WINDOW), lambda n: (0, 0, 0))
    stack = KV_RATIO * WINDOW
    return _call(
        body, name="attn_bwd", grid=(nblk,),
        in_specs=[pl.BlockSpec(memory_space=pltpu.SMEM), blk, full_kv, full_kv, blk, blk, blk,
                  pl.BlockSpec((WINDOW, LANES), lambda n: (n, 0)),
                  pl.BlockSpec((None, N_HEADS, WINDOW, WINDOW), lambda n: (jnp.minimum(n, 1), 0, 0, 0)), full_b],
        out_specs=[blk, blk, full_kv, full_kv, full_b, pl.BlockSpec((8, LANES), lambda n: (0, 0))],
        out_shape=[jax.ShapeDtypeStruct((S, ATTN_W), F32), jax.ShapeDtypeStruct((S, ATTN_W), F32), jax.ShapeDtypeStruct((S, KV_W), F32),
                   jax.ShapeDtypeStruct((S, KV_W), F32), jax.ShapeDtypeStruct((N_HEADS, WINDOW, WINDOW), F32),
                   jax.ShapeDtypeStruct((8, LANES), F32)],
        scratch_shapes=[pltpu.VMEM((N_KV, S, LANES), BF16), pltpu.VMEM((N_KV, S, LANES), BF16),
                        pltpu.VMEM((stack, LANES), BF16), pltpu.VMEM((stack, LANES), BF16), pltpu.VMEM((stack, LANES), F32),
                        pltpu.VMEM((stack, 2 * WINDOW), F32), pltpu.VMEM((stack, 2 * WINDOW), F32),
                        pltpu.VMEM((stack, 2 * WINDOW), BF16), pltpu.VMEM((stack, 2 * WINDOW), BF16),
                        pltpu.VMEM((stack, LANES), F32)],
        args=(sinks, q, k, v, ga, o, dc, lse, bias, dbias_in), sem=("arbitrary",), job=job, store=store)


def _pack_small(arrs):
    flat = []
    for a in arrs:
        v = a.reshape(-1)
        flat.append(jnp.pad(v, (0, (-v.shape[0]) % LANES)))
    v = jnp.concatenate(flat)
    v = jnp.pad(v, (0, (-v.shape[0]) % (8 * LANES)))
    return v.reshape(-1, LANES)


def _unpack_small(packed, shapes):
    v = packed.reshape(-1)
    outs, off = [], 0
    for shp in shapes:
        n = math.prod(shp)
        outs.append(v[off:off + n].reshape(shp))
        off += n + (-n) % LANES
    return outs


def _bias_to_ext(b):
    L = b.shape[0]
    z = jnp.zeros((L, HALF_TILE), b.dtype)
    parts = []
    for j in range(N_CHIPS):
        seg = b[:, j * SHARD:(j + 1) * SHARD]
        parts += [z, seg] if j % 2 else [seg, z]
    return jnp.concatenate(parts, axis=1).reshape(L, 1, EXT)


def _bias_from_ext(g):
    parts = []
    for j in range(N_CHIPS):
        o = j * SHARD_P + (HALF_TILE if j % 2 else 0)
        parts.append(g[:, o:o + SHARD])
    return jnp.concatenate(parts, axis=1)


def kernel(x, p, w_in, b_in, w_out, attn_sinks, rel_bias, w_pool, pool_scale, w_ple, w_gate_ple, ln_gain, ln_bias, loss_target, m_w_in, m_b_in, m_w_out, m_attn_sinks, m_rel_bias, m_w_pool, m_pool_scale, m_w_ple, m_w_gate_ple, m_ln_gain, m_ln_bias, v_w_in, v_b_in, v_w_out, v_attn_sinks, v_rel_bias, v_w_pool, v_pool_scale, v_w_ple, v_w_gate_ple, v_ln_gain, v_ln_bias):
    L = w_in.shape[0]
    S, D = x.shape[1], x.shape[2]
    assert D == D_MODEL and w_in.shape[2] == SHARD and S % WINDOW == 0
    alpha = (2.0 * L) ** 0.25
    xc, yc, cc = _mesh_pos()
    idx = jnp.stack([2 * xc + yc, yc, cc]).astype(I32)
    store = {}

    def wkeys(l, names):
        return [("w", l, t) for t in names]

    def weight(l, t):
        return _unhalves(store["w", l, t])

    w_in_t = jnp.swapaxes(w_in, 1, 2)
    w_pool2 = w_pool.reshape(L, w_pool.shape[1] * w_pool.shape[2], w_pool.shape[3])
    for l in range(L):
        store["w", l, "in"] = _halves(_piece_in(w_in_t, l, idx))
        store["w", l, "out"] = _halves(_piece(w_out, l, idx, "piece_out"))
        store["w", l, "gate"] = _halves(_piece(w_gate_ple, l, idx, "piece_gate"))
        store["w", l, "ple"] = _halves(_piece(w_ple, l, idx, "piece_ple"))
        store["w", l, "pool"] = _halves(_piece(w_pool2, l, idx, "piece_pool"))
    _allgather_now(wkeys(0, WEIGHTS), store)

    b_ext = _bias_to_ext(b_in)
    ps3 = pool_scale.reshape(L, 1, POOL_W)
    gain3 = ln_gain.reshape(L, 1, D)
    bias3 = ln_bias.reshape(L, 1, D)
    onehot = _bucket_onehot()
    bias_hqk = _masked_bias(_bias_fwd(rel_bias.T, onehot).reshape(N_HEADS, WINDOW, WINDOW))

    xs = x[0]
    xb = _to_bf16(xs)
    pb = _to_bf16(p.reshape(L * S, p.shape[3])).reshape(L, S, p.shape[3])
    saved = []
    for l in range(L):
        nxt = l + 1 < L
        job = _Job()
        if nxt:
            job.add(_ag_ici, wkeys(l + 1, ["in"]), 3)
        if l >= 1:
            job.add(_ag_fwd, wkeys(l, ["gate"]), 3)
        q, k, v, ga, u, gp = _inproj(xb, weight(l, "in"), b_ext, l, job, store)
        job = _Job().add(_ag_fwd, wkeys(l + 1, ["in"]), 3) if nxt else None
        gl = _matmul_nn(xb, weight(l, "gate"), "gate_logits", job, store)
        job = _Job().add(_ag_ici, wkeys(l + 1, ["out", "ple", "pool"]), 9) if nxt else None
        o, ca, lse = _attn_fwd(q, k, v, ga, bias_hqk, attn_sinks, l, job, store)
        job = _Job().add(_ag_fwd, wkeys(l + 1, ["out", "ple", "pool"]), 9) if nxt else None
        c = _pool_fwd(u, gp, ca, weight(l, "pool"), ps3, l, job, store)
        job = _Job().add(_ag_ici, wkeys(l + 1, ["gate"]), 3) if nxt else None
        y, yb, xhat, rstd = _outproj_ln(c, xs, gl, pb[l], weight(l, "out"), weight(l, "ple"), gain3, bias3, l, alpha, job, store)
        saved.append(dict(xb=xb, q=q, k=k, v=v, ga=ga, u=u, gp=gp, gl=gl, o=o, lse=lse, c=c, xhat=xhat, rstd=rstd))
        xs, xb = y, yb

    dy, loss_acc = _loss_and_grad(xs, loss_target[0])
    loss = lax.psum(0.5 * loss_acc[0, 0], ("x", "y", "c"))

    shapes = {t: store["w", 0, t].shape for t in WEIGHTS}
    for t in WEIGHTS:
        store["full", t] = lax.empty((L,) + shapes[t][1:], F32)

    def rs_keys(kind, l, names):
        return [(kind, l, t) for t in names]

    def rs_pair_job(l, names):
        for t in names:
            store["ra", l, t] = lax.empty((N_CHIPS,) + shapes[t][2:], F32)
        return _Job().add(_rs_pair, rs_keys("g", l, names) + rs_keys("ra", l, names), len(names))

    def rs_pair_add(l, names):
        for t in names:
            p32, p16 = _rs_pair_add(store["g", l, t], store["ra", l, t], idx, "rs_pair_add_" + t)
            store["p32", l, t], store["p16", l, t] = p32, p16
            store["rb", l, t] = lax.empty((3,) + shapes[t][2:], BF16)

    def rs_ici_job(l, names):
        return _Job().add(_rs_ici, rs_keys("p16", l, names) + rs_keys("rb", l, names), 3 * len(names))

    def rs_chip_add(l, names):
        for t in names:
            store["full", t] = _rs_chip_add(store["p32", l, t], store["rb", l, t], store["full", t], l, idx, "rs_chip_add_" + t)

    def rs_share_job(l):
        return _Job().add(functools.partial(_rs_share, layer=l), [("full", t) for t in WEIGHTS], len(WEIGHTS))

    dbias = jnp.zeros((N_HEADS, WINDOW, WINDOW), F32)
    small = [None] * L
    rest = ["out", "gate", "ple", "pool"]
    for l in reversed(range(L)):
        sv = saved[l]
        pl_l = pb[l]
        job = rs_share_job(l + 1) if l + 1 < L else None
        dz, dzb, d_e, d_gl, ggain, gbias = _ln_bwd(dy, sv["xhat"], sv["rstd"], sv["gl"], pl_l, weight(l, "ple"), gain3, l, job, store)
        dc = _matmul_nt_rows(dzb, weight(l, "out"), "d_mix_in")
        g_out = _matmul_tn(sv["c"], dzb, D // N_CHIPS, D, "grad_w_out", by_rows=True)
        g_ple = _matmul_tn(pl_l, d_e, pl_l.shape[1], D // N_CHIPS, "grad_w_ple", by_rows=False)
        g_gate = _matmul_tn(sv["xb"], d_gl, D // N_CHIPS, D, "grad_w_gate", by_rows=True)
        dd, dgp, gps, g_pool = _pool_bwd(sv["u"], sv["gp"], dc, weight(l, "pool"), ps3, l)
        for t, g in zip(rest, (g_out, g_gate, g_ple, g_pool)):
            store["g", l, t] = _halves(g)
        dq, dga, dk, dv, dbias, dsink = _attn_bwd(sv["q"], sv["k"], sv["v"], sv["ga"], sv["o"], dc, sv["lse"], bias_hqk,
                                                  attn_sinks, dbias, l, rs_pair_job(l, rest), store)
        rs_pair_add(l, rest)
        dh, gbe = _assemble_dh(dq, dk, dv, dga, dd, dgp)
        g_in = _matmul_tn(dh, sv["xb"], SHARD_P, D, "grad_w_in", by_rows=True, job=rs_ici_job(l, rest), store=store)
        rs_chip_add(l, rest)
        store["g", l, "in"] = _halves(g_in)
        dx1 = _matmul_nt_rows(d_gl, weight(l, "gate"), "d_x_gate", dz, alpha, job=rs_pair_job(l, ["in"]), store=store)
        rs_pair_add(l, ["in"])
        dy = _matmul_nn_acc(dh, weight(l, "in"), "d_x", dx1, 1.0, rs_ici_job(l, ["in"]), store)
        rs_chip_add(l, ["in"])
        small[l] = dict(b_in=_bias_from_ext(gbe)[0], sinks=dsink[0, :N_HEADS], ps=gps[0], gain=ggain[0], bias=gbias[0])
    grad_x = dy[None]

    _run_job("rs_pair_share", rs_share_job(0), store)
    full = {t: _unhalves(store["full", t]) for t in WEIGHTS}

    def t_back(a):
        return jnp.swapaxes(a, 1, 2)

    def pool4(a):
        return a.reshape(w_pool.shape)

    r_in = _adamw(w_in_t, full["in"], jnp.swapaxes(m_w_in, 1, 2), jnp.swapaxes(v_w_in, 1, 2), idx, "adamw_w_in", tr=HALF_TILE)
    gw_in, dw_in, nm_in, nv_in = [t_back(a) for a in r_in]
    gw_out, dw_out, nm_out, nv_out = _adamw(w_out, full["out"], m_w_out, v_w_out, idx, "adamw_w_out")
    gw_gate, dw_gate, nm_gate, nv_gate = _adamw(w_gate_ple, full["gate"], m_w_gate_ple, v_w_gate_ple, idx, "adamw_w_gate")
    gw_ple, dw_ple, nm_ple, nv_ple = _adamw(w_ple, full["ple"], m_w_ple, v_w_ple, idx, "adamw_w_ple")
    r_pool = _adamw(w_pool2, full["pool"], m_w_pool.reshape(w_pool2.shape), v_w_pool.reshape(w_pool2.shape), idx, "adamw_w_pool")
    gw_pool, dw_pool, nm_pool, nv_pool = [pool4(a) for a in r_pool]

    g_rel = _bias_bwd(dbias.reshape(N_HEADS, -1), onehot).T
    small_shapes = [b_in.shape, attn_sinks.shape, rel_bias.shape, pool_scale.shape, ln_gain.shape, ln_bias.shape]
    g_small = [jnp.stack([small[l]["b_in"] for l in range(L)]), jnp.stack([small[l]["sinks"] for l in range(L)]), g_rel,
               jnp.stack([small[l]["ps"] for l in range(L)]), jnp.stack([small[l]["gain"] for l in range(L)]),
               jnp.stack([small[l]["bias"] for l in range(L)])]
    packed = _small_allreduce_adamw(
        _pack_small(g_small),
        _pack_small([b_in, attn_sinks, rel_bias, pool_scale, ln_gain, ln_bias]),
        _pack_small([m_b_in, m_attn_sinks, m_rel_bias, m_pool_scale, m_ln_gain, m_ln_bias]),
        _pack_small([v_b_in, v_attn_sinks, v_rel_bias, v_pool_scale, v_ln_gain, v_ln_bias]))
    sg, sd, sm, sv_ = [_unpack_small(a, small_shapes) for a in packed]

    def order(big, sm_):
        return (big[0], sm_[0], big[1], sm_[1], sm_[2], big[2], sm_[3], big[3], big[4], sm_[4], sm_[5])

    return (loss, grad_x,
            *order((gw_in, gw_out, gw_pool, gw_ple, gw_gate), sg),
            *order((dw_in, dw_out, dw_pool, dw_ple, dw_gate), sd),
            *order((nm_in, nm_out, nm_pool, nm_ple, nm_gate), sm),
            *order((nv_in, nv_out, nv_pool, nv_ple, nv_gate), sv_))
```

```python
import functools
import math

import jax
import jax.numpy as jnp
from jax import lax
from jax.experimental import pallas as pl
from jax.experimental.pallas import tpu as pltpu

F32 = jnp.float32
BF16 = jnp.bfloat16
I32 = jnp.int32
MESH = pl.DeviceIdType.MESH

HEAD_DIM = 64
QK_SCALE = HEAD_DIM ** -0.5
WINDOW = 128
KV_RATIO = 8
POOL_WINDOWS = (2, 4, 8, 16)
POOL_HALO = 16
REL_BUCKETS = 32
REL_MAX_DIST = 128
LN_EPS = 1e-5
ADAM_LR, ADAM_B1, ADAM_B2, ADAM_EPS, ADAM_WD, ADAM_STEP = 0.001, 0.9, 0.999, 1e-08, 0.01, 10

LANES = 128
VMEM_LIMIT = 52 * 1024 * 1024
N_CHIPS = 4
N_DEV = 8

D_MODEL = 2048
ATTN_W = 1024
POOL_W = 1024
KV_W = 128
N_HEADS = ATTN_W // HEAD_DIM
N_KV = N_HEADS // KV_RATIO
IN_COLS = 4352
SHARD = IN_COLS // N_CHIPS
SHARD_P = 1152
EXT = N_CHIPS * SHARD_P
HALF_TILE = SHARD_P - SHARD
OFF_Q, OFF_KA, OFF_KB, OFF_V, OFF_GA, OFF_U, OFF_UA, OFF_UB, OFF_GP = 0, 1024, 1152, 1280, 1408, 2432, 3328, 3456, 3584
WEIGHTS = ("in", "out", "gate", "ple", "pool")


def _cparams(sem=None):
    return pltpu.CompilerParams(dimension_semantics=sem, vmem_limit_bytes=VMEM_LIMIT)


def _pcall(body, **kw):
    return pl.pallas_call(body, **kw)


def _sigmoid(x):
    return 1.0 / (1.0 + jnp.exp(-x))


def _nt(a, b):
    return lax.dot_general(a, b, (((1,), (1,)), ((), ())), preferred_element_type=F32)


def _tn(a, b):
    return lax.dot_general(a, b, (((0,), (0,)), ((), ())), preferred_element_type=F32)


def _nn(a, b):
    return jnp.dot(a, b, preferred_element_type=F32)


def _mesh_pos():
    return lax.axis_index("x"), lax.axis_index("y"), lax.axis_index("c")


def _peer_chips(x, y):
    return [(1 - x, y), (x, 1 - y), (1 - x, 1 - y)]


def _row_tile(rows, cap=256):
    t = min(rows, cap)
    while rows % t or t % 16:
        t -= 1
    return t


def _hbm_spec():
    return pl.BlockSpec(memory_space=pltpu.HBM)


def _halves(a):
    return a.reshape(a.shape[:-2] + (2, a.shape[-2] // 2, a.shape[-1]))


def _unhalves(a):
    return a.reshape(a.shape[:-3] + (2 * a.shape[-2], a.shape[-1]))


class _remote:
    def __init__(self, src, dst, send, recv, i, device):
        self.args = dict(src_ref=src, dst_ref=dst, send_sem=send.at[i], recv_sem=recv.at[i], device_id=device, device_id_type=MESH)

    def start(self):
        pltpu.make_async_remote_copy(**self.args).start()

    def wait_recv(self):
        pltpu.make_async_remote_copy(**self.args).wait_recv()

    def wait_send(self):
        pltpu.make_async_remote_copy(**self.args).wait_send()


class _Job:
    def __init__(self):
        self.keys, self.parts, self.n = [], [], 0

    def add(self, fn, keys, n):
        self.parts.append((fn, len(self.keys), len(keys), self.n))
        self.keys += list(keys)
        self.n += n
        return self

    def build(self, refs, send, recv):
        out = []
        for fn, i0, nb, base in self.parts:
            out += fn(refs[i0:i0 + nb], send, recv, base)
        return out


def _ag_ici(refs, send, recv, base):
    x, y, c = _mesh_pos()
    me = 2 * x + y
    out = []
    for t, g in enumerate(refs):
        for k, chip in enumerate(_peer_chips(x, y)):
            i = base + 3 * t + k
            dev = (*chip, c)
            out.append((_remote(g.at[me, c], g.at[me, c], send, recv, i, dev),
                        _remote(g.at[me, c], g.at[2 * chip[0] + chip[1], c], send, recv, i, dev)))
    return out


def _ag_fwd(refs, send, recv, base):
    x, y, c = _mesh_pos()
    out = []
    for t, g in enumerate(refs):
        for k, chip in enumerate(_peer_chips(x, y)):
            i = base + 3 * t + k
            slot = 2 * chip[0] + chip[1]
            dev = (x, y, 1 - c)
            out.append((_remote(g.at[slot, c], g.at[slot, c], send, recv, i, dev),
                        _remote(g.at[slot, c], g.at[slot, 1 - c], send, recv, i, dev)))
    return out


def _rs_pair(refs, send, recv, base):
    x, y, c = _mesh_pos()
    n = len(refs) // 2
    out = []
    for t in range(n):
        cp = _remote(refs[t].at[:, 1 - c], refs[n + t], send, recv, base + t, (x, y, 1 - c))
        out.append((cp, cp))
    return out


def _rs_ici(refs, send, recv, base):
    x, y, c = _mesh_pos()
    n = len(refs) // 2
    out = []
    for t in range(n):
        for k, chip in enumerate(_peer_chips(x, y)):
            cp = _remote(refs[t].at[2 * chip[0] + chip[1]], refs[n + t].at[k], send, recv, base + 3 * t + k, (*chip, c))
            out.append((cp, cp))
    return out


def _rs_share(refs, send, recv, base, layer):
    x, y, c = _mesh_pos()
    out = []
    for t, f in enumerate(refs):
        dev = (x, y, 1 - c)
        out.append((_remote(f.at[layer, c], f.at[layer, c], send, recv, base + t, dev),
                    _remote(f.at[layer, c], f.at[layer, 1 - c], send, recv, base + t, dev)))
    return out


def _call(body, *, name, grid, in_specs, out_specs, out_shape, args, scratch_shapes=(), sem=None, job=None, store=None):
    in_specs, out_specs, out_shape, scratch_shapes = list(in_specs), list(out_specs), list(out_shape), list(scratch_shapes)
    if job is None or job.n == 0:
        return list(_pcall(body, name=name, grid=grid, in_specs=in_specs, out_specs=out_specs, out_shape=out_shape,
                           scratch_shapes=scratch_shapes, compiler_params=_cparams(sem))(*args))
    bufs = [store[k] for k in job.keys]
    nb, n_in, n_out, n_sc = len(bufs), len(args), len(out_shape), len(scratch_shapes)

    def wrapped(*refs):
        ins = refs[:n_in]
        outs = refs[n_in + nb:n_in + nb + n_out]
        cb = refs[n_in + nb + n_out:n_in + 2 * nb + n_out]
        scratch = refs[n_in + 2 * nb + n_out:n_in + 2 * nb + n_out + n_sc]
        send, recv = refs[-2:]
        ids = [pl.program_id(a) for a in range(len(grid))]
        first = functools.reduce(jnp.logical_and, [i == 0 for i in ids])
        last = functools.reduce(jnp.logical_and, [i == g - 1 for i, g in zip(ids, grid)])

        @pl.when(first)
        def _():
            for s, _r in job.build(cb, send, recv):
                s.start()

        body(*ins, *outs, *scratch)

        @pl.when(last)
        def _():
            pairs = job.build(cb, send, recv)
            for _s, r in pairs:
                r.wait_recv()
            for s, _r in pairs:
                s.wait_send()

    res = _pcall(
        wrapped, name=name, grid=grid, in_specs=in_specs + [_hbm_spec()] * nb, out_specs=out_specs + [_hbm_spec()] * nb,
        out_shape=out_shape + [jax.ShapeDtypeStruct(b.shape, b.dtype) for b in bufs],
        scratch_shapes=scratch_shapes + [pltpu.SemaphoreType.DMA((job.n,)), pltpu.SemaphoreType.DMA((job.n,))],
        input_output_aliases={n_in + i: n_out + i for i in range(nb)},
        compiler_params=_cparams(("arbitrary",) * len(grid)))(*args, *bufs)
    for k, v in zip(job.keys, res[n_out:]):
        store[k] = v
    return list(res[:n_out])


def _run_job(name, job, store):
    bufs = [store[k] for k in job.keys]
    nb = len(bufs)

    def body(*refs):
        send, recv = refs[-2:]
        pairs = job.build(refs[nb:2 * nb], send, recv)
        for s, _r in pairs:
            s.start()
        for _s, r in pairs:
            r.wait_recv()
        for s, _r in pairs:
            s.wait_send()

    res = _pcall(body, name=name, in_specs=[_hbm_spec()] * nb, out_specs=[_hbm_spec()] * nb,
                 out_shape=[jax.ShapeDtypeStruct(b.shape, b.dtype) for b in bufs],
                 scratch_shapes=[pltpu.SemaphoreType.DMA((job.n,)), pltpu.SemaphoreType.DMA((job.n,))],
                 input_output_aliases={i: i for i in range(nb)})(*bufs)
    for k, v in zip(job.keys, res):
        store[k] = v


def _allgather_now(keys, store):
    bufs = [store[k] for k in keys]
    nb = len(bufs)

    def body(*refs):
        send, recv = refs[-2:]
        g = refs[nb:2 * nb]
        ici = _ag_ici(g, send, recv, 0)
        fwd = _ag_fwd(g, send, recv, 3 * nb)
        for s, _r in ici:
            s.start()
        for (_s, r), (fs, _fr) in zip(ici, fwd):
            r.wait_recv()
            fs.start()
        for _fs, fr in fwd:
            fr.wait_recv()
        for s, _r in ici + fwd:
            s.wait_send()

    res = _pcall(body, name="allgather_first_layer", in_specs=[_hbm_spec()] * nb, out_specs=[_hbm_spec()] * nb,
                 out_shape=[jax.ShapeDtypeStruct(b.shape, b.dtype) for b in bufs],
                 scratch_shapes=[pltpu.SemaphoreType.DMA((6 * nb,)), pltpu.SemaphoreType.DMA((6 * nb,))],
                 input_output_aliases={i: i for i in range(nb)})(*bufs)
    for k, v in zip(keys, res):
        store[k] = v


def _piece(w, l, idx, name):
    _, R, C = w.shape
    tr = _row_tile(R)

    def body(s_ref, w_ref, o_ref):
        del s_ref
        o_ref[...] = w_ref[...].astype(BF16)

    gs = pltpu.PrefetchScalarGridSpec(
        num_scalar_prefetch=1, grid=(R // tr,),
        in_specs=[pl.BlockSpec((None, tr, C), lambda r, s: (l, r, 0))],
        out_specs=pl.BlockSpec((None, tr, C), lambda r, s: (s[0], r, 0)))
    return _pcall(body, name=name, grid_spec=gs, out_shape=jax.ShapeDtypeStruct((N_CHIPS, R, C), BF16),
                  compiler_params=_cparams(("parallel",)))(idx, w)


def _piece_in(w_in_t, l, idx):
    _, R, D = w_in_t.shape
    tr = HALF_TILE
    nsrc = R // tr

    def body(s_ref, w_ref, o_ref):
        src = pl.program_id(0) - s_ref[1]
        ok = jnp.logical_and(src >= 0, src < nsrc)
        o_ref[...] = jnp.where(ok, w_ref[...], 0.0).astype(BF16)

    gs = pltpu.PrefetchScalarGridSpec(
        num_scalar_prefetch=1, grid=(SHARD_P // tr,),
        in_specs=[pl.BlockSpec((None, tr, D), lambda r, s: (l, jnp.clip(r - s[1], 0, nsrc - 1), 0))],
        out_specs=pl.BlockSpec((None, tr, D), lambda r, s: (s[0], r, 0)))
    return _pcall(body, name="piece_in", grid_spec=gs, out_shape=jax.ShapeDtypeStruct((N_CHIPS, SHARD_P, D), BF16),
                  compiler_params=_cparams(("parallel",)))(idx, w_in_t)


def _rs_pair_add(g5, r4, idx, name):
    J, _, h, C = g5.shape
    th = _row_tile(h)

    def body(s_ref, g_ref, r_ref, o32_ref, o16_ref):
        s = g_ref[...] + r_ref[...]
        o16_ref[...] = s.astype(BF16)

        @pl.when(pl.program_id(1) == s_ref[0])
        def _():
            o32_ref[...] = s

    spec = pl.BlockSpec((None, th, C), lambda r, j, s: (j, r, 0))
    gs = pltpu.PrefetchScalarGridSpec(
        num_scalar_prefetch=1, grid=(h // th, J),
        in_specs=[pl.BlockSpec((None, None, th, C), lambda r, j, s: (j, s[2], r, 0)), spec],
        out_specs=[pl.BlockSpec((th, C), lambda r, j, s: (r, 0)), spec])
    return _pcall(body, name=name, grid_spec=gs,
                  out_shape=[jax.ShapeDtypeStruct((h, C), F32), jax.ShapeDtypeStruct((J, h, C), BF16)],
                  compiler_params=_cparams(("parallel", "arbitrary")))(idx, g5, r4)


def _rs_chip_add(p32, r3, full, l, idx, name):
    h, C = p32.shape
    th = _row_tile(h)

    def body(s_ref, p_ref, r_ref, f_ref, o_ref):
        del s_ref, f_ref
        o_ref[...] = ((p_ref[...] + r_ref[0].astype(F32)) + r_ref[1].astype(F32)) + r_ref[2].astype(F32)

    gs = pltpu.PrefetchScalarGridSpec(
        num_scalar_prefetch=1, grid=(h // th,),
        in_specs=[pl.BlockSpec((th, C), lambda r, s: (r, 0)),
                  pl.BlockSpec((3, th, C), lambda r, s: (0, r, 0)),
                  pl.BlockSpec(memory_space=pl.ANY)],
        out_specs=pl.BlockSpec((None, None, th, C), lambda r, s: (l, s[2], r, 0)))
    return _pcall(body, name=name, grid_spec=gs, out_shape=jax.ShapeDtypeStruct(full.shape, F32),
                  input_output_aliases={3: 0}, compiler_params=_cparams(("parallel",)))(idx, p32, r3, full)


def _adamw_math(w, g, m, v):
    nm = ADAM_B1 * m + (1.0 - ADAM_B1) * g
    nv = ADAM_B2 * v + (1.0 - ADAM_B2) * (g * g)
    m_hat = nm / (1.0 - ADAM_B1 ** ADAM_STEP)
    v_hat = nv / (1.0 - ADAM_B2 ** ADAM_STEP)
    delta = -ADAM_LR * (m_hat / (jnp.sqrt(v_hat) + ADAM_EPS) + ADAM_WD * w)
    return delta, nm, nv


def _adamw(w, g, m, v, idx, name, tr=None):
    L, R, C = w.shape
    Rg = g.shape[1]
    tr = tr or _row_tile(R)
    shift = (Rg - R) // tr
    assert (Rg - R) % tr == 0

    def body(s_ref, w_ref, g_ref, m_ref, v_ref, go_ref, d_ref, nm_ref, nv_ref):
        del s_ref
        gv = g_ref[...]
        d, nm, nv = _adamw_math(w_ref[...], gv, m_ref[...], v_ref[...])
        go_ref[...] = gv
        d_ref[...] = d
        nm_ref[...] = nm
        nv_ref[...] = nv

    wspec = pl.BlockSpec((None, tr, C), lambda l, r, s: (l, r, 0))
    gs = pltpu.PrefetchScalarGridSpec(
        num_scalar_prefetch=1, grid=(L, R // tr),
        in_specs=[wspec, pl.BlockSpec((None, tr, C), lambda l, r, s: (l, r + shift * s[1], 0)), wspec, wspec],
        out_specs=[wspec, wspec, wspec, wspec])
    sds = jax.ShapeDtypeStruct((L, R, C), F32)
    return _pcall(body, name=name, grid_spec=gs, out_shape=[sds, sds, sds, sds],
                  compiler_params=_cparams(("parallel", "parallel")))(idx, w, g, m, v)


def _small_allreduce_adamw(gv, wv, mv, vv):
    NR = gv.shape[0]

    def body(g_ref, w_ref, m_ref, v_ref, go_ref, d_ref, nm_ref, nv_ref, gath, send, recv):
        x, y, c = _mesh_pos()
        rank = 4 * x + 2 * y + c
        gath[rank] = g_ref[...]
        cps = []
        for msk in range(1, N_DEV):
            bx, by, bc = (msk >> 2) & 1, (msk >> 1) & 1, msk & 1
            peer = (1 - x if bx else x, 1 - y if by else y, 1 - c if bc else c)
            cps.append(pltpu.make_async_remote_copy(src_ref=g_ref, dst_ref=gath.at[rank], send_sem=send.at[msk - 1],
                                                    recv_sem=recv.at[msk - 1], device_id=peer, device_id_type=MESH))
        for cp in cps:
            cp.start()
        for msk in range(1, N_DEV):
            bx, by, bc = (msk >> 2) & 1, (msk >> 1) & 1, msk & 1
            peer = (1 - x if bx else x, 1 - y if by else y, 1 - c if bc else c)
            prank = 4 * peer[0] + 2 * peer[1] + peer[2]
            pltpu.make_async_remote_copy(src_ref=g_ref, dst_ref=gath.at[prank], send_sem=send.at[msk - 1],
                                         recv_sem=recv.at[msk - 1], device_id=peer, device_id_type=MESH).wait_recv()
        for cp in cps:
            cp.wait_send()
        tot = gath[0]
        for r in range(1, N_DEV):
            tot = tot + gath[r]
        d, nm, nv = _adamw_math(w_ref[...], tot, m_ref[...], v_ref[...])
        go_ref[...] = tot
        d_ref[...] = d
        nm_ref[...] = nm
        nv_ref[...] = nv

    vm = pl.BlockSpec(memory_space=pltpu.VMEM)
    sds = jax.ShapeDtypeStruct((NR, LANES), F32)
    return _pcall(body, name="small_allreduce_adamw", in_specs=[vm, vm, vm, vm], out_specs=[vm, vm, vm, vm],
                  out_shape=[sds, sds, sds, sds],
                  scratch_shapes=[pltpu.VMEM((N_DEV, NR, LANES), F32), pltpu.SemaphoreType.DMA((N_DEV - 1,)),
                                  pltpu.SemaphoreType.DMA((N_DEV - 1,))],
                  compiler_params=pltpu.CompilerParams(vmem_limit_bytes=VMEM_LIMIT))(gv, wv, mv, vv)


def _split3(a):
    h1 = a.astype(BF16)
    r1 = a - h1.astype(F32)
    h2 = r1.astype(BF16)
    h3 = (r1 - h2.astype(F32)).astype(BF16)
    return h1, h2, h3


def _folded_dist():
    r = jnp.arange(WINDOW)[:, None]
    j = jnp.arange(WINDOW)[None, :]
    return jnp.where(j > r, r + WINDOW - j, r - j)


def _bucket_onehot():
    d = _folded_dist()
    max_exact = REL_BUCKETS // 2
    d_f = jnp.maximum(d, 1).astype(F32)
    large = max_exact + (jnp.log(d_f / max_exact) / math.log(REL_MAX_DIST / max_exact) * (REL_BUCKETS - max_exact)).astype(I32)
    large = jnp.minimum(large, REL_BUCKETS - 1)
    bucket = jnp.where(d < max_exact, d, large)
    oh = bucket[None] == jnp.arange(REL_BUCKETS)[:, None, None]
    return oh.reshape(REL_BUCKETS, WINDOW * WINDOW).astype(BF16)


def _bias_fwd(rel_bias_t, onehot):
    H, N = rel_bias_t.shape[0], onehot.shape[1]
    tn = 4096

    def body(t_ref, oh_ref, o_ref):
        h1, h2, h3 = _split3(t_ref[...])
        oh = oh_ref[...]
        o_ref[...] = (_nn(h1, oh) + _nn(h2, oh)) + _nn(h3, oh)

    return _pcall(body, name="bias_fwd", grid=(N // tn,),
                  in_specs=[pl.BlockSpec((H, REL_BUCKETS), lambda i: (0, 0)), pl.BlockSpec((REL_BUCKETS, tn), lambda i: (0, i))],
                  out_specs=pl.BlockSpec((H, tn), lambda i: (0, i)), out_shape=jax.ShapeDtypeStruct((H, N), F32),
                  compiler_params=_cparams(("parallel",)))(rel_bias_t, onehot)


def _bias_bwd(dbias, onehot):
    H, N = dbias.shape
    tn = 4096

    def body(d_ref, oh_ref, o_ref):
        @pl.when(pl.program_id(0) == 0)
        def _():
            o_ref[...] = jnp.zeros_like(o_ref)
        h1, h2, h3 = _split3(d_ref[...])
        oh = oh_ref[...]
        o_ref[...] += (_nt(h1, oh) + _nt(h2, oh)) + _nt(h3, oh)

    return _pcall(body, name="bias_bwd", grid=(N // tn,),
                  in_specs=[pl.BlockSpec((H, tn), lambda i: (0, i)), pl.BlockSpec((REL_BUCKETS, tn), lambda i: (0, i))],
                  out_specs=pl.BlockSpec((H, REL_BUCKETS), lambda i: (0, 0)), out_shape=jax.ShapeDtypeStruct((H, REL_BUCKETS), F32),
                  compiler_params=_cparams(("arbitrary",)))(dbias, onehot)


def _to_bf16(x):
    S, D = x.shape
    tm = min(S, 512)

    def body(x_ref, o_ref):
        o_ref[...] = x_ref[...].astype(BF16)

    return _pcall(body, name="to_bf16", grid=(S // tm,), in_specs=[pl.BlockSpec((tm, D), lambda i: (i, 0))],
                  out_specs=pl.BlockSpec((tm, D), lambda i: (i, 0)), out_shape=jax.ShapeDtypeStruct((S, D), BF16),
                  compiler_params=_cparams(("parallel",)))(x)


def _inproj(xb, w_t, b_ext, l, job, store):
    S, D = xb.shape
    tm = min(S, 512)

    def body(x_ref, w_ref, b_ref, q_ref, k_ref, v_ref, ga_ref, u_ref, gp_ref):
        j = pl.program_id(1)
        acc = _nt(x_ref[...], w_ref[...]) + b_ref[...]

        @pl.when(j == 0)
        def _():
            q_ref[...] = (acc[:, :1024] * QK_SCALE).astype(BF16)
            k_ref[...] = acc[:, 1024:1152]

        @pl.when(j == 1)
        def _():
            k_ref[...] += acc[:, 0:128]
            v_ref[...] = acc[:, 128:256]
            ga_ref[:, 0:896] = acc[:, 256:1152]

        @pl.when(j == 2)
        def _():
            ga_ref[:, 896:1024] = acc[:, 0:128]
            u_ref[:, 0:896] = acc[:, 128:1024]
            u_ref[:, 896:1024] = acc[:, 1024:1152]

        @pl.when(j == 3)
        def _():
            u_ref[:, 896:1024] += acc[:, 0:128]
            gp_ref[...] = acc[:, 128:1152]

    def ospec(w):
        return pl.BlockSpec((tm, w), lambda i, j: (i, 0))

    return _call(
        body, name="inproj", grid=(S // tm, N_CHIPS),
        in_specs=[pl.BlockSpec((tm, D), lambda i, j: (i, 0)),
                  pl.BlockSpec((None, SHARD_P, D), lambda i, j: (j, 0, 0)),
                  pl.BlockSpec((None, 1, SHARD_P), lambda i, j: (l, 0, j))],
        out_specs=[ospec(ATTN_W), ospec(KV_W), ospec(KV_W), ospec(ATTN_W), ospec(POOL_W), ospec(POOL_W)],
        out_shape=[jax.ShapeDtypeStruct((S, ATTN_W), BF16), jax.ShapeDtypeStruct((S, KV_W), F32), jax.ShapeDtypeStruct((S, KV_W), F32),
                   jax.ShapeDtypeStruct((S, ATTN_W), F32), jax.ShapeDtypeStruct((S, POOL_W), F32), jax.ShapeDtypeStruct((S, POOL_W), F32)],
        args=(xb, w_t, b_ext), sem=("parallel", "arbitrary"), job=job, store=store)


def _matmul_nn(a, b4, name, job, store):
    S, K = a.shape
    N = b4.shape[2]
    kq = b4.shape[1]
    tm, tn = min(S, 512), min(N, 1024)

    def body(a_ref, b_ref, o_ref):
        acc = _nn(a_ref[:, 0:kq], b_ref[0])
        for j in range(1, N_CHIPS):
            acc = acc + _nn(a_ref[:, j * kq:(j + 1) * kq], b_ref[j])
        o_ref[...] = acc

    return _call(body, name=name, grid=(N // tn, S // tm),
                 in_specs=[pl.BlockSpec((tm, K), lambda n, i: (i, 0)), pl.BlockSpec((N_CHIPS, kq, tn), lambda n, i: (0, 0, n))],
                 out_specs=[pl.BlockSpec((tm, tn), lambda n, i: (i, n))], out_shape=[jax.ShapeDtypeStruct((S, N), F32)],
                 args=(a, b4), sem=("parallel", "parallel"), job=job, store=store)[0]


def _masked_bias(bias):
    r = jnp.arange(WINDOW)[:, None]
    j = jnp.arange(WINDOW)[None, :]
    return jnp.stack([jnp.where(j > r, -1e30, bias), bias])


def _fold(full, tri):
    return jnp.where(tri, full[:, :WINDOW], full[:, WINDOW:])


def _unfold(folded, tri):
    return jnp.concatenate([jnp.where(tri, folded, 0.0).astype(BF16), jnp.where(tri, 0.0, folded).astype(BF16)], axis=1)


def _dup_heads(src_ref, dst_ref):
    a = src_ref[...]
    r = pltpu.roll(a, HEAD_DIM, axis=1)
    lo = lax.broadcasted_iota(I32, a.shape, 1) < HEAD_DIM
    dst_ref[0] = jnp.where(lo, a, r).astype(BF16)
    dst_ref[1] = jnp.where(lo, r, a).astype(BF16)


def _kv_block(ref, h, prev, cur):
    return jnp.concatenate([ref[h, pl.ds(prev, WINDOW), :], ref[h, pl.ds(cur, WINDOW), :]], axis=0)


def _rows(i):
    return slice(i * WINDOW, (i + 1) * WINDOW)


def _attn_fwd(q, k, v, ga, bias, sinks, l, job, store):
    S = q.shape[0]
    nblk = S // WINDOW

    def body(sink_ref, q_ref, k_ref, v_ref, ga_ref, bias_ref, o_ref, ca_ref, lse_ref, kd, vd, qs, ss, ps, os_):
        n = pl.program_id(0)

        @pl.when(n == 0)
        def _():
            _dup_heads(k_ref, kd)
            _dup_heads(v_ref, vd)

        cur = pl.multiple_of(n * WINDOW, WINDOW)
        prev = pl.multiple_of(jnp.maximum(n - 1, 0) * WINDOW, WINDOW)
        lane = lax.broadcasted_iota(I32, (WINDOW, LANES), 1)
        lo = lane < HEAD_DIM
        tri = lane > lax.broadcasted_iota(I32, (WINDOW, LANES), 0)
        lse_mat = jnp.zeros((WINDOW, LANES), F32)
        for g in range(N_KV):
            for i in range(KV_RATIO):
                h = g * KV_RATIO + i
                qp = q_ref[:, LANES * (h // 2):LANES * (h // 2 + 1)]
                qs[_rows(i), :] = jnp.where(lo if h % 2 == 0 else jnp.logical_not(lo), qp, jnp.zeros_like(qp))
            ss[...] = _nt(qs[...], _kv_block(kd, g, prev, cur))
            for i in range(KV_RATIO):
                h = g * KV_RATIO + i
                s = _fold(ss[_rows(i), :], tri) + bias_ref[h]
                sink = sink_ref[l, h]
                m = jnp.maximum(jnp.max(s, axis=1, keepdims=True), sink)
                e = jnp.exp(s - m)
                den = jnp.sum(e, axis=1, keepdims=True) + jnp.exp(sink - m)
                ps[_rows(i), :] = _unfold(e * (1.0 / den), tri)
                lse_mat = jnp.where(lane == h, m + jnp.log(den), lse_mat)
            os_[...] = _nn(ps[...], _kv_block(vd, g, prev, cur))
            for j in range(KV_RATIO // 2):
                pair = g * (KV_RATIO // 2) + j
                sl = slice(LANES * pair, LANES * (pair + 1))
                o_pair = jnp.where(lo, os_[_rows(2 * j), :], os_[_rows(2 * j + 1), :])
                o_ref[:, sl] = o_pair
                gav = ga_ref[:, sl]
                ca_ref[:, sl] = (o_pair * (gav * _sigmoid(gav))).astype(BF16)
        lse_ref[...] = lse_mat

    blk = pl.BlockSpec((WINDOW, ATTN_W), lambda n: (n, 0))
    full_kv = pl.BlockSpec((S, KV_W), lambda n: (0, 0))
    stack = KV_RATIO * WINDOW
    return _call(
        body, name="attn_fwd", grid=(nblk,),
        in_specs=[pl.BlockSpec(memory_space=pltpu.SMEM), blk, full_kv, full_kv, blk,
                  pl.BlockSpec((None, N_HEADS, WINDOW, WINDOW), lambda n: (jnp.minimum(n, 1), 0, 0, 0))],
        out_specs=[blk, blk, pl.BlockSpec((WINDOW, LANES), lambda n: (n, 0))],
        out_shape=[jax.ShapeDtypeStruct((S, ATTN_W), F32), jax.ShapeDtypeStruct((S, ATTN_W), BF16), jax.ShapeDtypeStruct((S, LANES), F32)],
        scratch_shapes=[pltpu.VMEM((N_KV, S, LANES), BF16), pltpu.VMEM((N_KV, S, LANES), BF16),
                        pltpu.VMEM((stack, LANES), BF16), pltpu.VMEM((stack, 2 * WINDOW), F32),
                        pltpu.VMEM((stack, 2 * WINDOW), BF16), pltpu.VMEM((stack, LANES), F32)],
        args=(sinks, q, k, v, ga, bias), sem=("arbitrary",), job=job, store=store)


def _pool_diff(u, halo, tile_index, tm):
    gw = POOL_W // len(POOL_WINDOWS)
    xh = jnp.concatenate([halo, u], axis=0)
    sums = []
    s = xh
    for step in (1, 2, 4, 8):
        s = s + pltpu.roll(s, step, axis=0)
        sums.append(s)
    t = tile_index * tm + lax.broadcasted_iota(I32, (tm, gw), 0)
    diffs = []
    for g, w in enumerate(POOL_WINDOWS):
        cols = slice(g * gw, (g + 1) * gw)
        cnt = jnp.minimum(t + 1, w).astype(F32)
        diffs.append(sums[g][POOL_HALO:, cols] / cnt - u[:, cols])
    return diffs


def _pool_weight(wp_ref, g):
    r = wp_ref.shape[1] // len(POOL_WINDOWS)
    return jnp.concatenate([wp_ref[j, g * r:(g + 1) * r, :] for j in range(N_CHIPS)], axis=0)


def _pool_fwd(u, gp, ca, w_pool, ps, l, job, store):
    S = u.shape[0]
    tm = min(S, 256)
    hb = tm // POOL_HALO
    gw = POOL_W // len(POOL_WINDOWS)

    def body(u_ref, uh_ref, gp_ref, ca_ref, wp_ref, ps_ref, c_ref):
        i = pl.program_id(0)
        uv = u_ref[...]
        halo = jnp.where(i > 0, uh_ref[...], 0.0)
        diffs = _pool_diff(uv, halo, i, tm)
        c_ref[:, 0:ATTN_W] = ca_ref[...]
        for g in range(len(POOL_WINDOWS)):
            cols = slice(g * gw, (g + 1) * gw)
            mm = _nn(diffs[g].astype(BF16), _pool_weight(wp_ref, g))
            gpv = gp_ref[:, cols]
            b = (mm * ps_ref[:, cols]) * (gpv * _sigmoid(gpv))
            c_ref[:, ATTN_W + g * gw:ATTN_W + (g + 1) * gw] = b.astype(BF16)

    row = pl.BlockSpec((tm, POOL_W), lambda i: (i, 0))
    return _call(
        body, name="pool_fwd", grid=(S // tm,),
        in_specs=[row, pl.BlockSpec((POOL_HALO, POOL_W), lambda i: (jnp.maximum(i * hb - 1, 0), 0)), row, row,
                  pl.BlockSpec(w_pool.shape, lambda i: (0, 0, 0)),
                  pl.BlockSpec((None, 1, POOL_W), lambda i: (l, 0, 0))],
        out_specs=[pl.BlockSpec((tm, ATTN_W + POOL_W), lambda i: (i, 0))],
        out_shape=[jax.ShapeDtypeStruct((S, ATTN_W + POOL_W), BF16)],
        args=(u, u, gp, ca, w_pool, ps), sem=("parallel",), job=job, store=store)[0]


def _ple_embed(p_ref, wple_ref):
    pb = p_ref[...].astype(BF16)
    return jnp.concatenate([_nn(pb, wple_ref[j]) for j in range(N_CHIPS)], axis=1)


def _outproj_ln(c, xh_in, gain_in, bias_in, l_in, gl, p, w_out, w_ple, gain, bias, l, alpha, job, store):
    S, D = xh_in.shape
    tm = min(S, 256)
    kq = D // N_CHIPS

    def body(c_ref, x_ref, gi_ref, bi_ref, gl_ref, p_ref, wo_ref, wp_ref, gain_ref, bias_ref, yb_ref, xh_ref, rs_ref):
        mix = _nn(c_ref[:, 0:kq], wo_ref[0])
        for j in range(1, N_CHIPS):
            mix = mix + _nn(c_ref[:, j * kq:(j + 1) * kq], wo_ref[j])
        ple = _sigmoid(gl_ref[...]) * _ple_embed(p_ref, wp_ref)
        x = x_ref[...] * gi_ref[...] + bi_ref[...]
        z = (alpha * x + mix) + ple
        mu = jnp.mean(z, axis=1, keepdims=True)
        zc = z - mu
        var = jnp.mean(zc * zc, axis=1, keepdims=True)
        rstd = lax.rsqrt(var + LN_EPS)
        xhat = zc * rstd
        yb_ref[...] = (xhat * gain_ref[...] + bias_ref[...]).astype(BF16)
        xh_ref[...] = xhat
        rs_ref[...] = rstd

    row = pl.BlockSpec((tm, D), lambda i: (i, 0))
    vec = pl.BlockSpec((None, 1, D), lambda i: (l, 0, 0))
    vec_in = pl.BlockSpec((None, 1, D), lambda i: (l_in, 0, 0))
    return _call(
        body, name="outproj_ln", grid=(S // tm,),
        in_specs=[row, row, vec_in, vec_in, row, pl.BlockSpec((tm, p.shape[1]), lambda i: (i, 0)),
                  pl.BlockSpec(w_out.shape, lambda i: (0, 0, 0)), pl.BlockSpec(w_ple.shape, lambda i: (0, 0, 0)), vec, vec],
        out_specs=[row, row, pl.BlockSpec((tm, 1), lambda i: (i, 0))],
        out_shape=[jax.ShapeDtypeStruct((S, D), BF16), jax.ShapeDtypeStruct((S, D), F32), jax.ShapeDtypeStruct((S, 1), F32)],
        args=(c, xh_in, gain_in, bias_in, gl, p, w_out, w_ple, gain, bias), sem=("parallel",), job=job, store=store)


def _loss_and_grad(xhat, gain, bias, l, target):
    S, D = xhat.shape
    tm = min(S, 512)

    def body(xh_ref, g_ref, b_ref, t_ref, dy_ref, acc_ref):
        @pl.when(pl.program_id(0) == 0)
        def _():
            acc_ref[...] = jnp.zeros_like(acc_ref)
        d = (xh_ref[...] * g_ref[...] + b_ref[...]) - t_ref[...]
        dy_ref[...] = d * (1.0 / D)
        acc_ref[...] += jnp.sum(jnp.mean(d * d, axis=1, keepdims=True), axis=0, keepdims=True)

    row = pl.BlockSpec((tm, D), lambda i: (i, 0))
    vec = pl.BlockSpec((None, 1, D), lambda i: (l, 0, 0))
    return _pcall(body, name="loss", grid=(S // tm,), in_specs=[row, vec, vec, row],
                  out_specs=[row, pl.BlockSpec((8, LANES), lambda i: (0, 0))],
                  out_shape=[jax.ShapeDtypeStruct((S, D), F32), jax.ShapeDtypeStruct((8, LANES), F32)],
                  compiler_params=_cparams(("arbitrary",)))(xhat, gain, bias, target)


def _ln_bwd(dy, xhat, rstd, gl, p, w_ple, gain, l, job, store):
    S, D = dy.shape
    tm = min(S, 256)

    def body(dy_ref, xh_ref, rs_ref, gl_ref, p_ref, wp_ref, gain_ref, dz_ref, dzb_ref, de_ref, dgl_ref, gg_ref, gb_ref):
        @pl.when(pl.program_id(0) == 0)
        def _():
            gg_ref[...] = jnp.zeros_like(gg_ref)
            gb_ref[...] = jnp.zeros_like(gb_ref)
        dyv = dy_ref[...]
        xh = xh_ref[...]
        dxh = dyv * gain_ref[...]
        m1 = jnp.mean(dxh, axis=1, keepdims=True)
        m2 = jnp.mean(dxh * xh, axis=1, keepdims=True)
        dz = rs_ref[...] * ((dxh - m1) - xh * m2)
        gg_ref[...] += jnp.sum(dyv * xh, axis=0, keepdims=True)
        gb_ref[...] += jnp.sum(dyv, axis=0, keepdims=True)
        sg = _sigmoid(gl_ref[...])
        e = _ple_embed(p_ref, wp_ref)
        dz_ref[...] = dz
        dzb_ref[...] = dz.astype(BF16)
        de_ref[...] = (dz * sg).astype(BF16)
        dgl_ref[...] = ((dz * e) * (sg * (1.0 - sg))).astype(BF16)

    row = pl.BlockSpec((tm, D), lambda i: (i, 0))
    vec_in = pl.BlockSpec((None, 1, D), lambda i: (l, 0, 0))
    vec_out = pl.BlockSpec((1, D), lambda i: (0, 0))
    bsd = jax.ShapeDtypeStruct((S, D), BF16)
    return _call(
        body, name="ln_bwd", grid=(S // tm,),
        in_specs=[row, row, pl.BlockSpec((tm, 1), lambda i: (i, 0)), row, pl.BlockSpec((tm, p.shape[1]), lambda i: (i, 0)),
                  pl.BlockSpec(w_ple.shape, lambda i: (0, 0, 0)), vec_in],
        out_specs=[row, row, row, row, vec_out, vec_out],
        out_shape=[jax.ShapeDtypeStruct((S, D), F32), bsd, bsd, bsd, jax.ShapeDtypeStruct((1, D), F32), jax.ShapeDtypeStruct((1, D), F32)],
        args=(dy, xhat, rstd, gl, p, w_ple, gain), sem=("arbitrary",), job=job, store=store)


def _matmul_nn_acc(a, b4, name, add, add_scale, job, store):
    S = a.shape[0]
    KS, tk, N = b4.shape
    tm, tn = min(S, 512), min(N, 512)

    def body(a_ref, b_ref, add_ref, o_ref):
        acc = add_scale * add_ref[...]
        for k in range(KS):
            acc = acc + _nn(a_ref[:, k * tk:(k + 1) * tk], b_ref[k])
        o_ref[...] = acc

    return _call(body, name=name, grid=(S // tm, N // tn),
                 in_specs=[pl.BlockSpec((tm, KS * tk), lambda i, n: (i, 0)),
                           pl.BlockSpec((KS, tk, tn), lambda i, n: (0, 0, n)),
                           pl.BlockSpec((tm, tn), lambda i, n: (i, n))],
                 out_specs=[pl.BlockSpec((tm, tn), lambda i, n: (i, n))], out_shape=[jax.ShapeDtypeStruct((S, N), F32)],
                 args=(a, b4, add), sem=("parallel", "parallel"), job=job, store=store)[0]


def _matmul_nt_rows(a, b4, name, add=None, add_scale=1.0, job=None, store=None):
    S, K = a.shape
    nq = b4.shape[1]
    tm = min(S, 512)
    out_spec = pl.BlockSpec((tm, nq), lambda j, i: (i, j))
    in_specs = [pl.BlockSpec((tm, K), lambda j, i: (i, 0)), pl.BlockSpec((None, nq, K), lambda j, i: (j, 0, 0))]
    if add is None:
        def body(a_ref, b_ref, o_ref):
            o_ref[...] = _nt(a_ref[...], b_ref[...])
        args = (a, b4)
    else:
        def body(a_ref, b_ref, add_ref, o_ref):
            o_ref[...] = _nt(a_ref[...], b_ref[...]) + add_scale * add_ref[...]
        in_specs.append(out_spec)
        args = (a, b4, add)

    return _call(body, name=name, grid=(N_CHIPS, S // tm), in_specs=in_specs,
                 out_specs=[out_spec], out_shape=[jax.ShapeDtypeStruct((S, N_CHIPS * nq), F32)],
                 args=args, sem=("parallel", "parallel"), job=job, store=store)[0]


def _matmul_tn(a, b, R, C, name, by_rows, job=None, store=None):
    S = a.shape[0]
    tn = min(C, 512)
    nt = C // tn

    def body(a_ref, b_ref, o_ref, at_ref):
        @pl.when(pl.program_id(1) == 0)
        def _():
            at_ref[...] = a_ref[...].T
        o_ref[...] = _nn(at_ref[...], b_ref[...])

    if by_rows:
        a_spec = pl.BlockSpec((S, R), lambda j, n: (0, j))
        b_spec = pl.BlockSpec((S, tn), lambda j, n: (0, n))
    else:
        a_spec = pl.BlockSpec((S, R), lambda j, n: (0, 0))
        b_spec = pl.BlockSpec((S, tn), lambda j, n: (0, j * nt + n))
    return _call(body, name=name, grid=(N_CHIPS, nt), in_specs=[a_spec, b_spec],
                 out_specs=[pl.BlockSpec((None, R, tn), lambda j, n: (j, 0, n))],
                 out_shape=[jax.ShapeDtypeStruct((N_CHIPS, R, C), F32)],
                 scratch_shapes=[pltpu.VMEM((R, S), BF16)],
                 args=(a, b), sem=("parallel", "arbitrary"), job=job, store=store)[0]


def _pool_bwd(u, gp, dc, w_pool, ps, l):
    S = u.shape[0]
    tm = min(S, 256)
    hb = tm // POOL_HALO
    ngrp = len(POOL_WINDOWS)
    gw = POOL_W // ngrp
    rr = gw // N_CHIPS

    def body(u_ref, uh_ref, gp_ref, dc_ref, wp_ref, ps_ref, dd_ref, dgp_ref, gps_ref, gwp_ref):
        i = pl.program_id(0)

        @pl.when(i == 0)
        def _():
            gps_ref[...] = jnp.zeros_like(gps_ref)
            gwp_ref[...] = jnp.zeros_like(gwp_ref)
        uv = u_ref[...]
        halo = jnp.where(i > 0, uh_ref[...], 0.0)
        diffs = _pool_diff(uv, halo, i, tm)
        for g in range(ngrp):
            cols = slice(g * gw, (g + 1) * gw)
            w = _pool_weight(wp_ref, g)
            db = diffs[g].astype(BF16)
            mm = _nn(db, w)
            gpv = gp_ref[:, cols]
            sg = _sigmoid(gpv)
            si = gpv * sg
            dsi = sg * (1.0 + gpv * (1.0 - sg))
            dcb = dc_ref[:, cols]
            psv = ps_ref[:, cols]
            d_mm = (dcb * si) * psv
            gps_ref[:, cols] += jnp.sum((dcb * si) * mm, axis=0, keepdims=True)
            dgp_ref[:, cols] = (dcb * (mm * psv)) * dsi
            d_mmb = d_mm.astype(BF16)
            dd_ref[:, cols] = _nt(d_mmb, w)
            gwt = _tn(db, d_mmb)
            for j in range(N_CHIPS):
                gwp_ref[j, g * rr:(g + 1) * rr, :] += gwt[j * rr:(j + 1) * rr, :]

    row = pl.BlockSpec((tm, POOL_W), lambda i: (i, 0))
    wspec = pl.BlockSpec(w_pool.shape, lambda i: (0, 0, 0))
    return _pcall(
        body, name="pool_bwd", grid=(S // tm,),
        in_specs=[row, pl.BlockSpec((POOL_HALO, POOL_W), lambda i: (jnp.maximum(i * hb - 1, 0), 0)), row,
                  pl.BlockSpec((tm, POOL_W), lambda i: (i, 1)), wspec, pl.BlockSpec((None, 1, POOL_W), lambda i: (l, 0, 0))],
        out_specs=[row, row, pl.BlockSpec((1, POOL_W), lambda i: (0, 0)), wspec],
        out_shape=[jax.ShapeDtypeStruct((S, POOL_W), F32), jax.ShapeDtypeStruct((S, POOL_W), F32),
                   jax.ShapeDtypeStruct((1, POOL_W), F32), jax.ShapeDtypeStruct(w_pool.shape, F32)],
        compiler_params=_cparams(("arbitrary",)),
    )(u, u, gp, dc, w_pool, ps)


def _pool_window_t(dd, halo_next, tile_index, tm):
    gw = POOL_W // len(POOL_WINDOWS)
    n = tm + POOL_HALO
    t = tile_index * tm + lax.broadcasted_iota(I32, (n, gw), 0)
    xh = jnp.concatenate([dd, halo_next], axis=0)
    outs = []
    for g, w in enumerate(POOL_WINDOWS):
        cols = slice(g * gw, (g + 1) * gw)
        cnt = jnp.minimum(t + 1, w).astype(F32)
        s = xh[:, cols] / cnt
        step = 1
        while step < w:
            s = s + pltpu.roll(s, n - step, axis=0)
            step *= 2
        outs.append(s[:tm] - dd[:, cols])
    return outs


def _assemble_dh(dq, dk, dv, dga, dd, dgp):
    S = dq.shape[0]
    tm = min(S, 256)
    hb = tm // POOL_HALO
    nt = S // tm

    def body(dq_ref, dk_ref, dv_ref, dga_ref, dd_ref, ddn_ref, dgp_ref, dh_ref, gb_ref):
        i = pl.program_id(0)

        @pl.when(i == 0)
        def _():
            gb_ref[...] = jnp.zeros_like(gb_ref)
        halo = jnp.where(i < nt - 1, ddn_ref[...], 0.0)
        du = jnp.concatenate(_pool_window_t(dd_ref[...], halo, i, tm), axis=1)
        dkv = dk_ref[...]
        dgav = dga_ref[...]
        parts = [(OFF_Q, dq_ref[...]), (OFF_KA, dkv), (OFF_KB, dkv), (OFF_V, dv_ref[...]), (OFF_GA, dgav),
                 (OFF_U, du[:, 0:896]), (OFF_UA, du[:, 896:1024]), (OFF_UB, du[:, 896:1024]), (OFF_GP, dgp_ref[...])]
        for off, val in parts:
            w = val.shape[1]
            dh_ref[:, off:off + w] = val.astype(BF16)
            gb_ref[:, off:off + w] += jnp.sum(val, axis=0, keepdims=True)

    def row(w):
        return pl.BlockSpec((tm, w), lambda i: (i, 0))

    return _pcall(
        body, name="assemble_dh", grid=(nt,),
        in_specs=[row(ATTN_W), row(KV_W), row(KV_W), row(ATTN_W), row(POOL_W),
                  pl.BlockSpec((POOL_HALO, POOL_W), lambda i: (jnp.minimum((i + 1) * hb, S // POOL_HALO - 1), 0)), row(POOL_W)],
        out_specs=[row(EXT), pl.BlockSpec((1, EXT), lambda i: (0, 0))],
        out_shape=[jax.ShapeDtypeStruct((S, EXT), BF16), jax.ShapeDtypeStruct((1, EXT), F32)],
        compiler_params=_cparams(("arbitrary",)),
    )(dq, dk, dv, dga, dd, dd, dgp)


def _attn_bwd(q, k, v, ga, o, dc, lse, bias, sinks, dbias_in, l, job, store):
    S = q.shape[0]
    nblk = S // WINDOW

    def body(sink_ref, q_ref, k_ref, v_ref, ga_ref, o_ref, dc_ref, lse_ref, bias_ref, dbin_ref,
             dq_ref, dga_ref, dk_ref, dv_ref, db_ref, ds_ref, kd, vd, qs, dos, dls, ss, dps, dss, pss, dqs):
        n = pl.program_id(0)

        @pl.when(n == 0)
        def _():
            _dup_heads(k_ref, kd)
            _dup_heads(v_ref, vd)
            dk_ref[...] = jnp.zeros_like(dk_ref)
            dv_ref[...] = jnp.zeros_like(dv_ref)
            db_ref[...] = dbin_ref[...]
            ds_ref[...] = jnp.zeros_like(ds_ref)

        cur = pl.multiple_of(n * WINDOW, WINDOW)
        prev = pl.multiple_of(jnp.maximum(n - 1, 0) * WINDOW, WINDOW)
        lane = lax.broadcasted_iota(I32, (WINDOW, LANES), 1)
        lane8 = lax.broadcasted_iota(I32, (8, LANES), 1)
        lo = lane < HEAD_DIM
        tri = lane > lax.broadcasted_iota(I32, (WINDOW, LANES), 0)
        lse_t = lse_ref[...]
        dk_t, dv_t = [], []
        dsk = jnp.zeros((8, LANES), F32)
        for g in range(N_KV):
            kb = _kv_block(kd, g, prev, cur)
            vb = _kv_block(vd, g, prev, cur)
            for j in range(KV_RATIO // 2):
                pair = g * (KV_RATIO // 2) + j
                sl = slice(LANES * pair, LANES * (pair + 1))
                qp = q_ref[:, sl]
                op = o_ref[:, sl]
                dcp = dc_ref[:, sl]
                gav = ga_ref[:, sl]
                sg = _sigmoid(gav)
                d_o = dcp * (gav * sg)
                dga_ref[:, sl] = (dcp * op) * (sg * (1.0 + gav * (1.0 - sg)))
                prod = d_o * op
                for par in range(2):
                    msk = lo if par == 0 else jnp.logical_not(lo)
                    rows = _rows(2 * j + par)
                    qs[rows, :] = jnp.where(msk, qp, jnp.zeros_like(qp))
                    dos[rows, :] = jnp.where(msk, d_o, 0.0).astype(BF16)
                    delta = jnp.sum(jnp.where(msk, prod, 0.0), axis=1, keepdims=True)
                    dls[rows, :] = jnp.broadcast_to(delta, (WINDOW, LANES))
            ss[...] = _nt(qs[...], kb)
            dps[...] = _nt(dos[...], vb)
            for i in range(KV_RATIO):
                h = g * KV_RATIO + i
                rows = _rows(i)
                lse_h = jnp.sum(jnp.where(lane == h, lse_t, 0.0), axis=1, keepdims=True)
                p = jnp.exp(_fold(ss[rows, :], tri) + bias_ref[h] - lse_h)
                delta = dls[rows, :]
                dsc = p * (_fold(dps[rows, :], tri) - delta)
                db_ref[h] += dsc
                psink = jnp.exp(sink_ref[l, h] - lse_h)
                dsk = dsk + jnp.where(lane8 == h, -jnp.sum(psink * delta, axis=0, keepdims=True), 0.0)
                dss[rows, :] = _unfold(dsc, tri)
                pss[rows, :] = _unfold(p, tri)
            dqs[...] = _nn(dss[...], kb) * QK_SCALE
            for j in range(KV_RATIO // 2):
                pair = g * (KV_RATIO // 2) + j
                dq_ref[:, LANES * pair:LANES * (pair + 1)] = jnp.where(lo, dqs[_rows(2 * j), :], dqs[_rows(2 * j + 1), :])
            dk_t.append(_tn(qs[...], dss[...]))
            dv_t.append(_tn(dos[...], pss[...]))

        def untranspose(acc):
            return jnp.concatenate([a[:HEAD_DIM] + a[HEAD_DIM:] for a in acc], axis=0).T

        dkb = untranspose(dk_t)
        dvb = untranspose(dv_t)
        dk_ref[pl.ds(prev, WINDOW), :] += dkb[:WINDOW]
        dk_ref[pl.ds(cur, WINDOW), :] += dkb[WINDOW:]
        dv_ref[pl.ds(prev, WINDOW), :] += dvb[:WINDOW]
        dv_ref[pl.ds(cur, WINDOW), :] += dvb[WINDOW:]
        ds_ref[...] += dsk

    blk = pl.BlockSpec((WINDOW, ATTN_W), lambda n: (n, 0))
    full_kv = pl.BlockSpec((S, KV_W), lambda n: (0, 0))
    full_b = pl.BlockSpec((N_HEADS, WINDOW, WINDOW), lambda n: (0, 0, 0))
    stack = KV_RATIO * WINDOW
    return _call(
        body, name="attn_bwd", grid=(nblk,),
        in_specs=[pl.BlockSpec(memory_space=pltpu.SMEM), blk, full_kv, full_kv, blk, blk, blk,
                  pl.BlockSpec((WINDOW, LANES), lambda n: (n, 0)),
                  pl.BlockSpec((None, N_HEADS, WINDOW, WINDOW), lambda n: (jnp.minimum(n, 1), 0, 0, 0)), full_b],
        out_specs=[blk, blk, full_kv, full_kv, full_b, pl.BlockSpec((8, LANES), lambda n: (0, 0))],
        out_shape=[jax.ShapeDtypeStruct((S, ATTN_W), F32), jax.ShapeDtypeStruct((S, ATTN_W), F32), jax.ShapeDtypeStruct((S, KV_W), F32),
                   jax.ShapeDtypeStruct((S, KV_W), F32), jax.ShapeDtypeStruct((N_HEADS, WINDOW, WINDOW), F32),
                   jax.ShapeDtypeStruct((8, LANES), F32)],
        scratch_shapes=[pltpu.VMEM((N_KV, S, LANES), BF16), pltpu.VMEM((N_KV, S, LANES), BF16),
                        pltpu.VMEM((stack, LANES), BF16), pltpu.VMEM((stack, LANES), BF16), pltpu.VMEM((stack, LANES), F32),
                        pltpu.VMEM((stack, 2 * WINDOW), F32), pltpu.VMEM((stack, 2 * WINDOW), F32),
                        pltpu.VMEM((stack, 2 * WINDOW), BF16), pltpu.VMEM((stack, 2 * WINDOW), BF16),
                        pltpu.VMEM((stack, LANES), F32)],
        args=(sinks, q, k, v, ga, o, dc, lse, bias, dbias_in), sem=("arbitrary",), job=job, store=store)


def _pack_small(arrs):
    flat = []
    for a in arrs:
        v = a.reshape(-1)
        flat.append(jnp.pad(v, (0, (-v.shape[0]) % LANES)))
    v = jnp.concatenate(flat)
    v = jnp.pad(v, (0, (-v.shape[0]) % (8 * LANES)))
    return v.reshape(-1, LANES)


def _unpack_small(packed, shapes):
    v = packed.reshape(-1)
    outs, off = [], 0
    for shp in shapes:
        n = math.prod(shp)
        outs.append(v[off:off + n].reshape(shp))
        off += n + (-n) % LANES
    return outs


def _bias_to_ext(b):
    L = b.shape[0]
    z = jnp.zeros((L, HALF_TILE), b.dtype)
    parts = []
    for j in range(N_CHIPS):
        seg = b[:, j * SHARD:(j + 1) * SHARD]
        parts += [z, seg] if j % 2 else [seg, z]
    return jnp.concatenate(parts, axis=1).reshape(L, 1, EXT)


def _bias_from_ext(g):
    parts = []
    for j in range(N_CHIPS):
        o = j * SHARD_P + (HALF_TILE if j % 2 else 0)
        parts.append(g[:, o:o + SHARD])
    return jnp.concatenate(parts, axis=1)


def kernel(x, p, w_in, b_in, w_out, attn_sinks, rel_bias, w_pool, pool_scale, w_ple, w_gate_ple, ln_gain, ln_bias, loss_target, m_w_in, m_b_in, m_w_out, m_attn_sinks, m_rel_bias, m_w_pool, m_pool_scale, m_w_ple, m_w_gate_ple, m_ln_gain, m_ln_bias, v_w_in, v_b_in, v_w_out, v_attn_sinks, v_rel_bias, v_w_pool, v_pool_scale, v_w_ple, v_w_gate_ple, v_ln_gain, v_ln_bias):
    L = w_in.shape[0]
    S, D = x.shape[1], x.shape[2]
    assert D == D_MODEL and w_in.shape[2] == SHARD and S % WINDOW == 0
    alpha = (2.0 * L) ** 0.25
    xc, yc, cc = _mesh_pos()
    idx = jnp.stack([2 * xc + yc, yc, cc]).astype(I32)
    store = {}

    def wkeys(l, names):
        return [("w", l, t) for t in names]

    def weight(l, t):
        return _unhalves(store["w", l, t])

    w_in_t = jnp.swapaxes(w_in, 1, 2)
    w_pool2 = w_pool.reshape(L, w_pool.shape[1] * w_pool.shape[2], w_pool.shape[3])
    for l in range(L):
        store["w", l, "in"] = _halves(_piece_in(w_in_t, l, idx))
        store["w", l, "out"] = _halves(_piece(w_out, l, idx, "piece_out"))
        store["w", l, "gate"] = _halves(_piece(w_gate_ple, l, idx, "piece_gate"))
        store["w", l, "ple"] = _halves(_piece(w_ple, l, idx, "piece_ple"))
        store["w", l, "pool"] = _halves(_piece(w_pool2, l, idx, "piece_pool"))
    _allgather_now(wkeys(0, WEIGHTS), store)

    b_ext = _bias_to_ext(b_in)
    ps3 = pool_scale.reshape(L, 1, POOL_W)
    gain3 = ln_gain.reshape(L, 1, D)
    bias3 = ln_bias.reshape(L, 1, D)
    onehot = _bucket_onehot()
    bias_hqk = _masked_bias(_bias_fwd(rel_bias.T, onehot).reshape(N_HEADS, WINDOW, WINDOW))

    xs = x[0]
    xb = _to_bf16(xs)
    pb = _to_bf16(p.reshape(L * S, p.shape[3])).reshape(L, S, p.shape[3])
    saved = []
    x_norm, x_gain, x_bias, x_l = xs, jnp.ones((1, 1, D), F32), jnp.zeros((1, 1, D), F32), 0
    late = ["out", "ple", "pool"]
    for l in range(L):
        nxt = l + 1 < L
        job = _Job()
        if nxt:
            job.add(_ag_ici, wkeys(l + 1, ["in"]), 3)
        if l >= 1:
            job.add(_ag_fwd, wkeys(l, ["gate"]), 3)
        q, k, v, ga, u, gp = _inproj(xb, weight(l, "in"), b_ext, l, job, store)
        gl = _matmul_nn(xb, weight(l, "gate"), "gate_logits", None, None)
        job = _Job().add(_ag_fwd, wkeys(l + 1, ["in"]), 3).add(_ag_ici, wkeys(l + 1, late), 9) if nxt else None
        o, ca, lse = _attn_fwd(q, k, v, ga, bias_hqk, attn_sinks, l, job, store)
        c = _pool_fwd(u, gp, ca, weight(l, "pool"), ps3, l, None, None)
        job = _Job().add(_ag_fwd, wkeys(l + 1, late), 9).add(_ag_ici, wkeys(l + 1, ["gate"]), 3) if nxt else None
        yb, xhat, rstd = _outproj_ln(c, x_norm, x_gain, x_bias, x_l, gl, pb[l], weight(l, "out"), weight(l, "ple"), gain3, bias3, l,
                                     alpha, job, store)
        saved.append(dict(xb=xb, q=q, k=k, v=v, ga=ga, u=u, gp=gp, gl=gl, o=o, lse=lse, c=c, xhat=xhat, rstd=rstd))
        x_norm, x_gain, x_bias, x_l, xb = xhat, gain3, bias3, l, yb

    dy, loss_acc = _loss_and_grad(x_norm, gain3, bias3, L - 1, loss_target[0])
    loss = lax.psum(0.5 * loss_acc[0, 0], ("x", "y", "c"))

    shapes = {t: store["w", 0, t].shape for t in WEIGHTS}
    for t in WEIGHTS:
        store["full", t] = lax.empty((L,) + shapes[t][1:], F32)

    def rs_keys(kind, l, names):
        return [(kind, l, t) for t in names]

    def rs_pair_job(l, names):
        for t in names:
            store["ra", l, t] = lax.empty((N_CHIPS,) + shapes[t][2:], F32)
        return _Job().add(_rs_pair, rs_keys("g", l, names) + rs_keys("ra", l, names), len(names))

    def rs_pair_add(l, names):
        for t in names:
            p32, p16 = _rs_pair_add(store["g", l, t], store["ra", l, t], idx, "rs_pair_add_" + t)
            store["p32", l, t], store["p16", l, t] = p32, p16
            store["rb", l, t] = lax.empty((3,) + shapes[t][2:], BF16)

    def rs_ici_job(l, names):
        return _Job().add(_rs_ici, rs_keys("p16", l, names) + rs_keys("rb", l, names), 3 * len(names))

    def rs_chip_add(l, names):
        for t in names:
            store["full", t] = _rs_chip_add(store["p32", l, t], store["rb", l, t], store["full", t], l, idx, "rs_chip_add_" + t)

    def rs_share_job(l):
        return _Job().add(functools.partial(_rs_share, layer=l), [("full", t) for t in WEIGHTS], len(WEIGHTS))

    dbias = jnp.zeros((N_HEADS, WINDOW, WINDOW), F32)
    small = [None] * L
    rest = ["out", "gate", "ple", "pool"]
    for l in reversed(range(L)):
        sv = saved[l]
        pl_l = pb[l]
        dz, dzb, d_e, d_gl, ggain, gbias = _ln_bwd(dy, sv["xhat"], sv["rstd"], sv["gl"], pl_l, weight(l, "ple"), gain3, l, None, None)
        dc = _matmul_nt_rows(dzb, weight(l, "out"), "d_mix_in")
        g_out = _matmul_tn(sv["c"], dzb, D // N_CHIPS, D, "grad_w_out", by_rows=True)
        g_ple = _matmul_tn(pl_l, d_e, pl_l.shape[1], D // N_CHIPS, "grad_w_ple", by_rows=False)
        g_gate = _matmul_tn(sv["xb"], d_gl, D // N_CHIPS, D, "grad_w_gate", by_rows=True)
        dd, dgp, gps, g_pool = _pool_bwd(sv["u"], sv["gp"], dc, weight(l, "pool"), ps3, l)
        for t, g in zip(rest, (g_out, g_gate, g_ple, g_pool)):
            store["g", l, t] = _halves(g)
        job = rs_pair_job(l, rest)
        if l + 1 < L:
            job.add(functools.partial(_rs_share, layer=l + 1), [("full", t) for t in WEIGHTS], len(WEIGHTS))
        dq, dga, dk, dv, dbias, dsink = _attn_bwd(sv["q"], sv["k"], sv["v"], sv["ga"], sv["o"], dc, sv["lse"], bias_hqk,
                                                  attn_sinks, dbias, l, job, store)
        rs_pair_add(l, rest)
        dh, gbe = _assemble_dh(dq, dk, dv, dga, dd, dgp)
        g_in = _matmul_tn(dh, sv["xb"], SHARD_P, D, "grad_w_in", by_rows=True, job=rs_ici_job(l, ["out"]), store=store)
        store["g", l, "in"] = _halves(g_in)
        job = rs_pair_job(l, ["in"])
        job.add(_rs_ici, rs_keys("p16", l, rest[1:]) + rs_keys("rb", l, rest[1:]), 3 * len(rest[1:]))
        dx1 = _matmul_nt_rows(d_gl, weight(l, "gate"), "d_x_gate", dz, alpha, job=job, store=store)
        rs_chip_add(l, rest)
        rs_pair_add(l, ["in"])
        dy = _matmul_nn_acc(dh, weight(l, "in"), "d_x", dx1, 1.0, rs_ici_job(l, ["in"]), store)
        rs_chip_add(l, ["in"])
        small[l] = dict(b_in=_bias_from_ext(gbe)[0], sinks=dsink[0, :N_HEADS], ps=gps[0], gain=ggain[0], bias=gbias[0])
    grad_x = dy[None]

    _run_job("rs_pair_share", rs_share_job(0), store)
    full = {t: _unhalves(store["full", t]) for t in WEIGHTS}

    def t_back(a):
        return jnp.swapaxes(a, 1, 2)

    def pool4(a):
        return a.reshape(w_pool.shape)

    r_in = _adamw(w_in_t, full["in"], jnp.swapaxes(m_w_in, 1, 2), jnp.swapaxes(v_w_in, 1, 2), idx, "adamw_w_in", tr=HALF_TILE)
    gw_in, dw_in, nm_in, nv_in = [t_back(a) for a in r_in]
    gw_out, dw_out, nm_out, nv_out = _adamw(w_out, full["out"], m_w_out, v_w_out, idx, "adamw_w_out")
    gw_gate, dw_gate, nm_gate, nv_gate = _adamw(w_gate_ple, full["gate"], m_w_gate_ple, v_w_gate_ple, idx, "adamw_w_gate")
    gw_ple, dw_ple, nm_ple, nv_ple = _adamw(w_ple, full["ple"], m_w_ple, v_w_ple, idx, "adamw_w_ple")
    r_pool = _adamw(w_pool2, full["pool"], m_w_pool.reshape(w_pool2.shape), v_w_pool.reshape(w_pool2.shape), idx, "adamw_w_pool")
    gw_pool, dw_pool, nm_pool, nv_pool = [pool4(a) for a in r_pool]

    g_rel = _bias_bwd(dbias.reshape(N_HEADS, -1), onehot).T
    small_shapes = [b_in.shape, attn_sinks.shape, rel_bias.shape, pool_scale.shape, ln_gain.shape, ln_bias.shape]
    g_small = [jnp.stack([small[l]["b_in"] for l in range(L)]), jnp.stack([small[l]["sinks"] for l in range(L)]), g_rel,
               jnp.stack([small[l]["ps"] for l in range(L)]), jnp.stack([small[l]["gain"] for l in range(L)]),
               jnp.stack([small[l]["bias"] for l in range(L)])]
    packed = _small_allreduce_adamw(
        _pack_small(g_small),
        _pack_small([b_in, attn_sinks, rel_bias, pool_scale, ln_gain, ln_bias]),
        _pack_small([m_b_in, m_attn_sinks, m_rel_bias, m_pool_scale, m_ln_gain, m_ln_bias]),
        _pack_small([v_b_in, v_attn_sinks, v_rel_bias, v_pool_scale, v_ln_gain, v_ln_bias]))
    sg, sd, sm, sv_ = [_unpack_small(a, small_shapes) for a in packed]

    def order(big, sm_):
        return (big[0], sm_[0], big[1], sm_[1], sm_[2], big[2], sm_[3], big[3], big[4], sm_[4], sm_[5])

    return (loss, grad_x,
            *order((gw_in, gw_out, gw_pool, gw_ple, gw_gate), sg),
            *order((dw_in, dw_out, dw_pool, dw_ple, dw_gate), sd),
            *order((nm_in, nm_out, nm_pool, nm_ple, nm_gate), sm),
            *order((nv_in, nv_out, nv_pool, nv_ple, nv_gate), sv_))
```

```python
import functools
import math

import jax
import jax.numpy as jnp
from jax import lax
from jax.experimental import pallas as pl
from jax.experimental.pallas import tpu as pltpu

F32 = jnp.float32
BF16 = jnp.bfloat16
I32 = jnp.int32
MESH = pl.DeviceIdType.MESH

HEAD_DIM = 64
QK_SCALE = HEAD_DIM ** -0.5
WINDOW = 128
KV_RATIO = 8
POOL_WINDOWS = (2, 4, 8, 16)
POOL_HALO = 16
REL_BUCKETS = 32
REL_MAX_DIST = 128
LN_EPS = 1e-5
ADAM_LR, ADAM_B1, ADAM_B2, ADAM_EPS, ADAM_WD, ADAM_STEP = 0.001, 0.9, 0.999, 1e-08, 0.01, 10

LANES = 128
VMEM_LIMIT = 52 * 1024 * 1024
N_CHIPS = 4
N_DEV = 8

D_MODEL = 2048
ATTN_W = 1024
POOL_W = 1024
KV_W = 128
N_HEADS = ATTN_W // HEAD_DIM
N_KV = N_HEADS // KV_RATIO
IN_COLS = 4352
SHARD = IN_COLS // N_CHIPS
SHARD_P = 1152
EXT = N_CHIPS * SHARD_P
HALF_TILE = SHARD_P - SHARD
OFF_Q, OFF_KA, OFF_KB, OFF_V, OFF_GA, OFF_U, OFF_UA, OFF_UB, OFF_GP = 0, 1024, 1152, 1280, 1408, 2432, 3328, 3456, 3584
WEIGHTS = ("in", "out", "gate", "ple", "pool")


def _cparams(sem=None):
    return pltpu.CompilerParams(dimension_semantics=sem, vmem_limit_bytes=VMEM_LIMIT)


def _pcall(body, **kw):
    return pl.pallas_call(body, **kw)


def _sigmoid(x):
    return 1.0 / (1.0 + jnp.exp(-x))


def _nt(a, b):
    return lax.dot_general(a, b, (((1,), (1,)), ((), ())), preferred_element_type=F32)


def _tn(a, b):
    return lax.dot_general(a, b, (((0,), (0,)), ((), ())), preferred_element_type=F32)


def _nn(a, b):
    return jnp.dot(a, b, preferred_element_type=F32)


def _mesh_pos():
    return lax.axis_index("x"), lax.axis_index("y"), lax.axis_index("c")


def _peer_chips(x, y):
    return [(1 - x, y), (x, 1 - y), (1 - x, 1 - y)]


def _row_tile(rows, cap=256):
    t = min(rows, cap)
    while rows % t or t % 16:
        t -= 1
    return t


def _hbm_spec():
    return pl.BlockSpec(memory_space=pltpu.HBM)


def _halves(a):
    return a.reshape(a.shape[:-2] + (2, a.shape[-2] // 2, a.shape[-1]))


def _unhalves(a):
    return a.reshape(a.shape[:-3] + (2 * a.shape[-2], a.shape[-1]))


class _remote:
    def __init__(self, src, dst, send, recv, i, device):
        self.args = dict(src_ref=src, dst_ref=dst, send_sem=send.at[i], recv_sem=recv.at[i], device_id=device, device_id_type=MESH)

    def start(self):
        pltpu.make_async_remote_copy(**self.args).start()

    def wait_recv(self):
        pltpu.make_async_remote_copy(**self.args).wait_recv()

    def wait_send(self):
        pltpu.make_async_remote_copy(**self.args).wait_send()


class _Job:
    def __init__(self):
        self.keys, self.parts, self.n = [], [], 0

    def add(self, fn, keys, n):
        self.parts.append((fn, len(self.keys), len(keys), self.n))
        self.keys += list(keys)
        self.n += n
        return self

    def build(self, refs, send, recv):
        out = []
        for fn, i0, nb, base in self.parts:
            out += fn(refs[i0:i0 + nb], send, recv, base)
        return out


def _ag_ici(refs, send, recv, base):
    x, y, c = _mesh_pos()
    me = 2 * x + y
    out = []
    for t, g in enumerate(refs):
        for k, chip in enumerate(_peer_chips(x, y)):
            i = base + 3 * t + k
            dev = (*chip, c)
            out.append((_remote(g.at[me, c], g.at[me, c], send, recv, i, dev),
                        _remote(g.at[me, c], g.at[2 * chip[0] + chip[1], c], send, recv, i, dev)))
    return out


def _ag_fwd(refs, send, recv, base):
    x, y, c = _mesh_pos()
    out = []
    for t, g in enumerate(refs):
        for k, chip in enumerate(_peer_chips(x, y)):
            i = base + 3 * t + k
            slot = 2 * chip[0] + chip[1]
            dev = (x, y, 1 - c)
            out.append((_remote(g.at[slot, c], g.at[slot, c], send, recv, i, dev),
                        _remote(g.at[slot, c], g.at[slot, 1 - c], send, recv, i, dev)))
    return out


def _rs_pair(refs, send, recv, base):
    x, y, c = _mesh_pos()
    n = len(refs) // 2
    out = []
    for t in range(n):
        cp = _remote(refs[t].at[:, 1 - c], refs[n + t], send, recv, base + t, (x, y, 1 - c))
        out.append((cp, cp))
    return out


def _rs_ici(refs, send, recv, base):
    x, y, c = _mesh_pos()
    n = len(refs) // 2
    out = []
    for t in range(n):
        for k, chip in enumerate(_peer_chips(x, y)):
            cp = _remote(refs[t].at[2 * chip[0] + chip[1]], refs[n + t].at[k], send, recv, base + 3 * t + k, (*chip, c))
            out.append((cp, cp))
    return out


def _rs_share(refs, send, recv, base, layer):
    x, y, c = _mesh_pos()
    out = []
    for t, f in enumerate(refs):
        dev = (x, y, 1 - c)
        out.append((_remote(f.at[layer, c], f.at[layer, c], send, recv, base + t, dev),
                    _remote(f.at[layer, c], f.at[layer, 1 - c], send, recv, base + t, dev)))
    return out


def _call(body, *, name, grid, in_specs, out_specs, out_shape, args, scratch_shapes=(), sem=None, job=None, store=None):
    in_specs, out_specs, out_shape, scratch_shapes = list(in_specs), list(out_specs), list(out_shape), list(scratch_shapes)
    if job is None or job.n == 0:
        return list(_pcall(body, name=name, grid=grid, in_specs=in_specs, out_specs=out_specs, out_shape=out_shape,
                           scratch_shapes=scratch_shapes, compiler_params=_cparams(sem))(*args))
    bufs = [store[k] for k in job.keys]
    nb, n_in, n_out, n_sc = len(bufs), len(args), len(out_shape), len(scratch_shapes)

    def wrapped(*refs):
        ins = refs[:n_in]
        outs = refs[n_in + nb:n_in + nb + n_out]
        cb = refs[n_in + nb + n_out:n_in + 2 * nb + n_out]
        scratch = refs[n_in + 2 * nb + n_out:n_in + 2 * nb + n_out + n_sc]
        send, recv = refs[-2:]
        ids = [pl.program_id(a) for a in range(len(grid))]
        first = functools.reduce(jnp.logical_and, [i == 0 for i in ids])
        last = functools.reduce(jnp.logical_and, [i == g - 1 for i, g in zip(ids, grid)])

        @pl.when(first)
        def _():
            for s, _r in job.build(cb, send, recv):
                s.start()

        body(*ins, *outs, *scratch)

        @pl.when(last)
        def _():
            pairs = job.build(cb, send, recv)
            for _s, r in pairs:
                r.wait_recv()
            for s, _r in pairs:
                s.wait_send()

    res = _pcall(
        wrapped, name=name, grid=grid, in_specs=in_specs + [_hbm_spec()] * nb, out_specs=out_specs + [_hbm_spec()] * nb,
        out_shape=out_shape + [jax.ShapeDtypeStruct(b.shape, b.dtype) for b in bufs],
        scratch_shapes=scratch_shapes + [pltpu.SemaphoreType.DMA((job.n,)), pltpu.SemaphoreType.DMA((job.n,))],
        input_output_aliases={n_in + i: n_out + i for i in range(nb)},
        compiler_params=_cparams(("arbitrary",) * len(grid)))(*args, *bufs)
    for k, v in zip(job.keys, res[n_out:]):
        store[k] = v
    return list(res[:n_out])


def _run_job(name, job, store):
    bufs = [store[k] for k in job.keys]
    nb = len(bufs)

    def body(*refs):
        send, recv = refs[-2:]
        pairs = job.build(refs[nb:2 * nb], send, recv)
        for s, _r in pairs:
            s.start()
        for _s, r in pairs:
            r.wait_recv()
        for s, _r in pairs:
            s.wait_send()

    res = _pcall(body, name=name, in_specs=[_hbm_spec()] * nb, out_specs=[_hbm_spec()] * nb,
                 out_shape=[jax.ShapeDtypeStruct(b.shape, b.dtype) for b in bufs],
                 scratch_shapes=[pltpu.SemaphoreType.DMA((job.n,)), pltpu.SemaphoreType.DMA((job.n,))],
                 input_output_aliases={i: i for i in range(nb)})(*bufs)
    for k, v in zip(job.keys, res):
        store[k] = v


def _allgather_now(keys, store):
    bufs = [store[k] for k in keys]
    nb = len(bufs)

    def body(*refs):
        send, recv = refs[-2:]
        g = refs[nb:2 * nb]
        ici = _ag_ici(g, send, recv, 0)
        fwd = _ag_fwd(g, send, recv, 3 * nb)
        for s, _r in ici:
            s.start()
        for (_s, r), (fs, _fr) in zip(ici, fwd):
            r.wait_recv()
            fs.start()
        for _fs, fr in fwd:
            fr.wait_recv()
        for s, _r in ici + fwd:
            s.wait_send()

    res = _pcall(body, name="allgather_first_layer", in_specs=[_hbm_spec()] * nb, out_specs=[_hbm_spec()] * nb,
                 out_shape=[jax.ShapeDtypeStruct(b.shape, b.dtype) for b in bufs],
                 scratch_shapes=[pltpu.SemaphoreType.DMA((6 * nb,)), pltpu.SemaphoreType.DMA((6 * nb,))],
                 input_output_aliases={i: i for i in range(nb)})(*bufs)
    for k, v in zip(keys, res):
        store[k] = v


def _piece(w, l, idx, name):
    _, R, C = w.shape
    tr = _row_tile(R)

    def body(s_ref, w_ref, o_ref):
        del s_ref
        o_ref[...] = w_ref[...].astype(BF16)

    gs = pltpu.PrefetchScalarGridSpec(
        num_scalar_prefetch=1, grid=(R // tr,),
        in_specs=[pl.BlockSpec((None, tr, C), lambda r, s: (l, r, 0))],
        out_specs=pl.BlockSpec((None, tr, C), lambda r, s: (s[0], r, 0)))
    return _pcall(body, name=name, grid_spec=gs, out_shape=jax.ShapeDtypeStruct((N_CHIPS, R, C), BF16),
                  compiler_params=_cparams(("parallel",)))(idx, w)


def _piece_in(w_in_t, l, idx):
    _, R, D = w_in_t.shape
    tr = HALF_TILE
    nsrc = R // tr

    def body(s_ref, w_ref, o_ref):
        src = pl.program_id(0) - s_ref[1]
        ok = jnp.logical_and(src >= 0, src < nsrc)
        o_ref[...] = jnp.where(ok, w_ref[...], 0.0).astype(BF16)

    gs = pltpu.PrefetchScalarGridSpec(
        num_scalar_prefetch=1, grid=(SHARD_P // tr,),
        in_specs=[pl.BlockSpec((None, tr, D), lambda r, s: (l, jnp.clip(r - s[1], 0, nsrc - 1), 0))],
        out_specs=pl.BlockSpec((None, tr, D), lambda r, s: (s[0], r, 0)))
    return _pcall(body, name="piece_in", grid_spec=gs, out_shape=jax.ShapeDtypeStruct((N_CHIPS, SHARD_P, D), BF16),
                  compiler_params=_cparams(("parallel",)))(idx, w_in_t)


def _rs_pair_add(g5, r4, idx, name):
    J, _, h, C = g5.shape
    th = _row_tile(h)

    def body(s_ref, g_ref, r_ref, o32_ref, o16_ref):
        s = g_ref[...] + r_ref[...]
        o16_ref[...] = s.astype(BF16)

        @pl.when(pl.program_id(1) == s_ref[0])
        def _():
            o32_ref[...] = s

    spec = pl.BlockSpec((None, th, C), lambda r, j, s: (j, r, 0))
    gs = pltpu.PrefetchScalarGridSpec(
        num_scalar_prefetch=1, grid=(h // th, J),
        in_specs=[pl.BlockSpec((None, None, th, C), lambda r, j, s: (j, s[2], r, 0)), spec],
        out_specs=[pl.BlockSpec((th, C), lambda r, j, s: (r, 0)), spec])
    return _pcall(body, name=name, grid_spec=gs,
                  out_shape=[jax.ShapeDtypeStruct((h, C), F32), jax.ShapeDtypeStruct((J, h, C), BF16)],
                  compiler_params=_cparams(("parallel", "arbitrary")))(idx, g5, r4)


def _rs_chip_add(p32, r3, full, l, idx, name):
    h, C = p32.shape
    th = _row_tile(h)

    def body(s_ref, p_ref, r_ref, f_ref, o_ref):
        del s_ref, f_ref
        o_ref[...] = ((p_ref[...] + r_ref[0].astype(F32)) + r_ref[1].astype(F32)) + r_ref[2].astype(F32)

    gs = pltpu.PrefetchScalarGridSpec(
        num_scalar_prefetch=1, grid=(h // th,),
        in_specs=[pl.BlockSpec((th, C), lambda r, s: (r, 0)),
                  pl.BlockSpec((3, th, C), lambda r, s: (0, r, 0)),
                  pl.BlockSpec(memory_space=pl.ANY)],
        out_specs=pl.BlockSpec((None, None, th, C), lambda r, s: (l, s[2], r, 0)))
    return _pcall(body, name=name, grid_spec=gs, out_shape=jax.ShapeDtypeStruct(full.shape, F32),
                  input_output_aliases={3: 0}, compiler_params=_cparams(("parallel",)))(idx, p32, r3, full)


def _adamw_math(w, g, m, v):
    nm = ADAM_B1 * m + (1.0 - ADAM_B1) * g
    nv = ADAM_B2 * v + (1.0 - ADAM_B2) * (g * g)
    m_hat = nm / (1.0 - ADAM_B1 ** ADAM_STEP)
    v_hat = nv / (1.0 - ADAM_B2 ** ADAM_STEP)
    delta = -ADAM_LR * (m_hat / (jnp.sqrt(v_hat) + ADAM_EPS) + ADAM_WD * w)
    return delta, nm, nv


def _adamw(w, g, m, v, idx, name, tr=None):
    L, R, C = w.shape
    Rg = g.shape[1]
    tr = tr or _row_tile(R)
    shift = (Rg - R) // tr
    assert (Rg - R) % tr == 0

    def body(s_ref, w_ref, g_ref, m_ref, v_ref, go_ref, d_ref, nm_ref, nv_ref):
        del s_ref
        gv = g_ref[...]
        d, nm, nv = _adamw_math(w_ref[...], gv, m_ref[...], v_ref[...])
        go_ref[...] = gv
        d_ref[...] = d
        nm_ref[...] = nm
        nv_ref[...] = nv

    wspec = pl.BlockSpec((None, tr, C), lambda l, r, s: (l, r, 0))
    gs = pltpu.PrefetchScalarGridSpec(
        num_scalar_prefetch=1, grid=(L, R // tr),
        in_specs=[wspec, pl.BlockSpec((None, tr, C), lambda l, r, s: (l, r + shift * s[1], 0)), wspec, wspec],
        out_specs=[wspec, wspec, wspec, wspec])
    sds = jax.ShapeDtypeStruct((L, R, C), F32)
    return _pcall(body, name=name, grid_spec=gs, out_shape=[sds, sds, sds, sds],
                  compiler_params=_cparams(("parallel", "parallel")))(idx, w, g, m, v)


def _small_allreduce_adamw(gv, wv, mv, vv):
    NR = gv.shape[0]

    def body(g_ref, w_ref, m_ref, v_ref, go_ref, d_ref, nm_ref, nv_ref, gath, send, recv):
        x, y, c = _mesh_pos()
        rank = 4 * x + 2 * y + c
        gath[rank] = g_ref[...]
        cps = []
        for msk in range(1, N_DEV):
            bx, by, bc = (msk >> 2) & 1, (msk >> 1) & 1, msk & 1
            peer = (1 - x if bx else x, 1 - y if by else y, 1 - c if bc else c)
            cps.append(pltpu.make_async_remote_copy(src_ref=g_ref, dst_ref=gath.at[rank], send_sem=send.at[msk - 1],
                                                    recv_sem=recv.at[msk - 1], device_id=peer, device_id_type=MESH))
        for cp in cps:
            cp.start()
        for msk in range(1, N_DEV):
            bx, by, bc = (msk >> 2) & 1, (msk >> 1) & 1, msk & 1
            peer = (1 - x if bx else x, 1 - y if by else y, 1 - c if bc else c)
            prank = 4 * peer[0] + 2 * peer[1] + peer[2]
            pltpu.make_async_remote_copy(src_ref=g_ref, dst_ref=gath.at[prank], send_sem=send.at[msk - 1],
                                         recv_sem=recv.at[msk - 1], device_id=peer, device_id_type=MESH).wait_recv()
        for cp in cps:
            cp.wait_send()
        tot = gath[0]
        for r in range(1, N_DEV):
            tot = tot + gath[r]
        d, nm, nv = _adamw_math(w_ref[...], tot, m_ref[...], v_ref[...])
        go_ref[...] = tot
        d_ref[...] = d
        nm_ref[...] = nm
        nv_ref[...] = nv

    vm = pl.BlockSpec(memory_space=pltpu.VMEM)
    sds = jax.ShapeDtypeStruct((NR, LANES), F32)
    return _pcall(body, name="small_allreduce_adamw", in_specs=[vm, vm, vm, vm], out_specs=[vm, vm, vm, vm],
                  out_shape=[sds, sds, sds, sds],
                  scratch_shapes=[pltpu.VMEM((N_DEV, NR, LANES), F32), pltpu.SemaphoreType.DMA((N_DEV - 1,)),
                                  pltpu.SemaphoreType.DMA((N_DEV - 1,))],
                  compiler_params=pltpu.CompilerParams(vmem_limit_bytes=VMEM_LIMIT))(gv, wv, mv, vv)


def _split3(a):
    h1 = a.astype(BF16)
    r1 = a - h1.astype(F32)
    h2 = r1.astype(BF16)
    h3 = (r1 - h2.astype(F32)).astype(BF16)
    return h1, h2, h3


def _folded_dist():
    r = jnp.arange(WINDOW)[:, None]
    j = jnp.arange(WINDOW)[None, :]
    return jnp.where(j > r, r + WINDOW - j, r - j)


def _bucket_onehot():
    d = _folded_dist()
    max_exact = REL_BUCKETS // 2
    d_f = jnp.maximum(d, 1).astype(F32)
    large = max_exact + (jnp.log(d_f / max_exact) / math.log(REL_MAX_DIST / max_exact) * (REL_BUCKETS - max_exact)).astype(I32)
    large = jnp.minimum(large, REL_BUCKETS - 1)
    bucket = jnp.where(d < max_exact, d, large)
    oh = bucket[None] == jnp.arange(REL_BUCKETS)[:, None, None]
    return oh.reshape(REL_BUCKETS, WINDOW * WINDOW).astype(BF16)


def _bias_fwd(rel_bias_t, onehot):
    H, N = rel_bias_t.shape[0], onehot.shape[1]
    tn = 4096

    def body(t_ref, oh_ref, o_ref):
        h1, h2, h3 = _split3(t_ref[...])
        oh = oh_ref[...]
        o_ref[...] = (_nn(h1, oh) + _nn(h2, oh)) + _nn(h3, oh)

    return _pcall(body, name="bias_fwd", grid=(N // tn,),
                  in_specs=[pl.BlockSpec((H, REL_BUCKETS), lambda i: (0, 0)), pl.BlockSpec((REL_BUCKETS, tn), lambda i: (0, i))],
                  out_specs=pl.BlockSpec((H, tn), lambda i: (0, i)), out_shape=jax.ShapeDtypeStruct((H, N), F32),
                  compiler_params=_cparams(("parallel",)))(rel_bias_t, onehot)


def _bias_bwd(dbias, onehot):
    H, N = dbias.shape
    tn = 4096

    def body(d_ref, oh_ref, o_ref):
        @pl.when(pl.program_id(0) == 0)
        def _():
            o_ref[...] = jnp.zeros_like(o_ref)
        h1, h2, h3 = _split3(d_ref[...])
        oh = oh_ref[...]
        o_ref[...] += (_nt(h1, oh) + _nt(h2, oh)) + _nt(h3, oh)

    return _pcall(body, name="bias_bwd", grid=(N // tn,),
                  in_specs=[pl.BlockSpec((H, tn), lambda i: (0, i)), pl.BlockSpec((REL_BUCKETS, tn), lambda i: (0, i))],
                  out_specs=pl.BlockSpec((H, REL_BUCKETS), lambda i: (0, 0)), out_shape=jax.ShapeDtypeStruct((H, REL_BUCKETS), F32),
                  compiler_params=_cparams(("arbitrary",)))(dbias, onehot)


def _to_bf16(x):
    S, D = x.shape
    tm = min(S, 512)

    def body(x_ref, o_ref):
        o_ref[...] = x_ref[...].astype(BF16)

    return _pcall(body, name="to_bf16", grid=(S // tm,), in_specs=[pl.BlockSpec((tm, D), lambda i: (i, 0))],
                  out_specs=pl.BlockSpec((tm, D), lambda i: (i, 0)), out_shape=jax.ShapeDtypeStruct((S, D), BF16),
                  compiler_params=_cparams(("parallel",)))(x)


def _inproj(xb, w_t, b_ext, l, job, store):
    S, D = xb.shape
    tm = min(S, 512)

    def body(x_ref, w_ref, b_ref, q_ref, k_ref, v_ref, ga_ref, u_ref, gp_ref):
        j = pl.program_id(1)
        acc = _nt(x_ref[...], w_ref[...]) + b_ref[...]

        @pl.when(j == 0)
        def _():
            q_ref[...] = (acc[:, :1024] * QK_SCALE).astype(BF16)
            k_ref[...] = acc[:, 1024:1152]

        @pl.when(j == 1)
        def _():
            k_ref[...] += acc[:, 0:128]
            v_ref[...] = acc[:, 128:256]
            ga_ref[:, 0:896] = acc[:, 256:1152]

        @pl.when(j == 2)
        def _():
            ga_ref[:, 896:1024] = acc[:, 0:128]
            u_ref[:, 0:896] = acc[:, 128:1024]
            u_ref[:, 896:1024] = acc[:, 1024:1152]

        @pl.when(j == 3)
        def _():
            u_ref[:, 896:1024] += acc[:, 0:128]
            gp_ref[...] = acc[:, 128:1152]

    def ospec(w):
        return pl.BlockSpec((tm, w), lambda i, j: (i, 0))

    return _call(
        body, name="inproj", grid=(S // tm, N_CHIPS),
        in_specs=[pl.BlockSpec((tm, D), lambda i, j: (i, 0)),
                  pl.BlockSpec((None, SHARD_P, D), lambda i, j: (j, 0, 0)),
                  pl.BlockSpec((None, 1, SHARD_P), lambda i, j: (l, 0, j))],
        out_specs=[ospec(ATTN_W), ospec(KV_W), ospec(KV_W), ospec(ATTN_W), ospec(POOL_W), ospec(POOL_W)],
        out_shape=[jax.ShapeDtypeStruct((S, ATTN_W), BF16), jax.ShapeDtypeStruct((S, KV_W), F32), jax.ShapeDtypeStruct((S, KV_W), F32),
                   jax.ShapeDtypeStruct((S, ATTN_W), F32), jax.ShapeDtypeStruct((S, POOL_W), F32), jax.ShapeDtypeStruct((S, POOL_W), F32)],
        args=(xb, w_t, b_ext), sem=("parallel", "arbitrary"), job=job, store=store)


def _matmul_nn(a, b4, name, job, store):
    S, K = a.shape
    N = b4.shape[2]
    kq = b4.shape[1]
    tm, tn = min(S, 512), min(N, 1024)

    def body(a_ref, b_ref, o_ref):
        acc = _nn(a_ref[:, 0:kq], b_ref[0])
        for j in range(1, N_CHIPS):
            acc = acc + _nn(a_ref[:, j * kq:(j + 1) * kq], b_ref[j])
        o_ref[...] = acc

    return _call(body, name=name, grid=(N // tn, S // tm),
                 in_specs=[pl.BlockSpec((tm, K), lambda n, i: (i, 0)), pl.BlockSpec((N_CHIPS, kq, tn), lambda n, i: (0, 0, n))],
                 out_specs=[pl.BlockSpec((tm, tn), lambda n, i: (i, n))], out_shape=[jax.ShapeDtypeStruct((S, N), F32)],
                 args=(a, b4), sem=("parallel", "parallel"), job=job, store=store)[0]


def _masked_bias(bias):
    r = jnp.arange(WINDOW)[:, None]
    j = jnp.arange(WINDOW)[None, :]
    return jnp.stack([jnp.where(j > r, -1e30, bias), bias])


def _fold(full, tri):
    return jnp.where(tri, full[:, :WINDOW], full[:, WINDOW:])


def _unfold(folded, tri):
    return jnp.concatenate([jnp.where(tri, folded, 0.0).astype(BF16), jnp.where(tri, 0.0, folded).astype(BF16)], axis=1)


def _dup_heads(src_ref, dst_ref):
    a = src_ref[...]
    r = pltpu.roll(a, HEAD_DIM, axis=1)
    lo = lax.broadcasted_iota(I32, a.shape, 1) < HEAD_DIM
    dst_ref[0] = jnp.where(lo, a, r).astype(BF16)
    dst_ref[1] = jnp.where(lo, r, a).astype(BF16)


def _kv_block(ref, h, prev, cur):
    return jnp.concatenate([ref[h, pl.ds(prev, WINDOW), :], ref[h, pl.ds(cur, WINDOW), :]], axis=0)


def _rows(i):
    return slice(i * WINDOW, (i + 1) * WINDOW)


def _attn_fwd(q, k, v, ga, bias, sinks, l, job, store):
    S = q.shape[0]
    nblk = S // WINDOW

    def body(sink_ref, q_ref, k_ref, v_ref, ga_ref, bias_ref, o_ref, ca_ref, lse_ref, kd, vd, qs, ss, ps, os_):
        n = pl.program_id(0)

        @pl.when(n == 0)
        def _():
            _dup_heads(k_ref, kd)
            _dup_heads(v_ref, vd)

        cur = pl.multiple_of(n * WINDOW, WINDOW)
        prev = pl.multiple_of(jnp.maximum(n - 1, 0) * WINDOW, WINDOW)
        lane = lax.broadcasted_iota(I32, (WINDOW, LANES), 1)
        lo = lane < HEAD_DIM
        tri = lane > lax.broadcasted_iota(I32, (WINDOW, LANES), 0)
        lse_mat = jnp.zeros((WINDOW, LANES), F32)
        for g in range(N_KV):
            for i in range(KV_RATIO):
                h = g * KV_RATIO + i
                qp = q_ref[:, LANES * (h // 2):LANES * (h // 2 + 1)]
                qs[_rows(i), :] = jnp.where(lo if h % 2 == 0 else jnp.logical_not(lo), qp, jnp.zeros_like(qp))
            ss[...] = _nt(qs[...], _kv_block(kd, g, prev, cur))
            for i in range(KV_RATIO):
                h = g * KV_RATIO + i
                s = _fold(ss[_rows(i), :], tri) + bias_ref[h]
                sink = sink_ref[l, h]
                m = jnp.maximum(jnp.max(s, axis=1, keepdims=True), sink)
                e = jnp.exp(s - m)
                den = jnp.sum(e, axis=1, keepdims=True) + jnp.exp(sink - m)
                ps[_rows(i), :] = _unfold(e * (1.0 / den), tri)
                lse_mat = jnp.where(lane == h, m + jnp.log(den), lse_mat)
            os_[...] = _nn(ps[...], _kv_block(vd, g, prev, cur))
            for j in range(KV_RATIO // 2):
                pair = g * (KV_RATIO // 2) + j
                sl = slice(LANES * pair, LANES * (pair + 1))
                o_pair = jnp.where(lo, os_[_rows(2 * j), :], os_[_rows(2 * j + 1), :])
                o_ref[:, sl] = o_pair
                gav = ga_ref[:, sl]
                ca_ref[:, sl] = (o_pair * (gav * _sigmoid(gav))).astype(BF16)
        lse_ref[...] = lse_mat

    blk = pl.BlockSpec((WINDOW, ATTN_W), lambda n: (n, 0))
    full_kv = pl.BlockSpec((S, KV_W), lambda n: (0, 0))
    stack = KV_RATIO * WINDOW
    return _call(
        body, name="attn_fwd", grid=(nblk,),
        in_specs=[pl.BlockSpec(memory_space=pltpu.SMEM), blk, full_kv, full_kv, blk,
                  pl.BlockSpec((None, N_HEADS, WINDOW, WINDOW), lambda n: (jnp.minimum(n, 1), 0, 0, 0))],
        out_specs=[blk, blk, pl.BlockSpec((WINDOW, LANES), lambda n: (n, 0))],
        out_shape=[jax.ShapeDtypeStruct((S, ATTN_W), F32), jax.ShapeDtypeStruct((S, ATTN_W), BF16), jax.ShapeDtypeStruct((S, LANES), F32)],
        scratch_shapes=[pltpu.VMEM((N_KV, S, LANES), BF16), pltpu.VMEM((N_KV, S, LANES), BF16),
                        pltpu.VMEM((stack, LANES), BF16), pltpu.VMEM((stack, 2 * WINDOW), F32),
                        pltpu.VMEM((stack, 2 * WINDOW), BF16), pltpu.VMEM((stack, LANES), F32)],
        args=(sinks, q, k, v, ga, bias), sem=("arbitrary",), job=job, store=store)


def _pool_diff(u, halo, tile_index, tm):
    gw = POOL_W // len(POOL_WINDOWS)
    xh = jnp.concatenate([halo, u], axis=0)
    sums = []
    s = xh
    for step in (1, 2, 4, 8):
        s = s + pltpu.roll(s, step, axis=0)
        sums.append(s)
    t = tile_index * tm + lax.broadcasted_iota(I32, (tm, gw), 0)
    diffs = []
    for g, w in enumerate(POOL_WINDOWS):
        cols = slice(g * gw, (g + 1) * gw)
        cnt = jnp.minimum(t + 1, w).astype(F32)
        diffs.append(sums[g][POOL_HALO:, cols] / cnt - u[:, cols])
    return diffs


def _pool_weight(wp_ref, g):
    r = wp_ref.shape[1] // len(POOL_WINDOWS)
    return jnp.concatenate([wp_ref[j, g * r:(g + 1) * r, :] for j in range(N_CHIPS)], axis=0)


def _pool_fwd(u, gp, ca, w_pool, ps, l, job, store):
    S = u.shape[0]
    tm = min(S, 256)
    hb = tm // POOL_HALO
    gw = POOL_W // len(POOL_WINDOWS)

    def body(u_ref, uh_ref, gp_ref, ca_ref, wp_ref, ps_ref, c_ref):
        i = pl.program_id(0)
        uv = u_ref[...]
        halo = jnp.where(i > 0, uh_ref[...], 0.0)
        diffs = _pool_diff(uv, halo, i, tm)
        c_ref[:, 0:ATTN_W] = ca_ref[...]
        for g in range(len(POOL_WINDOWS)):
            cols = slice(g * gw, (g + 1) * gw)
            mm = _nn(diffs[g].astype(BF16), _pool_weight(wp_ref, g))
            gpv = gp_ref[:, cols]
            b = (mm * ps_ref[:, cols]) * (gpv * _sigmoid(gpv))
            c_ref[:, ATTN_W + g * gw:ATTN_W + (g + 1) * gw] = b.astype(BF16)

    row = pl.BlockSpec((tm, POOL_W), lambda i: (i, 0))
    return _call(
        body, name="pool_fwd", grid=(S // tm,),
        in_specs=[row, pl.BlockSpec((POOL_HALO, POOL_W), lambda i: (jnp.maximum(i * hb - 1, 0), 0)), row, row,
                  pl.BlockSpec(w_pool.shape, lambda i: (0, 0, 0)),
                  pl.BlockSpec((None, 1, POOL_W), lambda i: (l, 0, 0))],
        out_specs=[pl.BlockSpec((tm, ATTN_W + POOL_W), lambda i: (i, 0))],
        out_shape=[jax.ShapeDtypeStruct((S, ATTN_W + POOL_W), BF16)],
        args=(u, u, gp, ca, w_pool, ps), sem=("parallel",), job=job, store=store)[0]


def _ple_embed(p_ref, wple_ref):
    pb = p_ref[...].astype(BF16)
    return jnp.concatenate([_nn(pb, wple_ref[j]) for j in range(N_CHIPS)], axis=1)


def _outproj_ln(c, xh_in, gain_in, bias_in, l_in, gl, p, w_out, w_ple, gain, bias, l, alpha, job, store):
    S, D = xh_in.shape
    tm = min(S, 256)
    kq = D // N_CHIPS

    def body(c_ref, x_ref, gi_ref, bi_ref, gl_ref, p_ref, wo_ref, wp_ref, gain_ref, bias_ref, yb_ref, xh_ref, rs_ref):
        mix = _nn(c_ref[:, 0:kq], wo_ref[0])
        for j in range(1, N_CHIPS):
            mix = mix + _nn(c_ref[:, j * kq:(j + 1) * kq], wo_ref[j])
        ple = _sigmoid(gl_ref[...]) * _ple_embed(p_ref, wp_ref)
        x = x_ref[...] * gi_ref[...] + bi_ref[...]
        z = (alpha * x + mix) + ple
        mu = jnp.mean(z, axis=1, keepdims=True)
        zc = z - mu
        var = jnp.mean(zc * zc, axis=1, keepdims=True)
        rstd = lax.rsqrt(var + LN_EPS)
        xhat = zc * rstd
        yb_ref[...] = (xhat * gain_ref[...] + bias_ref[...]).astype(BF16)
        xh_ref[...] = xhat
        rs_ref[...] = rstd

    row = pl.BlockSpec((tm, D), lambda i: (i, 0))
    vec = pl.BlockSpec((None, 1, D), lambda i: (l, 0, 0))
    vec_in = pl.BlockSpec((None, 1, D), lambda i: (l_in, 0, 0))
    return _call(
        body, name="outproj_ln", grid=(S // tm,),
        in_specs=[row, row, vec_in, vec_in, row, pl.BlockSpec((tm, p.shape[1]), lambda i: (i, 0)),
                  pl.BlockSpec(w_out.shape, lambda i: (0, 0, 0)), pl.BlockSpec(w_ple.shape, lambda i: (0, 0, 0)), vec, vec],
        out_specs=[row, row, pl.BlockSpec((tm, 1), lambda i: (i, 0))],
        out_shape=[jax.ShapeDtypeStruct((S, D), BF16), jax.ShapeDtypeStruct((S, D), F32), jax.ShapeDtypeStruct((S, 1), F32)],
        args=(c, xh_in, gain_in, bias_in, gl, p, w_out, w_ple, gain, bias), sem=("parallel",), job=job, store=store)


def _loss_and_grad(xhat, gain, bias, l, target):
    S, D = xhat.shape
    tm = min(S, 512)

    def body(xh_ref, g_ref, b_ref, t_ref, dy_ref, acc_ref):
        @pl.when(pl.program_id(0) == 0)
        def _():
            acc_ref[...] = jnp.zeros_like(acc_ref)
        d = (xh_ref[...] * g_ref[...] + b_ref[...]) - t_ref[...]
        dy_ref[...] = d * (1.0 / D)
        acc_ref[...] += jnp.sum(jnp.mean(d * d, axis=1, keepdims=True), axis=0, keepdims=True)

    row = pl.BlockSpec((tm, D), lambda i: (i, 0))
    vec = pl.BlockSpec((None, 1, D), lambda i: (l, 0, 0))
    return _pcall(body, name="loss", grid=(S // tm,), in_specs=[row, vec, vec, row],
                  out_specs=[row, pl.BlockSpec((8, LANES), lambda i: (0, 0))],
                  out_shape=[jax.ShapeDtypeStruct((S, D), F32), jax.ShapeDtypeStruct((8, LANES), F32)],
                  compiler_params=_cparams(("arbitrary",)))(xhat, gain, bias, target)


def _ln_bwd(dy, xhat, rstd, gl, p, w_ple, gain, l, job, store):
    S, D = dy.shape
    tm = min(S, 256)

    def body(dy_ref, xh_ref, rs_ref, gl_ref, p_ref, wp_ref, gain_ref, dz_ref, dzb_ref, de_ref, dgl_ref, gg_ref, gb_ref):
        @pl.when(pl.program_id(0) == 0)
        def _():
            gg_ref[...] = jnp.zeros_like(gg_ref)
            gb_ref[...] = jnp.zeros_like(gb_ref)
        dyv = dy_ref[...]
        xh = xh_ref[...]
        dxh = dyv * gain_ref[...]
        m1 = jnp.mean(dxh, axis=1, keepdims=True)
        m2 = jnp.mean(dxh * xh, axis=1, keepdims=True)
        dz = rs_ref[...] * ((dxh - m1) - xh * m2)
        gg_ref[...] += jnp.sum(dyv * xh, axis=0, keepdims=True)
        gb_ref[...] += jnp.sum(dyv, axis=0, keepdims=True)
        sg = _sigmoid(gl_ref[...])
        e = _ple_embed(p_ref, wp_ref)
        dz_ref[...] = dz
        dzb_ref[...] = dz.astype(BF16)
        de_ref[...] = (dz * sg).astype(BF16)
        dgl_ref[...] = ((dz * e) * (sg * (1.0 - sg))).astype(BF16)

    row = pl.BlockSpec((tm, D), lambda i: (i, 0))
    vec_in = pl.BlockSpec((None, 1, D), lambda i: (l, 0, 0))
    vec_out = pl.BlockSpec((1, D), lambda i: (0, 0))
    bsd = jax.ShapeDtypeStruct((S, D), BF16)
    return _call(
        body, name="ln_bwd", grid=(S // tm,),
        in_specs=[row, row, pl.BlockSpec((tm, 1), lambda i: (i, 0)), row, pl.BlockSpec((tm, p.shape[1]), lambda i: (i, 0)),
                  pl.BlockSpec(w_ple.shape, lambda i: (0, 0, 0)), vec_in],
        out_specs=[row, row, row, row, vec_out, vec_out],
        out_shape=[jax.ShapeDtypeStruct((S, D), F32), bsd, bsd, bsd, jax.ShapeDtypeStruct((1, D), F32), jax.ShapeDtypeStruct((1, D), F32)],
        args=(dy, xhat, rstd, gl, p, w_ple, gain), sem=("arbitrary",), job=job, store=store)


def _matmul_nn_acc(a, b4, name, add, add_scale, job, store):
    S = a.shape[0]
    KS, tk, N = b4.shape
    tm, tn = min(S, 512), min(N, 1024)

    def body(a_ref, b_ref, add_ref, o_ref):
        acc = add_scale * add_ref[...]
        for k in range(KS):
            acc = acc + _nn(a_ref[:, k * tk:(k + 1) * tk], b_ref[k])
        o_ref[...] = acc

    return _call(body, name=name, grid=(N // tn, S // tm),
                 in_specs=[pl.BlockSpec((tm, KS * tk), lambda n, i: (i, 0)),
                           pl.BlockSpec((KS, tk, tn), lambda n, i: (0, 0, n)),
                           pl.BlockSpec((tm, tn), lambda n, i: (i, n))],
                 out_specs=[pl.BlockSpec((tm, tn), lambda n, i: (i, n))], out_shape=[jax.ShapeDtypeStruct((S, N), F32)],
                 args=(a, b4, add), sem=("parallel", "parallel"), job=job, store=store)[0]


def _matmul_nt_rows(a, b4, name, add=None, add_scale=1.0, job=None, store=None):
    S, K = a.shape
    nq = b4.shape[1]
    tm = min(S, 512)
    out_spec = pl.BlockSpec((tm, N_CHIPS * nq), lambda i: (i, 0))
    in_specs = [pl.BlockSpec((tm, K), lambda i: (i, 0)), pl.BlockSpec(b4.shape, lambda i: (0, 0, 0))]
    if add is None:
        def body(a_ref, b_ref, o_ref):
            av = a_ref[...]
            for j in range(N_CHIPS):
                o_ref[:, j * nq:(j + 1) * nq] = _nt(av, b_ref[j])
        args = (a, b4)
    else:
        def body(a_ref, b_ref, add_ref, o_ref):
            av = a_ref[...]
            for j in range(N_CHIPS):
                cols = slice(j * nq, (j + 1) * nq)
                o_ref[:, cols] = _nt(av, b_ref[j]) + add_scale * add_ref[:, cols]
        in_specs.append(out_spec)
        args = (a, b4, add)

    return _call(body, name=name, grid=(S // tm,), in_specs=in_specs,
                 out_specs=[out_spec], out_shape=[jax.ShapeDtypeStruct((S, N_CHIPS * nq), F32)],
                 args=args, sem=("parallel",), job=job, store=store)[0]


def _matmul_tn(a, b, R, C, name, by_rows, job=None, store=None):
    S = a.shape[0]
    tn = min(C, 1024 if R <= 512 else 512)
    nt = C // tn

    def body(a_ref, b_ref, o_ref, at_ref):
        @pl.when(pl.program_id(1) == 0)
        def _():
            at_ref[...] = a_ref[...].T
        o_ref[...] = _nn(at_ref[...], b_ref[...])

    if by_rows:
        a_spec = pl.BlockSpec((S, R), lambda j, n: (0, j))
        b_spec = pl.BlockSpec((S, tn), lambda j, n: (0, n))
    else:
        a_spec = pl.BlockSpec((S, R), lambda j, n: (0, 0))
        b_spec = pl.BlockSpec((S, tn), lambda j, n: (0, j * nt + n))
    return _call(body, name=name, grid=(N_CHIPS, nt), in_specs=[a_spec, b_spec],
                 out_specs=[pl.BlockSpec((None, R, tn), lambda j, n: (j, 0, n))],
                 out_shape=[jax.ShapeDtypeStruct((N_CHIPS, R, C), F32)],
                 scratch_shapes=[pltpu.VMEM((R, S), BF16)],
                 args=(a, b), sem=("parallel", "arbitrary"), job=job, store=store)[0]


def _pool_bwd(u, gp, dc, w_pool, ps, l):
    S = u.shape[0]
    tm = min(S, 256)
    hb = tm // POOL_HALO
    ngrp = len(POOL_WINDOWS)
    gw = POOL_W // ngrp
    rr = gw // N_CHIPS

    def body(u_ref, uh_ref, gp_ref, dc_ref, wp_ref, ps_ref, dd_ref, dgp_ref, gps_ref, gwp_ref):
        i = pl.program_id(0)

        @pl.when(i == 0)
        def _():
            gps_ref[...] = jnp.zeros_like(gps_ref)
            gwp_ref[...] = jnp.zeros_like(gwp_ref)
        uv = u_ref[...]
        halo = jnp.where(i > 0, uh_ref[...], 0.0)
        diffs = _pool_diff(uv, halo, i, tm)
        for g in range(ngrp):
            cols = slice(g * gw, (g + 1) * gw)
            w = _pool_weight(wp_ref, g)
            db = diffs[g].astype(BF16)
            mm = _nn(db, w)
            gpv = gp_ref[:, cols]
            sg = _sigmoid(gpv)
            si = gpv * sg
            dsi = sg * (1.0 + gpv * (1.0 - sg))
            dcb = dc_ref[:, cols]
            psv = ps_ref[:, cols]
            d_mm = (dcb * si) * psv
            gps_ref[:, cols] += jnp.sum((dcb * si) * mm, axis=0, keepdims=True)
            dgp_ref[:, cols] = (dcb * (mm * psv)) * dsi
            d_mmb = d_mm.astype(BF16)
            dd_ref[:, cols] = _nt(d_mmb, w)
            gwt = _tn(db, d_mmb)
            for j in range(N_CHIPS):
                gwp_ref[j, g * rr:(g + 1) * rr, :] += gwt[j * rr:(j + 1) * rr, :]

    row = pl.BlockSpec((tm, POOL_W), lambda i: (i, 0))
    wspec = pl.BlockSpec(w_pool.shape, lambda i: (0, 0, 0))
    return _pcall(
        body, name="pool_bwd", grid=(S // tm,),
        in_specs=[row, pl.BlockSpec((POOL_HALO, POOL_W), lambda i: (jnp.maximum(i * hb - 1, 0), 0)), row,
                  pl.BlockSpec((tm, POOL_W), lambda i: (i, 1)), wspec, pl.BlockSpec((None, 1, POOL_W), lambda i: (l, 0, 0))],
        out_specs=[row, row, pl.BlockSpec((1, POOL_W), lambda i: (0, 0)), wspec],
        out_shape=[jax.ShapeDtypeStruct((S, POOL_W), F32), jax.ShapeDtypeStruct((S, POOL_W), F32),
                   jax.ShapeDtypeStruct((1, POOL_W), F32), jax.ShapeDtypeStruct(w_pool.shape, F32)],
        compiler_params=_cparams(("arbitrary",)),
    )(u, u, gp, dc, w_pool, ps)


def _pool_window_t(dd, halo_next, tile_index, tm):
    gw = POOL_W // len(POOL_WINDOWS)
    n = tm + POOL_HALO
    t = tile_index * tm + lax.broadcasted_iota(I32, (n, gw), 0)
    xh = jnp.concatenate([dd, halo_next], axis=0)
    outs = []
    for g, w in enumerate(POOL_WINDOWS):
        cols = slice(g * gw, (g + 1) * gw)
        cnt = jnp.minimum(t + 1, w).astype(F32)
        s = xh[:, cols] / cnt
        step = 1
        while step < w:
            s = s + pltpu.roll(s, n - step, axis=0)
            step *= 2
        outs.append(s[:tm] - dd[:, cols])
    return outs


def _assemble_dh(dq, dk, dv, dga, dd, dgp):
    S = dq.shape[0]
    tm = min(S, 256)
    hb = tm // POOL_HALO
    nt = S // tm

    def body(dq_ref, dk_ref, dv_ref, dga_ref, dd_ref, ddn_ref, dgp_ref, dh_ref, gb_ref):
        i = pl.program_id(0)

        @pl.when(i == 0)
        def _():
            gb_ref[...] = jnp.zeros_like(gb_ref)
        halo = jnp.where(i < nt - 1, ddn_ref[...], 0.0)
        du = jnp.concatenate(_pool_window_t(dd_ref[...], halo, i, tm), axis=1)
        dkv = dk_ref[...]
        dgav = dga_ref[...]
        parts = [(OFF_Q, dq_ref[...]), (OFF_KA, dkv), (OFF_KB, dkv), (OFF_V, dv_ref[...]), (OFF_GA, dgav),
                 (OFF_U, du[:, 0:896]), (OFF_UA, du[:, 896:1024]), (OFF_UB, du[:, 896:1024]), (OFF_GP, dgp_ref[...])]
        for off, val in parts:
            w = val.shape[1]
            dh_ref[:, off:off + w] = val.astype(BF16)
            gb_ref[:, off:off + w] += jnp.sum(val, axis=0, keepdims=True)

    def row(w):
        return pl.BlockSpec((tm, w), lambda i: (i, 0))

    return _pcall(
        body, name="assemble_dh", grid=(nt,),
        in_specs=[row(ATTN_W), row(KV_W), row(KV_W), row(ATTN_W), row(POOL_W),
                  pl.BlockSpec((POOL_HALO, POOL_W), lambda i: (jnp.minimum((i + 1) * hb, S // POOL_HALO - 1), 0)), row(POOL_W)],
        out_specs=[row(EXT), pl.BlockSpec((1, EXT), lambda i: (0, 0))],
        out_shape=[jax.ShapeDtypeStruct((S, EXT), BF16), jax.ShapeDtypeStruct((1, EXT), F32)],
        compiler_params=_cparams(("arbitrary",)),
    )(dq, dk, dv, dga, dd, dd, dgp)


def _attn_bwd(q, k, v, ga, o, dc, lse, bias, sinks, dbias_in, l, job, store):
    S = q.shape[0]
    nblk = S // WINDOW

    def body(sink_ref, q_ref, k_ref, v_ref, ga_ref, o_ref, dc_ref, lse_ref, bias_ref, dbin_ref,
             dq_ref, dga_ref, dk_ref, dv_ref, db_ref, ds_ref, kd, vd, qs, dos, dls, ss, dps, dss, pss, dqs):
        n = pl.program_id(0)

        @pl.when(n == 0)
        def _():
            _dup_heads(k_ref, kd)
            _dup_heads(v_ref, vd)
            dk_ref[...] = jnp.zeros_like(dk_ref)
            dv_ref[...] = jnp.zeros_like(dv_ref)
            db_ref[...] = dbin_ref[...]
            ds_ref[...] = jnp.zeros_like(ds_ref)

        cur = pl.multiple_of(n * WINDOW, WINDOW)
        prev = pl.multiple_of(jnp.maximum(n - 1, 0) * WINDOW, WINDOW)
        lane = lax.broadcasted_iota(I32, (WINDOW, LANES), 1)
        lane8 = lax.broadcasted_iota(I32, (8, LANES), 1)
        lo = lane < HEAD_DIM
        tri = lane > lax.broadcasted_iota(I32, (WINDOW, LANES), 0)
        lse_t = lse_ref[...]
        dk_t, dv_t = [], []
        dsk = jnp.zeros((8, LANES), F32)
        for g in range(N_KV):
            kb = _kv_block(kd, g, prev, cur)
            vb = _kv_block(vd, g, prev, cur)
            for j in range(KV_RATIO // 2):
                pair = g * (KV_RATIO // 2) + j
                sl = slice(LANES * pair, LANES * (pair + 1))
                qp = q_ref[:, sl]
                op = o_ref[:, sl]
                dcp = dc_ref[:, sl]
                gav = ga_ref[:, sl]
                sg = _sigmoid(gav)
                d_o = dcp * (gav * sg)
                dga_ref[:, sl] = (dcp * op) * (sg * (1.0 + gav * (1.0 - sg)))
                prod = d_o * op
                for par in range(2):
                    msk = lo if par == 0 else jnp.logical_not(lo)
                    rows = _rows(2 * j + par)
                    qs[rows, :] = jnp.where(msk, qp, jnp.zeros_like(qp))
                    dos[rows, :] = jnp.where(msk, d_o, 0.0).astype(BF16)
                    delta = jnp.sum(jnp.where(msk, prod, 0.0), axis=1, keepdims=True)
                    dls[rows, :] = jnp.broadcast_to(delta, (WINDOW, LANES))
            ss[...] = _nt(qs[...], kb)
            dps[...] = _nt(dos[...], vb)
            for i in range(KV_RATIO):
                h = g * KV_RATIO + i
                rows = _rows(i)
                lse_h = jnp.sum(jnp.where(lane == h, lse_t, 0.0), axis=1, keepdims=True)
                p = jnp.exp(_fold(ss[rows, :], tri) + bias_ref[h] - lse_h)
                delta = dls[rows, :]
                dsc = p * (_fold(dps[rows, :], tri) - delta)
                db_ref[h] += dsc
                psink = jnp.exp(sink_ref[l, h] - lse_h)
                dsk = dsk + jnp.where(lane8 == h, -jnp.sum(psink * delta, axis=0, keepdims=True), 0.0)
                dss[rows, :] = _unfold(dsc, tri)
                pss[rows, :] = _unfold(p, tri)
            dqs[...] = _nn(dss[...], kb) * QK_SCALE
            for j in range(KV_RATIO // 2):
                pair = g * (KV_RATIO // 2) + j
                dq_ref[:, LANES * pair:LANES * (pair + 1)] = jnp.where(lo, dqs[_rows(2 * j), :], dqs[_rows(2 * j + 1), :])
            dk_t.append(_tn(qs[...], dss[...]))
            dv_t.append(_tn(dos[...], pss[...]))

        def untranspose(acc):
            return jnp.concatenate([a[:HEAD_DIM] + a[HEAD_DIM:] for a in acc], axis=0).T

        dkb = untranspose(dk_t)
        dvb = untranspose(dv_t)
        dk_ref[pl.ds(prev, WINDOW), :] += dkb[:WINDOW]
        dk_ref[pl.ds(cur, WINDOW), :] += dkb[WINDOW:]
        dv_ref[pl.ds(prev, WINDOW), :] += dvb[:WINDOW]
        dv_ref[pl.ds(cur, WINDOW), :] += dvb[WINDOW:]
        ds_ref[...] += dsk

    blk = pl.BlockSpec((WINDOW, ATTN_W), lambda n: (n, 0))
    full_kv = pl.BlockSpec((S, KV_W), lambda n: (0, 0))
    full_b = pl.BlockSpec((N_HEADS, WINDOW, WINDOW), lambda n: (0, 0, 0))
    stack = KV_RATIO * WINDOW
    return _call(
        body, name="attn_bwd", grid=(nblk,),
        in_specs=[pl.BlockSpec(memory_space=pltpu.SMEM), blk, full_kv, full_kv, blk, blk, blk,
                  pl.BlockSpec((WINDOW, LANES), lambda n: (n, 0)),
                  pl.BlockSpec((None, N_HEADS, WINDOW, WINDOW), lambda n: (jnp.minimum(n, 1), 0, 0, 0)), full_b],
        out_specs=[blk, blk, full_kv, full_kv, full_b, pl.BlockSpec((8, LANES), lambda n: (0, 0))],
        out_shape=[jax.ShapeDtypeStruct((S, ATTN_W), F32), jax.ShapeDtypeStruct((S, ATTN_W), F32), jax.ShapeDtypeStruct((S, KV_W), F32),
                   jax.ShapeDtypeStruct((S, KV_W), F32), jax.ShapeDtypeStruct((N_HEADS, WINDOW, WINDOW), F32),
                   jax.ShapeDtypeStruct((8, LANES), F32)],
        scratch_shapes=[pltpu.VMEM((N_KV, S, LANES), BF16), pltpu.VMEM((N_KV, S, LANES), BF16),
                        pltpu.VMEM((stack, LANES), BF16), pltpu.VMEM((stack, LANES), BF16), pltpu.VMEM((stack, LANES), F32),
                        pltpu.VMEM((stack, 2 * WINDOW), F32), pltpu.VMEM((stack, 2 * WINDOW), F32),
                        pltpu.VMEM((stack, 2 * WINDOW), BF16), pltpu.VMEM((stack, 2 * WINDOW), BF16),
                        pltpu.VMEM((stack, LANES), F32)],
        args=(sinks, q, k, v, ga, o, dc, lse, bias, dbias_in), sem=("arbitrary",), job=job, store=store)


def _pack_small(arrs):
    flat = []
    for a in arrs:
        v = a.reshape(-1)
        flat.append(jnp.pad(v, (0, (-v.shape[0]) % LANES)))
    v = jnp.concatenate(flat)
    v = jnp.pad(v, (0, (-v.shape[0]) % (8 * LANES)))
    return v.reshape(-1, LANES)


def _unpack_small(packed, shapes):
    v = packed.reshape(-1)
    outs, off = [], 0
    for shp in shapes:
        n = math.prod(shp)
        outs.append(v[off:off + n].reshape(shp))
        off += n + (-n) % LANES
    return outs


def _bias_to_ext(b):
    L = b.shape[0]
    z = jnp.zeros((L, HALF_TILE), b.dtype)
    parts = []
    for j in range(N_CHIPS):
        seg = b[:, j * SHARD:(j + 1) * SHARD]
        parts += [z, seg] if j % 2 else [seg, z]
    return jnp.concatenate(parts, axis=1).reshape(L, 1, EXT)


def _bias_from_ext(g):
    parts = []
    for j in range(N_CHIPS):
        o = j * SHARD_P + (HALF_TILE if j % 2 else 0)
        parts.append(g[:, o:o + SHARD])
    return jnp.concatenate(parts, axis=1)


def kernel(x, p, w_in, b_in, w_out, attn_sinks, rel_bias, w_pool, pool_scale, w_ple, w_gate_ple, ln_gain, ln_bias, loss_target, m_w_in, m_b_in, m_w_out, m_attn_sinks, m_rel_bias, m_w_pool, m_pool_scale, m_w_ple, m_w_gate_ple, m_ln_gain, m_ln_bias, v_w_in, v_b_in, v_w_out, v_attn_sinks, v_rel_bias, v_w_pool, v_pool_scale, v_w_ple, v_w_gate_ple, v_ln_gain, v_ln_bias):
    L = w_in.shape[0]
    S, D = x.shape[1], x.shape[2]
    assert D == D_MODEL and w_in.shape[2] == SHARD and S % WINDOW == 0
    alpha = (2.0 * L) ** 0.25
    xc, yc, cc = _mesh_pos()
    idx = jnp.stack([2 * xc + yc, yc, cc]).astype(I32)
    store = {}

    def wkeys(l, names):
        return [("w", l, t) for t in names]

    def weight(l, t):
        return _unhalves(store["w", l, t])

    w_in_t = jnp.swapaxes(w_in, 1, 2)
    w_pool2 = w_pool.reshape(L, w_pool.shape[1] * w_pool.shape[2], w_pool.shape[3])
    for l in range(L):
        store["w", l, "in"] = _halves(_piece_in(w_in_t, l, idx))
        store["w", l, "out"] = _halves(_piece(w_out, l, idx, "piece_out"))
        store["w", l, "gate"] = _halves(_piece(w_gate_ple, l, idx, "piece_gate"))
        store["w", l, "ple"] = _halves(_piece(w_ple, l, idx, "piece_ple"))
        store["w", l, "pool"] = _halves(_piece(w_pool2, l, idx, "piece_pool"))
    _allgather_now(wkeys(0, WEIGHTS), store)

    b_ext = _bias_to_ext(b_in)
    ps3 = pool_scale.reshape(L, 1, POOL_W)
    gain3 = ln_gain.reshape(L, 1, D)
    bias3 = ln_bias.reshape(L, 1, D)
    onehot = _bucket_onehot()
    bias_hqk = _masked_bias(_bias_fwd(rel_bias.T, onehot).reshape(N_HEADS, WINDOW, WINDOW))

    xs = x[0]
    xb = _to_bf16(xs)
    pb = _to_bf16(p.reshape(L * S, p.shape[3])).reshape(L, S, p.shape[3])
    saved = []
    x_norm, x_gain, x_bias, x_l = xs, jnp.ones((1, 1, D), F32), jnp.zeros((1, 1, D), F32), 0
    late = ["out", "ple", "pool"]
    for l in range(L):
        nxt = l + 1 < L
        job = _Job()
        if nxt:
            job.add(_ag_ici, wkeys(l + 1, ["in"]), 3)
        if l >= 1:
            job.add(_ag_fwd, wkeys(l, ["gate"]), 3)
        q, k, v, ga, u, gp = _inproj(xb, weight(l, "in"), b_ext, l, job, store)
        gl = _matmul_nn(xb, weight(l, "gate"), "gate_logits", None, None)
        job = _Job().add(_ag_fwd, wkeys(l + 1, ["in"]), 3).add(_ag_ici, wkeys(l + 1, late), 9) if nxt else None
        o, ca, lse = _attn_fwd(q, k, v, ga, bias_hqk, attn_sinks, l, job, store)
        c = _pool_fwd(u, gp, ca, weight(l, "pool"), ps3, l, None, None)
        job = _Job().add(_ag_fwd, wkeys(l + 1, late), 9).add(_ag_ici, wkeys(l + 1, ["gate"]), 3) if nxt else None
        yb, xhat, rstd = _outproj_ln(c, x_norm, x_gain, x_bias, x_l, gl, pb[l], weight(l, "out"), weight(l, "ple"), gain3, bias3, l,
                                     alpha, job, store)
        saved.append(dict(xb=xb, q=q, k=k, v=v, ga=ga, u=u, gp=gp, gl=gl, o=o, lse=lse, c=c, xhat=xhat, rstd=rstd))
        x_norm, x_gain, x_bias, x_l, xb = xhat, gain3, bias3, l, yb

    dy, loss_acc = _loss_and_grad(x_norm, gain3, bias3, L - 1, loss_target[0])
    loss = lax.psum(0.5 * loss_acc[0, 0], ("x", "y", "c"))

    shapes = {t: store["w", 0, t].shape for t in WEIGHTS}
    for t in WEIGHTS:
        store["full", t] = lax.empty((L,) + shapes[t][1:], F32)

    def rs_keys(kind, l, names):
        return [(kind, l, t) for t in names]

    def rs_pair_job(l, names):
        for t in names:
            store["ra", l, t] = lax.empty((N_CHIPS,) + shapes[t][2:], F32)
        return _Job().add(_rs_pair, rs_keys("g", l, names) + rs_keys("ra", l, names), len(names))

    def rs_pair_add(l, names):
        for t in names:
            p32, p16 = _rs_pair_add(store["g", l, t], store["ra", l, t], idx, "rs_pair_add_" + t)
            store["p32", l, t], store["p16", l, t] = p32, p16
            store["rb", l, t] = lax.empty((3,) + shapes[t][2:], BF16)

    def rs_ici_job(l, names):
        return _Job().add(_rs_ici, rs_keys("p16", l, names) + rs_keys("rb", l, names), 3 * len(names))

    def rs_chip_add(l, names):
        for t in names:
            store["full", t] = _rs_chip_add(store["p32", l, t], store["rb", l, t], store["full", t], l, idx, "rs_chip_add_" + t)

    def rs_share_job(l):
        return _Job().add(functools.partial(_rs_share, layer=l), [("full", t) for t in WEIGHTS], len(WEIGHTS))

    dbias = jnp.zeros((N_HEADS, WINDOW, WINDOW), F32)
    small = [None] * L
    rest = ["out", "gate", "ple", "pool"]
    for l in reversed(range(L)):
        sv = saved[l]
        pl_l = pb[l]
        dz, dzb, d_e, d_gl, ggain, gbias = _ln_bwd(dy, sv["xhat"], sv["rstd"], sv["gl"], pl_l, weight(l, "ple"), gain3, l, None, None)
        dc = _matmul_nt_rows(dzb, weight(l, "out"), "d_mix_in")
        g_out = _matmul_tn(sv["c"], dzb, D // N_CHIPS, D, "grad_w_out", by_rows=True)
        g_ple = _matmul_tn(pl_l, d_e, pl_l.shape[1], D // N_CHIPS, "grad_w_ple", by_rows=False)
        g_gate = _matmul_tn(sv["xb"], d_gl, D // N_CHIPS, D, "grad_w_gate", by_rows=True)
        dd, dgp, gps, g_pool = _pool_bwd(sv["u"], sv["gp"], dc, weight(l, "pool"), ps3, l)
        for t, g in zip(rest, (g_out, g_gate, g_ple, g_pool)):
            store["g", l, t] = _halves(g)
        job = rs_pair_job(l, rest)
        if l + 1 < L:
            job.add(functools.partial(_rs_share, layer=l + 1), [("full", t) for t in WEIGHTS], len(WEIGHTS))
        dq, dga, dk, dv, dbias, dsink = _attn_bwd(sv["q"], sv["k"], sv["v"], sv["ga"], sv["o"], dc, sv["lse"], bias_hqk,
                                                  attn_sinks, dbias, l, job, store)
        rs_pair_add(l, rest)
        dh, gbe = _assemble_dh(dq, dk, dv, dga, dd, dgp)
        g_in = _matmul_tn(dh, sv["xb"], SHARD_P, D, "grad_w_in", by_rows=True, job=rs_ici_job(l, ["out"]), store=store)
        store["g", l, "in"] = _halves(g_in)
        job = rs_pair_job(l, ["in"])
        job.add(_rs_ici, rs_keys("p16", l, rest[1:]) + rs_keys("rb", l, rest[1:]), 3 * len(rest[1:]))
        dx1 = _matmul_nt_rows(d_gl, weight(l, "gate"), "d_x_gate", dz, alpha, job=job, store=store)
        rs_chip_add(l, rest)
        rs_pair_add(l, ["in"])
        dy = _matmul_nn_acc(dh, weight(l, "in"), "d_x", dx1, 1.0, rs_ici_job(l, ["in"]), store)
        rs_chip_add(l, ["in"])
        small[l] = dict(b_in=_bias_from_ext(gbe)[0], sinks=dsink[0, :N_HEADS], ps=gps[0], gain=ggain[0], bias=gbias[0])
    grad_x = dy[None]

    _run_job("rs_pair_share", rs_share_job(0), store)
    full = {t: _unhalves(store["full", t]) for t in WEIGHTS}

    def t_back(a):
        return jnp.swapaxes(a, 1, 2)

    def pool4(a):
        return a.reshape(w_pool.shape)

    r_in = _adamw(w_in_t, full["in"], jnp.swapaxes(m_w_in, 1, 2), jnp.swapaxes(v_w_in, 1, 2), idx, "adamw_w_in", tr=HALF_TILE)
    gw_in, dw_in, nm_in, nv_in = [t_back(a) for a in r_in]
    gw_out, dw_out, nm_out, nv_out = _adamw(w_out, full["out"], m_w_out, v_w_out, idx, "adamw_w_out")
    gw_gate, dw_gate, nm_gate, nv_gate = _adamw(w_gate_ple, full["gate"], m_w_gate_ple, v_w_gate_ple, idx, "adamw_w_gate")
    gw_ple, dw_ple, nm_ple, nv_ple = _adamw(w_ple, full["ple"], m_w_ple, v_w_ple, idx, "adamw_w_ple")
    r_pool = _adamw(w_pool2, full["pool"], m_w_pool.reshape(w_pool2.shape), v_w_pool.reshape(w_pool2.shape), idx, "adamw_w_pool")
    gw_pool, dw_pool, nm_pool, nv_pool = [pool4(a) for a in r_pool]

    g_rel = _bias_bwd(dbias.reshape(N_HEADS, -1), onehot).T
    small_shapes = [b_in.shape, attn_sinks.shape, rel_bias.shape, pool_scale.shape, ln_gain.shape, ln_bias.shape]
    g_small = [jnp.stack([small[l]["b_in"] for l in range(L)]), jnp.stack([small[l]["sinks"] for l in range(L)]), g_rel,
               jnp.stack([small[l]["ps"] for l in range(L)]), jnp.stack([small[l]["gain"] for l in range(L)]),
               jnp.stack([small[l]["bias"] for l in range(L)])]
    packed = _small_allreduce_adamw(
        _pack_small(g_small),
        _pack_small([b_in, attn_sinks, rel_bias, pool_scale, ln_gain, ln_bias]),
        _pack_small([m_b_in, m_attn_sinks, m_rel_bias, m_pool_scale, m_ln_gain, m_ln_bias]),
        _pack_small([v_b_in, v_attn_sinks, v_rel_bias, v_pool_scale, v_ln_gain, v_ln_bias]))
    sg, sd, sm, sv_ = [_unpack_small(a, small_shapes) for a in packed]

    def order(big, sm_):
        return (big[0], sm_[0], big[1], sm_[1], sm_[2], big[2], sm_[3], big[3], big[4], sm_[4], sm_[5])

    return (loss, grad_x,
            *order((gw_in, gw_out, gw_pool, gw_ple, gw_gate), sg),
            *order((dw_in, dw_out, dw_pool, dw_ple, dw_gate), sd),
            *order((nm_in, nm_out, nm_pool, nm_ple, nm_gate), sm),
            *order((nv_in, nv_out, nv_pool, nv_ple, nv_gate), sv_))
```

```python
import functools
import math

import jax
import jax.numpy as jnp
from jax import lax
from jax.experimental import pallas as pl
from jax.experimental.pallas import tpu as pltpu

F32 = jnp.float32
BF16 = jnp.bfloat16
I32 = jnp.int32
MESH = pl.DeviceIdType.MESH

HEAD_DIM = 64
QK_SCALE = HEAD_DIM ** -0.5
WINDOW = 128
KV_RATIO = 8
POOL_WINDOWS = (2, 4, 8, 16)
POOL_HALO = 16
REL_BUCKETS = 32
REL_MAX_DIST = 128
LN_EPS = 1e-5
ADAM_LR, ADAM_B1, ADAM_B2, ADAM_EPS, ADAM_WD, ADAM_STEP = 0.001, 0.9, 0.999, 1e-08, 0.01, 10

LANES = 128
VMEM_LIMIT = 52 * 1024 * 1024
N_CHIPS = 4
N_DEV = 8

D_MODEL = 2048
ATTN_W = 1024
POOL_W = 1024
KV_W = 128
N_HEADS = ATTN_W // HEAD_DIM
N_KV = N_HEADS // KV_RATIO
IN_COLS = 4352
SHARD = IN_COLS // N_CHIPS
SHARD_P = 1152
EXT = N_CHIPS * SHARD_P
HALF_TILE = SHARD_P - SHARD
OFF_Q, OFF_KA, OFF_KB, OFF_V, OFF_GA, OFF_U, OFF_UA, OFF_UB, OFF_GP = 0, 1024, 1152, 1280, 1408, 2432, 3328, 3456, 3584
WEIGHTS = ("in", "out", "gate", "ple", "pool")


def _cparams(sem=None):
    return pltpu.CompilerParams(dimension_semantics=sem, vmem_limit_bytes=VMEM_LIMIT)


def _pcall(body, **kw):
    return pl.pallas_call(body, **kw)


def _sigmoid(x):
    return 1.0 / (1.0 + jnp.exp(-x))


def _nt(a, b):
    return lax.dot_general(a, b, (((1,), (1,)), ((), ())), preferred_element_type=F32)


def _tn(a, b):
    return lax.dot_general(a, b, (((0,), (0,)), ((), ())), preferred_element_type=F32)


def _nn(a, b):
    return jnp.dot(a, b, preferred_element_type=F32)


def _mesh_pos():
    return lax.axis_index("x"), lax.axis_index("y"), lax.axis_index("c")


def _peer_chips(x, y):
    return [(1 - x, y), (x, 1 - y), (1 - x, 1 - y)]


def _row_tile(rows, cap=256):
    t = min(rows, cap)
    while rows % t or t % 16:
        t -= 1
    return t


def _hbm_spec():
    return pl.BlockSpec(memory_space=pltpu.HBM)


def _halves(a):
    return a.reshape(a.shape[:-2] + (2, a.shape[-2] // 2, a.shape[-1]))


def _unhalves(a):
    return a.reshape(a.shape[:-3] + (2 * a.shape[-2], a.shape[-1]))


class _remote:
    def __init__(self, src, dst, send, recv, i, device):
        self.args = dict(src_ref=src, dst_ref=dst, send_sem=send.at[i], recv_sem=recv.at[i], device_id=device, device_id_type=MESH)

    def start(self):
        pltpu.make_async_remote_copy(**self.args).start()

    def wait_recv(self):
        pltpu.make_async_remote_copy(**self.args).wait_recv()

    def wait_send(self):
        pltpu.make_async_remote_copy(**self.args).wait_send()


class _Job:
    def __init__(self):
        self.keys, self.parts, self.n = [], [], 0

    def add(self, fn, keys, n):
        self.parts.append((fn, len(self.keys), len(keys), self.n))
        self.keys += list(keys)
        self.n += n
        return self

    def build(self, refs, send, recv):
        out = []
        for fn, i0, nb, base in self.parts:
            out += fn(refs[i0:i0 + nb], send, recv, base)
        return out


def _ag_ici(refs, send, recv, base):
    x, y, c = _mesh_pos()
    me = 2 * x + y
    out = []
    for t, g in enumerate(refs):
        for k, chip in enumerate(_peer_chips(x, y)):
            i = base + 3 * t + k
            dev = (*chip, c)
            out.append((_remote(g.at[me, c], g.at[me, c], send, recv, i, dev),
                        _remote(g.at[me, c], g.at[2 * chip[0] + chip[1], c], send, recv, i, dev)))
    return out


def _ag_fwd(refs, send, recv, base):
    x, y, c = _mesh_pos()
    out = []
    for t, g in enumerate(refs):
        for k, chip in enumerate(_peer_chips(x, y)):
            i = base + 3 * t + k
            slot = 2 * chip[0] + chip[1]
            dev = (x, y, 1 - c)
            out.append((_remote(g.at[slot, c], g.at[slot, c], send, recv, i, dev),
                        _remote(g.at[slot, c], g.at[slot, 1 - c], send, recv, i, dev)))
    return out


def _rs_pair(refs, send, recv, base):
    x, y, c = _mesh_pos()
    n = len(refs) // 2
    out = []
    for t in range(n):
        cp = _remote(refs[t].at[:, 1 - c], refs[n + t], send, recv, base + t, (x, y, 1 - c))
        out.append((cp, cp))
    return out


def _rs_ici(refs, send, recv, base, rows=None):
    x, y, c = _mesh_pos()
    n = len(refs) // 2
    rsl = slice(None) if rows is None else pl.ds(rows[0], rows[1])
    out = []
    for t in range(n):
        for k, chip in enumerate(_peer_chips(x, y)):
            cp = _remote(refs[t].at[2 * chip[0] + chip[1], rsl], refs[n + t].at[k, rsl], send, recv, base + 3 * t + k, (*chip, c))
            out.append((cp, cp))
    return out


def _rs_share(refs, send, recv, base, layer):
    x, y, c = _mesh_pos()
    out = []
    for t, f in enumerate(refs):
        dev = (x, y, 1 - c)
        out.append((_remote(f.at[layer, c], f.at[layer, c], send, recv, base + t, dev),
                    _remote(f.at[layer, c], f.at[layer, 1 - c], send, recv, base + t, dev)))
    return out


def _call(body, *, name, grid, in_specs, out_specs, out_shape, args, scratch_shapes=(), sem=None, job=None, store=None):
    in_specs, out_specs, out_shape, scratch_shapes = list(in_specs), list(out_specs), list(out_shape), list(scratch_shapes)
    if job is None or job.n == 0:
        return list(_pcall(body, name=name, grid=grid, in_specs=in_specs, out_specs=out_specs, out_shape=out_shape,
                           scratch_shapes=scratch_shapes, compiler_params=_cparams(sem))(*args))
    bufs = [store[k] for k in job.keys]
    nb, n_in, n_out, n_sc = len(bufs), len(args), len(out_shape), len(scratch_shapes)

    def wrapped(*refs):
        ins = refs[:n_in]
        outs = refs[n_in + nb:n_in + nb + n_out]
        cb = refs[n_in + nb + n_out:n_in + 2 * nb + n_out]
        scratch = refs[n_in + 2 * nb + n_out:n_in + 2 * nb + n_out + n_sc]
        send, recv = refs[-2:]
        ids = [pl.program_id(a) for a in range(len(grid))]
        first = functools.reduce(jnp.logical_and, [i == 0 for i in ids])
        last = functools.reduce(jnp.logical_and, [i == g - 1 for i, g in zip(ids, grid)])

        @pl.when(first)
        def _():
            for s, _r in job.build(cb, send, recv):
                s.start()

        body(*ins, *outs, *scratch)

        @pl.when(last)
        def _():
            pairs = job.build(cb, send, recv)
            for _s, r in pairs:
                r.wait_recv()
            for s, _r in pairs:
                s.wait_send()

    res = _pcall(
        wrapped, name=name, grid=grid, in_specs=in_specs + [_hbm_spec()] * nb, out_specs=out_specs + [_hbm_spec()] * nb,
        out_shape=out_shape + [jax.ShapeDtypeStruct(b.shape, b.dtype) for b in bufs],
        scratch_shapes=scratch_shapes + [pltpu.SemaphoreType.DMA((job.n,)), pltpu.SemaphoreType.DMA((job.n,))],
        input_output_aliases={n_in + i: n_out + i for i in range(nb)},
        compiler_params=_cparams(("arbitrary",) * len(grid)))(*args, *bufs)
    for k, v in zip(job.keys, res[n_out:]):
        store[k] = v
    return list(res[:n_out])


def _run_job(name, job, store):
    bufs = [store[k] for k in job.keys]
    nb = len(bufs)

    def body(*refs):
        send, recv = refs[-2:]
        pairs = job.build(refs[nb:2 * nb], send, recv)
        for s, _r in pairs:
            s.start()
        for _s, r in pairs:
            r.wait_recv()
        for s, _r in pairs:
            s.wait_send()

    res = _pcall(body, name=name, in_specs=[_hbm_spec()] * nb, out_specs=[_hbm_spec()] * nb,
                 out_shape=[jax.ShapeDtypeStruct(b.shape, b.dtype) for b in bufs],
                 scratch_shapes=[pltpu.SemaphoreType.DMA((job.n,)), pltpu.SemaphoreType.DMA((job.n,))],
                 input_output_aliases={i: i for i in range(nb)})(*bufs)
    for k, v in zip(job.keys, res):
        store[k] = v


def _allgather_now(keys, store):
    bufs = [store[k] for k in keys]
    nb = len(bufs)

    def body(*refs):
        send, recv = refs[-2:]
        g = refs[nb:2 * nb]
        ici = _ag_ici(g, send, recv, 0)
        fwd = _ag_fwd(g, send, recv, 3 * nb)
        for s, _r in ici:
            s.start()
        for (_s, r), (fs, _fr) in zip(ici, fwd):
            r.wait_recv()
            fs.start()
        for _fs, fr in fwd:
            fr.wait_recv()
        for s, _r in ici + fwd:
            s.wait_send()

    res = _pcall(body, name="allgather_first_layer", in_specs=[_hbm_spec()] * nb, out_specs=[_hbm_spec()] * nb,
                 out_shape=[jax.ShapeDtypeStruct(b.shape, b.dtype) for b in bufs],
                 scratch_shapes=[pltpu.SemaphoreType.DMA((6 * nb,)), pltpu.SemaphoreType.DMA((6 * nb,))],
                 input_output_aliases={i: i for i in range(nb)})(*bufs)
    for k, v in zip(keys, res):
        store[k] = v


def _piece(w, l, idx, name):
    _, R, C = w.shape
    tr = _row_tile(R)

    def body(s_ref, w_ref, o_ref):
        del s_ref
        o_ref[...] = w_ref[...].astype(BF16)

    gs = pltpu.PrefetchScalarGridSpec(
        num_scalar_prefetch=1, grid=(R // tr,),
        in_specs=[pl.BlockSpec((None, tr, C), lambda r, s: (l, r, 0))],
        out_specs=pl.BlockSpec((None, tr, C), lambda r, s: (s[0], r, 0)))
    return _pcall(body, name=name, grid_spec=gs, out_shape=jax.ShapeDtypeStruct((N_CHIPS, R, C), BF16),
                  compiler_params=_cparams(("parallel",)))(idx, w)


def _piece_in(w_in_t, l, idx):
    _, R, D = w_in_t.shape
    tr = HALF_TILE
    nsrc = R // tr

    def body(s_ref, w_ref, o_ref):
        src = pl.program_id(0) - s_ref[1]
        ok = jnp.logical_and(src >= 0, src < nsrc)
        o_ref[...] = jnp.where(ok, w_ref[...], 0.0).astype(BF16)

    gs = pltpu.PrefetchScalarGridSpec(
        num_scalar_prefetch=1, grid=(SHARD_P // tr,),
        in_specs=[pl.BlockSpec((None, tr, D), lambda r, s: (l, jnp.clip(r - s[1], 0, nsrc - 1), 0))],
        out_specs=pl.BlockSpec((None, tr, D), lambda r, s: (s[0], r, 0)))
    return _pcall(body, name="piece_in", grid_spec=gs, out_shape=jax.ShapeDtypeStruct((N_CHIPS, SHARD_P, D), BF16),
                  compiler_params=_cparams(("parallel",)))(idx, w_in_t)


def _rs_pair_add(g5, r4, idx, name):
    J, _, h, C = g5.shape
    th = _row_tile(h)

    def body(s_ref, g_ref, r_ref, o32_ref, o16_ref):
        s = g_ref[...] + r_ref[...]
        o16_ref[...] = s.astype(BF16)

        @pl.when(pl.program_id(1) == s_ref[0])
        def _():
            o32_ref[...] = s

    spec = pl.BlockSpec((None, th, C), lambda r, j, s: (j, r, 0))
    gs = pltpu.PrefetchScalarGridSpec(
        num_scalar_prefetch=1, grid=(h // th, J),
        in_specs=[pl.BlockSpec((None, None, th, C), lambda r, j, s: (j, s[2], r, 0)), spec],
        out_specs=[pl.BlockSpec((th, C), lambda r, j, s: (r, 0)), spec])
    return _pcall(body, name=name, grid_spec=gs,
                  out_shape=[jax.ShapeDtypeStruct((h, C), F32), jax.ShapeDtypeStruct((J, h, C), BF16)],
                  compiler_params=_cparams(("parallel", "arbitrary")))(idx, g5, r4)


def _rs_chip_add(p32, r3, full, l, idx, name):
    h, C = p32.shape
    th = _row_tile(h)

    def body(s_ref, p_ref, r_ref, f_ref, o_ref):
        del s_ref, f_ref
        o_ref[...] = ((p_ref[...] + r_ref[0].astype(F32)) + r_ref[1].astype(F32)) + r_ref[2].astype(F32)

    gs = pltpu.PrefetchScalarGridSpec(
        num_scalar_prefetch=1, grid=(h // th,),
        in_specs=[pl.BlockSpec((th, C), lambda r, s: (r, 0)),
                  pl.BlockSpec((3, th, C), lambda r, s: (0, r, 0)),
                  pl.BlockSpec(memory_space=pl.ANY)],
        out_specs=pl.BlockSpec((None, None, th, C), lambda r, s: (l, s[2], r, 0)))
    return _pcall(body, name=name, grid_spec=gs, out_shape=jax.ShapeDtypeStruct(full.shape, F32),
                  input_output_aliases={3: 0}, compiler_params=_cparams(("parallel",)))(idx, p32, r3, full)


def _adamw_math(w, g, m, v):
    nm = ADAM_B1 * m + (1.0 - ADAM_B1) * g
    nv = ADAM_B2 * v + (1.0 - ADAM_B2) * (g * g)
    m_hat = nm / (1.0 - ADAM_B1 ** ADAM_STEP)
    v_hat = nv / (1.0 - ADAM_B2 ** ADAM_STEP)
    delta = -ADAM_LR * (m_hat / (jnp.sqrt(v_hat) + ADAM_EPS) + ADAM_WD * w)
    return delta, nm, nv


def _adamw(w, g, m, v, idx, name, tr=None):
    L, R, C = w.shape
    Rg = g.shape[1]
    tr = tr or _row_tile(R)
    shift = (Rg - R) // tr
    assert (Rg - R) % tr == 0

    def body(s_ref, w_ref, g_ref, m_ref, v_ref, go_ref, d_ref, nm_ref, nv_ref):
        del s_ref
        gv = g_ref[...]
        d, nm, nv = _adamw_math(w_ref[...], gv, m_ref[...], v_ref[...])
        go_ref[...] = gv
        d_ref[...] = d
        nm_ref[...] = nm
        nv_ref[...] = nv

    wspec = pl.BlockSpec((None, tr, C), lambda l, r, s: (l, r, 0))
    gs = pltpu.PrefetchScalarGridSpec(
        num_scalar_prefetch=1, grid=(L, R // tr),
        in_specs=[wspec, pl.BlockSpec((None, tr, C), lambda l, r, s: (l, r + shift * s[1], 0)), wspec, wspec],
        out_specs=[wspec, wspec, wspec, wspec])
    sds = jax.ShapeDtypeStruct((L, R, C), F32)
    return _pcall(body, name=name, grid_spec=gs, out_shape=[sds, sds, sds, sds],
                  compiler_params=_cparams(("parallel", "parallel")))(idx, w, g, m, v)


def _small_allreduce_adamw(gv, wv, mv, vv):
    NR = gv.shape[0]

    def body(g_ref, w_ref, m_ref, v_ref, go_ref, d_ref, nm_ref, nv_ref, gath, send, recv):
        x, y, c = _mesh_pos()
        rank = 4 * x + 2 * y + c
        gath[rank] = g_ref[...]
        cps = []
        for msk in range(1, N_DEV):
            bx, by, bc = (msk >> 2) & 1, (msk >> 1) & 1, msk & 1
            peer = (1 - x if bx else x, 1 - y if by else y, 1 - c if bc else c)
            cps.append(pltpu.make_async_remote_copy(src_ref=g_ref, dst_ref=gath.at[rank], send_sem=send.at[msk - 1],
                                                    recv_sem=recv.at[msk - 1], device_id=peer, device_id_type=MESH))
        for cp in cps:
            cp.start()
        for msk in range(1, N_DEV):
            bx, by, bc = (msk >> 2) & 1, (msk >> 1) & 1, msk & 1
            peer = (1 - x if bx else x, 1 - y if by else y, 1 - c if bc else c)
            prank = 4 * peer[0] + 2 * peer[1] + peer[2]
            pltpu.make_async_remote_copy(src_ref=g_ref, dst_ref=gath.at[prank], send_sem=send.at[msk - 1],
                                         recv_sem=recv.at[msk - 1], device_id=peer, device_id_type=MESH).wait_recv()
        for cp in cps:
            cp.wait_send()
        tot = gath[0]
        for r in range(1, N_DEV):
            tot = tot + gath[r]
        d, nm, nv = _adamw_math(w_ref[...], tot, m_ref[...], v_ref[...])
        go_ref[...] = tot
        d_ref[...] = d
        nm_ref[...] = nm
        nv_ref[...] = nv

    vm = pl.BlockSpec(memory_space=pltpu.VMEM)
    sds = jax.ShapeDtypeStruct((NR, LANES), F32)
    return _pcall(body, name="small_allreduce_adamw", in_specs=[vm, vm, vm, vm], out_specs=[vm, vm, vm, vm],
                  out_shape=[sds, sds, sds, sds],
                  scratch_shapes=[pltpu.VMEM((N_DEV, NR, LANES), F32), pltpu.SemaphoreType.DMA((N_DEV - 1,)),
                                  pltpu.SemaphoreType.DMA((N_DEV - 1,))],
                  compiler_params=pltpu.CompilerParams(vmem_limit_bytes=VMEM_LIMIT))(gv, wv, mv, vv)


def _split3(a):
    h1 = a.astype(BF16)
    r1 = a - h1.astype(F32)
    h2 = r1.astype(BF16)
    h3 = (r1 - h2.astype(F32)).astype(BF16)
    return h1, h2, h3


def _folded_dist():
    r = jnp.arange(WINDOW)[:, None]
    j = jnp.arange(WINDOW)[None, :]
    return jnp.where(j > r, r + WINDOW - j, r - j)


def _bucket_onehot():
    d = _folded_dist()
    max_exact = REL_BUCKETS // 2
    d_f = jnp.maximum(d, 1).astype(F32)
    large = max_exact + (jnp.log(d_f / max_exact) / math.log(REL_MAX_DIST / max_exact) * (REL_BUCKETS - max_exact)).astype(I32)
    large = jnp.minimum(large, REL_BUCKETS - 1)
    bucket = jnp.where(d < max_exact, d, large)
    oh = bucket[None] == jnp.arange(REL_BUCKETS)[:, None, None]
    return oh.reshape(REL_BUCKETS, WINDOW * WINDOW).astype(BF16)


def _bias_fwd(rel_bias_t, onehot):
    H, N = rel_bias_t.shape[0], onehot.shape[1]
    tn = 4096

    def body(t_ref, oh_ref, o_ref):
        h1, h2, h3 = _split3(t_ref[...])
        oh = oh_ref[...]
        o_ref[...] = (_nn(h1, oh) + _nn(h2, oh)) + _nn(h3, oh)

    return _pcall(body, name="bias_fwd", grid=(N // tn,),
                  in_specs=[pl.BlockSpec((H, REL_BUCKETS), lambda i: (0, 0)), pl.BlockSpec((REL_BUCKETS, tn), lambda i: (0, i))],
                  out_specs=pl.BlockSpec((H, tn), lambda i: (0, i)), out_shape=jax.ShapeDtypeStruct((H, N), F32),
                  compiler_params=_cparams(("parallel",)))(rel_bias_t, onehot)


def _bias_bwd(dbias, onehot):
    H, N = dbias.shape
    tn = 4096

    def body(d_ref, oh_ref, o_ref):
        @pl.when(pl.program_id(0) == 0)
        def _():
            o_ref[...] = jnp.zeros_like(o_ref)
        h1, h2, h3 = _split3(d_ref[...])
        oh = oh_ref[...]
        o_ref[...] += (_nt(h1, oh) + _nt(h2, oh)) + _nt(h3, oh)

    return _pcall(body, name="bias_bwd", grid=(N // tn,),
                  in_specs=[pl.BlockSpec((H, tn), lambda i: (0, i)), pl.BlockSpec((REL_BUCKETS, tn), lambda i: (0, i))],
                  out_specs=pl.BlockSpec((H, REL_BUCKETS), lambda i: (0, 0)), out_shape=jax.ShapeDtypeStruct((H, REL_BUCKETS), F32),
                  compiler_params=_cparams(("arbitrary",)))(dbias, onehot)


def _to_bf16(x):
    S, D = x.shape
    tm = min(S, 512)

    def body(x_ref, o_ref):
        o_ref[...] = x_ref[...].astype(BF16)

    return _pcall(body, name="to_bf16", grid=(S // tm,), in_specs=[pl.BlockSpec((tm, D), lambda i: (i, 0))],
                  out_specs=pl.BlockSpec((tm, D), lambda i: (i, 0)), out_shape=jax.ShapeDtypeStruct((S, D), BF16),
                  compiler_params=_cparams(("parallel",)))(x)


def _inproj(xb, w_t, b_ext, l, job, store):
    S, D = xb.shape
    tm = min(S, 512)

    def body(x_ref, w_ref, b_ref, q_ref, k_ref, v_ref, ga_ref, u_ref, gp_ref):
        xv = x_ref[...]

        def window(j):
            return _nt(xv, w_ref[j]) + b_ref[:, j * SHARD_P:(j + 1) * SHARD_P]

        acc = window(0)
        q_ref[...] = (acc[:, :1024] * QK_SCALE).astype(BF16)
        k_lo = acc[:, 1024:1152]
        acc = window(1)
        k_ref[...] = k_lo + acc[:, 0:128]
        v_ref[...] = acc[:, 128:256]
        ga_ref[:, 0:896] = acc[:, 256:1152]
        acc = window(2)
        ga_ref[:, 896:1024] = acc[:, 0:128]
        u_ref[:, 0:896] = acc[:, 128:1024]
        u_lo = acc[:, 1024:1152]
        acc = window(3)
        u_ref[:, 896:1024] = u_lo + acc[:, 0:128]
        gp_ref[...] = acc[:, 128:1152]

    def ospec(w):
        return pl.BlockSpec((tm, w), lambda i: (i, 0))

    return _call(
        body, name="inproj", grid=(S // tm,),
        in_specs=[pl.BlockSpec((tm, D), lambda i: (i, 0)),
                  pl.BlockSpec((N_CHIPS, SHARD_P, D), lambda i: (0, 0, 0), pipeline_mode=pl.Buffered(1)),
                  pl.BlockSpec((None, 1, EXT), lambda i: (l, 0, 0))],
        out_specs=[ospec(ATTN_W), ospec(KV_W), ospec(KV_W), ospec(ATTN_W), ospec(POOL_W), ospec(POOL_W)],
        out_shape=[jax.ShapeDtypeStruct((S, ATTN_W), BF16), jax.ShapeDtypeStruct((S, KV_W), F32), jax.ShapeDtypeStruct((S, KV_W), F32),
                   jax.ShapeDtypeStruct((S, ATTN_W), F32), jax.ShapeDtypeStruct((S, POOL_W), F32), jax.ShapeDtypeStruct((S, POOL_W), F32)],
        args=(xb, w_t, b_ext), sem=("parallel",), job=job, store=store)


def _matmul_nn(a, b4, name, job, store):
    S, K = a.shape
    N = b4.shape[2]
    kq = b4.shape[1]
    tm, tn = min(S, 512), min(N, 1024)

    def body(a_ref, b_ref, o_ref):
        acc = _nn(a_ref[:, 0:kq], b_ref[0])
        for j in range(1, N_CHIPS):
            acc = acc + _nn(a_ref[:, j * kq:(j + 1) * kq], b_ref[j])
        o_ref[...] = acc

    return _call(body, name=name, grid=(N // tn, S // tm),
                 in_specs=[pl.BlockSpec((tm, K), lambda n, i: (i, 0)), pl.BlockSpec((N_CHIPS, kq, tn), lambda n, i: (0, 0, n))],
                 out_specs=[pl.BlockSpec((tm, tn), lambda n, i: (i, n))], out_shape=[jax.ShapeDtypeStruct((S, N), F32)],
                 args=(a, b4), sem=("parallel", "parallel"), job=job, store=store)[0]


def _masked_bias(bias):
    r = jnp.arange(WINDOW)[:, None]
    j = jnp.arange(WINDOW)[None, :]
    return jnp.stack([jnp.where(j > r, -1e30, bias), bias])


def _fold(full, tri):
    return jnp.where(tri, full[:, :WINDOW], full[:, WINDOW:])


def _unfold(folded, tri):
    return jnp.concatenate([jnp.where(tri, folded, 0.0).astype(BF16), jnp.where(tri, 0.0, folded).astype(BF16)], axis=1)


def _dup_heads(src_ref, dst_ref):
    a = src_ref[...]
    r = pltpu.roll(a, HEAD_DIM, axis=1)
    lo = lax.broadcasted_iota(I32, a.shape, 1) < HEAD_DIM
    dst_ref[0] = jnp.where(lo, a, r).astype(BF16)
    dst_ref[1] = jnp.where(lo, r, a).astype(BF16)


def _kv_block(ref, h, prev, cur):
    return jnp.concatenate([ref[h, pl.ds(prev, WINDOW), :], ref[h, pl.ds(cur, WINDOW), :]], axis=0)


def _rows(i):
    return slice(i * WINDOW, (i + 1) * WINDOW)


def _attn_fwd(q, k, v, ga, bias, sinks, l, job, store):
    S = q.shape[0]
    nblk = S // WINDOW

    def body(sink_ref, q_ref, k_ref, v_ref, ga_ref, bias_ref, o_ref, ca_ref, lse_ref, kd, vd, qs, ss, ps, os_):
        n = pl.program_id(0)

        @pl.when(n == 0)
        def _():
            _dup_heads(k_ref, kd)
            _dup_heads(v_ref, vd)

        cur = pl.multiple_of(n * WINDOW, WINDOW)
        prev = pl.multiple_of(jnp.maximum(n - 1, 0) * WINDOW, WINDOW)
        lane = lax.broadcasted_iota(I32, (WINDOW, LANES), 1)
        lo = lane < HEAD_DIM
        tri = lane > lax.broadcasted_iota(I32, (WINDOW, LANES), 0)
        lse_mat = jnp.zeros((WINDOW, LANES), F32)
        for g in range(N_KV):
            for i in range(KV_RATIO):
                h = g * KV_RATIO + i
                qp = q_ref[:, LANES * (h // 2):LANES * (h // 2 + 1)]
                qs[_rows(i), :] = jnp.where(lo if h % 2 == 0 else jnp.logical_not(lo), qp, jnp.zeros_like(qp))
            ss[...] = _nt(qs[...], _kv_block(kd, g, prev, cur))
            for i in range(KV_RATIO):
                h = g * KV_RATIO + i
                s = _fold(ss[_rows(i), :], tri) + bias_ref[h]
                sink = sink_ref[l, h]
                m = jnp.maximum(jnp.max(s, axis=1, keepdims=True), sink)
                e = jnp.exp(s - m)
                den = jnp.sum(e, axis=1, keepdims=True) + jnp.exp(sink - m)
                ps[_rows(i), :] = _unfold(e * (1.0 / den), tri)
                lse_mat = jnp.where(lane == h, m + jnp.log(den), lse_mat)
            os_[...] = _nn(ps[...], _kv_block(vd, g, prev, cur))
            for j in range(KV_RATIO // 2):
                pair = g * (KV_RATIO // 2) + j
                sl = slice(LANES * pair, LANES * (pair + 1))
                o_pair = jnp.where(lo, os_[_rows(2 * j), :], os_[_rows(2 * j + 1), :])
                o_ref[:, sl] = o_pair
                gav = ga_ref[:, sl]
                ca_ref[:, sl] = (o_pair * (gav * _sigmoid(gav))).astype(BF16)
        lse_ref[...] = lse_mat

    blk = pl.BlockSpec((WINDOW, ATTN_W), lambda n: (n, 0))
    full_kv = pl.BlockSpec((S, KV_W), lambda n: (0, 0))
    stack = KV_RATIO * WINDOW
    return _call(
        body, name="attn_fwd", grid=(nblk,),
        in_specs=[pl.BlockSpec(memory_space=pltpu.SMEM), blk, full_kv, full_kv, blk,
                  pl.BlockSpec((None, N_HEADS, WINDOW, WINDOW), lambda n: (jnp.minimum(n, 1), 0, 0, 0))],
        out_specs=[blk, blk, pl.BlockSpec((WINDOW, LANES), lambda n: (n, 0))],
        out_shape=[jax.ShapeDtypeStruct((S, ATTN_W), F32), jax.ShapeDtypeStruct((S, ATTN_W), BF16), jax.ShapeDtypeStruct((S, LANES), F32)],
        scratch_shapes=[pltpu.VMEM((N_KV, S, LANES), BF16), pltpu.VMEM((N_KV, S, LANES), BF16),
                        pltpu.VMEM((stack, LANES), BF16), pltpu.VMEM((stack, 2 * WINDOW), F32),
                        pltpu.VMEM((stack, 2 * WINDOW), BF16), pltpu.VMEM((stack, LANES), F32)],
        args=(sinks, q, k, v, ga, bias), sem=("arbitrary",), job=job, store=store)


def _pool_diff(u, halo, tile_index, tm):
    gw = POOL_W // len(POOL_WINDOWS)
    xh = jnp.concatenate([halo, u], axis=0)
    sums = []
    s = xh
    for step in (1, 2, 4, 8):
        s = s + pltpu.roll(s, step, axis=0)
        sums.append(s)
    t = tile_index * tm + lax.broadcasted_iota(I32, (tm, gw), 0)
    diffs = []
    for g, w in enumerate(POOL_WINDOWS):
        cols = slice(g * gw, (g + 1) * gw)
        cnt = jnp.minimum(t + 1, w).astype(F32)
        diffs.append(sums[g][POOL_HALO:, cols] / cnt - u[:, cols])
    return diffs


def _pool_weight(wp_ref, g):
    r = wp_ref.shape[1] // len(POOL_WINDOWS)
    return jnp.concatenate([wp_ref[j, g * r:(g + 1) * r, :] for j in range(N_CHIPS)], axis=0)


def _pool_fwd(u, gp, ca, w_pool, ps, l, job, store):
    S = u.shape[0]
    tm = min(S, 256)
    hb = tm // POOL_HALO
    gw = POOL_W // len(POOL_WINDOWS)

    def body(u_ref, uh_ref, gp_ref, ca_ref, wp_ref, ps_ref, c_ref):
        i = pl.program_id(0)
        uv = u_ref[...]
        halo = jnp.where(i > 0, uh_ref[...], 0.0)
        diffs = _pool_diff(uv, halo, i, tm)
        c_ref[:, 0:ATTN_W] = ca_ref[...]
        for g in range(len(POOL_WINDOWS)):
            cols = slice(g * gw, (g + 1) * gw)
            mm = _nn(diffs[g].astype(BF16), _pool_weight(wp_ref, g))
            gpv = gp_ref[:, cols]
            b = (mm * ps_ref[:, cols]) * (gpv * _sigmoid(gpv))
            c_ref[:, ATTN_W + g * gw:ATTN_W + (g + 1) * gw] = b.astype(BF16)

    row = pl.BlockSpec((tm, POOL_W), lambda i: (i, 0))
    return _call(
        body, name="pool_fwd", grid=(S // tm,),
        in_specs=[row, pl.BlockSpec((POOL_HALO, POOL_W), lambda i: (jnp.maximum(i * hb - 1, 0), 0)), row, row,
                  pl.BlockSpec(w_pool.shape, lambda i: (0, 0, 0)),
                  pl.BlockSpec((None, 1, POOL_W), lambda i: (l, 0, 0))],
        out_specs=[pl.BlockSpec((tm, ATTN_W + POOL_W), lambda i: (i, 0))],
        out_shape=[jax.ShapeDtypeStruct((S, ATTN_W + POOL_W), BF16)],
        args=(u, u, gp, ca, w_pool, ps), sem=("parallel",), job=job, store=store)[0]


def _ple_embed(p_ref, wple_ref):
    pb = p_ref[...].astype(BF16)
    return jnp.concatenate([_nn(pb, wple_ref[j]) for j in range(N_CHIPS)], axis=1)


def _outproj_ln(c, xh_in, gain_in, bias_in, l_in, gl, p, w_out, w_ple, gain, bias, l, alpha, job, store):
    S, D = xh_in.shape
    tm = min(S, 256)
    kq = D // N_CHIPS

    def body(c_ref, x_ref, gi_ref, bi_ref, gl_ref, p_ref, wo_ref, wp_ref, gain_ref, bias_ref, yb_ref, xh_ref, rs_ref):
        mix = _nn(c_ref[:, 0:kq], wo_ref[0])
        for j in range(1, N_CHIPS):
            mix = mix + _nn(c_ref[:, j * kq:(j + 1) * kq], wo_ref[j])
        ple = _sigmoid(gl_ref[...]) * _ple_embed(p_ref, wp_ref)
        x = x_ref[...] * gi_ref[...] + bi_ref[...]
        z = (alpha * x + mix) + ple
        mu = jnp.mean(z, axis=1, keepdims=True)
        zc = z - mu
        var = jnp.mean(zc * zc, axis=1, keepdims=True)
        rstd = lax.rsqrt(var + LN_EPS)
        xhat = zc * rstd
        yb_ref[...] = (xhat * gain_ref[...] + bias_ref[...]).astype(BF16)
        xh_ref[...] = xhat
        rs_ref[...] = rstd

    row = pl.BlockSpec((tm, D), lambda i: (i, 0))
    vec = pl.BlockSpec((None, 1, D), lambda i: (l, 0, 0))
    vec_in = pl.BlockSpec((None, 1, D), lambda i: (l_in, 0, 0))
    return _call(
        body, name="outproj_ln", grid=(S // tm,),
        in_specs=[row, row, vec_in, vec_in, row, pl.BlockSpec((tm, p.shape[1]), lambda i: (i, 0)),
                  pl.BlockSpec(w_out.shape, lambda i: (0, 0, 0)), pl.BlockSpec(w_ple.shape, lambda i: (0, 0, 0)), vec, vec],
        out_specs=[row, row, pl.BlockSpec((tm, 1), lambda i: (i, 0))],
        out_shape=[jax.ShapeDtypeStruct((S, D), BF16), jax.ShapeDtypeStruct((S, D), F32), jax.ShapeDtypeStruct((S, 1), F32)],
        args=(c, xh_in, gain_in, bias_in, gl, p, w_out, w_ple, gain, bias), sem=("parallel",), job=job, store=store)


def _loss_and_grad(xhat, gain, bias, l, target):
    S, D = xhat.shape
    tm = min(S, 512)

    def body(xh_ref, g_ref, b_ref, t_ref, dy_ref, acc_ref):
        @pl.when(pl.program_id(0) == 0)
        def _():
            acc_ref[...] = jnp.zeros_like(acc_ref)
        d = (xh_ref[...] * g_ref[...] + b_ref[...]) - t_ref[...]
        dy_ref[...] = d * (1.0 / D)
        acc_ref[...] += jnp.sum(jnp.mean(d * d, axis=1, keepdims=True), axis=0, keepdims=True)

    row = pl.BlockSpec((tm, D), lambda i: (i, 0))
    vec = pl.BlockSpec((None, 1, D), lambda i: (l, 0, 0))
    return _pcall(body, name="loss", grid=(S // tm,), in_specs=[row, vec, vec, row],
                  out_specs=[row, pl.BlockSpec((8, LANES), lambda i: (0, 0))],
                  out_shape=[jax.ShapeDtypeStruct((S, D), F32), jax.ShapeDtypeStruct((8, LANES), F32)],
                  compiler_params=_cparams(("arbitrary",)))(xhat, gain, bias, target)


def _ln_bwd(dy, xhat, rstd, gl, p, w_ple, gain, l, job, store):
    S, D = dy.shape
    tm = min(S, 256)

    def body(dy_ref, xh_ref, rs_ref, gl_ref, p_ref, wp_ref, gain_ref, dz_ref, dzb_ref, de_ref, dgl_ref, gg_ref, gb_ref):
        @pl.when(pl.program_id(0) == 0)
        def _():
            gg_ref[...] = jnp.zeros_like(gg_ref)
            gb_ref[...] = jnp.zeros_like(gb_ref)
        dyv = dy_ref[...]
        xh = xh_ref[...]
        dxh = dyv * gain_ref[...]
        m1 = jnp.mean(dxh, axis=1, keepdims=True)
        m2 = jnp.mean(dxh * xh, axis=1, keepdims=True)
        dz = rs_ref[...] * ((dxh - m1) - xh * m2)
        gg_ref[...] += jnp.sum(dyv * xh, axis=0, keepdims=True)
        gb_ref[...] += jnp.sum(dyv, axis=0, keepdims=True)
        sg = _sigmoid(gl_ref[...])
        e = _ple_embed(p_ref, wp_ref)
        dz_ref[...] = dz
        dzb_ref[...] = dz.astype(BF16)
        de_ref[...] = (dz * sg).astype(BF16)
        dgl_ref[...] = ((dz * e) * (sg * (1.0 - sg))).astype(BF16)

    row = pl.BlockSpec((tm, D), lambda i: (i, 0))
    vec_in = pl.BlockSpec((None, 1, D), lambda i: (l, 0, 0))
    vec_out = pl.BlockSpec((1, D), lambda i: (0, 0))
    bsd = jax.ShapeDtypeStruct((S, D), BF16)
    return _call(
        body, name="ln_bwd", grid=(S // tm,),
        in_specs=[row, row, pl.BlockSpec((tm, 1), lambda i: (i, 0)), row, pl.BlockSpec((tm, p.shape[1]), lambda i: (i, 0)),
                  pl.BlockSpec(w_ple.shape, lambda i: (0, 0, 0)), vec_in],
        out_specs=[row, row, row, row, vec_out, vec_out],
        out_shape=[jax.ShapeDtypeStruct((S, D), F32), bsd, bsd, bsd, jax.ShapeDtypeStruct((1, D), F32), jax.ShapeDtypeStruct((1, D), F32)],
        args=(dy, xhat, rstd, gl, p, w_ple, gain), sem=("arbitrary",), job=job, store=store)


def _matmul_nn_acc(a, b4, name, add, add_scale, job, store):
    S = a.shape[0]
    KS, tk, N = b4.shape
    tm, tn = min(S, 512), min(N, 1024)

    def body(a_ref, b_ref, add_ref, o_ref):
        acc = add_scale * add_ref[...]
        for k in range(KS):
            acc = acc + _nn(a_ref[:, k * tk:(k + 1) * tk], b_ref[k])
        o_ref[...] = acc

    return _call(body, name=name, grid=(N // tn, S // tm),
                 in_specs=[pl.BlockSpec((tm, KS * tk), lambda n, i: (i, 0)),
                           pl.BlockSpec((KS, tk, tn), lambda n, i: (0, 0, n)),
                           pl.BlockSpec((tm, tn), lambda n, i: (i, n))],
                 out_specs=[pl.BlockSpec((tm, tn), lambda n, i: (i, n))], out_shape=[jax.ShapeDtypeStruct((S, N), F32)],
                 args=(a, b4, add), sem=("parallel", "parallel"), job=job, store=store)[0]


def _matmul_nt_rows(a, b4, name, add=None, add_scale=1.0, job=None, store=None):
    S, K = a.shape
    nq = b4.shape[1]
    tm = min(S, 512)
    out_spec = pl.BlockSpec((tm, N_CHIPS * nq), lambda i: (i, 0))
    in_specs = [pl.BlockSpec((tm, K), lambda i: (i, 0)), pl.BlockSpec(b4.shape, lambda i: (0, 0, 0))]
    if add is None:
        def body(a_ref, b_ref, o_ref):
            av = a_ref[...]
            for j in range(N_CHIPS):
                o_ref[:, j * nq:(j + 1) * nq] = _nt(av, b_ref[j])
        args = (a, b4)
    else:
        def body(a_ref, b_ref, add_ref, o_ref):
            av = a_ref[...]
            for j in range(N_CHIPS):
                cols = slice(j * nq, (j + 1) * nq)
                o_ref[:, cols] = _nt(av, b_ref[j]) + add_scale * add_ref[:, cols]
        in_specs.append(out_spec)
        args = (a, b4, add)

    return _call(body, name=name, grid=(S // tm,), in_specs=in_specs,
                 out_specs=[out_spec], out_shape=[jax.ShapeDtypeStruct((S, N_CHIPS * nq), F32)],
                 args=args, sem=("parallel",), job=job, store=store)[0]


def _matmul_tn(a, b, R, C, name, by_rows, job=None, store=None):
    S = a.shape[0]
    tn = min(C, 1024 if R <= 512 else 512)
    nt = C // tn

    def body(a_ref, b_ref, o_ref, at_ref):
        @pl.when(pl.program_id(1) == 0)
        def _():
            at_ref[...] = a_ref[...].T
        o_ref[...] = _nn(at_ref[...], b_ref[...])

    if by_rows:
        a_spec = pl.BlockSpec((S, R), lambda j, n: (0, j))
        b_spec = pl.BlockSpec((S, tn), lambda j, n: (0, n))
    else:
        a_spec = pl.BlockSpec((S, R), lambda j, n: (0, 0))
        b_spec = pl.BlockSpec((S, tn), lambda j, n: (0, j * nt + n))
    return _call(body, name=name, grid=(N_CHIPS, nt), in_specs=[a_spec, b_spec],
                 out_specs=[pl.BlockSpec((None, R, tn), lambda j, n: (j, 0, n))],
                 out_shape=[jax.ShapeDtypeStruct((N_CHIPS, R, C), F32)],
                 scratch_shapes=[pltpu.VMEM((R, S), BF16)],
                 args=(a, b), sem=("parallel", "arbitrary"), job=job, store=store)[0]


def _pool_bwd(u, gp, dc, w_pool, ps, l, job=None, store=None):
    S = u.shape[0]
    tm = min(S, 256)
    hb = tm // POOL_HALO
    ngrp = len(POOL_WINDOWS)
    gw = POOL_W // ngrp
    rr = gw // N_CHIPS

    def body(u_ref, uh_ref, gp_ref, dc_ref, wp_ref, ps_ref, dd_ref, dgp_ref, gps_ref, gwp_ref):
        i = pl.program_id(0)

        @pl.when(i == 0)
        def _():
            gps_ref[...] = jnp.zeros_like(gps_ref)
            gwp_ref[...] = jnp.zeros_like(gwp_ref)
        uv = u_ref[...]
        halo = jnp.where(i > 0, uh_ref[...], 0.0)
        diffs = _pool_diff(uv, halo, i, tm)
        for g in range(ngrp):
            cols = slice(g * gw, (g + 1) * gw)
            w = _pool_weight(wp_ref, g)
            db = diffs[g].astype(BF16)
            mm = _nn(db, w)
            gpv = gp_ref[:, cols]
            sg = _sigmoid(gpv)
            si = gpv * sg
            dsi = sg * (1.0 + gpv * (1.0 - sg))
            dcb = dc_ref[:, cols]
            psv = ps_ref[:, cols]
            d_mm = (dcb * si) * psv
            gps_ref[:, cols] += jnp.sum((dcb * si) * mm, axis=0, keepdims=True)
            dgp_ref[:, cols] = (dcb * (mm * psv)) * dsi
            d_mmb = d_mm.astype(BF16)
            dd_ref[:, cols] = _nt(d_mmb, w)
            gwt = _tn(db, d_mmb)
            for j in range(N_CHIPS):
                gwp_ref[j, g * rr:(g + 1) * rr, :] += gwt[j * rr:(j + 1) * rr, :]

    row = pl.BlockSpec((tm, POOL_W), lambda i: (i, 0))
    wspec = pl.BlockSpec(w_pool.shape, lambda i: (0, 0, 0))
    return _call(
        body, name="pool_bwd", grid=(S // tm,),
        in_specs=[row, pl.BlockSpec((POOL_HALO, POOL_W), lambda i: (jnp.maximum(i * hb - 1, 0), 0)), row,
                  pl.BlockSpec((tm, POOL_W), lambda i: (i, 1)), wspec, pl.BlockSpec((None, 1, POOL_W), lambda i: (l, 0, 0))],
        out_specs=[row, row, pl.BlockSpec((1, POOL_W), lambda i: (0, 0)), wspec],
        out_shape=[jax.ShapeDtypeStruct((S, POOL_W), F32), jax.ShapeDtypeStruct((S, POOL_W), F32),
                   jax.ShapeDtypeStruct((1, POOL_W), F32), jax.ShapeDtypeStruct(w_pool.shape, F32)],
        args=(u, u, gp, dc, w_pool, ps), sem=("arbitrary",), job=job, store=store)


def _pool_window_t(dd, halo_next, tile_index, tm):
    gw = POOL_W // len(POOL_WINDOWS)
    n = tm + POOL_HALO
    t = tile_index * tm + lax.broadcasted_iota(I32, (n, gw), 0)
    xh = jnp.concatenate([dd, halo_next], axis=0)
    outs = []
    for g, w in enumerate(POOL_WINDOWS):
        cols = slice(g * gw, (g + 1) * gw)
        cnt = jnp.minimum(t + 1, w).astype(F32)
        s = xh[:, cols] / cnt
        step = 1
        while step < w:
            s = s + pltpu.roll(s, n - step, axis=0)
            step *= 2
        outs.append(s[:tm] - dd[:, cols])
    return outs


def _assemble_dh(dq, dk, dv, dga, dd, dgp):
    S = dq.shape[0]
    tm = min(S, 256)
    hb = tm // POOL_HALO
    nt = S // tm

    def body(dq_ref, dk_ref, dv_ref, dga_ref, dd_ref, ddn_ref, dgp_ref, dh_ref, gb_ref):
        i = pl.program_id(0)

        @pl.when(i == 0)
        def _():
            gb_ref[...] = jnp.zeros_like(gb_ref)
        halo = jnp.where(i < nt - 1, ddn_ref[...], 0.0)
        du = jnp.concatenate(_pool_window_t(dd_ref[...], halo, i, tm), axis=1)
        dkv = dk_ref[...]
        dgav = dga_ref[...]
        parts = [(OFF_Q, dq_ref[...]), (OFF_KA, dkv), (OFF_KB, dkv), (OFF_V, dv_ref[...]), (OFF_GA, dgav),
                 (OFF_U, du[:, 0:896]), (OFF_UA, du[:, 896:1024]), (OFF_UB, du[:, 896:1024]), (OFF_GP, dgp_ref[...])]
        for off, val in parts:
            w = val.shape[1]
            dh_ref[:, off:off + w] = val.astype(BF16)
            gb_ref[:, off:off + w] += jnp.sum(val, axis=0, keepdims=True)

    def row(w):
        return pl.BlockSpec((tm, w), lambda i: (i, 0))

    return _pcall(
        body, name="assemble_dh", grid=(nt,),
        in_specs=[row(ATTN_W), row(KV_W), row(KV_W), row(ATTN_W), row(POOL_W),
                  pl.BlockSpec((POOL_HALO, POOL_W), lambda i: (jnp.minimum((i + 1) * hb, S // POOL_HALO - 1), 0)), row(POOL_W)],
        out_specs=[row(EXT), pl.BlockSpec((1, EXT), lambda i: (0, 0))],
        out_shape=[jax.ShapeDtypeStruct((S, EXT), BF16), jax.ShapeDtypeStruct((1, EXT), F32)],
        compiler_params=_cparams(("arbitrary",)),
    )(dq, dk, dv, dga, dd, dd, dgp)


def _attn_bwd(q, k, v, ga, o, dc, lse, bias, sinks, dbias_in, l, job, store):
    S = q.shape[0]
    nblk = S // WINDOW

    def body(sink_ref, q_ref, k_ref, v_ref, ga_ref, o_ref, dc_ref, lse_ref, bias_ref, dbin_ref,
             dq_ref, dga_ref, dk_ref, dv_ref, db_ref, ds_ref, kd, vd, qs, dos, dls, ss, dps, dss, pss, dqs):
        n = pl.program_id(0)

        @pl.when(n == 0)
        def _():
            _dup_heads(k_ref, kd)
            _dup_heads(v_ref, vd)
            dk_ref[...] = jnp.zeros_like(dk_ref)
            dv_ref[...] = jnp.zeros_like(dv_ref)
            db_ref[...] = dbin_ref[...]
            ds_ref[...] = jnp.zeros_like(ds_ref)

        cur = pl.multiple_of(n * WINDOW, WINDOW)
        prev = pl.multiple_of(jnp.maximum(n - 1, 0) * WINDOW, WINDOW)
        lane = lax.broadcasted_iota(I32, (WINDOW, LANES), 1)
        lane8 = lax.broadcasted_iota(I32, (8, LANES), 1)
        lo = lane < HEAD_DIM
        tri = lane > lax.broadcasted_iota(I32, (WINDOW, LANES), 0)
        lse_t = lse_ref[...]
        dk_t, dv_t = [], []
        dsk = jnp.zeros((8, LANES), F32)
        for g in range(N_KV):
            kb = _kv_block(kd, g, prev, cur)
            vb = _kv_block(vd, g, prev, cur)
            for j in range(KV_RATIO // 2):
                pair = g * (KV_RATIO // 2) + j
                sl = slice(LANES * pair, LANES * (pair + 1))
                qp = q_ref[:, sl]
                op = o_ref[:, sl]
                dcp = dc_ref[:, sl]
                gav = ga_ref[:, sl]
                sg = _sigmoid(gav)
                d_o = dcp * (gav * sg)
                dga_ref[:, sl] = (dcp * op) * (sg * (1.0 + gav * (1.0 - sg)))
                prod = d_o * op
                for par in range(2):
                    msk = lo if par == 0 else jnp.logical_not(lo)
                    rows = _rows(2 * j + par)
                    qs[rows, :] = jnp.where(msk, qp, jnp.zeros_like(qp))
                    dos[rows, :] = jnp.where(msk, d_o, 0.0).astype(BF16)
                    delta = jnp.sum(jnp.where(msk, prod, 0.0), axis=1, keepdims=True)
                    dls[rows, :] = jnp.broadcast_to(delta, (WINDOW, LANES))
            ss[...] = _nt(qs[...], kb)
            dps[...] = _nt(dos[...], vb)
            for i in range(KV_RATIO):
                h = g * KV_RATIO + i
                rows = _rows(i)
                lse_h = jnp.sum(jnp.where(lane == h, lse_t, 0.0), axis=1, keepdims=True)
                p = jnp.exp(_fold(ss[rows, :], tri) + bias_ref[h] - lse_h)
                delta = dls[rows, :]
                dsc = p * (_fold(dps[rows, :], tri) - delta)
                db_ref[h] += dsc
                psink = jnp.exp(sink_ref[l, h] - lse_h)
                dsk = dsk + jnp.where(lane8 == h, -jnp.sum(psink * delta, axis=0, keepdims=True), 0.0)
                dss[rows, :] = _unfold(dsc, tri)
                pss[rows, :] = _unfold(p, tri)
            dqs[...] = _nn(dss[...], kb) * QK_SCALE
            for j in range(KV_RATIO // 2):
                pair = g * (KV_RATIO // 2) + j
                dq_ref[:, LANES * pair:LANES * (pair + 1)] = jnp.where(lo, dqs[_rows(2 * j), :], dqs[_rows(2 * j + 1), :])
            dk_t.append(_tn(qs[...], dss[...]))
            dv_t.append(_tn(dos[...], pss[...]))

        def untranspose(acc):
            return jnp.concatenate([a[:HEAD_DIM] + a[HEAD_DIM:] for a in acc], axis=0).T

        dkb = untranspose(dk_t)
        dvb = untranspose(dv_t)
        dk_ref[pl.ds(prev, WINDOW), :] += dkb[:WINDOW]
        dk_ref[pl.ds(cur, WINDOW), :] += dkb[WINDOW:]
        dv_ref[pl.ds(prev, WINDOW), :] += dvb[:WINDOW]
        dv_ref[pl.ds(cur, WINDOW), :] += dvb[WINDOW:]
        ds_ref[...] += dsk

    blk = pl.BlockSpec((WINDOW, ATTN_W), lambda n: (n, 0))
    full_kv = pl.BlockSpec((S, KV_W), lambda n: (0, 0))
    full_b = pl.BlockSpec((N_HEADS, WINDOW, WINDOW), lambda n: (0, 0, 0))
    stack = KV_RATIO * WINDOW
    return _call(
        body, name="attn_bwd", grid=(nblk,),
        in_specs=[pl.BlockSpec(memory_space=pltpu.SMEM), blk, full_kv, full_kv, blk, blk, blk,
                  pl.BlockSpec((WINDOW, LANES), lambda n: (n, 0)),
                  pl.BlockSpec((None, N_HEADS, WINDOW, WINDOW), lambda n: (jnp.minimum(n, 1), 0, 0, 0)), full_b],
        out_specs=[blk, blk, full_kv, full_kv, full_b, pl.BlockSpec((8, LANES), lambda n: (0, 0))],
        out_shape=[jax.ShapeDtypeStruct((S, ATTN_W), F32), jax.ShapeDtypeStruct((S, ATTN_W), F32), jax.ShapeDtypeStruct((S, KV_W), F32),
                   jax.ShapeDtypeStruct((S, KV_W), F32), jax.ShapeDtypeStruct((N_HEADS, WINDOW, WINDOW), F32),
                   jax.ShapeDtypeStruct((8, LANES), F32)],
        scratch_shapes=[pltpu.VMEM((N_KV, S, LANES), BF16), pltpu.VMEM((N_KV, S, LANES), BF16),
                        pltpu.VMEM((stack, LANES), BF16), pltpu.VMEM((stack, LANES), BF16), pltpu.VMEM((stack, LANES), F32),
                        pltpu.VMEM((stack, 2 * WINDOW), F32), pltpu.VMEM((stack, 2 * WINDOW), F32),
                        pltpu.VMEM((stack, 2 * WINDOW), BF16), pltpu.VMEM((stack, 2 * WINDOW), BF16),
                        pltpu.VMEM((stack, LANES), F32)],
        args=(sinks, q, k, v, ga, o, dc, lse, bias, dbias_in), sem=("arbitrary",), job=job, store=store)


def _pack_small(arrs):
    flat = []
    for a in arrs:
        v = a.reshape(-1)
        flat.append(jnp.pad(v, (0, (-v.shape[0]) % LANES)))
    v = jnp.concatenate(flat)
    v = jnp.pad(v, (0, (-v.shape[0]) % (8 * LANES)))
    return v.reshape(-1, LANES)


def _unpack_small(packed, shapes):
    v = packed.reshape(-1)
    outs, off = [], 0
    for shp in shapes:
        n = math.prod(shp)
        outs.append(v[off:off + n].reshape(shp))
        off += n + (-n) % LANES
    return outs


def _bias_to_ext(b):
    L = b.shape[0]
    z = jnp.zeros((L, HALF_TILE), b.dtype)
    parts = []
    for j in range(N_CHIPS):
        seg = b[:, j * SHARD:(j + 1) * SHARD]
        parts += [z, seg] if j % 2 else [seg, z]
    return jnp.concatenate(parts, axis=1).reshape(L, 1, EXT)


def _bias_from_ext(g):
    parts = []
    for j in range(N_CHIPS):
        o = j * SHARD_P + (HALF_TILE if j % 2 else 0)
        parts.append(g[:, o:o + SHARD])
    return jnp.concatenate(parts, axis=1)


def kernel(x, p, w_in, b_in, w_out, attn_sinks, rel_bias, w_pool, pool_scale, w_ple, w_gate_ple, ln_gain, ln_bias, loss_target, m_w_in, m_b_in, m_w_out, m_attn_sinks, m_rel_bias, m_w_pool, m_pool_scale, m_w_ple, m_w_gate_ple, m_ln_gain, m_ln_bias, v_w_in, v_b_in, v_w_out, v_attn_sinks, v_rel_bias, v_w_pool, v_pool_scale, v_w_ple, v_w_gate_ple, v_ln_gain, v_ln_bias):
    L = w_in.shape[0]
    S, D = x.shape[1], x.shape[2]
    assert D == D_MODEL and w_in.shape[2] == SHARD and S % WINDOW == 0
    alpha = (2.0 * L) ** 0.25
    xc, yc, cc = _mesh_pos()
    idx = jnp.stack([2 * xc + yc, yc, cc]).astype(I32)
    store = {}

    def wkeys(l, names):
        return [("w", l, t) for t in names]

    def weight(l, t):
        return _unhalves(store["w", l, t])

    w_in_t = jnp.swapaxes(w_in, 1, 2)
    w_pool2 = w_pool.reshape(L, w_pool.shape[1] * w_pool.shape[2], w_pool.shape[3])
    for l in range(L):
        store["w", l, "in"] = _halves(_piece_in(w_in_t, l, idx))
        store["w", l, "out"] = _halves(_piece(w_out, l, idx, "piece_out"))
        store["w", l, "gate"] = _halves(_piece(w_gate_ple, l, idx, "piece_gate"))
        store["w", l, "ple"] = _halves(_piece(w_ple, l, idx, "piece_ple"))
        store["w", l, "pool"] = _halves(_piece(w_pool2, l, idx, "piece_pool"))
    _allgather_now(wkeys(0, WEIGHTS), store)

    b_ext = _bias_to_ext(b_in)
    ps3 = pool_scale.reshape(L, 1, POOL_W)
    gain3 = ln_gain.reshape(L, 1, D)
    bias3 = ln_bias.reshape(L, 1, D)
    onehot = _bucket_onehot()
    bias_hqk = _masked_bias(_bias_fwd(rel_bias.T, onehot).reshape(N_HEADS, WINDOW, WINDOW))

    xs = x[0]
    xb = _to_bf16(xs)
    pb = _to_bf16(p.reshape(L * S, p.shape[3])).reshape(L, S, p.shape[3])
    saved = []
    x_norm, x_gain, x_bias, x_l = xs, jnp.ones((1, 1, D), F32), jnp.zeros((1, 1, D), F32), 0
    late = ["out", "ple", "pool"]
    for l in range(L):
        nxt = l + 1 < L
        job = _Job()
        if nxt:
            job.add(_ag_ici, wkeys(l + 1, ["in"]), 3)
        if l >= 1:
            job.add(_ag_fwd, wkeys(l, ["gate"]), 3)
        q, k, v, ga, u, gp = _inproj(xb, weight(l, "in"), b_ext, l, job, store)
        gl = _matmul_nn(xb, weight(l, "gate"), "gate_logits", None, None)
        job = _Job().add(_ag_fwd, wkeys(l + 1, ["in"]), 3).add(_ag_ici, wkeys(l + 1, late), 9) if nxt else None
        o, ca, lse = _attn_fwd(q, k, v, ga, bias_hqk, attn_sinks, l, job, store)
        c = _pool_fwd(u, gp, ca, weight(l, "pool"), ps3, l, None, None)
        job = _Job().add(_ag_fwd, wkeys(l + 1, late), 9).add(_ag_ici, wkeys(l + 1, ["gate"]), 3) if nxt else None
        yb, xhat, rstd = _outproj_ln(c, x_norm, x_gain, x_bias, x_l, gl, pb[l], weight(l, "out"), weight(l, "ple"), gain3, bias3, l,
                                     alpha, job, store)
        saved.append(dict(xb=xb, q=q, k=k, v=v, ga=ga, u=u, gp=gp, gl=gl, o=o, lse=lse, c=c, xhat=xhat, rstd=rstd))
        x_norm, x_gain, x_bias, x_l, xb = xhat, gain3, bias3, l, yb

    dy, loss_acc = _loss_and_grad(x_norm, gain3, bias3, L - 1, loss_target[0])
    loss = lax.psum(0.5 * loss_acc[0, 0], ("x", "y", "c"))

    shapes = {t: store["w", 0, t].shape for t in WEIGHTS}
    for t in WEIGHTS:
        store["full", t] = lax.empty((L,) + shapes[t][1:], F32)

    def rs_keys(kind, l, names):
        return [(kind, l, t) for t in names]

    def rs_pair_job(l, names):
        for t in names:
            store["ra", l, t] = lax.empty((N_CHIPS,) + shapes[t][2:], F32)
        return _Job().add(_rs_pair, rs_keys("g", l, names) + rs_keys("ra", l, names), len(names))

    def rs_pair_add(l, names):
        for t in names:
            p32, p16 = _rs_pair_add(store["g", l, t], store["ra", l, t], idx, "rs_pair_add_" + t)
            store["p32", l, t], store["p16", l, t] = p32, p16
            store["rb", l, t] = lax.empty((3,) + shapes[t][2:], BF16)

    def rs_ici_job(l, names):
        return _Job().add(_rs_ici, rs_keys("p16", l, names) + rs_keys("rb", l, names), 3 * len(names))

    def rs_chip_add(l, names):
        for t in names:
            store["full", t] = _rs_chip_add(store["p32", l, t], store["rb", l, t], store["full", t], l, idx, "rs_chip_add_" + t)

    def rs_share_job(l):
        return _Job().add(functools.partial(_rs_share, layer=l), [("full", t) for t in WEIGHTS], len(WEIGHTS))

    dbias = jnp.zeros((N_HEADS, WINDOW, WINDOW), F32)
    small = [None] * L
    h_in = shapes["in"][2]
    head_rows = (h_in * 5 // 8) // 16 * 16
    early = ["out", "gate", "ple"]

    def in_ici_job(l, rows):
        return _Job().add(functools.partial(_rs_ici, rows=rows), rs_keys("p16", l, ["in"]) + rs_keys("rb", l, ["in"]), 3)

    for l in reversed(range(L)):
        sv = saved[l]
        pl_l = pb[l]
        job = in_ici_job(l + 1, (head_rows, h_in - head_rows)) if l + 1 < L else None
        dz, dzb, d_e, d_gl, ggain, gbias = _ln_bwd(dy, sv["xhat"], sv["rstd"], sv["gl"], pl_l, weight(l, "ple"), gain3, l, job, store)
        if l + 1 < L:
            rs_chip_add(l + 1, ["in"])
        dc = _matmul_nt_rows(dzb, weight(l, "out"), "d_mix_in", job=rs_share_job(l + 1) if l + 1 < L else None, store=store)
        g_out = _matmul_tn(sv["c"], dzb, D // N_CHIPS, D, "grad_w_out", by_rows=True)
        g_ple = _matmul_tn(pl_l, d_e, pl_l.shape[1], D // N_CHIPS, "grad_w_ple", by_rows=False)
        g_gate = _matmul_tn(sv["xb"], d_gl, D // N_CHIPS, D, "grad_w_gate", by_rows=True)
        for t, g in zip(early, (g_out, g_gate, g_ple)):
            store["g", l, t] = _halves(g)
        dd, dgp, gps, g_pool = _pool_bwd(sv["u"], sv["gp"], dc, weight(l, "pool"), ps3, l, rs_pair_job(l, early), store)
        store["g", l, "pool"] = _halves(g_pool)
        rs_pair_add(l, early)
        job = rs_pair_job(l, ["pool"])
        job.add(_rs_ici, rs_keys("p16", l, ["out"]) + rs_keys("rb", l, ["out"]), 3)
        dq, dga, dk, dv, dbias, dsink = _attn_bwd(sv["q"], sv["k"], sv["v"], sv["ga"], sv["o"], dc, sv["lse"], bias_hqk,
                                                  attn_sinks, dbias, l, job, store)
        rs_pair_add(l, ["pool"])
        dh, gbe = _assemble_dh(dq, dk, dv, dga, dd, dgp)
        g_in = _matmul_tn(dh, sv["xb"], SHARD_P, D, "grad_w_in", by_rows=True, job=rs_ici_job(l, ["gate", "ple", "pool"]), store=store)
        rs_chip_add(l, ["out", "gate", "ple", "pool"])
        store["g", l, "in"] = _halves(g_in)
        dx1 = _matmul_nt_rows(d_gl, weight(l, "gate"), "d_x_gate", dz, alpha, job=rs_pair_job(l, ["in"]), store=store)
        rs_pair_add(l, ["in"])
        dy = _matmul_nn_acc(dh, weight(l, "in"), "d_x", dx1, 1.0, in_ici_job(l, (0, head_rows) if l > 0 else (0, h_in)), store)
        small[l] = dict(b_in=_bias_from_ext(gbe)[0], sinks=dsink[0, :N_HEADS], ps=gps[0], gain=ggain[0], bias=gbias[0])
    rs_chip_add(0, ["in"])
    grad_x = dy[None]

    _run_job("rs_pair_share", rs_share_job(0), store)
    full = {t: _unhalves(store["full", t]) for t in WEIGHTS}

    def t_back(a):
        return jnp.swapaxes(a, 1, 2)

    def pool4(a):
        return a.reshape(w_pool.shape)

    r_in = _adamw(w_in_t, full["in"], jnp.swapaxes(m_w_in, 1, 2), jnp.swapaxes(v_w_in, 1, 2), idx, "adamw_w_in", tr=HALF_TILE)
    gw_in, dw_in, nm_in, nv_in = [t_back(a) for a in r_in]
    gw_out, dw_out, nm_out, nv_out = _adamw(w_out, full["out"], m_w_out, v_w_out, idx, "adamw_w_out")
    gw_gate, dw_gate, nm_gate, nv_gate = _adamw(w_gate_ple, full["gate"], m_w_gate_ple, v_w_gate_ple, idx, "adamw_w_gate")
    gw_ple, dw_ple, nm_ple, nv_ple = _adamw(w_ple, full["ple"], m_w_ple, v_w_ple, idx, "adamw_w_ple")
    r_pool = _adamw(w_pool2, full["pool"], m_w_pool.reshape(w_pool2.shape), v_w_pool.reshape(w_pool2.shape), idx, "adamw_w_pool")
    gw_pool, dw_pool, nm_pool, nv_pool = [pool4(a) for a in r_pool]

    g_rel = _bias_bwd(dbias.reshape(N_HEADS, -1), onehot).T
    small_shapes = [b_in.shape, attn_sinks.shape, rel_bias.shape, pool_scale.shape, ln_gain.shape, ln_bias.shape]
    g_small = [jnp.stack([small[l]["b_in"] for l in range(L)]), jnp.stack([small[l]["sinks"] for l in range(L)]), g_rel,
               jnp.stack([small[l]["ps"] for l in range(L)]), jnp.stack([small[l]["gain"] for l in range(L)]),
               jnp.stack([small[l]["bias"] for l in range(L)])]
    packed = _small_allreduce_adamw(
        _pack_small(g_small),
        _pack_small([b_in, attn_sinks, rel_bias, pool_scale, ln_gain, ln_bias]),
        _pack_small([m_b_in, m_attn_sinks, m_rel_bias, m_pool_scale, m_ln_gain, m_ln_bias]),
        _pack_small([v_b_in, v_attn_sinks, v_rel_bias, v_pool_scale, v_ln_gain, v_ln_bias]))
    sg, sd, sm, sv_ = [_unpack_small(a, small_shapes) for a in packed]

    def order(big, sm_):
        return (big[0], sm_[0], big[1], sm_[1], sm_[2], big[2], sm_[3], big[3], big[4], sm_[4], sm_[5])

    return (loss, grad_x,
            *order((gw_in, gw_out, gw_pool, gw_ple, gw_gate), sg),
            *order((dw_in, dw_out, dw_pool, dw_ple, dw_gate), sd),
            *order((nm_in, nm_out, nm_pool, nm_ple, nm_gate), sm),
            *order((nv_in, nv_out, nv_pool, nv_ple, nv_gate), sv_))
```

```python
import functools
import math

import jax
import jax.numpy as jnp
from jax import lax
from jax.experimental import pallas as pl
from jax.experimental.pallas import tpu as pltpu

F32 = jnp.float32
BF16 = jnp.bfloat16
I32 = jnp.int32
MESH = pl.DeviceIdType.MESH

HEAD_DIM = 64
QK_SCALE = HEAD_DIM ** -0.5
WINDOW = 128
KV_RATIO = 8
POOL_WINDOWS = (2, 4, 8, 16)
POOL_HALO = 16
REL_BUCKETS = 32
REL_MAX_DIST = 128
LN_EPS = 1e-5
ADAM_LR, ADAM_B1, ADAM_B2, ADAM_EPS, ADAM_WD, ADAM_STEP = 0.001, 0.9, 0.999, 1e-08, 0.01, 10

LANES = 128
VMEM_LIMIT = 52 * 1024 * 1024
N_CHIPS = 4
N_DEV = 8

D_MODEL = 2048
ATTN_W = 1024
POOL_W = 1024
KV_W = 128
N_HEADS = ATTN_W // HEAD_DIM
N_KV = N_HEADS // KV_RATIO
IN_COLS = 4352
SHARD = IN_COLS // N_CHIPS
SHARD_P = 1152
EXT = N_CHIPS * SHARD_P
HALF_TILE = SHARD_P - SHARD
OFF_Q, OFF_KA, OFF_KB, OFF_V, OFF_GA, OFF_U, OFF_UA, OFF_UB, OFF_GP = 0, 1024, 1152, 1280, 1408, 2432, 3328, 3456, 3584
WEIGHTS = ("in", "out", "gate", "ple", "pool")


def _cparams(sem=None):
    return pltpu.CompilerParams(dimension_semantics=sem, vmem_limit_bytes=VMEM_LIMIT)


def _pcall(body, **kw):
    return pl.pallas_call(body, **kw)


def _sigmoid(x):
    return 1.0 / (1.0 + jnp.exp(-x))


def _nt(a, b):
    return lax.dot_general(a, b, (((1,), (1,)), ((), ())), preferred_element_type=F32)


def _tn(a, b):
    return lax.dot_general(a, b, (((0,), (0,)), ((), ())), preferred_element_type=F32)


def _nn(a, b):
    return jnp.dot(a, b, preferred_element_type=F32)


def _mesh_pos():
    return lax.axis_index("x"), lax.axis_index("y"), lax.axis_index("c")


def _peer_chips(x, y):
    return [(1 - x, y), (x, 1 - y), (1 - x, 1 - y)]


def _row_tile(rows, cap=256):
    t = min(rows, cap)
    while rows % t or t % 16:
        t -= 1
    return t


def _hbm_spec():
    return pl.BlockSpec(memory_space=pltpu.HBM)


def _halves(a):
    return a.reshape(a.shape[:-2] + (2, a.shape[-2] // 2, a.shape[-1]))


def _unhalves(a):
    return a.reshape(a.shape[:-3] + (2 * a.shape[-2], a.shape[-1]))


class _remote:
    def __init__(self, src, dst, send, recv, i, device):
        self.args = dict(src_ref=src, dst_ref=dst, send_sem=send.at[i], recv_sem=recv.at[i], device_id=device, device_id_type=MESH)

    def start(self):
        pltpu.make_async_remote_copy(**self.args).start()

    def wait_recv(self):
        pltpu.make_async_remote_copy(**self.args).wait_recv()

    def wait_send(self):
        pltpu.make_async_remote_copy(**self.args).wait_send()


class _Job:
    def __init__(self):
        self.keys, self.parts, self.n = [], [], 0

    def add(self, fn, keys, n):
        self.parts.append((fn, len(self.keys), len(keys), self.n))
        self.keys += list(keys)
        self.n += n
        return self

    def build(self, refs, send, recv):
        out = []
        for fn, i0, nb, base in self.parts:
            out += fn(refs[i0:i0 + nb], send, recv, base)
        return out


def _ag_ici(refs, send, recv, base):
    x, y, c = _mesh_pos()
    me = 2 * x + y
    out = []
    for t, g in enumerate(refs):
        for k, chip in enumerate(_peer_chips(x, y)):
            i = base + 3 * t + k
            dev = (*chip, c)
            out.append((_remote(g.at[me, c], g.at[me, c], send, recv, i, dev),
                        _remote(g.at[me, c], g.at[2 * chip[0] + chip[1], c], send, recv, i, dev)))
    return out


def _ag_fwd(refs, send, recv, base):
    x, y, c = _mesh_pos()
    out = []
    for t, g in enumerate(refs):
        for k, chip in enumerate(_peer_chips(x, y)):
            i = base + 3 * t + k
            slot = 2 * chip[0] + chip[1]
            dev = (x, y, 1 - c)
            out.append((_remote(g.at[slot, c], g.at[slot, c], send, recv, i, dev),
                        _remote(g.at[slot, c], g.at[slot, 1 - c], send, recv, i, dev)))
    return out


def _rs_pair(refs, send, recv, base):
    x, y, c = _mesh_pos()
    n = len(refs) // 2
    out = []
    for t in range(n):
        cp = _remote(refs[t].at[:, 1 - c], refs[n + t], send, recv, base + t, (x, y, 1 - c))
        out.append((cp, cp))
    return out


def _rs_ici(refs, send, recv, base, rows=None):
    x, y, c = _mesh_pos()
    n = len(refs) // 2
    rsl = slice(None) if rows is None else pl.ds(rows[0], rows[1])
    out = []
    for t in range(n):
        for k, chip in enumerate(_peer_chips(x, y)):
            cp = _remote(refs[t].at[2 * chip[0] + chip[1], rsl], refs[n + t].at[k, rsl], send, recv, base + 3 * t + k, (*chip, c))
            out.append((cp, cp))
    return out


def _rs_share(refs, send, recv, base, layer):
    x, y, c = _mesh_pos()
    out = []
    for t, f in enumerate(refs):
        dev = (x, y, 1 - c)
        out.append((_remote(f.at[layer, c], f.at[layer, c], send, recv, base + t, dev),
                    _remote(f.at[layer, c], f.at[layer, 1 - c], send, recv, base + t, dev)))
    return out


def _call(body, *, name, grid, in_specs, out_specs, out_shape, args, scratch_shapes=(), sem=None, job=None, store=None):
    in_specs, out_specs, out_shape, scratch_shapes = list(in_specs), list(out_specs), list(out_shape), list(scratch_shapes)
    if job is None or job.n == 0:
        return list(_pcall(body, name=name, grid=grid, in_specs=in_specs, out_specs=out_specs, out_shape=out_shape,
                           scratch_shapes=scratch_shapes, compiler_params=_cparams(sem))(*args))
    bufs = [store[k] for k in job.keys]
    nb, n_in, n_out, n_sc = len(bufs), len(args), len(out_shape), len(scratch_shapes)

    def wrapped(*refs):
        ins = refs[:n_in]
        outs = refs[n_in + nb:n_in + nb + n_out]
        cb = refs[n_in + nb + n_out:n_in + 2 * nb + n_out]
        scratch = refs[n_in + 2 * nb + n_out:n_in + 2 * nb + n_out + n_sc]
        send, recv = refs[-2:]
        ids = [pl.program_id(a) for a in range(len(grid))]
        first = functools.reduce(jnp.logical_and, [i == 0 for i in ids])
        last = functools.reduce(jnp.logical_and, [i == g - 1 for i, g in zip(ids, grid)])

        @pl.when(first)
        def _():
            for s, _r in job.build(cb, send, recv):
                s.start()

        body(*ins, *outs, *scratch)

        @pl.when(last)
        def _():
            pairs = job.build(cb, send, recv)
            for _s, r in pairs:
                r.wait_recv()
            for s, _r in pairs:
                s.wait_send()

    res = _pcall(
        wrapped, name=name, grid=grid, in_specs=in_specs + [_hbm_spec()] * nb, out_specs=out_specs + [_hbm_spec()] * nb,
        out_shape=out_shape + [jax.ShapeDtypeStruct(b.shape, b.dtype) for b in bufs],
        scratch_shapes=scratch_shapes + [pltpu.SemaphoreType.DMA((job.n,)), pltpu.SemaphoreType.DMA((job.n,))],
        input_output_aliases={n_in + i: n_out + i for i in range(nb)},
        compiler_params=_cparams(("arbitrary",) * len(grid)))(*args, *bufs)
    for k, v in zip(job.keys, res[n_out:]):
        store[k] = v
    return list(res[:n_out])


def _run_job(name, job, store):
    bufs = [store[k] for k in job.keys]
    nb = len(bufs)

    def body(*refs):
        send, recv = refs[-2:]
        pairs = job.build(refs[nb:2 * nb], send, recv)
        for s, _r in pairs:
            s.start()
        for _s, r in pairs:
            r.wait_recv()
        for s, _r in pairs:
            s.wait_send()

    res = _pcall(body, name=name, in_specs=[_hbm_spec()] * nb, out_specs=[_hbm_spec()] * nb,
                 out_shape=[jax.ShapeDtypeStruct(b.shape, b.dtype) for b in bufs],
                 scratch_shapes=[pltpu.SemaphoreType.DMA((job.n,)), pltpu.SemaphoreType.DMA((job.n,))],
                 input_output_aliases={i: i for i in range(nb)})(*bufs)
    for k, v in zip(job.keys, res):
        store[k] = v


def _allgather_now(keys, store):
    bufs = [store[k] for k in keys]
    nb = len(bufs)

    def body(*refs):
        send, recv = refs[-2:]
        g = refs[nb:2 * nb]
        ici = _ag_ici(g, send, recv, 0)
        fwd = _ag_fwd(g, send, recv, 3 * nb)
        for s, _r in ici:
            s.start()
        for (_s, r), (fs, _fr) in zip(ici, fwd):
            r.wait_recv()
            fs.start()
        for _fs, fr in fwd:
            fr.wait_recv()
        for s, _r in ici + fwd:
            s.wait_send()

    res = _pcall(body, name="allgather_first_layer", in_specs=[_hbm_spec()] * nb, out_specs=[_hbm_spec()] * nb,
                 out_shape=[jax.ShapeDtypeStruct(b.shape, b.dtype) for b in bufs],
                 scratch_shapes=[pltpu.SemaphoreType.DMA((6 * nb,)), pltpu.SemaphoreType.DMA((6 * nb,))],
                 input_output_aliases={i: i for i in range(nb)})(*bufs)
    for k, v in zip(keys, res):
        store[k] = v


def _piece(w, l, idx, name):
    _, R, C = w.shape
    tr = _row_tile(R)

    def body(s_ref, w_ref, o_ref):
        del s_ref
        o_ref[...] = w_ref[...].astype(BF16)

    gs = pltpu.PrefetchScalarGridSpec(
        num_scalar_prefetch=1, grid=(R // tr,),
        in_specs=[pl.BlockSpec((None, tr, C), lambda r, s: (l, r, 0))],
        out_specs=pl.BlockSpec((None, tr, C), lambda r, s: (s[0], r, 0)))
    return _pcall(body, name=name, grid_spec=gs, out_shape=jax.ShapeDtypeStruct((N_CHIPS, R, C), BF16),
                  compiler_params=_cparams(("parallel",)))(idx, w)


def _piece_in(w_in_t, l, idx):
    _, R, D = w_in_t.shape
    tr = HALF_TILE
    nsrc = R // tr

    def body(s_ref, w_ref, o_ref):
        src = pl.program_id(0) - s_ref[1]
        ok = jnp.logical_and(src >= 0, src < nsrc)
        o_ref[...] = jnp.where(ok, w_ref[...], 0.0).astype(BF16)

    gs = pltpu.PrefetchScalarGridSpec(
        num_scalar_prefetch=1, grid=(SHARD_P // tr,),
        in_specs=[pl.BlockSpec((None, tr, D), lambda r, s: (l, jnp.clip(r - s[1], 0, nsrc - 1), 0))],
        out_specs=pl.BlockSpec((None, tr, D), lambda r, s: (s[0], r, 0)))
    return _pcall(body, name="piece_in", grid_spec=gs, out_shape=jax.ShapeDtypeStruct((N_CHIPS, SHARD_P, D), BF16),
                  compiler_params=_cparams(("parallel",)))(idx, w_in_t)


def _rs_pair_add(g5, r4, idx, name):
    J, _, h, C = g5.shape
    th = _row_tile(h)

    def body(s_ref, g_ref, r_ref, o32_ref, o16_ref):
        s = g_ref[...] + r_ref[...]
        o16_ref[...] = s.astype(BF16)

        @pl.when(pl.program_id(1) == s_ref[0])
        def _():
            o32_ref[...] = s

    spec = pl.BlockSpec((None, th, C), lambda r, j, s: (j, r, 0))
    gs = pltpu.PrefetchScalarGridSpec(
        num_scalar_prefetch=1, grid=(h // th, J),
        in_specs=[pl.BlockSpec((None, None, th, C), lambda r, j, s: (j, s[2], r, 0)), spec],
        out_specs=[pl.BlockSpec((th, C), lambda r, j, s: (r, 0)), spec])
    return _pcall(body, name=name, grid_spec=gs,
                  out_shape=[jax.ShapeDtypeStruct((h, C), F32), jax.ShapeDtypeStruct((J, h, C), BF16)],
                  compiler_params=_cparams(("parallel", "arbitrary")))(idx, g5, r4)


def _rs_chip_add(p32, r3, full, l, idx, name):
    h, C = p32.shape
    th = _row_tile(h)

    def body(s_ref, p_ref, r_ref, f_ref, o_ref):
        del s_ref, f_ref
        o_ref[...] = ((p_ref[...] + r_ref[0].astype(F32)) + r_ref[1].astype(F32)) + r_ref[2].astype(F32)

    gs = pltpu.PrefetchScalarGridSpec(
        num_scalar_prefetch=1, grid=(h // th,),
        in_specs=[pl.BlockSpec((th, C), lambda r, s: (r, 0)),
                  pl.BlockSpec((3, th, C), lambda r, s: (0, r, 0)),
                  pl.BlockSpec(memory_space=pl.ANY)],
        out_specs=pl.BlockSpec((None, None, th, C), lambda r, s: (l, s[2], r, 0)))
    return _pcall(body, name=name, grid_spec=gs, out_shape=jax.ShapeDtypeStruct(full.shape, F32),
                  input_output_aliases={3: 0}, compiler_params=_cparams(("parallel",)))(idx, p32, r3, full)


def _adamw_math(w, g, m, v):
    nm = ADAM_B1 * m + (1.0 - ADAM_B1) * g
    nv = ADAM_B2 * v + (1.0 - ADAM_B2) * (g * g)
    m_hat = nm / (1.0 - ADAM_B1 ** ADAM_STEP)
    v_hat = nv / (1.0 - ADAM_B2 ** ADAM_STEP)
    delta = -ADAM_LR * (m_hat / (jnp.sqrt(v_hat) + ADAM_EPS) + ADAM_WD * w)
    return delta, nm, nv


def _adamw(w, g, m, v, idx, name, tr=None):
    L, R, C = w.shape
    Rg = g.shape[1]
    tr = tr or _row_tile(R)
    shift = (Rg - R) // tr
    assert (Rg - R) % tr == 0

    def body(s_ref, w_ref, g_ref, m_ref, v_ref, go_ref, d_ref, nm_ref, nv_ref):
        del s_ref
        gv = g_ref[...]
        d, nm, nv = _adamw_math(w_ref[...], gv, m_ref[...], v_ref[...])
        go_ref[...] = gv
        d_ref[...] = d
        nm_ref[...] = nm
        nv_ref[...] = nv

    wspec = pl.BlockSpec((None, tr, C), lambda l, r, s: (l, r, 0))
    gs = pltpu.PrefetchScalarGridSpec(
        num_scalar_prefetch=1, grid=(L, R // tr),
        in_specs=[wspec, pl.BlockSpec((None, tr, C), lambda l, r, s: (l, r + shift * s[1], 0)), wspec, wspec],
        out_specs=[wspec, wspec, wspec, wspec])
    sds = jax.ShapeDtypeStruct((L, R, C), F32)
    return _pcall(body, name=name, grid_spec=gs, out_shape=[sds, sds, sds, sds],
                  compiler_params=_cparams(("parallel", "parallel")))(idx, w, g, m, v)


def _small_allreduce_adamw(gv, wv, mv, vv):
    NR = gv.shape[0]

    def body(g_ref, w_ref, m_ref, v_ref, go_ref, d_ref, nm_ref, nv_ref, gath, send, recv):
        x, y, c = _mesh_pos()
        rank = 4 * x + 2 * y + c
        gath[rank] = g_ref[...]
        cps = []
        for msk in range(1, N_DEV):
            bx, by, bc = (msk >> 2) & 1, (msk >> 1) & 1, msk & 1
            peer = (1 - x if bx else x, 1 - y if by else y, 1 - c if bc else c)
            cps.append(pltpu.make_async_remote_copy(src_ref=g_ref, dst_ref=gath.at[rank], send_sem=send.at[msk - 1],
                                                    recv_sem=recv.at[msk - 1], device_id=peer, device_id_type=MESH))
        for cp in cps:
            cp.start()
        for msk in range(1, N_DEV):
            bx, by, bc = (msk >> 2) & 1, (msk >> 1) & 1, msk & 1
            peer = (1 - x if bx else x, 1 - y if by else y, 1 - c if bc else c)
            prank = 4 * peer[0] + 2 * peer[1] + peer[2]
            pltpu.make_async_remote_copy(src_ref=g_ref, dst_ref=gath.at[prank], send_sem=send.at[msk - 1],
                                         recv_sem=recv.at[msk - 1], device_id=peer, device_id_type=MESH).wait_recv()
        for cp in cps:
            cp.wait_send()
        tot = gath[0]
        for r in range(1, N_DEV):
            tot = tot + gath[r]
        d, nm, nv = _adamw_math(w_ref[...], tot, m_ref[...], v_ref[...])
        go_ref[...] = tot
        d_ref[...] = d
        nm_ref[...] = nm
        nv_ref[...] = nv

    vm = pl.BlockSpec(memory_space=pltpu.VMEM)
    sds = jax.ShapeDtypeStruct((NR, LANES), F32)
    return _pcall(body, name="small_allreduce_adamw", in_specs=[vm, vm, vm, vm], out_specs=[vm, vm, vm, vm],
                  out_shape=[sds, sds, sds, sds],
                  scratch_shapes=[pltpu.VMEM((N_DEV, NR, LANES), F32), pltpu.SemaphoreType.DMA((N_DEV - 1,)),
                                  pltpu.SemaphoreType.DMA((N_DEV - 1,))],
                  compiler_params=pltpu.CompilerParams(vmem_limit_bytes=VMEM_LIMIT))(gv, wv, mv, vv)


def _split3(a):
    h1 = a.astype(BF16)
    r1 = a - h1.astype(F32)
    h2 = r1.astype(BF16)
    h3 = (r1 - h2.astype(F32)).astype(BF16)
    return h1, h2, h3


def _folded_dist():
    r = jnp.arange(WINDOW)[:, None]
    j = jnp.arange(WINDOW)[None, :]
    return jnp.where(j > r, r + WINDOW - j, r - j)


def _bucket_onehot():
    d = _folded_dist()
    max_exact = REL_BUCKETS // 2
    d_f = jnp.maximum(d, 1).astype(F32)
    large = max_exact + (jnp.log(d_f / max_exact) / math.log(REL_MAX_DIST / max_exact) * (REL_BUCKETS - max_exact)).astype(I32)
    large = jnp.minimum(large, REL_BUCKETS - 1)
    bucket = jnp.where(d < max_exact, d, large)
    oh = bucket[None] == jnp.arange(REL_BUCKETS)[:, None, None]
    return oh.reshape(REL_BUCKETS, WINDOW * WINDOW).astype(BF16)


def _bias_fwd(rel_bias_t, onehot):
    H, N = rel_bias_t.shape[0], onehot.shape[1]
    tn = 4096

    def body(t_ref, oh_ref, o_ref):
        h1, h2, h3 = _split3(t_ref[...])
        oh = oh_ref[...]
        o_ref[...] = (_nn(h1, oh) + _nn(h2, oh)) + _nn(h3, oh)

    return _pcall(body, name="bias_fwd", grid=(N // tn,),
                  in_specs=[pl.BlockSpec((H, REL_BUCKETS), lambda i: (0, 0)), pl.BlockSpec((REL_BUCKETS, tn), lambda i: (0, i))],
                  out_specs=pl.BlockSpec((H, tn), lambda i: (0, i)), out_shape=jax.ShapeDtypeStruct((H, N), F32),
                  compiler_params=_cparams(("parallel",)))(rel_bias_t, onehot)


def _bias_bwd(dbias, onehot):
    H, N = dbias.shape
    tn = 4096

    def body(d_ref, oh_ref, o_ref):
        @pl.when(pl.program_id(0) == 0)
        def _():
            o_ref[...] = jnp.zeros_like(o_ref)
        h1, h2, h3 = _split3(d_ref[...])
        oh = oh_ref[...]
        o_ref[...] += (_nt(h1, oh) + _nt(h2, oh)) + _nt(h3, oh)

    return _pcall(body, name="bias_bwd", grid=(N // tn,),
                  in_specs=[pl.BlockSpec((H, tn), lambda i: (0, i)), pl.BlockSpec((REL_BUCKETS, tn), lambda i: (0, i))],
                  out_specs=pl.BlockSpec((H, REL_BUCKETS), lambda i: (0, 0)), out_shape=jax.ShapeDtypeStruct((H, REL_BUCKETS), F32),
                  compiler_params=_cparams(("arbitrary",)))(dbias, onehot)


def _to_bf16(x):
    S, D = x.shape
    tm = min(S, 512)

    def body(x_ref, o_ref):
        o_ref[...] = x_ref[...].astype(BF16)

    return _pcall(body, name="to_bf16", grid=(S // tm,), in_specs=[pl.BlockSpec((tm, D), lambda i: (i, 0))],
                  out_specs=pl.BlockSpec((tm, D), lambda i: (i, 0)), out_shape=jax.ShapeDtypeStruct((S, D), BF16),
                  compiler_params=_cparams(("parallel",)))(x)


def _inproj(xb, w_t, b_ext, l, job, store):
    S, D = xb.shape
    tm = min(S, 512)

    def body(x_ref, w_ref, b_ref, q_ref, k_ref, v_ref, ga_ref, u_ref, gp_ref):
        xv = x_ref[...]

        def window(j):
            return _nt(xv, w_ref[j]) + b_ref[:, j * SHARD_P:(j + 1) * SHARD_P]

        acc = window(0)
        q_ref[...] = (acc[:, :1024] * QK_SCALE).astype(BF16)
        k_lo = acc[:, 1024:1152]
        acc = window(1)
        k_ref[...] = k_lo + acc[:, 0:128]
        v_ref[...] = acc[:, 128:256]
        ga_ref[:, 0:896] = acc[:, 256:1152]
        acc = window(2)
        ga_ref[:, 896:1024] = acc[:, 0:128]
        u_ref[:, 0:896] = acc[:, 128:1024]
        u_lo = acc[:, 1024:1152]
        acc = window(3)
        u_ref[:, 896:1024] = u_lo + acc[:, 0:128]
        gp_ref[...] = acc[:, 128:1152]

    def ospec(w):
        return pl.BlockSpec((tm, w), lambda i: (i, 0))

    return _call(
        body, name="inproj", grid=(S // tm,),
        in_specs=[pl.BlockSpec((tm, D), lambda i: (i, 0)),
                  pl.BlockSpec((N_CHIPS, SHARD_P, D), lambda i: (0, 0, 0), pipeline_mode=pl.Buffered(1)),
                  pl.BlockSpec((None, 1, EXT), lambda i: (l, 0, 0))],
        out_specs=[ospec(ATTN_W), ospec(KV_W), ospec(KV_W), ospec(ATTN_W), ospec(POOL_W), ospec(POOL_W)],
        out_shape=[jax.ShapeDtypeStruct((S, ATTN_W), BF16), jax.ShapeDtypeStruct((S, KV_W), F32), jax.ShapeDtypeStruct((S, KV_W), F32),
                   jax.ShapeDtypeStruct((S, ATTN_W), F32), jax.ShapeDtypeStruct((S, POOL_W), F32), jax.ShapeDtypeStruct((S, POOL_W), F32)],
        args=(xb, w_t, b_ext), sem=("parallel",), job=job, store=store)


def _matmul_nn(a, b4, name, job, store):
    S, K = a.shape
    N = b4.shape[2]
    kq = b4.shape[1]
    tm, tn = min(S, 512), min(N, 1024)

    def body(a_ref, b_ref, o_ref):
        acc = _nn(a_ref[:, 0:kq], b_ref[0])
        for j in range(1, N_CHIPS):
            acc = acc + _nn(a_ref[:, j * kq:(j + 1) * kq], b_ref[j])
        o_ref[...] = acc

    return _call(body, name=name, grid=(N // tn, S // tm),
                 in_specs=[pl.BlockSpec((tm, K), lambda n, i: (i, 0)), pl.BlockSpec((N_CHIPS, kq, tn), lambda n, i: (0, 0, n))],
                 out_specs=[pl.BlockSpec((tm, tn), lambda n, i: (i, n))], out_shape=[jax.ShapeDtypeStruct((S, N), F32)],
                 args=(a, b4), sem=("parallel", "parallel"), job=job, store=store)[0]


def _masked_bias(bias):
    r = jnp.arange(WINDOW)[:, None]
    j = jnp.arange(WINDOW)[None, :]
    return jnp.stack([jnp.where(j > r, -1e30, bias), bias])


def _fold(full, tri):
    return jnp.where(tri, full[:, :WINDOW], full[:, WINDOW:])


def _unfold(folded, tri):
    return jnp.concatenate([jnp.where(tri, folded, 0.0).astype(BF16), jnp.where(tri, 0.0, folded).astype(BF16)], axis=1)


def _dup_heads(src_ref, dst_ref):
    a = src_ref[...]
    r = pltpu.roll(a, HEAD_DIM, axis=1)
    lo = lax.broadcasted_iota(I32, a.shape, 1) < HEAD_DIM
    dst_ref[0] = jnp.where(lo, a, r).astype(BF16)
    dst_ref[1] = jnp.where(lo, r, a).astype(BF16)


def _kv_block(ref, h, prev, cur):
    return jnp.concatenate([ref[h, pl.ds(prev, WINDOW), :], ref[h, pl.ds(cur, WINDOW), :]], axis=0)


def _rows(i):
    return slice(i * WINDOW, (i + 1) * WINDOW)


def _attn_fwd(q, k, v, ga, bias, sinks, l, job, store):
    S = q.shape[0]
    nblk = S // WINDOW

    def body(sink_ref, q_ref, k_ref, v_ref, ga_ref, bias_ref, o_ref, ca_ref, lse_ref, kd, vd, qs, ss, ps, os_):
        n = pl.program_id(0)

        @pl.when(n == 0)
        def _():
            _dup_heads(k_ref, kd)
            _dup_heads(v_ref, vd)

        cur = pl.multiple_of(n * WINDOW, WINDOW)
        prev = pl.multiple_of(jnp.maximum(n - 1, 0) * WINDOW, WINDOW)
        lane = lax.broadcasted_iota(I32, (WINDOW, LANES), 1)
        lo = lane < HEAD_DIM
        tri = lane > lax.broadcasted_iota(I32, (WINDOW, LANES), 0)
        lse_mat = jnp.zeros((WINDOW, LANES), F32)
        for g in range(N_KV):
            for i in range(KV_RATIO):
                h = g * KV_RATIO + i
                qp = q_ref[:, LANES * (h // 2):LANES * (h // 2 + 1)]
                qs[_rows(i), :] = jnp.where(lo if h % 2 == 0 else jnp.logical_not(lo), qp, jnp.zeros_like(qp))
            ss[...] = _nt(qs[...], _kv_block(kd, g, prev, cur))
            for i in range(KV_RATIO):
                h = g * KV_RATIO + i
                s = _fold(ss[_rows(i), :], tri) + bias_ref[h]
                sink = sink_ref[l, h]
                m = jnp.maximum(jnp.max(s, axis=1, keepdims=True), sink)
                e = jnp.exp(s - m)
                den = jnp.sum(e, axis=1, keepdims=True) + jnp.exp(sink - m)
                ps[_rows(i), :] = _unfold(e * (1.0 / den), tri)
                lse_mat = jnp.where(lane == h, m + jnp.log(den), lse_mat)
            os_[...] = _nn(ps[...], _kv_block(vd, g, prev, cur))
            for j in range(KV_RATIO // 2):
                pair = g * (KV_RATIO // 2) + j
                sl = slice(LANES * pair, LANES * (pair + 1))
                o_pair = jnp.where(lo, os_[_rows(2 * j), :], os_[_rows(2 * j + 1), :])
                o_ref[:, sl] = o_pair
                gav = ga_ref[:, sl]
                ca_ref[:, sl] = (o_pair * (gav * _sigmoid(gav))).astype(BF16)
        lse_ref[...] = lse_mat

    blk = pl.BlockSpec((WINDOW, ATTN_W), lambda n: (n, 0))
    full_kv = pl.BlockSpec((S, KV_W), lambda n: (0, 0))
    stack = KV_RATIO * WINDOW
    return _call(
        body, name="attn_fwd", grid=(nblk,),
        in_specs=[pl.BlockSpec(memory_space=pltpu.SMEM), blk, full_kv, full_kv, blk,
                  pl.BlockSpec((None, N_HEADS, WINDOW, WINDOW), lambda n: (jnp.minimum(n, 1), 0, 0, 0))],
        out_specs=[blk, blk, pl.BlockSpec((WINDOW, LANES), lambda n: (n, 0))],
        out_shape=[jax.ShapeDtypeStruct((S, ATTN_W), F32), jax.ShapeDtypeStruct((S, ATTN_W), BF16), jax.ShapeDtypeStruct((S, LANES), F32)],
        scratch_shapes=[pltpu.VMEM((N_KV, S, LANES), BF16), pltpu.VMEM((N_KV, S, LANES), BF16),
                        pltpu.VMEM((stack, LANES), BF16), pltpu.VMEM((stack, 2 * WINDOW), F32),
                        pltpu.VMEM((stack, 2 * WINDOW), BF16), pltpu.VMEM((stack, LANES), F32)],
        args=(sinks, q, k, v, ga, bias), sem=("arbitrary",), job=job, store=store)


def _pool_diff(u, halo, tile_index, tm):
    gw = POOL_W // len(POOL_WINDOWS)
    xh = jnp.concatenate([halo, u], axis=0)
    sums = []
    s = xh
    for step in (1, 2, 4, 8):
        s = s + pltpu.roll(s, step, axis=0)
        sums.append(s)
    t = tile_index * tm + lax.broadcasted_iota(I32, (tm, gw), 0)
    diffs = []
    for g, w in enumerate(POOL_WINDOWS):
        cols = slice(g * gw, (g + 1) * gw)
        cnt = jnp.minimum(t + 1, w).astype(F32)
        diffs.append(sums[g][POOL_HALO:, cols] / cnt - u[:, cols])
    return diffs


def _pool_weight(wp_ref, g):
    r = wp_ref.shape[1] // len(POOL_WINDOWS)
    return jnp.concatenate([wp_ref[j, g * r:(g + 1) * r, :] for j in range(N_CHIPS)], axis=0)


def _pool_fwd(u, gp, ca, w_pool, ps, l, job, store):
    S = u.shape[0]
    tm = min(S, 256)
    hb = tm // POOL_HALO
    gw = POOL_W // len(POOL_WINDOWS)

    def body(u_ref, uh_ref, gp_ref, ca_ref, wp_ref, ps_ref, c_ref):
        i = pl.program_id(0)
        uv = u_ref[...]
        halo = jnp.where(i > 0, uh_ref[...], 0.0)
        diffs = _pool_diff(uv, halo, i, tm)
        c_ref[:, 0:ATTN_W] = ca_ref[...]
        for g in range(len(POOL_WINDOWS)):
            cols = slice(g * gw, (g + 1) * gw)
            mm = _nn(diffs[g].astype(BF16), _pool_weight(wp_ref, g))
            gpv = gp_ref[:, cols]
            b = (mm * ps_ref[:, cols]) * (gpv * _sigmoid(gpv))
            c_ref[:, ATTN_W + g * gw:ATTN_W + (g + 1) * gw] = b.astype(BF16)

    row = pl.BlockSpec((tm, POOL_W), lambda i: (i, 0))
    return _call(
        body, name="pool_fwd", grid=(S // tm,),
        in_specs=[row, pl.BlockSpec((POOL_HALO, POOL_W), lambda i: (jnp.maximum(i * hb - 1, 0), 0)), row, row,
                  pl.BlockSpec(w_pool.shape, lambda i: (0, 0, 0)),
                  pl.BlockSpec((None, 1, POOL_W), lambda i: (l, 0, 0))],
        out_specs=[pl.BlockSpec((tm, ATTN_W + POOL_W), lambda i: (i, 0))],
        out_shape=[jax.ShapeDtypeStruct((S, ATTN_W + POOL_W), BF16)],
        args=(u, u, gp, ca, w_pool, ps), sem=("parallel",), job=job, store=store)[0]


def _ple_embed(p_ref, wple_ref):
    pb = p_ref[...].astype(BF16)
    return jnp.concatenate([_nn(pb, wple_ref[j]) for j in range(N_CHIPS)], axis=1)


def _outproj_ln(c, xh_in, gain_in, bias_in, l_in, gl, p, w_out, w_ple, gain, bias, l, alpha, job, store):
    S, D = xh_in.shape
    tm = min(S, 256)
    kq = D // N_CHIPS

    def body(c_ref, x_ref, gi_ref, bi_ref, gl_ref, p_ref, wo_ref, wp_ref, gain_ref, bias_ref, yb_ref, xh_ref, rs_ref):
        mix = _nn(c_ref[:, 0:kq], wo_ref[0])
        for j in range(1, N_CHIPS):
            mix = mix + _nn(c_ref[:, j * kq:(j + 1) * kq], wo_ref[j])
        ple = _sigmoid(gl_ref[...]) * _ple_embed(p_ref, wp_ref)
        x = x_ref[...] * gi_ref[...] + bi_ref[...]
        z = (alpha * x + mix) + ple
        mu = jnp.mean(z, axis=1, keepdims=True)
        zc = z - mu
        var = jnp.mean(zc * zc, axis=1, keepdims=True)
        rstd = lax.rsqrt(var + LN_EPS)
        xhat = zc * rstd
        yb_ref[...] = (xhat * gain_ref[...] + bias_ref[...]).astype(BF16)
        xh_ref[...] = xhat
        rs_ref[...] = rstd

    row = pl.BlockSpec((tm, D), lambda i: (i, 0))
    vec = pl.BlockSpec((None, 1, D), lambda i: (l, 0, 0))
    vec_in = pl.BlockSpec((None, 1, D), lambda i: (l_in, 0, 0))
    return _call(
        body, name="outproj_ln", grid=(S // tm,),
        in_specs=[row, row, vec_in, vec_in, row, pl.BlockSpec((tm, p.shape[1]), lambda i: (i, 0)),
                  pl.BlockSpec(w_out.shape, lambda i: (0, 0, 0)), pl.BlockSpec(w_ple.shape, lambda i: (0, 0, 0)), vec, vec],
        out_specs=[row, row, pl.BlockSpec((tm, 1), lambda i: (i, 0))],
        out_shape=[jax.ShapeDtypeStruct((S, D), BF16), jax.ShapeDtypeStruct((S, D), F32), jax.ShapeDtypeStruct((S, 1), F32)],
        args=(c, xh_in, gain_in, bias_in, gl, p, w_out, w_ple, gain, bias), sem=("parallel",), job=job, store=store)


def _loss_and_grad(xhat, gain, bias, l, target):
    S, D = xhat.shape
    tm = min(S, 512)

    def body(xh_ref, g_ref, b_ref, t_ref, dy_ref, acc_ref):
        @pl.when(pl.program_id(0) == 0)
        def _():
            acc_ref[...] = jnp.zeros_like(acc_ref)
        d = (xh_ref[...] * g_ref[...] + b_ref[...]) - t_ref[...]
        dy_ref[...] = d * (1.0 / D)
        acc_ref[...] += jnp.sum(jnp.mean(d * d, axis=1, keepdims=True), axis=0, keepdims=True)

    row = pl.BlockSpec((tm, D), lambda i: (i, 0))
    vec = pl.BlockSpec((None, 1, D), lambda i: (l, 0, 0))
    return _pcall(body, name="loss", grid=(S // tm,), in_specs=[row, vec, vec, row],
                  out_specs=[row, pl.BlockSpec((8, LANES), lambda i: (0, 0))],
                  out_shape=[jax.ShapeDtypeStruct((S, D), F32), jax.ShapeDtypeStruct((8, LANES), F32)],
                  compiler_params=_cparams(("arbitrary",)))(xhat, gain, bias, target)


def _ln_bwd(dy, xhat, rstd, gl, p, w_ple, w_out, gain, l, job, store):
    S, D = dy.shape
    tm = min(S, 256)

    nq = w_out.shape[1]

    def body(dy_ref, xh_ref, rs_ref, gl_ref, p_ref, wp_ref, wo_ref, gain_ref, dz_ref, dzb_ref, de_ref, dgl_ref, dc_ref, gg_ref, gb_ref):
        @pl.when(pl.program_id(0) == 0)
        def _():
            gg_ref[...] = jnp.zeros_like(gg_ref)
            gb_ref[...] = jnp.zeros_like(gb_ref)
        dyv = dy_ref[...]
        xh = xh_ref[...]
        dxh = dyv * gain_ref[...]
        m1 = jnp.mean(dxh, axis=1, keepdims=True)
        m2 = jnp.mean(dxh * xh, axis=1, keepdims=True)
        dz = rs_ref[...] * ((dxh - m1) - xh * m2)
        gg_ref[...] += jnp.sum(dyv * xh, axis=0, keepdims=True)
        gb_ref[...] += jnp.sum(dyv, axis=0, keepdims=True)
        sg = _sigmoid(gl_ref[...])
        e = _ple_embed(p_ref, wp_ref)
        dzb = dz.astype(BF16)
        dz_ref[...] = dz
        dzb_ref[...] = dzb
        de_ref[...] = (dz * sg).astype(BF16)
        dgl_ref[...] = ((dz * e) * (sg * (1.0 - sg))).astype(BF16)
        for j in range(N_CHIPS):
            dc_ref[:, j * nq:(j + 1) * nq] = _nt(dzb, wo_ref[j])

    row = pl.BlockSpec((tm, D), lambda i: (i, 0))
    vec_in = pl.BlockSpec((None, 1, D), lambda i: (l, 0, 0))
    vec_out = pl.BlockSpec((1, D), lambda i: (0, 0))
    bsd = jax.ShapeDtypeStruct((S, D), BF16)
    fsd = jax.ShapeDtypeStruct((S, D), F32)
    return _call(
        body, name="ln_bwd", grid=(S // tm,),
        in_specs=[row, row, pl.BlockSpec((tm, 1), lambda i: (i, 0)), row, pl.BlockSpec((tm, p.shape[1]), lambda i: (i, 0)),
                  pl.BlockSpec(w_ple.shape, lambda i: (0, 0, 0), pipeline_mode=pl.Buffered(1)),
                  pl.BlockSpec(w_out.shape, lambda i: (0, 0, 0), pipeline_mode=pl.Buffered(1)), vec_in],
        out_specs=[row, row, row, row, row, vec_out, vec_out],
        out_shape=[fsd, bsd, bsd, bsd, fsd, jax.ShapeDtypeStruct((1, D), F32), jax.ShapeDtypeStruct((1, D), F32)],
        args=(dy, xhat, rstd, gl, p, w_ple, w_out, gain), sem=("arbitrary",), job=job, store=store)


def _matmul_nn_acc(a, b4, name, add, add_scale, job, store):
    S = a.shape[0]
    KS, tk, N = b4.shape
    tm, tn = min(S, 512), min(N, 1024)

    def body(a_ref, b_ref, add_ref, o_ref):
        acc = add_scale * add_ref[...]
        for k in range(KS):
            acc = acc + _nn(a_ref[:, k * tk:(k + 1) * tk], b_ref[k])
        o_ref[...] = acc

    return _call(body, name=name, grid=(N // tn, S // tm),
                 in_specs=[pl.BlockSpec((tm, KS * tk), lambda n, i: (i, 0)),
                           pl.BlockSpec((KS, tk, tn), lambda n, i: (0, 0, n)),
                           pl.BlockSpec((tm, tn), lambda n, i: (i, n))],
                 out_specs=[pl.BlockSpec((tm, tn), lambda n, i: (i, n))], out_shape=[jax.ShapeDtypeStruct((S, N), F32)],
                 args=(a, b4, add), sem=("parallel", "parallel"), job=job, store=store)[0]


def _matmul_nt_rows(a, b4, name, add=None, add_scale=1.0, job=None, store=None):
    S, K = a.shape
    nq = b4.shape[1]
    tm = min(S, 512)
    out_spec = pl.BlockSpec((tm, N_CHIPS * nq), lambda i: (i, 0))
    in_specs = [pl.BlockSpec((tm, K), lambda i: (i, 0)), pl.BlockSpec(b4.shape, lambda i: (0, 0, 0))]
    if add is None:
        def body(a_ref, b_ref, o_ref):
            av = a_ref[...]
            for j in range(N_CHIPS):
                o_ref[:, j * nq:(j + 1) * nq] = _nt(av, b_ref[j])
        args = (a, b4)
    else:
        def body(a_ref, b_ref, add_ref, o_ref):
            av = a_ref[...]
            for j in range(N_CHIPS):
                cols = slice(j * nq, (j + 1) * nq)
                o_ref[:, cols] = _nt(av, b_ref[j]) + add_scale * add_ref[:, cols]
        in_specs.append(out_spec)
        args = (a, b4, add)

    return _call(body, name=name, grid=(S // tm,), in_specs=in_specs,
                 out_specs=[out_spec], out_shape=[jax.ShapeDtypeStruct((S, N_CHIPS * nq), F32)],
                 args=args, sem=("parallel",), job=job, store=store)[0]


def _matmul_tn(a, b, R, C, name, by_rows, job=None, store=None):
    S = a.shape[0]
    tn = min(C, 1024 if R <= 512 else 512)
    nt = C // tn

    def body(a_ref, b_ref, o_ref, at_ref):
        @pl.when(pl.program_id(1) == 0)
        def _():
            at_ref[...] = a_ref[...].T
        o_ref[...] = _nn(at_ref[...], b_ref[...])

    if by_rows:
        a_spec = pl.BlockSpec((S, R), lambda j, n: (0, j))
        b_spec = pl.BlockSpec((S, tn), lambda j, n: (0, n))
    else:
        a_spec = pl.BlockSpec((S, R), lambda j, n: (0, 0))
        b_spec = pl.BlockSpec((S, tn), lambda j, n: (0, j * nt + n))
    return _call(body, name=name, grid=(N_CHIPS, nt), in_specs=[a_spec, b_spec],
                 out_specs=[pl.BlockSpec((None, R, tn), lambda j, n: (j, 0, n))],
                 out_shape=[jax.ShapeDtypeStruct((N_CHIPS, R, C), F32)],
                 scratch_shapes=[pltpu.VMEM((R, S), BF16)],
                 args=(a, b), sem=("parallel", "arbitrary"), job=job, store=store)[0]


def _pool_bwd(u, gp, dc, w_pool, ps, l, job=None, store=None):
    S = u.shape[0]
    tm = min(S, 256)
    hb = tm // POOL_HALO
    ngrp = len(POOL_WINDOWS)
    gw = POOL_W // ngrp
    rr = gw // N_CHIPS

    def body(u_ref, uh_ref, gp_ref, dc_ref, wp_ref, ps_ref, dd_ref, dgp_ref, gps_ref, gwp_ref):
        i = pl.program_id(0)

        @pl.when(i == 0)
        def _():
            gps_ref[...] = jnp.zeros_like(gps_ref)
            gwp_ref[...] = jnp.zeros_like(gwp_ref)
        uv = u_ref[...]
        halo = jnp.where(i > 0, uh_ref[...], 0.0)
        diffs = _pool_diff(uv, halo, i, tm)
        for g in range(ngrp):
            cols = slice(g * gw, (g + 1) * gw)
            w = _pool_weight(wp_ref, g)
            db = diffs[g].astype(BF16)
            mm = _nn(db, w)
            gpv = gp_ref[:, cols]
            sg = _sigmoid(gpv)
            si = gpv * sg
            dsi = sg * (1.0 + gpv * (1.0 - sg))
            dcb = dc_ref[:, cols]
            psv = ps_ref[:, cols]
            d_mm = (dcb * si) * psv
            gps_ref[:, cols] += jnp.sum((dcb * si) * mm, axis=0, keepdims=True)
            dgp_ref[:, cols] = (dcb * (mm * psv)) * dsi
            d_mmb = d_mm.astype(BF16)
            dd_ref[:, cols] = _nt(d_mmb, w)
            gwt = _tn(db, d_mmb)
            for j in range(N_CHIPS):
                gwp_ref[j, g * rr:(g + 1) * rr, :] += gwt[j * rr:(j + 1) * rr, :]

    row = pl.BlockSpec((tm, POOL_W), lambda i: (i, 0))
    wspec = pl.BlockSpec(w_pool.shape, lambda i: (0, 0, 0))
    return _call(
        body, name="pool_bwd", grid=(S // tm,),
        in_specs=[row, pl.BlockSpec((POOL_HALO, POOL_W), lambda i: (jnp.maximum(i * hb - 1, 0), 0)), row,
                  pl.BlockSpec((tm, POOL_W), lambda i: (i, 1)), wspec, pl.BlockSpec((None, 1, POOL_W), lambda i: (l, 0, 0))],
        out_specs=[row, row, pl.BlockSpec((1, POOL_W), lambda i: (0, 0)), wspec],
        out_shape=[jax.ShapeDtypeStruct((S, POOL_W), F32), jax.ShapeDtypeStruct((S, POOL_W), F32),
                   jax.ShapeDtypeStruct((1, POOL_W), F32), jax.ShapeDtypeStruct(w_pool.shape, F32)],
        args=(u, u, gp, dc, w_pool, ps), sem=("arbitrary",), job=job, store=store)


def _pool_window_t(dd, halo_next, tile_index, tm):
    gw = POOL_W // len(POOL_WINDOWS)
    n = tm + POOL_HALO
    t = tile_index * tm + lax.broadcasted_iota(I32, (n, gw), 0)
    xh = jnp.concatenate([dd, halo_next], axis=0)
    outs = []
    for g, w in enumerate(POOL_WINDOWS):
        cols = slice(g * gw, (g + 1) * gw)
        cnt = jnp.minimum(t + 1, w).astype(F32)
        s = xh[:, cols] / cnt
        step = 1
        while step < w:
            s = s + pltpu.roll(s, n - step, axis=0)
            step *= 2
        outs.append(s[:tm] - dd[:, cols])
    return outs


def _assemble_dh(dq, dk, dv, dga, dd, dgp):
    S = dq.shape[0]
    tm = min(S, 256)
    hb = tm // POOL_HALO
    nt = S // tm

    def body(dq_ref, dk_ref, dv_ref, dga_ref, dd_ref, ddn_ref, dgp_ref, dh_ref, gb_ref):
        i = pl.program_id(0)

        @pl.when(i == 0)
        def _():
            gb_ref[...] = jnp.zeros_like(gb_ref)
        halo = jnp.where(i < nt - 1, ddn_ref[...], 0.0)
        du = jnp.concatenate(_pool_window_t(dd_ref[...], halo, i, tm), axis=1)
        dkv = dk_ref[...]
        dgav = dga_ref[...]
        parts = [(OFF_Q, dq_ref[...]), (OFF_KA, dkv), (OFF_KB, dkv), (OFF_V, dv_ref[...]), (OFF_GA, dgav),
                 (OFF_U, du[:, 0:896]), (OFF_UA, du[:, 896:1024]), (OFF_UB, du[:, 896:1024]), (OFF_GP, dgp_ref[...])]
        for off, val in parts:
            w = val.shape[1]
            dh_ref[:, off:off + w] = val.astype(BF16)
            gb_ref[:, off:off + w] += jnp.sum(val, axis=0, keepdims=True)

    def row(w):
        return pl.BlockSpec((tm, w), lambda i: (i, 0))

    return _pcall(
        body, name="assemble_dh", grid=(nt,),
        in_specs=[row(ATTN_W), row(KV_W), row(KV_W), row(ATTN_W), row(POOL_W),
                  pl.BlockSpec((POOL_HALO, POOL_W), lambda i: (jnp.minimum((i + 1) * hb, S // POOL_HALO - 1), 0)), row(POOL_W)],
        out_specs=[row(EXT), pl.BlockSpec((1, EXT), lambda i: (0, 0))],
        out_shape=[jax.ShapeDtypeStruct((S, EXT), BF16), jax.ShapeDtypeStruct((1, EXT), F32)],
        compiler_params=_cparams(("arbitrary",)),
    )(dq, dk, dv, dga, dd, dd, dgp)


def _attn_bwd(q, k, v, ga, o, dc, lse, bias, sinks, dbias_in, l, job, store):
    S = q.shape[0]
    nblk = S // WINDOW

    def body(sink_ref, q_ref, k_ref, v_ref, ga_ref, o_ref, dc_ref, lse_ref, bias_ref, dbin_ref,
             dq_ref, dga_ref, dk_ref, dv_ref, db_ref, ds_ref, kd, vd, qs, dos, dls, ss, dps, dss, pss, dqs):
        n = pl.program_id(0)

        @pl.when(n == 0)
        def _():
            _dup_heads(k_ref, kd)
            _dup_heads(v_ref, vd)
            dk_ref[...] = jnp.zeros_like(dk_ref)
            dv_ref[...] = jnp.zeros_like(dv_ref)
            db_ref[...] = dbin_ref[...]
            ds_ref[...] = jnp.zeros_like(ds_ref)

        cur = pl.multiple_of(n * WINDOW, WINDOW)
        prev = pl.multiple_of(jnp.maximum(n - 1, 0) * WINDOW, WINDOW)
        lane = lax.broadcasted_iota(I32, (WINDOW, LANES), 1)
        lane8 = lax.broadcasted_iota(I32, (8, LANES), 1)
        lo = lane < HEAD_DIM
        tri = lane > lax.broadcasted_iota(I32, (WINDOW, LANES), 0)
        lse_t = lse_ref[...]
        dk_t, dv_t = [], []
        dsk = jnp.zeros((8, LANES), F32)
        for g in range(N_KV):
            kb = _kv_block(kd, g, prev, cur)
            vb = _kv_block(vd, g, prev, cur)
            for j in range(KV_RATIO // 2):
                pair = g * (KV_RATIO // 2) + j
                sl = slice(LANES * pair, LANES * (pair + 1))
                qp = q_ref[:, sl]
                op = o_ref[:, sl]
                dcp = dc_ref[:, sl]
                gav = ga_ref[:, sl]
                sg = _sigmoid(gav)
                d_o = dcp * (gav * sg)
                dga_ref[:, sl] = (dcp * op) * (sg * (1.0 + gav * (1.0 - sg)))
                prod = d_o * op
                for par in range(2):
                    msk = lo if par == 0 else jnp.logical_not(lo)
                    rows = _rows(2 * j + par)
                    qs[rows, :] = jnp.where(msk, qp, jnp.zeros_like(qp))
                    dos[rows, :] = jnp.where(msk, d_o, 0.0).astype(BF16)
                    delta = jnp.sum(jnp.where(msk, prod, 0.0), axis=1, keepdims=True)
                    dls[rows, :] = jnp.broadcast_to(delta, (WINDOW, LANES))
            ss[...] = _nt(qs[...], kb)
            dps[...] = _nt(dos[...], vb)
            for i in range(KV_RATIO):
                h = g * KV_RATIO + i
                rows = _rows(i)
                lse_h = jnp.sum(jnp.where(lane == h, lse_t, 0.0), axis=1, keepdims=True)
                p = jnp.exp(_fold(ss[rows, :], tri) + bias_ref[h] - lse_h)
                delta = dls[rows, :]
                dsc = p * (_fold(dps[rows, :], tri) - delta)
                db_ref[h] += dsc
                psink = jnp.exp(sink_ref[l, h] - lse_h)
                dsk = dsk + jnp.where(lane8 == h, -jnp.sum(psink * delta, axis=0, keepdims=True), 0.0)
                dss[rows, :] = _unfold(dsc, tri)
                pss[rows, :] = _unfold(p, tri)
            dqs[...] = _nn(dss[...], kb) * QK_SCALE
            for j in range(KV_RATIO // 2):
                pair = g * (KV_RATIO // 2) + j
                dq_ref[:, LANES * pair:LANES * (pair + 1)] = jnp.where(lo, dqs[_rows(2 * j), :], dqs[_rows(2 * j + 1), :])
            dk_t.append(_tn(qs[...], dss[...]))
            dv_t.append(_tn(dos[...], pss[...]))

        def untranspose(acc):
            return jnp.concatenate([a[:HEAD_DIM] + a[HEAD_DIM:] for a in acc], axis=0).T

        dkb = untranspose(dk_t)
        dvb = untranspose(dv_t)
        dk_ref[pl.ds(prev, WINDOW), :] += dkb[:WINDOW]
        dk_ref[pl.ds(cur, WINDOW), :] += dkb[WINDOW:]
        dv_ref[pl.ds(prev, WINDOW), :] += dvb[:WINDOW]
        dv_ref[pl.ds(cur, WINDOW), :] += dvb[WINDOW:]
        ds_ref[...] += dsk

    blk = pl.BlockSpec((WINDOW, ATTN_W), lambda n: (n, 0))
    full_kv = pl.BlockSpec((S, KV_W), lambda n: (0, 0))
    full_b = pl.BlockSpec((N_HEADS, WINDOW, WINDOW), lambda n: (0, 0, 0))
    stack = KV_RATIO * WINDOW
    return _call(
        body, name="attn_bwd", grid=(nblk,),
        in_specs=[pl.BlockSpec(memory_space=pltpu.SMEM), blk, full_kv, full_kv, blk, blk, blk,
                  pl.BlockSpec((WINDOW, LANES), lambda n: (n, 0)),
                  pl.BlockSpec((None, N_HEADS, WINDOW, WINDOW), lambda n: (jnp.minimum(n, 1), 0, 0, 0)), full_b],
        out_specs=[blk, blk, full_kv, full_kv, full_b, pl.BlockSpec((8, LANES), lambda n: (0, 0))],
        out_shape=[jax.ShapeDtypeStruct((S, ATTN_W), F32), jax.ShapeDtypeStruct((S, ATTN_W), F32), jax.ShapeDtypeStruct((S, KV_W), F32),
                   jax.ShapeDtypeStruct((S, KV_W), F32), jax.ShapeDtypeStruct((N_HEADS, WINDOW, WINDOW), F32),
                   jax.ShapeDtypeStruct((8, LANES), F32)],
        scratch_shapes=[pltpu.VMEM((N_KV, S, LANES), BF16), pltpu.VMEM((N_KV, S, LANES), BF16),
                        pltpu.VMEM((stack, LANES), BF16), pltpu.VMEM((stack, LANES), BF16), pltpu.VMEM((stack, LANES), F32),
                        pltpu.VMEM((stack, 2 * WINDOW), F32), pltpu.VMEM((stack, 2 * WINDOW), F32),
                        pltpu.VMEM((stack, 2 * WINDOW), BF16), pltpu.VMEM((stack, 2 * WINDOW), BF16),
                        pltpu.VMEM((stack, LANES), F32)],
        args=(sinks, q, k, v, ga, o, dc, lse, bias, dbias_in), sem=("arbitrary",), job=job, store=store)


def _pack_small(arrs):
    flat = []
    for a in arrs:
        v = a.reshape(-1)
        flat.append(jnp.pad(v, (0, (-v.shape[0]) % LANES)))
    v = jnp.concatenate(flat)
    v = jnp.pad(v, (0, (-v.shape[0]) % (8 * LANES)))
    return v.reshape(-1, LANES)


def _unpack_small(packed, shapes):
    v = packed.reshape(-1)
    outs, off = [], 0
    for shp in shapes:
        n = math.prod(shp)
        outs.append(v[off:off + n].reshape(shp))
        off += n + (-n) % LANES
    return outs


def _bias_to_ext(b):
    L = b.shape[0]
    z = jnp.zeros((L, HALF_TILE), b.dtype)
    parts = []
    for j in range(N_CHIPS):
        seg = b[:, j * SHARD:(j + 1) * SHARD]
        parts += [z, seg] if j % 2 else [seg, z]
    return jnp.concatenate(parts, axis=1).reshape(L, 1, EXT)


def _bias_from_ext(g):
    parts = []
    for j in range(N_CHIPS):
        o = j * SHARD_P + (HALF_TILE if j % 2 else 0)
        parts.append(g[:, o:o + SHARD])
    return jnp.concatenate(parts, axis=1)


def kernel(x, p, w_in, b_in, w_out, attn_sinks, rel_bias, w_pool, pool_scale, w_ple, w_gate_ple, ln_gain, ln_bias, loss_target, m_w_in, m_b_in, m_w_out, m_attn_sinks, m_rel_bias, m_w_pool, m_pool_scale, m_w_ple, m_w_gate_ple, m_ln_gain, m_ln_bias, v_w_in, v_b_in, v_w_out, v_attn_sinks, v_rel_bias, v_w_pool, v_pool_scale, v_w_ple, v_w_gate_ple, v_ln_gain, v_ln_bias):
    L = w_in.shape[0]
    S, D = x.shape[1], x.shape[2]
    assert D == D_MODEL and w_in.shape[2] == SHARD and S % WINDOW == 0
    alpha = (2.0 * L) ** 0.25
    xc, yc, cc = _mesh_pos()
    idx = jnp.stack([2 * xc + yc, yc, cc]).astype(I32)
    store = {}

    def wkeys(l, names):
        return [("w", l, t) for t in names]

    def weight(l, t):
        return _unhalves(store["w", l, t])

    w_in_t = jnp.swapaxes(w_in, 1, 2)
    w_pool2 = w_pool.reshape(L, w_pool.shape[1] * w_pool.shape[2], w_pool.shape[3])
    for l in range(L):
        store["w", l, "in"] = _halves(_piece_in(w_in_t, l, idx))
        store["w", l, "out"] = _halves(_piece(w_out, l, idx, "piece_out"))
        store["w", l, "gate"] = _halves(_piece(w_gate_ple, l, idx, "piece_gate"))
        store["w", l, "ple"] = _halves(_piece(w_ple, l, idx, "piece_ple"))
        store["w", l, "pool"] = _halves(_piece(w_pool2, l, idx, "piece_pool"))
    _allgather_now(wkeys(0, WEIGHTS), store)

    b_ext = _bias_to_ext(b_in)
    ps3 = pool_scale.reshape(L, 1, POOL_W)
    gain3 = ln_gain.reshape(L, 1, D)
    bias3 = ln_bias.reshape(L, 1, D)
    onehot = _bucket_onehot()
    bias_hqk = _masked_bias(_bias_fwd(rel_bias.T, onehot).reshape(N_HEADS, WINDOW, WINDOW))

    xs = x[0]
    xb = _to_bf16(xs)
    pb = _to_bf16(p.reshape(L * S, p.shape[3])).reshape(L, S, p.shape[3])
    saved = []
    x_norm, x_gain, x_bias, x_l = xs, jnp.ones((1, 1, D), F32), jnp.zeros((1, 1, D), F32), 0
    late = ["out", "ple", "pool"]
    for l in range(L):
        nxt = l + 1 < L
        job = _Job()
        if nxt:
            job.add(_ag_ici, wkeys(l + 1, ["in"]), 3)
        if l >= 1:
            job.add(_ag_fwd, wkeys(l, ["gate"]), 3)
        q, k, v, ga, u, gp = _inproj(xb, weight(l, "in"), b_ext, l, job, store)
        gl = _matmul_nn(xb, weight(l, "gate"), "gate_logits", None, None)
        job = _Job().add(_ag_fwd, wkeys(l + 1, ["in"]), 3).add(_ag_ici, wkeys(l + 1, late), 9) if nxt else None
        o, ca, lse = _attn_fwd(q, k, v, ga, bias_hqk, attn_sinks, l, job, store)
        c = _pool_fwd(u, gp, ca, weight(l, "pool"), ps3, l, None, None)
        job = _Job().add(_ag_fwd, wkeys(l + 1, late), 9).add(_ag_ici, wkeys(l + 1, ["gate"]), 3) if nxt else None
        yb, xhat, rstd = _outproj_ln(c, x_norm, x_gain, x_bias, x_l, gl, pb[l], weight(l, "out"), weight(l, "ple"), gain3, bias3, l,
                                     alpha, job, store)
        saved.append(dict(xb=xb, q=q, k=k, v=v, ga=ga, u=u, gp=gp, gl=gl, o=o, lse=lse, c=c, xhat=xhat, rstd=rstd))
        x_norm, x_gain, x_bias, x_l, xb = xhat, gain3, bias3, l, yb

    dy, loss_acc = _loss_and_grad(x_norm, gain3, bias3, L - 1, loss_target[0])
    loss = lax.psum(0.5 * loss_acc[0, 0], ("x", "y", "c"))

    shapes = {t: store["w", 0, t].shape for t in WEIGHTS}
    for t in WEIGHTS:
        store["full", t] = lax.empty((L,) + shapes[t][1:], F32)

    def rs_keys(kind, l, names):
        return [(kind, l, t) for t in names]

    def rs_pair_job(l, names):
        for t in names:
            store["ra", l, t] = lax.empty((N_CHIPS,) + shapes[t][2:], F32)
        return _Job().add(_rs_pair, rs_keys("g", l, names) + rs_keys("ra", l, names), len(names))

    def rs_pair_add(l, names):
        for t in names:
            p32, p16 = _rs_pair_add(store["g", l, t], store["ra", l, t], idx, "rs_pair_add_" + t)
            store["p32", l, t], store["p16", l, t] = p32, p16
            store["rb", l, t] = lax.empty((3,) + shapes[t][2:], BF16)

    def rs_ici_job(l, names):
        return _Job().add(_rs_ici, rs_keys("p16", l, names) + rs_keys("rb", l, names), 3 * len(names))

    def rs_chip_add(l, names):
        for t in names:
            store["full", t] = _rs_chip_add(store["p32", l, t], store["rb", l, t], store["full", t], l, idx, "rs_chip_add_" + t)

    def rs_share_job(l):
        return _Job().add(functools.partial(_rs_share, layer=l), [("full", t) for t in WEIGHTS], len(WEIGHTS))

    dbias = jnp.zeros((N_HEADS, WINDOW, WINDOW), F32)
    small = [None] * L
    h_in = shapes["in"][2]
    head_rows = (h_in * 5 // 8) // 16 * 16
    early = ["out", "gate", "ple"]

    def in_ici_job(l, rows):
        return _Job().add(functools.partial(_rs_ici, rows=rows), rs_keys("p16", l, ["in"]) + rs_keys("rb", l, ["in"]), 3)

    for l in reversed(range(L)):
        sv = saved[l]
        pl_l = pb[l]
        job = in_ici_job(l + 1, (head_rows, h_in - head_rows)) if l + 1 < L else None
        dz, dzb, d_e, d_gl, dc, ggain, gbias = _ln_bwd(dy, sv["xhat"], sv["rstd"], sv["gl"], pl_l, weight(l, "ple"), weight(l, "out"),
                                                       gain3, l, job, store)
        if l + 1 < L:
            rs_chip_add(l + 1, ["in"])
        g_out = _matmul_tn(sv["c"], dzb, D // N_CHIPS, D, "grad_w_out", by_rows=True,
                           job=rs_share_job(l + 1) if l + 1 < L else None, store=store)
        g_ple = _matmul_tn(pl_l, d_e, pl_l.shape[1], D // N_CHIPS, "grad_w_ple", by_rows=False)
        g_gate = _matmul_tn(sv["xb"], d_gl, D // N_CHIPS, D, "grad_w_gate", by_rows=True)
        for t, g in zip(early, (g_out, g_gate, g_ple)):
            store["g", l, t] = _halves(g)
        dd, dgp, gps, g_pool = _pool_bwd(sv["u"], sv["gp"], dc, weight(l, "pool"), ps3, l, rs_pair_job(l, early), store)
        store["g", l, "pool"] = _halves(g_pool)
        rs_pair_add(l, early)
        job = rs_pair_job(l, ["pool"])
        job.add(_rs_ici, rs_keys("p16", l, ["out"]) + rs_keys("rb", l, ["out"]), 3)
        dq, dga, dk, dv, dbias, dsink = _attn_bwd(sv["q"], sv["k"], sv["v"], sv["ga"], sv["o"], dc, sv["lse"], bias_hqk,
                                                  attn_sinks, dbias, l, job, store)
        rs_pair_add(l, ["pool"])
        dh, gbe = _assemble_dh(dq, dk, dv, dga, dd, dgp)
        g_in = _matmul_tn(dh, sv["xb"], SHARD_P, D, "grad_w_in", by_rows=True, job=rs_ici_job(l, ["gate", "ple", "pool"]), store=store)
        rs_chip_add(l, ["out", "gate", "ple", "pool"])
        store["g", l, "in"] = _halves(g_in)
        dx1 = _matmul_nt_rows(d_gl, weight(l, "gate"), "d_x_gate", dz, alpha, job=rs_pair_job(l, ["in"]), store=store)
        rs_pair_add(l, ["in"])
        dy = _matmul_nn_acc(dh, weight(l, "in"), "d_x", dx1, 1.0, in_ici_job(l, (0, head_rows) if l > 0 else (0, h_in)), store)
        small[l] = dict(b_in=_bias_from_ext(gbe)[0], sinks=dsink[0, :N_HEADS], ps=gps[0], gain=ggain[0], bias=gbias[0])
    rs_chip_add(0, ["in"])
    grad_x = dy[None]

    _run_job("rs_pair_share", rs_share_job(0), store)
    full = {t: _unhalves(store["full", t]) for t in WEIGHTS}

    def t_back(a):
        return jnp.swapaxes(a, 1, 2)

    def pool4(a):
        return a.reshape(w_pool.shape)

    r_in = _adamw(w_in_t, full["in"], jnp.swapaxes(m_w_in, 1, 2), jnp.swapaxes(v_w_in, 1, 2), idx, "adamw_w_in", tr=HALF_TILE)
    gw_in, dw_in, nm_in, nv_in = [t_back(a) for a in r_in]
    gw_out, dw_out, nm_out, nv_out = _adamw(w_out, full["out"], m_w_out, v_w_out, idx, "adamw_w_out")
    gw_gate, dw_gate, nm_gate, nv_gate = _adamw(w_gate_ple, full["gate"], m_w_gate_ple, v_w_gate_ple, idx, "adamw_w_gate")
    gw_ple, dw_ple, nm_ple, nv_ple = _adamw(w_ple, full["ple"], m_w_ple, v_w_ple, idx, "adamw_w_ple")
    r_pool = _adamw(w_pool2, full["pool"], m_w_pool.reshape(w_pool2.shape), v_w_pool.reshape(w_pool2.shape), idx, "adamw_w_pool")
    gw_pool, dw_pool, nm_pool, nv_pool = [pool4(a) for a in r_pool]

    g_rel = _bias_bwd(dbias.reshape(N_HEADS, -1), onehot).T
    small_shapes = [b_in.shape, attn_sinks.shape, rel_bias.shape, pool_scale.shape, ln_gain.shape, ln_bias.shape]
    g_small = [jnp.stack([small[l]["b_in"] for l in range(L)]), jnp.stack([small[l]["sinks"] for l in range(L)]), g_rel,
               jnp.stack([small[l]["ps"] for l in range(L)]), jnp.stack([small[l]["gain"] for l in range(L)]),
               jnp.stack([small[l]["bias"] for l in range(L)])]
    packed = _small_allreduce_adamw(
        _pack_small(g_small),
        _pack_small([b_in, attn_sinks, rel_bias, pool_scale, ln_gain, ln_bias]),
        _pack_small([m_b_in, m_attn_sinks, m_rel_bias, m_pool_scale, m_ln_gain, m_ln_bias]),
        _pack_small([v_b_in, v_attn_sinks, v_rel_bias, v_pool_scale, v_ln_gain, v_ln_bias]))
    sg, sd, sm, sv_ = [_unpack_small(a, small_shapes) for a in packed]

    def order(big, sm_):
        return (big[0], sm_[0], big[1], sm_[1], sm_[2], big[2], sm_[3], big[3], big[4], sm_[4], sm_[5])

    return (loss, grad_x,
            *order((gw_in, gw_out, gw_pool, gw_ple, gw_gate), sg),
            *order((dw_in, dw_out, dw_pool, dw_ple, dw_gate), sd),
            *order((nm_in, nm_out, nm_pool, nm_ple, nm_gate), sm),
            *order((nv_in, nv_out, nv_pool, nv_ple, nv_gate), sv_))
```

```python
import functools
import math

import jax
import jax.numpy as jnp
from jax import lax
from jax.experimental import pallas as pl
from jax.experimental.pallas import tpu as pltpu

F32 = jnp.float32
BF16 = jnp.bfloat16
ACT = jnp.bfloat16
I32 = jnp.int32
MESH = pl.DeviceIdType.MESH

HEAD_DIM = 64
QK_SCALE = HEAD_DIM ** -0.5
WINDOW = 128
KV_RATIO = 8
POOL_WINDOWS = (2, 4, 8, 16)
POOL_HALO = 16
REL_BUCKETS = 32
REL_MAX_DIST = 128
LN_EPS = 1e-5
ADAM_LR, ADAM_B1, ADAM_B2, ADAM_EPS, ADAM_WD, ADAM_STEP = 0.001, 0.9, 0.999, 1e-08, 0.01, 10

LANES = 128
VMEM_LIMIT = 52 * 1024 * 1024
N_CHIPS = 4
N_DEV = 8

D_MODEL = 2048
ATTN_W = 1024
POOL_W = 1024
KV_W = 128
N_HEADS = ATTN_W // HEAD_DIM
N_KV = N_HEADS // KV_RATIO
IN_COLS = 4352
SHARD = IN_COLS // N_CHIPS
SHARD_P = 1152
EXT = N_CHIPS * SHARD_P
HALF_TILE = SHARD_P - SHARD
OFF_Q, OFF_KA, OFF_KB, OFF_V, OFF_GA, OFF_U, OFF_UA, OFF_UB, OFF_GP = 0, 1024, 1152, 1280, 1408, 2432, 3328, 3456, 3584
WEIGHTS = ("in", "out", "gate", "ple", "pool")


def _cparams(sem=None):
    return pltpu.CompilerParams(dimension_semantics=sem, vmem_limit_bytes=VMEM_LIMIT)


def _pcall(body, **kw):
    return pl.pallas_call(body, **kw)


def _sigmoid(x):
    return 1.0 / (1.0 + jnp.exp(-x))


def _nt(a, b):
    return lax.dot_general(a, b, (((1,), (1,)), ((), ())), preferred_element_type=F32)


def _tn(a, b):
    return lax.dot_general(a, b, (((0,), (0,)), ((), ())), preferred_element_type=F32)


def _nn(a, b):
    return jnp.dot(a, b, preferred_element_type=F32)


def _mesh_pos():
    return lax.axis_index("x"), lax.axis_index("y"), lax.axis_index("c")


def _peer_chips(x, y):
    return [(1 - x, y), (x, 1 - y), (1 - x, 1 - y)]


def _row_tile(rows, cap=256):
    t = min(rows, cap)
    while rows % t or t % 16:
        t -= 1
    return t


def _hbm_spec():
    return pl.BlockSpec(memory_space=pltpu.HBM)


def _halves(a):
    return a.reshape(a.shape[:-2] + (2, a.shape[-2] // 2, a.shape[-1]))


def _unhalves(a):
    return a.reshape(a.shape[:-3] + (2 * a.shape[-2], a.shape[-1]))


class _remote:
    def __init__(self, src, dst, send, recv, i, device):
        self.args = dict(src_ref=src, dst_ref=dst, send_sem=send.at[i], recv_sem=recv.at[i], device_id=device, device_id_type=MESH)

    def start(self):
        pltpu.make_async_remote_copy(**self.args).start()

    def wait_recv(self):
        pltpu.make_async_remote_copy(**self.args).wait_recv()

    def wait_send(self):
        pltpu.make_async_remote_copy(**self.args).wait_send()


class _Job:
    def __init__(self):
        self.keys, self.parts, self.n = [], [], 0

    def add(self, fn, keys, n):
        self.parts.append((fn, len(self.keys), len(keys), self.n))
        self.keys += list(keys)
        self.n += n
        return self

    def build(self, refs, send, recv):
        out = []
        for fn, i0, nb, base in self.parts:
            out += fn(refs[i0:i0 + nb], send, recv, base)
        return out


def _ag_ici(refs, send, recv, base):
    x, y, c = _mesh_pos()
    me = 2 * x + y
    out = []
    for t, g in enumerate(refs):
        for k, chip in enumerate(_peer_chips(x, y)):
            i = base + 3 * t + k
            dev = (*chip, c)
            out.append((_remote(g.at[me, c], g.at[me, c], send, recv, i, dev),
                        _remote(g.at[me, c], g.at[2 * chip[0] + chip[1], c], send, recv, i, dev)))
    return out


def _ag_fwd(refs, send, recv, base):
    x, y, c = _mesh_pos()
    out = []
    for t, g in enumerate(refs):
        for k, chip in enumerate(_peer_chips(x, y)):
            i = base + 3 * t + k
            slot = 2 * chip[0] + chip[1]
            dev = (x, y, 1 - c)
            out.append((_remote(g.at[slot, c], g.at[slot, c], send, recv, i, dev),
                        _remote(g.at[slot, c], g.at[slot, 1 - c], send, recv, i, dev)))
    return out


def _rs_pair(refs, send, recv, base):
    x, y, c = _mesh_pos()
    n = len(refs) // 2
    out = []
    for t in range(n):
        cp = _remote(refs[t].at[:, 1 - c], refs[n + t], send, recv, base + t, (x, y, 1 - c))
        out.append((cp, cp))
    return out


def _rs_ici(refs, send, recv, base, rows=None):
    x, y, c = _mesh_pos()
    n = len(refs) // 2
    rsl = slice(None) if rows is None else pl.ds(rows[0], rows[1])
    out = []
    for t in range(n):
        for k, chip in enumerate(_peer_chips(x, y)):
            cp = _remote(refs[t].at[2 * chip[0] + chip[1], rsl], refs[n + t].at[k, rsl], send, recv, base + 3 * t + k, (*chip, c))
            out.append((cp, cp))
    return out


def _rs_share(refs, send, recv, base, layer):
    x, y, c = _mesh_pos()
    out = []
    for t, f in enumerate(refs):
        dev = (x, y, 1 - c)
        out.append((_remote(f.at[layer, c], f.at[layer, c], send, recv, base + t, dev),
                    _remote(f.at[layer, c], f.at[layer, 1 - c], send, recv, base + t, dev)))
    return out


def _call(body, *, name, grid, in_specs, out_specs, out_shape, args, scratch_shapes=(), sem=None, job=None, store=None):
    in_specs, out_specs, out_shape, scratch_shapes = list(in_specs), list(out_specs), list(out_shape), list(scratch_shapes)
    if job is None or job.n == 0:
        return list(_pcall(body, name=name, grid=grid, in_specs=in_specs, out_specs=out_specs, out_shape=out_shape,
                           scratch_shapes=scratch_shapes, compiler_params=_cparams(sem))(*args))
    bufs = [store[k] for k in job.keys]
    nb, n_in, n_out, n_sc = len(bufs), len(args), len(out_shape), len(scratch_shapes)

    def wrapped(*refs):
        ins = refs[:n_in]
        outs = refs[n_in + nb:n_in + nb + n_out]
        cb = refs[n_in + nb + n_out:n_in + 2 * nb + n_out]
        scratch = refs[n_in + 2 * nb + n_out:n_in + 2 * nb + n_out + n_sc]
        send, recv = refs[-2:]
        ids = [pl.program_id(a) for a in range(len(grid))]
        first = functools.reduce(jnp.logical_and, [i == 0 for i in ids])
        last = functools.reduce(jnp.logical_and, [i == g - 1 for i, g in zip(ids, grid)])

        @pl.when(first)
        def _():
            for s, _r in job.build(cb, send, recv):
                s.start()

        body(*ins, *outs, *scratch)

        @pl.when(last)
        def _():
            pairs = job.build(cb, send, recv)
            for _s, r in pairs:
                r.wait_recv()
            for s, _r in pairs:
                s.wait_send()

    res = _pcall(
        wrapped, name=name, grid=grid, in_specs=in_specs + [_hbm_spec()] * nb, out_specs=out_specs + [_hbm_spec()] * nb,
        out_shape=out_shape + [jax.ShapeDtypeStruct(b.shape, b.dtype) for b in bufs],
        scratch_shapes=scratch_shapes + [pltpu.SemaphoreType.DMA((job.n,)), pltpu.SemaphoreType.DMA((job.n,))],
        input_output_aliases={n_in + i: n_out + i for i in range(nb)},
        compiler_params=_cparams(("arbitrary",) * len(grid)))(*args, *bufs)
    for k, v in zip(job.keys, res[n_out:]):
        store[k] = v
    return list(res[:n_out])


def _run_job(name, job, store):
    bufs = [store[k] for k in job.keys]
    nb = len(bufs)

    def body(*refs):
        send, recv = refs[-2:]
        pairs = job.build(refs[nb:2 * nb], send, recv)
        for s, _r in pairs:
            s.start()
        for _s, r in pairs:
            r.wait_recv()
        for s, _r in pairs:
            s.wait_send()

    res = _pcall(body, name=name, in_specs=[_hbm_spec()] * nb, out_specs=[_hbm_spec()] * nb,
                 out_shape=[jax.ShapeDtypeStruct(b.shape, b.dtype) for b in bufs],
                 scratch_shapes=[pltpu.SemaphoreType.DMA((job.n,)), pltpu.SemaphoreType.DMA((job.n,))],
                 input_output_aliases={i: i for i in range(nb)})(*bufs)
    for k, v in zip(job.keys, res):
        store[k] = v


def _allgather_now(keys, store):
    bufs = [store[k] for k in keys]
    nb = len(bufs)

    def body(*refs):
        send, recv = refs[-2:]
        g = refs[nb:2 * nb]
        ici = _ag_ici(g, send, recv, 0)
        fwd = _ag_fwd(g, send, recv, 3 * nb)
        for s, _r in ici:
            s.start()
        for (_s, r), (fs, _fr) in zip(ici, fwd):
            r.wait_recv()
            fs.start()
        for _fs, fr in fwd:
            fr.wait_recv()
        for s, _r in ici + fwd:
            s.wait_send()

    res = _pcall(body, name="allgather_first_layer", in_specs=[_hbm_spec()] * nb, out_specs=[_hbm_spec()] * nb,
                 out_shape=[jax.ShapeDtypeStruct(b.shape, b.dtype) for b in bufs],
                 scratch_shapes=[pltpu.SemaphoreType.DMA((6 * nb,)), pltpu.SemaphoreType.DMA((6 * nb,))],
                 input_output_aliases={i: i for i in range(nb)})(*bufs)
    for k, v in zip(keys, res):
        store[k] = v


def _piece(w, l, idx, name):
    _, R, C = w.shape
    tr = _row_tile(R)

    def body(s_ref, w_ref, o_ref):
        del s_ref
        o_ref[...] = w_ref[...].astype(BF16)

    gs = pltpu.PrefetchScalarGridSpec(
        num_scalar_prefetch=1, grid=(R // tr,),
        in_specs=[pl.BlockSpec((None, tr, C), lambda r, s: (l, r, 0))],
        out_specs=pl.BlockSpec((None, tr, C), lambda r, s: (s[0], r, 0)))
    return _pcall(body, name=name, grid_spec=gs, out_shape=jax.ShapeDtypeStruct((N_CHIPS, R, C), BF16),
                  compiler_params=_cparams(("parallel",)))(idx, w)


def _piece_in(w_in_t, l, idx):
    _, R, D = w_in_t.shape
    tr = HALF_TILE
    nsrc = R // tr

    def body(s_ref, w_ref, o_ref):
        src = pl.program_id(0) - s_ref[1]
        ok = jnp.logical_and(src >= 0, src < nsrc)
        o_ref[...] = jnp.where(ok, w_ref[...], 0.0).astype(BF16)

    gs = pltpu.PrefetchScalarGridSpec(
        num_scalar_prefetch=1, grid=(SHARD_P // tr,),
        in_specs=[pl.BlockSpec((None, tr, D), lambda r, s: (l, jnp.clip(r - s[1], 0, nsrc - 1), 0))],
        out_specs=pl.BlockSpec((None, tr, D), lambda r, s: (s[0], r, 0)))
    return _pcall(body, name="piece_in", grid_spec=gs, out_shape=jax.ShapeDtypeStruct((N_CHIPS, SHARD_P, D), BF16),
                  compiler_params=_cparams(("parallel",)))(idx, w_in_t)


def _rs_pair_add(g5, r4, idx, name):
    J, _, h, C = g5.shape
    th = _row_tile(h)

    def body(s_ref, g_ref, r_ref, o32_ref, o16_ref):
        s = g_ref[...] + r_ref[...]
        o16_ref[...] = s.astype(BF16)

        @pl.when(pl.program_id(1) == s_ref[0])
        def _():
            o32_ref[...] = s

    spec = pl.BlockSpec((None, th, C), lambda r, j, s: (j, r, 0))
    gs = pltpu.PrefetchScalarGridSpec(
        num_scalar_prefetch=1, grid=(h // th, J),
        in_specs=[pl.BlockSpec((None, None, th, C), lambda r, j, s: (j, s[2], r, 0)), spec],
        out_specs=[pl.BlockSpec((th, C), lambda r, j, s: (r, 0)), spec])
    return _pcall(body, name=name, grid_spec=gs,
                  out_shape=[jax.ShapeDtypeStruct((h, C), F32), jax.ShapeDtypeStruct((J, h, C), BF16)],
                  compiler_params=_cparams(("parallel", "arbitrary")))(idx, g5, r4)


def _rs_chip_add(p32, r3, full, l, idx, name):
    h, C = p32.shape
    th = _row_tile(h)

    def body(s_ref, p_ref, r_ref, f_ref, o_ref):
        del s_ref, f_ref
        o_ref[...] = ((p_ref[...] + r_ref[0].astype(F32)) + r_ref[1].astype(F32)) + r_ref[2].astype(F32)

    gs = pltpu.PrefetchScalarGridSpec(
        num_scalar_prefetch=1, grid=(h // th,),
        in_specs=[pl.BlockSpec((th, C), lambda r, s: (r, 0)),
                  pl.BlockSpec((3, th, C), lambda r, s: (0, r, 0)),
                  pl.BlockSpec(memory_space=pl.ANY)],
        out_specs=pl.BlockSpec((None, None, th, C), lambda r, s: (l, s[2], r, 0)))
    return _pcall(body, name=name, grid_spec=gs, out_shape=jax.ShapeDtypeStruct(full.shape, F32),
                  input_output_aliases={3: 0}, compiler_params=_cparams(("parallel",)))(idx, p32, r3, full)


def _adamw_math(w, g, m, v):
    nm = ADAM_B1 * m + (1.0 - ADAM_B1) * g
    nv = ADAM_B2 * v + (1.0 - ADAM_B2) * (g * g)
    m_hat = nm / (1.0 - ADAM_B1 ** ADAM_STEP)
    v_hat = nv / (1.0 - ADAM_B2 ** ADAM_STEP)
    delta = -ADAM_LR * (m_hat / (jnp.sqrt(v_hat) + ADAM_EPS) + ADAM_WD * w)
    return delta, nm, nv


def _adamw(w, g, m, v, idx, name, tr=None):
    L, R, C = w.shape
    Rg = g.shape[1]
    tr = tr or _row_tile(R)
    shift = (Rg - R) // tr
    assert (Rg - R) % tr == 0

    def body(s_ref, w_ref, g_ref, m_ref, v_ref, go_ref, d_ref, nm_ref, nv_ref):
        del s_ref
        gv = g_ref[...]
        d, nm, nv = _adamw_math(w_ref[...], gv, m_ref[...], v_ref[...])
        go_ref[...] = gv
        d_ref[...] = d
        nm_ref[...] = nm
        nv_ref[...] = nv

    wspec = pl.BlockSpec((None, tr, C), lambda l, r, s: (l, r, 0))
    gs = pltpu.PrefetchScalarGridSpec(
        num_scalar_prefetch=1, grid=(L, R // tr),
        in_specs=[wspec, pl.BlockSpec((None, tr, C), lambda l, r, s: (l, r + shift * s[1], 0)), wspec, wspec],
        out_specs=[wspec, wspec, wspec, wspec])
    sds = jax.ShapeDtypeStruct((L, R, C), F32)
    return _pcall(body, name=name, grid_spec=gs, out_shape=[sds, sds, sds, sds],
                  compiler_params=_cparams(("parallel", "parallel")))(idx, w, g, m, v)


def _small_allreduce_adamw(gv, wv, mv, vv):
    NR = gv.shape[0]

    def body(g_ref, w_ref, m_ref, v_ref, go_ref, d_ref, nm_ref, nv_ref, gath, send, recv):
        x, y, c = _mesh_pos()
        rank = 4 * x + 2 * y + c
        gath[rank] = g_ref[...]
        cps = []
        for msk in range(1, N_DEV):
            bx, by, bc = (msk >> 2) & 1, (msk >> 1) & 1, msk & 1
            peer = (1 - x if bx else x, 1 - y if by else y, 1 - c if bc else c)
            cps.append(pltpu.make_async_remote_copy(src_ref=g_ref, dst_ref=gath.at[rank], send_sem=send.at[msk - 1],
                                                    recv_sem=recv.at[msk - 1], device_id=peer, device_id_type=MESH))
        for cp in cps:
            cp.start()
        for msk in range(1, N_DEV):
            bx, by, bc = (msk >> 2) & 1, (msk >> 1) & 1, msk & 1
            peer = (1 - x if bx else x, 1 - y if by else y, 1 - c if bc else c)
            prank = 4 * peer[0] + 2 * peer[1] + peer[2]
            pltpu.make_async_remote_copy(src_ref=g_ref, dst_ref=gath.at[prank], send_sem=send.at[msk - 1],
                                         recv_sem=recv.at[msk - 1], device_id=peer, device_id_type=MESH).wait_recv()
        for cp in cps:
            cp.wait_send()
        tot = gath[0]
        for r in range(1, N_DEV):
            tot = tot + gath[r]
        d, nm, nv = _adamw_math(w_ref[...], tot, m_ref[...], v_ref[...])
        go_ref[...] = tot
        d_ref[...] = d
        nm_ref[...] = nm
        nv_ref[...] = nv

    vm = pl.BlockSpec(memory_space=pltpu.VMEM)
    sds = jax.ShapeDtypeStruct((NR, LANES), F32)
    return _pcall(body, name="small_allreduce_adamw", in_specs=[vm, vm, vm, vm], out_specs=[vm, vm, vm, vm],
                  out_shape=[sds, sds, sds, sds],
                  scratch_shapes=[pltpu.VMEM((N_DEV, NR, LANES), F32), pltpu.SemaphoreType.DMA((N_DEV - 1,)),
                                  pltpu.SemaphoreType.DMA((N_DEV - 1,))],
                  compiler_params=pltpu.CompilerParams(vmem_limit_bytes=VMEM_LIMIT))(gv, wv, mv, vv)


def _split3(a):
    h1 = a.astype(BF16)
    r1 = a - h1.astype(F32)
    h2 = r1.astype(BF16)
    h3 = (r1 - h2.astype(F32)).astype(BF16)
    return h1, h2, h3


def _folded_dist():
    r = jnp.arange(WINDOW)[:, None]
    j = jnp.arange(WINDOW)[None, :]
    return jnp.where(j > r, r + WINDOW - j, r - j)


def _bucket_onehot():
    d = _folded_dist()
    max_exact = REL_BUCKETS // 2
    d_f = jnp.maximum(d, 1).astype(F32)
    large = max_exact + (jnp.log(d_f / max_exact) / math.log(REL_MAX_DIST / max_exact) * (REL_BUCKETS - max_exact)).astype(I32)
    large = jnp.minimum(large, REL_BUCKETS - 1)
    bucket = jnp.where(d < max_exact, d, large)
    oh = bucket[None] == jnp.arange(REL_BUCKETS)[:, None, None]
    return oh.reshape(REL_BUCKETS, WINDOW * WINDOW).astype(BF16)


def _bias_fwd(rel_bias_t, onehot):
    H, N = rel_bias_t.shape[0], onehot.shape[1]
    tn = 4096

    def body(t_ref, oh_ref, o_ref):
        h1, h2, h3 = _split3(t_ref[...])
        oh = oh_ref[...]
        o_ref[...] = (_nn(h1, oh) + _nn(h2, oh)) + _nn(h3, oh)

    return _pcall(body, name="bias_fwd", grid=(N // tn,),
                  in_specs=[pl.BlockSpec((H, REL_BUCKETS), lambda i: (0, 0)), pl.BlockSpec((REL_BUCKETS, tn), lambda i: (0, i))],
                  out_specs=pl.BlockSpec((H, tn), lambda i: (0, i)), out_shape=jax.ShapeDtypeStruct((H, N), F32),
                  compiler_params=_cparams(("parallel",)))(rel_bias_t, onehot)


def _bias_bwd(dbias, onehot):
    H, N = dbias.shape
    tn = 4096

    def body(d_ref, oh_ref, o_ref):
        @pl.when(pl.program_id(0) == 0)
        def _():
            o_ref[...] = jnp.zeros_like(o_ref)
        h1, h2, h3 = _split3(d_ref[...])
        oh = oh_ref[...]
        o_ref[...] += (_nt(h1, oh) + _nt(h2, oh)) + _nt(h3, oh)

    return _pcall(body, name="bias_bwd", grid=(N // tn,),
                  in_specs=[pl.BlockSpec((H, tn), lambda i: (0, i)), pl.BlockSpec((REL_BUCKETS, tn), lambda i: (0, i))],
                  out_specs=pl.BlockSpec((H, REL_BUCKETS), lambda i: (0, 0)), out_shape=jax.ShapeDtypeStruct((H, REL_BUCKETS), F32),
                  compiler_params=_cparams(("arbitrary",)))(dbias, onehot)


def _to_bf16(x):
    S, D = x.shape
    tm = min(S, 512)

    def body(x_ref, o_ref):
        o_ref[...] = x_ref[...].astype(BF16)

    return _pcall(body, name="to_bf16", grid=(S // tm,), in_specs=[pl.BlockSpec((tm, D), lambda i: (i, 0))],
                  out_specs=pl.BlockSpec((tm, D), lambda i: (i, 0)), out_shape=jax.ShapeDtypeStruct((S, D), BF16),
                  compiler_params=_cparams(("parallel",)))(x)


def _inproj(xb, w_t, b_ext, l, job, store):
    S, D = xb.shape
    tm = min(S, 512)

    def body(x_ref, w_ref, b_ref, q_ref, k_ref, v_ref, ga_ref, u_ref, gp_ref):
        xv = x_ref[...]

        def window(j):
            return _nt(xv, w_ref[j]) + b_ref[:, j * SHARD_P:(j + 1) * SHARD_P]

        acc = window(0)
        q_ref[...] = (acc[:, :1024] * QK_SCALE).astype(BF16)
        k_lo = acc[:, 1024:1152]
        acc = window(1)
        k_ref[...] = k_lo + acc[:, 0:128]
        v_ref[...] = acc[:, 128:256]
        ga_ref[:, 0:896] = acc[:, 256:1152].astype(ACT)
        acc = window(2)
        ga_ref[:, 896:1024] = acc[:, 0:128].astype(ACT)
        u_ref[:, 0:896] = acc[:, 128:1024]
        u_lo = acc[:, 1024:1152]
        acc = window(3)
        u_ref[:, 896:1024] = u_lo + acc[:, 0:128]
        gp_ref[...] = acc[:, 128:1152].astype(ACT)

    def ospec(w):
        return pl.BlockSpec((tm, w), lambda i: (i, 0))

    return _call(
        body, name="inproj", grid=(S // tm,),
        in_specs=[pl.BlockSpec((tm, D), lambda i: (i, 0)),
                  pl.BlockSpec((N_CHIPS, SHARD_P, D), lambda i: (0, 0, 0), pipeline_mode=pl.Buffered(1)),
                  pl.BlockSpec((None, 1, EXT), lambda i: (l, 0, 0))],
        out_specs=[ospec(ATTN_W), ospec(KV_W), ospec(KV_W), ospec(ATTN_W), ospec(POOL_W), ospec(POOL_W)],
        out_shape=[jax.ShapeDtypeStruct((S, ATTN_W), BF16), jax.ShapeDtypeStruct((S, KV_W), F32), jax.ShapeDtypeStruct((S, KV_W), F32),
                   jax.ShapeDtypeStruct((S, ATTN_W), ACT), jax.ShapeDtypeStruct((S, POOL_W), F32), jax.ShapeDtypeStruct((S, POOL_W), ACT)],
        args=(xb, w_t, b_ext), sem=("parallel",), job=job, store=store)


def _matmul_nn(a, b4, name, job, store):
    S, K = a.shape
    N = b4.shape[2]
    kq = b4.shape[1]
    tm, tn = min(S, 512), min(N, 1024)

    def body(a_ref, b_ref, o_ref):
        acc = _nn(a_ref[:, 0:kq], b_ref[0])
        for j in range(1, N_CHIPS):
            acc = acc + _nn(a_ref[:, j * kq:(j + 1) * kq], b_ref[j])
        o_ref[...] = acc.astype(ACT)

    return _call(body, name=name, grid=(N // tn, S // tm),
                 in_specs=[pl.BlockSpec((tm, K), lambda n, i: (i, 0)), pl.BlockSpec((N_CHIPS, kq, tn), lambda n, i: (0, 0, n))],
                 out_specs=[pl.BlockSpec((tm, tn), lambda n, i: (i, n))], out_shape=[jax.ShapeDtypeStruct((S, N), ACT)],
                 args=(a, b4), sem=("parallel", "parallel"), job=job, store=store)[0]


def _masked_bias(bias):
    r = jnp.arange(WINDOW)[:, None]
    j = jnp.arange(WINDOW)[None, :]
    return jnp.stack([jnp.where(j > r, -1e30, bias), bias])


def _fold(full, tri):
    return jnp.where(tri, full[:, :WINDOW], full[:, WINDOW:])


def _unfold(folded, tri):
    return jnp.concatenate([jnp.where(tri, folded, 0.0).astype(BF16), jnp.where(tri, 0.0, folded).astype(BF16)], axis=1)


def _dup_heads(src_ref, dst_ref):
    a = src_ref[...]
    r = pltpu.roll(a, HEAD_DIM, axis=1)
    lo = lax.broadcasted_iota(I32, a.shape, 1) < HEAD_DIM
    dst_ref[0] = jnp.where(lo, a, r).astype(BF16)
    dst_ref[1] = jnp.where(lo, r, a).astype(BF16)


def _kv_block(ref, h, prev, cur):
    return jnp.concatenate([ref[h, pl.ds(prev, WINDOW), :], ref[h, pl.ds(cur, WINDOW), :]], axis=0)


def _rows(i):
    return slice(i * WINDOW, (i + 1) * WINDOW)


def _attn_fwd(q, k, v, ga, bias, sinks, l, job, store):
    S = q.shape[0]
    nblk = S // WINDOW

    def body(sink_ref, q_ref, k_ref, v_ref, ga_ref, bias_ref, o_ref, ca_ref, lse_ref, kd, vd, qs, ss, ps, os_):
        n = pl.program_id(0)

        @pl.when(n == 0)
        def _():
            _dup_heads(k_ref, kd)
            _dup_heads(v_ref, vd)

        cur = pl.multiple_of(n * WINDOW, WINDOW)
        prev = pl.multiple_of(jnp.maximum(n - 1, 0) * WINDOW, WINDOW)
        lane = lax.broadcasted_iota(I32, (WINDOW, LANES), 1)
        lo = lane < HEAD_DIM
        tri = lane > lax.broadcasted_iota(I32, (WINDOW, LANES), 0)
        lse_mat = jnp.zeros((WINDOW, LANES), F32)
        for g in range(N_KV):
            for i in range(KV_RATIO):
                h = g * KV_RATIO + i
                qp = q_ref[:, LANES * (h // 2):LANES * (h // 2 + 1)]
                qs[_rows(i), :] = jnp.where(lo if h % 2 == 0 else jnp.logical_not(lo), qp, jnp.zeros_like(qp))
            ss[...] = _nt(qs[...], _kv_block(kd, g, prev, cur))
            for i in range(KV_RATIO):
                h = g * KV_RATIO + i
                s = _fold(ss[_rows(i), :], tri) + bias_ref[h]
                sink = sink_ref[l, h]
                m = jnp.maximum(jnp.max(s, axis=1, keepdims=True), sink)
                e = jnp.exp(s - m)
                den = jnp.sum(e, axis=1, keepdims=True) + jnp.exp(sink - m)
                ps[_rows(i), :] = _unfold(e * (1.0 / den), tri)
                lse_mat = jnp.where(lane == h, m + jnp.log(den), lse_mat)
            os_[...] = _nn(ps[...], _kv_block(vd, g, prev, cur))
            for j in range(KV_RATIO // 2):
                pair = g * (KV_RATIO // 2) + j
                sl = slice(LANES * pair, LANES * (pair + 1))
                o_pair = jnp.where(lo, os_[_rows(2 * j), :], os_[_rows(2 * j + 1), :])
                o_ref[:, sl] = o_pair.astype(ACT)
                gav = ga_ref[:, sl].astype(F32)
                ca_ref[:, sl] = (o_pair * (gav * _sigmoid(gav))).astype(BF16)
        lse_ref[...] = lse_mat

    blk = pl.BlockSpec((WINDOW, ATTN_W), lambda n: (n, 0))
    full_kv = pl.BlockSpec((S, KV_W), lambda n: (0, 0))
    stack = KV_RATIO * WINDOW
    return _call(
        body, name="attn_fwd", grid=(nblk,),
        in_specs=[pl.BlockSpec(memory_space=pltpu.SMEM), blk, full_kv, full_kv, blk,
                  pl.BlockSpec((None, N_HEADS, WINDOW, WINDOW), lambda n: (jnp.minimum(n, 1), 0, 0, 0))],
        out_specs=[blk, blk, pl.BlockSpec((WINDOW, LANES), lambda n: (n, 0))],
        out_shape=[jax.ShapeDtypeStruct((S, ATTN_W), ACT), jax.ShapeDtypeStruct((S, ATTN_W), BF16), jax.ShapeDtypeStruct((S, LANES), F32)],
        scratch_shapes=[pltpu.VMEM((N_KV, S, LANES), BF16), pltpu.VMEM((N_KV, S, LANES), BF16),
                        pltpu.VMEM((stack, LANES), BF16), pltpu.VMEM((stack, 2 * WINDOW), F32),
                        pltpu.VMEM((stack, 2 * WINDOW), BF16), pltpu.VMEM((stack, LANES), F32)],
        args=(sinks, q, k, v, ga, bias), sem=("arbitrary",), job=job, store=store)


def _pool_diff(u, halo, tile_index, tm):
    gw = POOL_W // len(POOL_WINDOWS)
    xh = jnp.concatenate([halo, u], axis=0)
    sums = []
    s = xh
    for step in (1, 2, 4, 8):
        s = s + pltpu.roll(s, step, axis=0)
        sums.append(s)
    t = tile_index * tm + lax.broadcasted_iota(I32, (tm, gw), 0)
    diffs = []
    for g, w in enumerate(POOL_WINDOWS):
        cols = slice(g * gw, (g + 1) * gw)
        cnt = jnp.minimum(t + 1, w).astype(F32)
        diffs.append(sums[g][POOL_HALO:, cols] / cnt - u[:, cols])
    return diffs


def _pool_weight(wp_ref, g):
    r = wp_ref.shape[1] // len(POOL_WINDOWS)
    return jnp.concatenate([wp_ref[j, g * r:(g + 1) * r, :] for j in range(N_CHIPS)], axis=0)


def _pool_fwd(u, gp, ca, w_pool, ps, l, job, store):
    S = u.shape[0]
    tm = min(S, 256)
    hb = tm // POOL_HALO
    gw = POOL_W // len(POOL_WINDOWS)

    def body(u_ref, uh_ref, gp_ref, ca_ref, wp_ref, ps_ref, c_ref):
        i = pl.program_id(0)
        uv = u_ref[...]
        halo = jnp.where(i > 0, uh_ref[...], 0.0)
        diffs = _pool_diff(uv, halo, i, tm)
        c_ref[:, 0:ATTN_W] = ca_ref[...]
        for g in range(len(POOL_WINDOWS)):
            cols = slice(g * gw, (g + 1) * gw)
            mm = _nn(diffs[g].astype(BF16), _pool_weight(wp_ref, g))
            gpv = gp_ref[:, cols].astype(F32)
            b = (mm * ps_ref[:, cols]) * (gpv * _sigmoid(gpv))
            c_ref[:, ATTN_W + g * gw:ATTN_W + (g + 1) * gw] = b.astype(BF16)

    row = pl.BlockSpec((tm, POOL_W), lambda i: (i, 0))
    return _call(
        body, name="pool_fwd", grid=(S // tm,),
        in_specs=[row, pl.BlockSpec((POOL_HALO, POOL_W), lambda i: (jnp.maximum(i * hb - 1, 0), 0)), row, row,
                  pl.BlockSpec(w_pool.shape, lambda i: (0, 0, 0)),
                  pl.BlockSpec((None, 1, POOL_W), lambda i: (l, 0, 0))],
        out_specs=[pl.BlockSpec((tm, ATTN_W + POOL_W), lambda i: (i, 0))],
        out_shape=[jax.ShapeDtypeStruct((S, ATTN_W + POOL_W), BF16)],
        args=(u, u, gp, ca, w_pool, ps), sem=("parallel",), job=job, store=store)[0]


def _ple_embed(p_ref, wple_ref):
    pb = p_ref[...].astype(BF16)
    return jnp.concatenate([_nn(pb, wple_ref[j]) for j in range(N_CHIPS)], axis=1)


def _outproj_ln(c, xh_in, gain_in, bias_in, l_in, gl, p, w_out, w_ple, gain, bias, l, alpha, job, store):
    S, D = xh_in.shape
    tm = min(S, 256)
    kq = D // N_CHIPS

    def body(c_ref, x_ref, gi_ref, bi_ref, gl_ref, p_ref, wo_ref, wp_ref, gain_ref, bias_ref, yb_ref, xh_ref, rs_ref):
        mix = _nn(c_ref[:, 0:kq], wo_ref[0])
        for j in range(1, N_CHIPS):
            mix = mix + _nn(c_ref[:, j * kq:(j + 1) * kq], wo_ref[j])
        ple = _sigmoid(gl_ref[...].astype(F32)) * _ple_embed(p_ref, wp_ref)
        x = x_ref[...] * gi_ref[...] + bi_ref[...]
        z = (alpha * x + mix) + ple
        mu = jnp.mean(z, axis=1, keepdims=True)
        zc = z - mu
        var = jnp.mean(zc * zc, axis=1, keepdims=True)
        rstd = lax.rsqrt(var + LN_EPS)
        xhat = zc * rstd
        yb_ref[...] = (xhat * gain_ref[...] + bias_ref[...]).astype(BF16)
        xh_ref[...] = xhat
        rs_ref[...] = rstd

    row = pl.BlockSpec((tm, D), lambda i: (i, 0))
    vec = pl.BlockSpec((None, 1, D), lambda i: (l, 0, 0))
    vec_in = pl.BlockSpec((None, 1, D), lambda i: (l_in, 0, 0))
    return _call(
        body, name="outproj_ln", grid=(S // tm,),
        in_specs=[row, row, vec_in, vec_in, row, pl.BlockSpec((tm, p.shape[1]), lambda i: (i, 0)),
                  pl.BlockSpec(w_out.shape, lambda i: (0, 0, 0)), pl.BlockSpec(w_ple.shape, lambda i: (0, 0, 0)), vec, vec],
        out_specs=[row, row, pl.BlockSpec((tm, 1), lambda i: (i, 0))],
        out_shape=[jax.ShapeDtypeStruct((S, D), BF16), jax.ShapeDtypeStruct((S, D), F32), jax.ShapeDtypeStruct((S, 1), F32)],
        args=(c, xh_in, gain_in, bias_in, gl, p, w_out, w_ple, gain, bias), sem=("parallel",), job=job, store=store)


def _loss_and_grad(xhat, gain, bias, l, target):
    S, D = xhat.shape
    tm = min(S, 512)

    def body(xh_ref, g_ref, b_ref, t_ref, dy_ref, acc_ref):
        @pl.when(pl.program_id(0) == 0)
        def _():
            acc_ref[...] = jnp.zeros_like(acc_ref)
        d = (xh_ref[...] * g_ref[...] + b_ref[...]) - t_ref[...]
        dy_ref[...] = d * (1.0 / D)
        acc_ref[...] += jnp.sum(jnp.mean(d * d, axis=1, keepdims=True), axis=0, keepdims=True)

    row = pl.BlockSpec((tm, D), lambda i: (i, 0))
    vec = pl.BlockSpec((None, 1, D), lambda i: (l, 0, 0))
    return _pcall(body, name="loss", grid=(S // tm,), in_specs=[row, vec, vec, row],
                  out_specs=[row, pl.BlockSpec((8, LANES), lambda i: (0, 0))],
                  out_shape=[jax.ShapeDtypeStruct((S, D), F32), jax.ShapeDtypeStruct((8, LANES), F32)],
                  compiler_params=_cparams(("arbitrary",)))(xhat, gain, bias, target)


def _ln_bwd(dy, xhat, rstd, gl, p, w_ple, w_out, gain, l, job, store):
    S, D = dy.shape
    tm = min(S, 256)

    nq = w_out.shape[1]

    def body(dy_ref, xh_ref, rs_ref, gl_ref, p_ref, wp_ref, wo_ref, gain_ref, dz_ref, dzb_ref, de_ref, dgl_ref, dc_ref, gg_ref, gb_ref):
        @pl.when(pl.program_id(0) == 0)
        def _():
            gg_ref[...] = jnp.zeros_like(gg_ref)
            gb_ref[...] = jnp.zeros_like(gb_ref)
        dyv = dy_ref[...]
        xh = xh_ref[...]
        dxh = dyv * gain_ref[...]
        m1 = jnp.mean(dxh, axis=1, keepdims=True)
        m2 = jnp.mean(dxh * xh, axis=1, keepdims=True)
        dz = rs_ref[...] * ((dxh - m1) - xh * m2)
        gg_ref[...] += jnp.sum(dyv * xh, axis=0, keepdims=True)
        gb_ref[...] += jnp.sum(dyv, axis=0, keepdims=True)
        sg = _sigmoid(gl_ref[...].astype(F32))
        e = _ple_embed(p_ref, wp_ref)
        dzb = dz.astype(BF16)
        dz_ref[...] = dz
        dzb_ref[...] = dzb
        de_ref[...] = (dz * sg).astype(BF16)
        dgl_ref[...] = ((dz * e) * (sg * (1.0 - sg))).astype(BF16)
        for j in range(N_CHIPS):
            dc_ref[:, j * nq:(j + 1) * nq] = _nt(dzb, wo_ref[j]).astype(ACT)

    row = pl.BlockSpec((tm, D), lambda i: (i, 0))
    vec_in = pl.BlockSpec((None, 1, D), lambda i: (l, 0, 0))
    vec_out = pl.BlockSpec((1, D), lambda i: (0, 0))
    bsd = jax.ShapeDtypeStruct((S, D), BF16)
    fsd = jax.ShapeDtypeStruct((S, D), F32)
    return _call(
        body, name="ln_bwd", grid=(S // tm,),
        in_specs=[row, row, pl.BlockSpec((tm, 1), lambda i: (i, 0)), row, pl.BlockSpec((tm, p.shape[1]), lambda i: (i, 0)),
                  pl.BlockSpec(w_ple.shape, lambda i: (0, 0, 0), pipeline_mode=pl.Buffered(1)),
                  pl.BlockSpec(w_out.shape, lambda i: (0, 0, 0), pipeline_mode=pl.Buffered(1)), vec_in],
        out_specs=[row, row, row, row, row, vec_out, vec_out],
        out_shape=[fsd, bsd, bsd, bsd, jax.ShapeDtypeStruct((S, D), ACT), jax.ShapeDtypeStruct((1, D), F32), jax.ShapeDtypeStruct((1, D), F32)],
        args=(dy, xhat, rstd, gl, p, w_ple, w_out, gain), sem=("arbitrary",), job=job, store=store)


def _matmul_nn_acc(a, b4, name, add, add_scale, job, store):
    S = a.shape[0]
    KS, tk, N = b4.shape
    tm, tn = min(S, 512), min(N, 1024)

    def body(a_ref, b_ref, add_ref, o_ref):
        acc = add_scale * add_ref[...]
        for k in range(KS):
            acc = acc + _nn(a_ref[:, k * tk:(k + 1) * tk], b_ref[k])
        o_ref[...] = acc

    return _call(body, name=name, grid=(N // tn, S // tm),
                 in_specs=[pl.BlockSpec((tm, KS * tk), lambda n, i: (i, 0)),
                           pl.BlockSpec((KS, tk, tn), lambda n, i: (0, 0, n)),
                           pl.BlockSpec((tm, tn), lambda n, i: (i, n))],
                 out_specs=[pl.BlockSpec((tm, tn), lambda n, i: (i, n))], out_shape=[jax.ShapeDtypeStruct((S, N), F32)],
                 args=(a, b4, add), sem=("parallel", "parallel"), job=job, store=store)[0]


def _matmul_nt_rows(a, b4, name, add=None, add_scale=1.0, job=None, store=None):
    S, K = a.shape
    nq = b4.shape[1]
    tm = min(S, 512)
    out_spec = pl.BlockSpec((tm, N_CHIPS * nq), lambda i: (i, 0))
    in_specs = [pl.BlockSpec((tm, K), lambda i: (i, 0)), pl.BlockSpec(b4.shape, lambda i: (0, 0, 0))]
    if add is None:
        def body(a_ref, b_ref, o_ref):
            av = a_ref[...]
            for j in range(N_CHIPS):
                o_ref[:, j * nq:(j + 1) * nq] = _nt(av, b_ref[j])
        args = (a, b4)
    else:
        def body(a_ref, b_ref, add_ref, o_ref):
            av = a_ref[...]
            for j in range(N_CHIPS):
                cols = slice(j * nq, (j + 1) * nq)
                o_ref[:, cols] = _nt(av, b_ref[j]) + add_scale * add_ref[:, cols]
        in_specs.append(out_spec)
        args = (a, b4, add)

    return _call(body, name=name, grid=(S // tm,), in_specs=in_specs,
                 out_specs=[out_spec], out_shape=[jax.ShapeDtypeStruct((S, N_CHIPS * nq), F32)],
                 args=args, sem=("parallel",), job=job, store=store)[0]


def _matmul_tn(a, b, R, C, name, by_rows, job=None, store=None):
    S = a.shape[0]
    tn = min(C, 1024 if R <= 512 else 512)
    nt = C // tn

    def body(a_ref, b_ref, o_ref, at_ref):
        @pl.when(pl.program_id(1) == 0)
        def _():
            at_ref[...] = a_ref[...].T
        o_ref[...] = _nn(at_ref[...], b_ref[...])

    if by_rows:
        a_spec = pl.BlockSpec((S, R), lambda j, n: (0, j))
        b_spec = pl.BlockSpec((S, tn), lambda j, n: (0, n))
    else:
        a_spec = pl.BlockSpec((S, R), lambda j, n: (0, 0))
        b_spec = pl.BlockSpec((S, tn), lambda j, n: (0, j * nt + n))
    return _call(body, name=name, grid=(N_CHIPS, nt), in_specs=[a_spec, b_spec],
                 out_specs=[pl.BlockSpec((None, R, tn), lambda j, n: (j, 0, n))],
                 out_shape=[jax.ShapeDtypeStruct((N_CHIPS, R, C), F32)],
                 scratch_shapes=[pltpu.VMEM((R, S), BF16)],
                 args=(a, b), sem=("parallel", "arbitrary"), job=job, store=store)[0]


def _pool_bwd(u, gp, dc, w_pool, ps, l, job=None, store=None):
    S = u.shape[0]
    tm = min(S, 256)
    hb = tm // POOL_HALO
    ngrp = len(POOL_WINDOWS)
    gw = POOL_W // ngrp
    rr = gw // N_CHIPS

    def body(u_ref, uh_ref, gp_ref, dc_ref, wp_ref, ps_ref, dd_ref, dgp_ref, gps_ref, gwp_ref):
        i = pl.program_id(0)

        @pl.when(i == 0)
        def _():
            gps_ref[...] = jnp.zeros_like(gps_ref)
            gwp_ref[...] = jnp.zeros_like(gwp_ref)
        uv = u_ref[...]
        halo = jnp.where(i > 0, uh_ref[...], 0.0)
        diffs = _pool_diff(uv, halo, i, tm)
        for g in range(ngrp):
            cols = slice(g * gw, (g + 1) * gw)
            w = _pool_weight(wp_ref, g)
            db = diffs[g].astype(BF16)
            mm = _nn(db, w)
            gpv = gp_ref[:, cols].astype(F32)
            sg = _sigmoid(gpv)
            si = gpv * sg
            dsi = sg * (1.0 + gpv * (1.0 - sg))
            dcb = dc_ref[:, cols].astype(F32)
            psv = ps_ref[:, cols]
            d_mm = (dcb * si) * psv
            gps_ref[:, cols] += jnp.sum((dcb * si) * mm, axis=0, keepdims=True)
            dgp_ref[:, cols] = (dcb * (mm * psv)) * dsi
            d_mmb = d_mm.astype(BF16)
            dd_ref[:, cols] = _nt(d_mmb, w)
            gwt = _tn(db, d_mmb)
            for j in range(N_CHIPS):
                gwp_ref[j, g * rr:(g + 1) * rr, :] += gwt[j * rr:(j + 1) * rr, :]

    row = pl.BlockSpec((tm, POOL_W), lambda i: (i, 0))
    wspec = pl.BlockSpec(w_pool.shape, lambda i: (0, 0, 0))
    return _call(
        body, name="pool_bwd", grid=(S // tm,),
        in_specs=[row, pl.BlockSpec((POOL_HALO, POOL_W), lambda i: (jnp.maximum(i * hb - 1, 0), 0)), row,
                  pl.BlockSpec((tm, POOL_W), lambda i: (i, 1)), wspec, pl.BlockSpec((None, 1, POOL_W), lambda i: (l, 0, 0))],
        out_specs=[row, row, pl.BlockSpec((1, POOL_W), lambda i: (0, 0)), wspec],
        out_shape=[jax.ShapeDtypeStruct((S, POOL_W), F32), jax.ShapeDtypeStruct((S, POOL_W), F32),
                   jax.ShapeDtypeStruct((1, POOL_W), F32), jax.ShapeDtypeStruct(w_pool.shape, F32)],
        args=(u, u, gp, dc, w_pool, ps), sem=("arbitrary",), job=job, store=store)


def _pool_window_t(dd, halo_next, tile_index, tm):
    gw = POOL_W // len(POOL_WINDOWS)
    n = tm + POOL_HALO
    t = tile_index * tm + lax.broadcasted_iota(I32, (n, gw), 0)
    xh = jnp.concatenate([dd, halo_next], axis=0)
    outs = []
    for g, w in enumerate(POOL_WINDOWS):
        cols = slice(g * gw, (g + 1) * gw)
        cnt = jnp.minimum(t + 1, w).astype(F32)
        s = xh[:, cols] / cnt
        step = 1
        while step < w:
            s = s + pltpu.roll(s, n - step, axis=0)
            step *= 2
        outs.append(s[:tm] - dd[:, cols])
    return outs


def _assemble_dh(dq, dk, dv, dga, dd, dgp):
    S = dq.shape[0]
    tm = min(S, 256)
    hb = tm // POOL_HALO
    nt = S // tm

    def body(dq_ref, dk_ref, dv_ref, dga_ref, dd_ref, ddn_ref, dgp_ref, dh_ref, gb_ref):
        i = pl.program_id(0)

        @pl.when(i == 0)
        def _():
            gb_ref[...] = jnp.zeros_like(gb_ref)
        halo = jnp.where(i < nt - 1, ddn_ref[...], 0.0)
        du = jnp.concatenate(_pool_window_t(dd_ref[...], halo, i, tm), axis=1)
        dkv = dk_ref[...]
        dgav = dga_ref[...]
        parts = [(OFF_Q, dq_ref[...]), (OFF_KA, dkv), (OFF_KB, dkv), (OFF_V, dv_ref[...]), (OFF_GA, dgav),
                 (OFF_U, du[:, 0:896]), (OFF_UA, du[:, 896:1024]), (OFF_UB, du[:, 896:1024]), (OFF_GP, dgp_ref[...])]
        for off, val in parts:
            w = val.shape[1]
            dh_ref[:, off:off + w] = val.astype(BF16)
            gb_ref[:, off:off + w] += jnp.sum(val, axis=0, keepdims=True)

    def row(w):
        return pl.BlockSpec((tm, w), lambda i: (i, 0))

    return _pcall(
        body, name="assemble_dh", grid=(nt,),
        in_specs=[row(ATTN_W), row(KV_W), row(KV_W), row(ATTN_W), row(POOL_W),
                  pl.BlockSpec((POOL_HALO, POOL_W), lambda i: (jnp.minimum((i + 1) * hb, S // POOL_HALO - 1), 0)), row(POOL_W)],
        out_specs=[row(EXT), pl.BlockSpec((1, EXT), lambda i: (0, 0))],
        out_shape=[jax.ShapeDtypeStruct((S, EXT), BF16), jax.ShapeDtypeStruct((1, EXT), F32)],
        compiler_params=_cparams(("arbitrary",)),
    )(dq, dk, dv, dga, dd, dd, dgp)


def _attn_bwd(q, k, v, ga, o, dc, lse, bias, sinks, dbias_in, l, job, store):
    S = q.shape[0]
    nblk = S // WINDOW

    def body(sink_ref, q_ref, k_ref, v_ref, ga_ref, o_ref, dc_ref, lse_ref, bias_ref, dbin_ref,
             dq_ref, dga_ref, dk_ref, dv_ref, db_ref, ds_ref, kd, vd, qs, dos, dls, ss, dps, dss, pss, dqs):
        n = pl.program_id(0)

        @pl.when(n == 0)
        def _():
            _dup_heads(k_ref, kd)
            _dup_heads(v_ref, vd)
            dk_ref[...] = jnp.zeros_like(dk_ref)
            dv_ref[...] = jnp.zeros_like(dv_ref)
            db_ref[...] = dbin_ref[...]
            ds_ref[...] = jnp.zeros_like(ds_ref)

        cur = pl.multiple_of(n * WINDOW, WINDOW)
        prev = pl.multiple_of(jnp.maximum(n - 1, 0) * WINDOW, WINDOW)
        lane = lax.broadcasted_iota(I32, (WINDOW, LANES), 1)
        lane8 = lax.broadcasted_iota(I32, (8, LANES), 1)
        lo = lane < HEAD_DIM
        tri = lane > lax.broadcasted_iota(I32, (WINDOW, LANES), 0)
        lse_t = lse_ref[...]
        dk_t, dv_t = [], []
        dsk = jnp.zeros((8, LANES), F32)
        for g in range(N_KV):
            kb = _kv_block(kd, g, prev, cur)
            vb = _kv_block(vd, g, prev, cur)
            for j in range(KV_RATIO // 2):
                pair = g * (KV_RATIO // 2) + j
                sl = slice(LANES * pair, LANES * (pair + 1))
                qp = q_ref[:, sl]
                op = o_ref[:, sl].astype(F32)
                dcp = dc_ref[:, sl].astype(F32)
                gav = ga_ref[:, sl].astype(F32)
                sg = _sigmoid(gav)
                d_o = dcp * (gav * sg)
                dga_ref[:, sl] = (dcp * op) * (sg * (1.0 + gav * (1.0 - sg)))
                prod = d_o * op
                for par in range(2):
                    msk = lo if par == 0 else jnp.logical_not(lo)
                    rows = _rows(2 * j + par)
                    qs[rows, :] = jnp.where(msk, qp, jnp.zeros_like(qp))
                    dos[rows, :] = jnp.where(msk, d_o, 0.0).astype(BF16)
                    delta = jnp.sum(jnp.where(msk, prod, 0.0), axis=1, keepdims=True)
                    dls[rows, :] = jnp.broadcast_to(delta, (WINDOW, LANES))
            ss[...] = _nt(qs[...], kb)
            dps[...] = _nt(dos[...], vb)
            for i in range(KV_RATIO):
                h = g * KV_RATIO + i
                rows = _rows(i)
                lse_h = jnp.sum(jnp.where(lane == h, lse_t, 0.0), axis=1, keepdims=True)
                p = jnp.exp(_fold(ss[rows, :], tri) + bias_ref[h] - lse_h)
                delta = dls[rows, :]
                dsc = p * (_fold(dps[rows, :], tri) - delta)
                db_ref[h] += dsc
                psink = jnp.exp(sink_ref[l, h] - lse_h)
                dsk = dsk + jnp.where(lane8 == h, -jnp.sum(psink * delta, axis=0, keepdims=True), 0.0)
                dss[rows, :] = _unfold(dsc, tri)
                pss[rows, :] = _unfold(p, tri)
            dqs[...] = _nn(dss[...], kb) * QK_SCALE
            for j in range(KV_RATIO // 2):
                pair = g * (KV_RATIO // 2) + j
                dq_ref[:, LANES * pair:LANES * (pair + 1)] = jnp.where(lo, dqs[_rows(2 * j), :], dqs[_rows(2 * j + 1), :])
            dk_t.append(_tn(qs[...], dss[...]))
            dv_t.append(_tn(dos[...], pss[...]))

        def untranspose(acc):
            return jnp.concatenate([a[:HEAD_DIM] + a[HEAD_DIM:] for a in acc], axis=0).T

        dkb = untranspose(dk_t)
        dvb = untranspose(dv_t)
        dk_ref[pl.ds(prev, WINDOW), :] += dkb[:WINDOW]
        dk_ref[pl.ds(cur, WINDOW), :] += dkb[WINDOW:]
        dv_ref[pl.ds(prev, WINDOW), :] += dvb[:WINDOW]
        dv_ref[pl.ds(cur, WINDOW), :] += dvb[WINDOW:]
        ds_ref[...] += dsk

    blk = pl.BlockSpec((WINDOW, ATTN_W), lambda n: (n, 0))
    full_kv = pl.BlockSpec((S, KV_W), lambda n: (0, 0))
    full_b = pl.BlockSpec((N_HEADS, WINDOW, WINDOW), lambda n: (0, 0, 0))
    stack = KV_RATIO * WINDOW
    return _call(
        body, name="attn_bwd", grid=(nblk,),
        in_specs=[pl.BlockSpec(memory_space=pltpu.SMEM), blk, full_kv, full_kv, blk, blk, blk,
                  pl.BlockSpec((WINDOW, LANES), lambda n: (n, 0)),
                  pl.BlockSpec((None, N_HEADS, WINDOW, WINDOW), lambda n: (jnp.minimum(n, 1), 0, 0, 0)), full_b],
        out_specs=[blk, blk, full_kv, full_kv, full_b, pl.BlockSpec((8, LANES), lambda n: (0, 0))],
        out_shape=[jax.ShapeDtypeStruct((S, ATTN_W), F32), jax.ShapeDtypeStruct((S, ATTN_W), F32), jax.ShapeDtypeStruct((S, KV_W), F32),
                   jax.ShapeDtypeStruct((S, KV_W), F32), jax.ShapeDtypeStruct((N_HEADS, WINDOW, WINDOW), F32),
                   jax.ShapeDtypeStruct((8, LANES), F32)],
        scratch_shapes=[pltpu.VMEM((N_KV, S, LANES), BF16), pltpu.VMEM((N_KV, S, LANES), BF16),
                        pltpu.VMEM((stack, LANES), BF16), pltpu.VMEM((stack, LANES), BF16), pltpu.VMEM((stack, LANES), F32),
                        pltpu.VMEM((stack, 2 * WINDOW), F32), pltpu.VMEM((stack, 2 * WINDOW), F32),
                        pltpu.VMEM((stack, 2 * WINDOW), BF16), pltpu.VMEM((stack, 2 * WINDOW), BF16),
                        pltpu.VMEM((stack, LANES), F32)],
        args=(sinks, q, k, v, ga, o, dc, lse, bias, dbias_in), sem=("arbitrary",), job=job, store=store)


def _pack_small(arrs):
    flat = []
    for a in arrs:
        v = a.reshape(-1)
        flat.append(jnp.pad(v, (0, (-v.shape[0]) % LANES)))
    v = jnp.concatenate(flat)
    v = jnp.pad(v, (0, (-v.shape[0]) % (8 * LANES)))
    return v.reshape(-1, LANES)


def _unpack_small(packed, shapes):
    v = packed.reshape(-1)
    outs, off = [], 0
    for shp in shapes:
        n = math.prod(shp)
        outs.append(v[off:off + n].reshape(shp))
        off += n + (-n) % LANES
    return outs


def _bias_to_ext(b):
    L = b.shape[0]
    z = jnp.zeros((L, HALF_TILE), b.dtype)
    parts = []
    for j in range(N_CHIPS):
        seg = b[:, j * SHARD:(j + 1) * SHARD]
        parts += [z, seg] if j % 2 else [seg, z]
    return jnp.concatenate(parts, axis=1).reshape(L, 1, EXT)


def _bias_from_ext(g):
    parts = []
    for j in range(N_CHIPS):
        o = j * SHARD_P + (HALF_TILE if j % 2 else 0)
        parts.append(g[:, o:o + SHARD])
    return jnp.concatenate(parts, axis=1)


def kernel(x, p, w_in, b_in, w_out, attn_sinks, rel_bias, w_pool, pool_scale, w_ple, w_gate_ple, ln_gain, ln_bias, loss_target, m_w_in, m_b_in, m_w_out, m_attn_sinks, m_rel_bias, m_w_pool, m_pool_scale, m_w_ple, m_w_gate_ple, m_ln_gain, m_ln_bias, v_w_in, v_b_in, v_w_out, v_attn_sinks, v_rel_bias, v_w_pool, v_pool_scale, v_w_ple, v_w_gate_ple, v_ln_gain, v_ln_bias):
    L = w_in.shape[0]
    S, D = x.shape[1], x.shape[2]
    assert D == D_MODEL and w_in.shape[2] == SHARD and S % WINDOW == 0
    alpha = (2.0 * L) ** 0.25
    xc, yc, cc = _mesh_pos()
    idx = jnp.stack([2 * xc + yc, yc, cc]).astype(I32)
    store = {}

    def wkeys(l, names):
        return [("w", l, t) for t in names]

    def weight(l, t):
        return _unhalves(store["w", l, t])

    w_in_t = jnp.swapaxes(w_in, 1, 2)
    w_pool2 = w_pool.reshape(L, w_pool.shape[1] * w_pool.shape[2], w_pool.shape[3])
    for l in range(L):
        store["w", l, "in"] = _halves(_piece_in(w_in_t, l, idx))
        store["w", l, "out"] = _halves(_piece(w_out, l, idx, "piece_out"))
        store["w", l, "gate"] = _halves(_piece(w_gate_ple, l, idx, "piece_gate"))
        store["w", l, "ple"] = _halves(_piece(w_ple, l, idx, "piece_ple"))
        store["w", l, "pool"] = _halves(_piece(w_pool2, l, idx, "piece_pool"))
    _allgather_now(wkeys(0, WEIGHTS), store)

    b_ext = _bias_to_ext(b_in)
    ps3 = pool_scale.reshape(L, 1, POOL_W)
    gain3 = ln_gain.reshape(L, 1, D)
    bias3 = ln_bias.reshape(L, 1, D)
    onehot = _bucket_onehot()
    bias_hqk = _masked_bias(_bias_fwd(rel_bias.T, onehot).reshape(N_HEADS, WINDOW, WINDOW))

    xs = x[0]
    xb = _to_bf16(xs)
    pb = _to_bf16(p.reshape(L * S, p.shape[3])).reshape(L, S, p.shape[3])
    saved = []
    x_norm, x_gain, x_bias, x_l = xs, jnp.ones((1, 1, D), F32), jnp.zeros((1, 1, D), F32), 0
    late = ["out", "ple", "pool"]
    for l in range(L):
        nxt = l + 1 < L
        job = _Job()
        if nxt:
            job.add(_ag_ici, wkeys(l + 1, ["in"]), 3)
        if l >= 1:
            job.add(_ag_fwd, wkeys(l, ["gate"]), 3)
        q, k, v, ga, u, gp = _inproj(xb, weight(l, "in"), b_ext, l, job, store)
        gl = _matmul_nn(xb, weight(l, "gate"), "gate_logits", None, None)
        job = _Job().add(_ag_fwd, wkeys(l + 1, ["in"]), 3).add(_ag_ici, wkeys(l + 1, late), 9) if nxt else None
        o, ca, lse = _attn_fwd(q, k, v, ga, bias_hqk, attn_sinks, l, job, store)
        c = _pool_fwd(u, gp, ca, weight(l, "pool"), ps3, l, None, None)
        job = _Job().add(_ag_fwd, wkeys(l + 1, late), 9).add(_ag_ici, wkeys(l + 1, ["gate"]), 3) if nxt else None
        yb, xhat, rstd = _outproj_ln(c, x_norm, x_gain, x_bias, x_l, gl, pb[l], weight(l, "out"), weight(l, "ple"), gain3, bias3, l,
                                     alpha, job, store)
        saved.append(dict(xb=xb, q=q, k=k, v=v, ga=ga, u=u, gp=gp, gl=gl, o=o, lse=lse, c=c, xhat=xhat, rstd=rstd))
        x_norm, x_gain, x_bias, x_l, xb = xhat, gain3, bias3, l, yb

    dy, loss_acc = _loss_and_grad(x_norm, gain3, bias3, L - 1, loss_target[0])
    loss = lax.psum(0.5 * loss_acc[0, 0], ("x", "y", "c"))

    shapes = {t: store["w", 0, t].shape for t in WEIGHTS}
    for t in WEIGHTS:
        store["full", t] = lax.empty((L,) + shapes[t][1:], F32)

    def rs_keys(kind, l, names):
        return [(kind, l, t) for t in names]

    def rs_pair_job(l, names):
        for t in names:
            store["ra", l, t] = lax.empty((N_CHIPS,) + shapes[t][2:], F32)
        return _Job().add(_rs_pair, rs_keys("g", l, names) + rs_keys("ra", l, names), len(names))

    def rs_pair_add(l, names):
        for t in names:
            p32, p16 = _rs_pair_add(store["g", l, t], store["ra", l, t], idx, "rs_pair_add_" + t)
            store["p32", l, t], store["p16", l, t] = p32, p16
            store["rb", l, t] = lax.empty((3,) + shapes[t][2:], BF16)

    def rs_ici_job(l, names):
        return _Job().add(_rs_ici, rs_keys("p16", l, names) + rs_keys("rb", l, names), 3 * len(names))

    def rs_chip_add(l, names):
        for t in names:
            store["full", t] = _rs_chip_add(store["p32", l, t], store["rb", l, t], store["full", t], l, idx, "rs_chip_add_" + t)

    def rs_share_job(l):
        return _Job().add(functools.partial(_rs_share, layer=l), [("full", t) for t in WEIGHTS], len(WEIGHTS))

    dbias = jnp.zeros((N_HEADS, WINDOW, WINDOW), F32)
    small = [None] * L
    h_in = shapes["in"][2]
    head_rows = (h_in * 5 // 8) // 16 * 16
    early = ["out", "gate", "ple"]

    def in_ici_job(l, rows):
        return _Job().add(functools.partial(_rs_ici, rows=rows), rs_keys("p16", l, ["in"]) + rs_keys("rb", l, ["in"]), 3)

    for l in reversed(range(L)):
        sv = saved[l]
        pl_l = pb[l]
        job = in_ici_job(l + 1, (head_rows, h_in - head_rows)) if l + 1 < L else None
        dz, dzb, d_e, d_gl, dc, ggain, gbias = _ln_bwd(dy, sv["xhat"], sv["rstd"], sv["gl"], pl_l, weight(l, "ple"), weight(l, "out"),
                                                       gain3, l, job, store)
        if l + 1 < L:
            rs_chip_add(l + 1, ["in"])
        g_out = _matmul_tn(sv["c"], dzb, D // N_CHIPS, D, "grad_w_out", by_rows=True,
                           job=rs_share_job(l + 1) if l + 1 < L else None, store=store)
        g_ple = _matmul_tn(pl_l, d_e, pl_l.shape[1], D // N_CHIPS, "grad_w_ple", by_rows=False)
        g_gate = _matmul_tn(sv["xb"], d_gl, D // N_CHIPS, D, "grad_w_gate", by_rows=True)
        for t, g in zip(early, (g_out, g_gate, g_ple)):
            store["g", l, t] = _halves(g)
        dd, dgp, gps, g_pool = _pool_bwd(sv["u"], sv["gp"], dc, weight(l, "pool"), ps3, l, rs_pair_job(l, early), store)
        store["g", l, "pool"] = _halves(g_pool)
        rs_pair_add(l, early)
        job = rs_pair_job(l, ["pool"])
        job.add(_rs_ici, rs_keys("p16", l, ["out"]) + rs_keys("rb", l, ["out"]), 3)
        dq, dga, dk, dv, dbias, dsink = _attn_bwd(sv["q"], sv["k"], sv["v"], sv["ga"], sv["o"], dc, sv["lse"], bias_hqk,
                                                  attn_sinks, dbias, l, job, store)
        rs_pair_add(l, ["pool"])
        dh, gbe = _assemble_dh(dq, dk, dv, dga, dd, dgp)
        g_in = _matmul_tn(dh, sv["xb"], SHARD_P, D, "grad_w_in", by_rows=True, job=rs_ici_job(l, ["gate", "ple", "pool"]), store=store)
        rs_chip_add(l, ["out", "gate", "ple", "pool"])
        store["g", l, "in"] = _halves(g_in)
        dx1 = _matmul_nt_rows(d_gl, weight(l, "gate"), "d_x_gate", dz, alpha, job=rs_pair_job(l, ["in"]), store=store)
        rs_pair_add(l, ["in"])
        dy = _matmul_nn_acc(dh, weight(l, "in"), "d_x", dx1, 1.0, in_ici_job(l, (0, head_rows) if l > 0 else (0, h_in)), store)
        small[l] = dict(b_in=_bias_from_ext(gbe)[0], sinks=dsink[0, :N_HEADS], ps=gps[0], gain=ggain[0], bias=gbias[0])
    rs_chip_add(0, ["in"])
    grad_x = dy[None]

    _run_job("rs_pair_share", rs_share_job(0), store)
    full = {t: _unhalves(store["full", t]) for t in WEIGHTS}

    def t_back(a):
        return jnp.swapaxes(a, 1, 2)

    def pool4(a):
        return a.reshape(w_pool.shape)

    r_in = _adamw(w_in_t, full["in"], jnp.swapaxes(m_w_in, 1, 2), jnp.swapaxes(v_w_in, 1, 2), idx, "adamw_w_in", tr=HALF_TILE)
    gw_in, dw_in, nm_in, nv_in = [t_back(a) for a in r_in]
    gw_out, dw_out, nm_out, nv_out = _adamw(w_out, full["out"], m_w_out, v_w_out, idx, "adamw_w_out")
    gw_gate, dw_gate, nm_gate, nv_gate = _adamw(w_gate_ple, full["gate"], m_w_gate_ple, v_w_gate_ple, idx, "adamw_w_gate")
    gw_ple, dw_ple, nm_ple, nv_ple = _adamw(w_ple, full["ple"], m_w_ple, v_w_ple, idx, "adamw_w_ple")
    r_pool = _adamw(w_pool2, full["pool"], m_w_pool.reshape(w_pool2.shape), v_w_pool.reshape(w_pool2.shape), idx, "adamw_w_pool")
    gw_pool, dw_pool, nm_pool, nv_pool = [pool4(a) for a in r_pool]

    g_rel = _bias_bwd(dbias.reshape(N_HEADS, -1), onehot).T
    small_shapes = [b_in.shape, attn_sinks.shape, rel_bias.shape, pool_scale.shape, ln_gain.shape, ln_bias.shape]
    g_small = [jnp.stack([small[l]["b_in"] for l in range(L)]), jnp.stack([small[l]["sinks"] for l in range(L)]), g_rel,
               jnp.stack([small[l]["ps"] for l in range(L)]), jnp.stack([small[l]["gain"] for l in range(L)]),
               jnp.stack([small[l]["bias"] for l in range(L)])]
    packed = _small_allreduce_adamw(
        _pack_small(g_small),
        _pack_small([b_in, attn_sinks, rel_bias, pool_scale, ln_gain, ln_bias]),
        _pack_small([m_b_in, m_attn_sinks, m_rel_bias, m_pool_scale, m_ln_gain, m_ln_bias]),
        _pack_small([v_b_in, v_attn_sinks, v_rel_bias, v_pool_scale, v_ln_gain, v_ln_bias]))
    sg, sd, sm, sv_ = [_unpack_small(a, small_shapes) for a in packed]

    def order(big, sm_):
        return (big[0], sm_[0], big[1], sm_[1], sm_[2], big[2], sm_[3], big[3], big[4], sm_[4], sm_[5])

    return (loss, grad_x,
            *order((gw_in, gw_out, gw_pool, gw_ple, gw_gate), sg),
            *order((dw_in, dw_out, dw_pool, dw_ple, dw_gate), sd),
            *order((nm_in, nm_out, nm_pool, nm_ple, nm_gate), sm),
            *order((nv_in, nv_out, nv_pool, nv_ple, nv_gate), sv_))
```

```python
import functools
import math

import jax
import jax.numpy as jnp
from jax import lax
from jax.experimental import pallas as pl
from jax.experimental.pallas import tpu as pltpu

F32 = jnp.float32
BF16 = jnp.bfloat16
ACT = jnp.bfloat16
I32 = jnp.int32
MESH = pl.DeviceIdType.MESH

HEAD_DIM = 64
QK_SCALE = HEAD_DIM ** -0.5
WINDOW = 128
KV_RATIO = 8
POOL_WINDOWS = (2, 4, 8, 16)
POOL_HALO = 16
REL_BUCKETS = 32
REL_MAX_DIST = 128
LN_EPS = 1e-5
ADAM_LR, ADAM_B1, ADAM_B2, ADAM_EPS, ADAM_WD, ADAM_STEP = 0.001, 0.9, 0.999, 1e-08, 0.01, 10

LANES = 128
VMEM_LIMIT = 52 * 1024 * 1024
N_CHIPS = 4
N_DEV = 8

D_MODEL = 2048
ATTN_W = 1024
POOL_W = 1024
KV_W = 128
N_HEADS = ATTN_W // HEAD_DIM
N_KV = N_HEADS // KV_RATIO
IN_COLS = 4352
SHARD = IN_COLS // N_CHIPS
SHARD_P = 1152
EXT = N_CHIPS * SHARD_P
HALF_TILE = SHARD_P - SHARD
OFF_Q, OFF_KA, OFF_KB, OFF_V, OFF_GA, OFF_U, OFF_UA, OFF_UB, OFF_GP = 0, 1024, 1152, 1280, 1408, 2432, 3328, 3456, 3584
WEIGHTS = ("in", "out", "gate", "ple", "pool")


def _cparams(sem=None):
    return pltpu.CompilerParams(dimension_semantics=sem, vmem_limit_bytes=VMEM_LIMIT)


def _pcall(body, **kw):
    return pl.pallas_call(body, **kw)


def _sigmoid(x):
    return 1.0 / (1.0 + jnp.exp(-x))


def _nt(a, b):
    return lax.dot_general(a, b, (((1,), (1,)), ((), ())), preferred_element_type=F32)


def _tn(a, b):
    return lax.dot_general(a, b, (((0,), (0,)), ((), ())), preferred_element_type=F32)


def _nn(a, b):
    return jnp.dot(a, b, preferred_element_type=F32)


def _mesh_pos():
    return lax.axis_index("x"), lax.axis_index("y"), lax.axis_index("c")


def _peer_chips(x, y):
    return [(1 - x, y), (x, 1 - y), (1 - x, 1 - y)]


def _row_tile(rows, cap=256):
    t = min(rows, cap)
    while rows % t or t % 16:
        t -= 1
    return t


def _hbm_spec():
    return pl.BlockSpec(memory_space=pltpu.HBM)


def _halves(a):
    return a.reshape(a.shape[:-2] + (2, a.shape[-2] // 2, a.shape[-1]))


def _unhalves(a):
    return a.reshape(a.shape[:-3] + (2 * a.shape[-2], a.shape[-1]))


class _remote:
    def __init__(self, src, dst, send, recv, i, device):
        self.args = dict(src_ref=src, dst_ref=dst, send_sem=send.at[i], recv_sem=recv.at[i], device_id=device, device_id_type=MESH)

    def start(self):
        pltpu.make_async_remote_copy(**self.args).start()

    def wait_recv(self):
        pltpu.make_async_remote_copy(**self.args).wait_recv()

    def wait_send(self):
        pltpu.make_async_remote_copy(**self.args).wait_send()


class _Job:
    def __init__(self):
        self.keys, self.parts, self.n = [], [], 0

    def add(self, fn, keys, n):
        self.parts.append((fn, len(self.keys), len(keys), self.n))
        self.keys += list(keys)
        self.n += n
        return self

    def build(self, refs, send, recv):
        out = []
        for fn, i0, nb, base in self.parts:
            out += fn(refs[i0:i0 + nb], send, recv, base)
        return out


def _ag_ici(refs, send, recv, base):
    x, y, c = _mesh_pos()
    me = 2 * x + y
    out = []
    for t, g in enumerate(refs):
        for k, chip in enumerate(_peer_chips(x, y)):
            i = base + 3 * t + k
            dev = (*chip, c)
            out.append((_remote(g.at[me, c], g.at[me, c], send, recv, i, dev),
                        _remote(g.at[me, c], g.at[2 * chip[0] + chip[1], c], send, recv, i, dev)))
    return out


def _ag_fwd(refs, send, recv, base):
    x, y, c = _mesh_pos()
    out = []
    for t, g in enumerate(refs):
        for k, chip in enumerate(_peer_chips(x, y)):
            i = base + 3 * t + k
            slot = 2 * chip[0] + chip[1]
            dev = (x, y, 1 - c)
            out.append((_remote(g.at[slot, c], g.at[slot, c], send, recv, i, dev),
                        _remote(g.at[slot, c], g.at[slot, 1 - c], send, recv, i, dev)))
    return out


def _rs_pair(refs, send, recv, base):
    x, y, c = _mesh_pos()
    n = len(refs) // 2
    out = []
    for t in range(n):
        cp = _remote(refs[t].at[:, 1 - c], refs[n + t], send, recv, base + t, (x, y, 1 - c))
        out.append((cp, cp))
    return out


def _rs_ici(refs, send, recv, base, rows=None):
    x, y, c = _mesh_pos()
    n = len(refs) // 2
    rsl = slice(None) if rows is None else pl.ds(rows[0], rows[1])
    out = []
    for t in range(n):
        for k, chip in enumerate(_peer_chips(x, y)):
            cp = _remote(refs[t].at[2 * chip[0] + chip[1], rsl], refs[n + t].at[k, rsl], send, recv, base + 3 * t + k, (*chip, c))
            out.append((cp, cp))
    return out


def _rs_share(refs, send, recv, base, layer):
    x, y, c = _mesh_pos()
    out = []
    for t, f in enumerate(refs):
        dev = (x, y, 1 - c)
        out.append((_remote(f.at[layer, c], f.at[layer, c], send, recv, base + t, dev),
                    _remote(f.at[layer, c], f.at[layer, 1 - c], send, recv, base + t, dev)))
    return out


def _call(body, *, name, grid, in_specs, out_specs, out_shape, args, scratch_shapes=(), sem=None, job=None, store=None):
    in_specs, out_specs, out_shape, scratch_shapes = list(in_specs), list(out_specs), list(out_shape), list(scratch_shapes)
    if job is None or job.n == 0:
        return list(_pcall(body, name=name, grid=grid, in_specs=in_specs, out_specs=out_specs, out_shape=out_shape,
                           scratch_shapes=scratch_shapes, compiler_params=_cparams(sem))(*args))
    bufs = [store[k] for k in job.keys]
    nb, n_in, n_out, n_sc = len(bufs), len(args), len(out_shape), len(scratch_shapes)

    def wrapped(*refs):
        ins = refs[:n_in]
        outs = refs[n_in + nb:n_in + nb + n_out]
        cb = refs[n_in + nb + n_out:n_in + 2 * nb + n_out]
        scratch = refs[n_in + 2 * nb + n_out:n_in + 2 * nb + n_out + n_sc]
        send, recv = refs[-2:]
        ids = [pl.program_id(a) for a in range(len(grid))]
        first = functools.reduce(jnp.logical_and, [i == 0 for i in ids])
        last = functools.reduce(jnp.logical_and, [i == g - 1 for i, g in zip(ids, grid)])

        @pl.when(first)
        def _():
            for s, _r in job.build(cb, send, recv):
                s.start()

        body(*ins, *outs, *scratch)

        @pl.when(last)
        def _():
            pairs = job.build(cb, send, recv)
            for _s, r in pairs:
                r.wait_recv()
            for s, _r in pairs:
                s.wait_send()

    res = _pcall(
        wrapped, name=name, grid=grid, in_specs=in_specs + [_hbm_spec()] * nb, out_specs=out_specs + [_hbm_spec()] * nb,
        out_shape=out_shape + [jax.ShapeDtypeStruct(b.shape, b.dtype) for b in bufs],
        scratch_shapes=scratch_shapes + [pltpu.SemaphoreType.DMA((job.n,)), pltpu.SemaphoreType.DMA((job.n,))],
        input_output_aliases={n_in + i: n_out + i for i in range(nb)},
        compiler_params=_cparams(("arbitrary",) * len(grid)))(*args, *bufs)
    for k, v in zip(job.keys, res[n_out:]):
        store[k] = v
    return list(res[:n_out])


def _run_job(name, job, store):
    bufs = [store[k] for k in job.keys]
    nb = len(bufs)

    def body(*refs):
        send, recv = refs[-2:]
        pairs = job.build(refs[nb:2 * nb], send, recv)
        for s, _r in pairs:
            s.start()
        for _s, r in pairs:
            r.wait_recv()
        for s, _r in pairs:
            s.wait_send()

    res = _pcall(body, name=name, in_specs=[_hbm_spec()] * nb, out_specs=[_hbm_spec()] * nb,
                 out_shape=[jax.ShapeDtypeStruct(b.shape, b.dtype) for b in bufs],
                 scratch_shapes=[pltpu.SemaphoreType.DMA((job.n,)), pltpu.SemaphoreType.DMA((job.n,))],
                 input_output_aliases={i: i for i in range(nb)})(*bufs)
    for k, v in zip(job.keys, res):
        store[k] = v


def _allgather_now(keys, store):
    bufs = [store[k] for k in keys]
    nb = len(bufs)

    def body(*refs):
        send, recv = refs[-2:]
        g = refs[nb:2 * nb]
        ici = _ag_ici(g, send, recv, 0)
        fwd = _ag_fwd(g, send, recv, 3 * nb)
        for s, _r in ici:
            s.start()
        for (_s, r), (fs, _fr) in zip(ici, fwd):
            r.wait_recv()
            fs.start()
        for _fs, fr in fwd:
            fr.wait_recv()
        for s, _r in ici + fwd:
            s.wait_send()

    res = _pcall(body, name="allgather_first_layer", in_specs=[_hbm_spec()] * nb, out_specs=[_hbm_spec()] * nb,
                 out_shape=[jax.ShapeDtypeStruct(b.shape, b.dtype) for b in bufs],
                 scratch_shapes=[pltpu.SemaphoreType.DMA((6 * nb,)), pltpu.SemaphoreType.DMA((6 * nb,))],
                 input_output_aliases={i: i for i in range(nb)})(*bufs)
    for k, v in zip(keys, res):
        store[k] = v


def _piece(w, l, idx, name):
    _, R, C = w.shape
    tr = _row_tile(R)

    def body(s_ref, w_ref, o_ref):
        del s_ref
        o_ref[...] = w_ref[...].astype(BF16)

    gs = pltpu.PrefetchScalarGridSpec(
        num_scalar_prefetch=1, grid=(R // tr,),
        in_specs=[pl.BlockSpec((None, tr, C), lambda r, s: (l, r, 0))],
        out_specs=pl.BlockSpec((None, tr, C), lambda r, s: (s[0], r, 0)))
    return _pcall(body, name=name, grid_spec=gs, out_shape=jax.ShapeDtypeStruct((N_CHIPS, R, C), BF16),
                  compiler_params=_cparams(("parallel",)))(idx, w)


def _piece_in(w_in_t, l, idx):
    _, R, D = w_in_t.shape
    tr = HALF_TILE
    nsrc = R // tr

    def body(s_ref, w_ref, o_ref):
        src = pl.program_id(0) - s_ref[1]
        ok = jnp.logical_and(src >= 0, src < nsrc)
        o_ref[...] = jnp.where(ok, w_ref[...], 0.0).astype(BF16)

    gs = pltpu.PrefetchScalarGridSpec(
        num_scalar_prefetch=1, grid=(SHARD_P // tr,),
        in_specs=[pl.BlockSpec((None, tr, D), lambda r, s: (l, jnp.clip(r - s[1], 0, nsrc - 1), 0))],
        out_specs=pl.BlockSpec((None, tr, D), lambda r, s: (s[0], r, 0)))
    return _pcall(body, name="piece_in", grid_spec=gs, out_shape=jax.ShapeDtypeStruct((N_CHIPS, SHARD_P, D), BF16),
                  compiler_params=_cparams(("parallel",)))(idx, w_in_t)


def _rs_pair_add(g5, r4, idx, name):
    J, _, h, C = g5.shape
    th = _row_tile(h)

    def body(s_ref, g_ref, r_ref, o32_ref, o16_ref):
        s = g_ref[...] + r_ref[...]
        o16_ref[...] = s.astype(BF16)

        @pl.when(pl.program_id(1) == s_ref[0])
        def _():
            o32_ref[...] = s

    spec = pl.BlockSpec((None, th, C), lambda r, j, s: (j, r, 0))
    gs = pltpu.PrefetchScalarGridSpec(
        num_scalar_prefetch=1, grid=(h // th, J),
        in_specs=[pl.BlockSpec((None, None, th, C), lambda r, j, s: (j, s[2], r, 0)), spec],
        out_specs=[pl.BlockSpec((th, C), lambda r, j, s: (r, 0)), spec])
    return _pcall(body, name=name, grid_spec=gs,
                  out_shape=[jax.ShapeDtypeStruct((h, C), F32), jax.ShapeDtypeStruct((J, h, C), BF16)],
                  compiler_params=_cparams(("parallel", "arbitrary")))(idx, g5, r4)


def _rs_chip_add(p32, r3, full, l, idx, name):
    h, C = p32.shape
    th = _row_tile(h)

    def body(s_ref, p_ref, r_ref, f_ref, o_ref):
        del s_ref, f_ref
        o_ref[...] = ((p_ref[...] + r_ref[0].astype(F32)) + r_ref[1].astype(F32)) + r_ref[2].astype(F32)

    gs = pltpu.PrefetchScalarGridSpec(
        num_scalar_prefetch=1, grid=(h // th,),
        in_specs=[pl.BlockSpec((th, C), lambda r, s: (r, 0)),
                  pl.BlockSpec((3, th, C), lambda r, s: (0, r, 0)),
                  pl.BlockSpec(memory_space=pl.ANY)],
        out_specs=pl.BlockSpec((None, None, th, C), lambda r, s: (l, s[2], r, 0)))
    return _pcall(body, name=name, grid_spec=gs, out_shape=jax.ShapeDtypeStruct(full.shape, F32),
                  input_output_aliases={3: 0}, compiler_params=_cparams(("parallel",)))(idx, p32, r3, full)


def _adamw_math(w, g, m, v):
    nm = ADAM_B1 * m + (1.0 - ADAM_B1) * g
    nv = ADAM_B2 * v + (1.0 - ADAM_B2) * (g * g)
    m_hat = nm / (1.0 - ADAM_B1 ** ADAM_STEP)
    v_hat = nv / (1.0 - ADAM_B2 ** ADAM_STEP)
    delta = -ADAM_LR * (m_hat / (jnp.sqrt(v_hat) + ADAM_EPS) + ADAM_WD * w)
    return delta, nm, nv


def _adamw(w, g, m, v, idx, name, tr=None):
    L, R, C = w.shape
    Rg = g.shape[1]
    tr = tr or _row_tile(R)
    shift = (Rg - R) // tr
    assert (Rg - R) % tr == 0

    def body(s_ref, w_ref, g_ref, m_ref, v_ref, go_ref, d_ref, nm_ref, nv_ref):
        del s_ref
        gv = g_ref[...]
        d, nm, nv = _adamw_math(w_ref[...], gv, m_ref[...], v_ref[...])
        go_ref[...] = gv
        d_ref[...] = d
        nm_ref[...] = nm
        nv_ref[...] = nv

    wspec = pl.BlockSpec((None, tr, C), lambda l, r, s: (l, r, 0))
    gs = pltpu.PrefetchScalarGridSpec(
        num_scalar_prefetch=1, grid=(L, R // tr),
        in_specs=[wspec, pl.BlockSpec((None, tr, C), lambda l, r, s: (l, r + shift * s[1], 0)), wspec, wspec],
        out_specs=[wspec, wspec, wspec, wspec])
    sds = jax.ShapeDtypeStruct((L, R, C), F32)
    return _pcall(body, name=name, grid_spec=gs, out_shape=[sds, sds, sds, sds],
                  compiler_params=_cparams(("parallel", "parallel")))(idx, w, g, m, v)


def _small_allreduce_adamw(gv, wv, mv, vv):
    NR = gv.shape[0]

    def body(g_ref, w_ref, m_ref, v_ref, go_ref, d_ref, nm_ref, nv_ref, gath, send, recv):
        x, y, c = _mesh_pos()
        rank = 4 * x + 2 * y + c
        gath[rank] = g_ref[...]
        cps = []
        for msk in range(1, N_DEV):
            bx, by, bc = (msk >> 2) & 1, (msk >> 1) & 1, msk & 1
            peer = (1 - x if bx else x, 1 - y if by else y, 1 - c if bc else c)
            cps.append(pltpu.make_async_remote_copy(src_ref=g_ref, dst_ref=gath.at[rank], send_sem=send.at[msk - 1],
                                                    recv_sem=recv.at[msk - 1], device_id=peer, device_id_type=MESH))
        for cp in cps:
            cp.start()
        for msk in range(1, N_DEV):
            bx, by, bc = (msk >> 2) & 1, (msk >> 1) & 1, msk & 1
            peer = (1 - x if bx else x, 1 - y if by else y, 1 - c if bc else c)
            prank = 4 * peer[0] + 2 * peer[1] + peer[2]
            pltpu.make_async_remote_copy(src_ref=g_ref, dst_ref=gath.at[prank], send_sem=send.at[msk - 1],
                                         recv_sem=recv.at[msk - 1], device_id=peer, device_id_type=MESH).wait_recv()
        for cp in cps:
            cp.wait_send()
        tot = gath[0]
        for r in range(1, N_DEV):
            tot = tot + gath[r]
        d, nm, nv = _adamw_math(w_ref[...], tot, m_ref[...], v_ref[...])
        go_ref[...] = tot
        d_ref[...] = d
        nm_ref[...] = nm
        nv_ref[...] = nv

    vm = pl.BlockSpec(memory_space=pltpu.VMEM)
    sds = jax.ShapeDtypeStruct((NR, LANES), F32)
    return _pcall(body, name="small_allreduce_adamw", in_specs=[vm, vm, vm, vm], out_specs=[vm, vm, vm, vm],
                  out_shape=[sds, sds, sds, sds],
                  scratch_shapes=[pltpu.VMEM((N_DEV, NR, LANES), F32), pltpu.SemaphoreType.DMA((N_DEV - 1,)),
                                  pltpu.SemaphoreType.DMA((N_DEV - 1,))],
                  compiler_params=pltpu.CompilerParams(vmem_limit_bytes=VMEM_LIMIT))(gv, wv, mv, vv)


def _split3(a):
    h1 = a.astype(BF16)
    r1 = a - h1.astype(F32)
    h2 = r1.astype(BF16)
    h3 = (r1 - h2.astype(F32)).astype(BF16)
    return h1, h2, h3


def _folded_dist():
    r = jnp.arange(WINDOW)[:, None]
    j = jnp.arange(WINDOW)[None, :]
    return jnp.where(j > r, r + WINDOW - j, r - j)


def _bucket_onehot():
    d = _folded_dist()
    max_exact = REL_BUCKETS // 2
    d_f = jnp.maximum(d, 1).astype(F32)
    large = max_exact + (jnp.log(d_f / max_exact) / math.log(REL_MAX_DIST / max_exact) * (REL_BUCKETS - max_exact)).astype(I32)
    large = jnp.minimum(large, REL_BUCKETS - 1)
    bucket = jnp.where(d < max_exact, d, large)
    oh = bucket[None] == jnp.arange(REL_BUCKETS)[:, None, None]
    return oh.reshape(REL_BUCKETS, WINDOW * WINDOW).astype(BF16)


def _bias_fwd(rel_bias_t, onehot):
    H, N = rel_bias_t.shape[0], onehot.shape[1]
    tn = 4096

    def body(t_ref, oh_ref, o_ref):
        h1, h2, h3 = _split3(t_ref[...])
        oh = oh_ref[...]
        o_ref[...] = (_nn(h1, oh) + _nn(h2, oh)) + _nn(h3, oh)

    return _pcall(body, name="bias_fwd", grid=(N // tn,),
                  in_specs=[pl.BlockSpec((H, REL_BUCKETS), lambda i: (0, 0)), pl.BlockSpec((REL_BUCKETS, tn), lambda i: (0, i))],
                  out_specs=pl.BlockSpec((H, tn), lambda i: (0, i)), out_shape=jax.ShapeDtypeStruct((H, N), F32),
                  compiler_params=_cparams(("parallel",)))(rel_bias_t, onehot)


def _bias_bwd(dbias, onehot):
    H, N = dbias.shape
    tn = 4096

    def body(d_ref, oh_ref, o_ref):
        @pl.when(pl.program_id(0) == 0)
        def _():
            o_ref[...] = jnp.zeros_like(o_ref)
        h1, h2, h3 = _split3(d_ref[...])
        oh = oh_ref[...]
        o_ref[...] += (_nt(h1, oh) + _nt(h2, oh)) + _nt(h3, oh)

    return _pcall(body, name="bias_bwd", grid=(N // tn,),
                  in_specs=[pl.BlockSpec((H, tn), lambda i: (0, i)), pl.BlockSpec((REL_BUCKETS, tn), lambda i: (0, i))],
                  out_specs=pl.BlockSpec((H, REL_BUCKETS), lambda i: (0, 0)), out_shape=jax.ShapeDtypeStruct((H, REL_BUCKETS), F32),
                  compiler_params=_cparams(("arbitrary",)))(dbias, onehot)


def _to_bf16(x):
    S, D = x.shape
    tm = min(S, 512)

    def body(x_ref, o_ref):
        o_ref[...] = x_ref[...].astype(BF16)

    return _pcall(body, name="to_bf16", grid=(S // tm,), in_specs=[pl.BlockSpec((tm, D), lambda i: (i, 0))],
                  out_specs=pl.BlockSpec((tm, D), lambda i: (i, 0)), out_shape=jax.ShapeDtypeStruct((S, D), BF16),
                  compiler_params=_cparams(("parallel",)))(x)


def _inproj(xb, w_t, b_ext, l, job, store):
    S, D = xb.shape
    tm = min(S, 512)

    def body(x_ref, w_ref, b_ref, q_ref, k_ref, v_ref, ga_ref, u_ref, gp_ref):
        xv = x_ref[...]

        def window(j):
            return _nt(xv, w_ref[j]) + b_ref[:, j * SHARD_P:(j + 1) * SHARD_P]

        acc = window(0)
        q_ref[...] = (acc[:, :1024] * QK_SCALE).astype(BF16)
        k_lo = acc[:, 1024:1152]
        acc = window(1)
        k_ref[...] = k_lo + acc[:, 0:128]
        v_ref[...] = acc[:, 128:256]
        ga_ref[:, 0:896] = acc[:, 256:1152].astype(ACT)
        acc = window(2)
        ga_ref[:, 896:1024] = acc[:, 0:128].astype(ACT)
        u_ref[:, 0:896] = acc[:, 128:1024]
        u_lo = acc[:, 1024:1152]
        acc = window(3)
        u_ref[:, 896:1024] = u_lo + acc[:, 0:128]
        gp_ref[...] = acc[:, 128:1152].astype(ACT)

    def ospec(w):
        return pl.BlockSpec((tm, w), lambda i: (i, 0))

    return _call(
        body, name="inproj", grid=(S // tm,),
        in_specs=[pl.BlockSpec((tm, D), lambda i: (i, 0)),
                  pl.BlockSpec((N_CHIPS, SHARD_P, D), lambda i: (0, 0, 0), pipeline_mode=pl.Buffered(1)),
                  pl.BlockSpec((None, 1, EXT), lambda i: (l, 0, 0))],
        out_specs=[ospec(ATTN_W), ospec(KV_W), ospec(KV_W), ospec(ATTN_W), ospec(POOL_W), ospec(POOL_W)],
        out_shape=[jax.ShapeDtypeStruct((S, ATTN_W), BF16), jax.ShapeDtypeStruct((S, KV_W), F32), jax.ShapeDtypeStruct((S, KV_W), F32),
                   jax.ShapeDtypeStruct((S, ATTN_W), ACT), jax.ShapeDtypeStruct((S, POOL_W), F32), jax.ShapeDtypeStruct((S, POOL_W), ACT)],
        args=(xb, w_t, b_ext), sem=("parallel",), job=job, store=store)


def _matmul_nn(a, b4, name, job, store):
    S, K = a.shape
    N = b4.shape[2]
    kq = b4.shape[1]
    tm, tn = min(S, 512), min(N, 1024)

    def body(a_ref, b_ref, o_ref):
        acc = _nn(a_ref[:, 0:kq], b_ref[0])
        for j in range(1, N_CHIPS):
            acc = acc + _nn(a_ref[:, j * kq:(j + 1) * kq], b_ref[j])
        o_ref[...] = acc.astype(ACT)

    return _call(body, name=name, grid=(N // tn, S // tm),
                 in_specs=[pl.BlockSpec((tm, K), lambda n, i: (i, 0)), pl.BlockSpec((N_CHIPS, kq, tn), lambda n, i: (0, 0, n))],
                 out_specs=[pl.BlockSpec((tm, tn), lambda n, i: (i, n))], out_shape=[jax.ShapeDtypeStruct((S, N), ACT)],
                 args=(a, b4), sem=("parallel", "parallel"), job=job, store=store)[0]


def _masked_bias(bias):
    r = jnp.arange(WINDOW)[:, None]
    j = jnp.arange(WINDOW)[None, :]
    return jnp.stack([jnp.where(j > r, -1e30, bias), bias])


def _fold(full, tri):
    return jnp.where(tri, full[:, :WINDOW], full[:, WINDOW:])


def _unfold(folded, tri):
    return jnp.concatenate([jnp.where(tri, folded, 0.0).astype(BF16), jnp.where(tri, 0.0, folded).astype(BF16)], axis=1)


def _dup_heads(src_ref, dst_ref):
    a = src_ref[...]
    r = pltpu.roll(a, HEAD_DIM, axis=1)
    lo = lax.broadcasted_iota(I32, a.shape, 1) < HEAD_DIM
    dst_ref[0] = jnp.where(lo, a, r).astype(BF16)
    dst_ref[1] = jnp.where(lo, r, a).astype(BF16)


def _kv_block(ref, h, prev, cur):
    return jnp.concatenate([ref[h, pl.ds(prev, WINDOW), :], ref[h, pl.ds(cur, WINDOW), :]], axis=0)


def _rows(i):
    return slice(i * WINDOW, (i + 1) * WINDOW)


def _attn_fwd(q, k, v, ga, bias, sinks, l, job, store):
    S = q.shape[0]
    nblk = S // WINDOW

    def body(sink_ref, q_ref, k_ref, v_ref, ga_ref, bias_ref, o_ref, ca_ref, lse_ref, kd, vd, qs, ss, ps, os_):
        n = pl.program_id(0)

        @pl.when(n == 0)
        def _():
            _dup_heads(k_ref, kd)
            _dup_heads(v_ref, vd)

        cur = pl.multiple_of(n * WINDOW, WINDOW)
        prev = pl.multiple_of(jnp.maximum(n - 1, 0) * WINDOW, WINDOW)
        lane = lax.broadcasted_iota(I32, (WINDOW, LANES), 1)
        lo = lane < HEAD_DIM
        tri = lane > lax.broadcasted_iota(I32, (WINDOW, LANES), 0)
        lse_mat = jnp.zeros((WINDOW, LANES), F32)
        for g in range(N_KV):
            for i in range(KV_RATIO):
                h = g * KV_RATIO + i
                qp = q_ref[:, LANES * (h // 2):LANES * (h // 2 + 1)]
                qs[_rows(i), :] = jnp.where(lo if h % 2 == 0 else jnp.logical_not(lo), qp, jnp.zeros_like(qp))
            ss[...] = _nt(qs[...], _kv_block(kd, g, prev, cur))
            for i in range(KV_RATIO):
                h = g * KV_RATIO + i
                s = _fold(ss[_rows(i), :], tri) + bias_ref[h]
                sink = sink_ref[l, h]
                m = jnp.maximum(jnp.max(s, axis=1, keepdims=True), sink)
                e = jnp.exp(s - m)
                den = jnp.sum(e, axis=1, keepdims=True) + jnp.exp(sink - m)
                ps[_rows(i), :] = _unfold(e * (1.0 / den), tri)
                lse_mat = jnp.where(lane == h, m + jnp.log(den), lse_mat)
            os_[...] = _nn(ps[...], _kv_block(vd, g, prev, cur))
            for j in range(KV_RATIO // 2):
                pair = g * (KV_RATIO // 2) + j
                sl = slice(LANES * pair, LANES * (pair + 1))
                o_pair = jnp.where(lo, os_[_rows(2 * j), :], os_[_rows(2 * j + 1), :])
                o_ref[:, sl] = o_pair.astype(ACT)
                gav = ga_ref[:, sl].astype(F32)
                ca_ref[:, sl] = (o_pair * (gav * _sigmoid(gav))).astype(BF16)
        lse_ref[...] = lse_mat

    blk = pl.BlockSpec((WINDOW, ATTN_W), lambda n: (n, 0))
    full_kv = pl.BlockSpec((S, KV_W), lambda n: (0, 0))
    stack = KV_RATIO * WINDOW
    return _call(
        body, name="attn_fwd", grid=(nblk,),
        in_specs=[pl.BlockSpec(memory_space=pltpu.SMEM), blk, full_kv, full_kv, blk,
                  pl.BlockSpec((None, N_HEADS, WINDOW, WINDOW), lambda n: (jnp.minimum(n, 1), 0, 0, 0))],
        out_specs=[blk, blk, pl.BlockSpec((WINDOW, LANES), lambda n: (n, 0))],
        out_shape=[jax.ShapeDtypeStruct((S, ATTN_W), ACT), jax.ShapeDtypeStruct((S, ATTN_W), BF16), jax.ShapeDtypeStruct((S, LANES), F32)],
        scratch_shapes=[pltpu.VMEM((N_KV, S, LANES), BF16), pltpu.VMEM((N_KV, S, LANES), BF16),
                        pltpu.VMEM((stack, LANES), BF16), pltpu.VMEM((stack, 2 * WINDOW), F32),
                        pltpu.VMEM((stack, 2 * WINDOW), BF16), pltpu.VMEM((stack, LANES), F32)],
        args=(sinks, q, k, v, ga, bias), sem=("arbitrary",), job=job, store=store)


def _pool_diff(u, halo, tile_index, tm):
    gw = POOL_W // len(POOL_WINDOWS)
    xh = jnp.concatenate([halo, u], axis=0)
    sums = []
    s = xh
    for step in (1, 2, 4, 8):
        s = s + pltpu.roll(s, step, axis=0)
        sums.append(s)
    t = tile_index * tm + lax.broadcasted_iota(I32, (tm, gw), 0)
    diffs = []
    for g, w in enumerate(POOL_WINDOWS):
        cols = slice(g * gw, (g + 1) * gw)
        cnt = jnp.minimum(t + 1, w).astype(F32)
        diffs.append(sums[g][POOL_HALO:, cols] / cnt - u[:, cols])
    return diffs


def _pool_weight(wp_ref, g):
    r = wp_ref.shape[1] // len(POOL_WINDOWS)
    return jnp.concatenate([wp_ref[j, g * r:(g + 1) * r, :] for j in range(N_CHIPS)], axis=0)


def _pool_fwd(u, gp, ca, w_pool, ps, l, job, store):
    S = u.shape[0]
    tm = min(S, 256)
    hb = tm // POOL_HALO
    gw = POOL_W // len(POOL_WINDOWS)

    def body(u_ref, uh_ref, gp_ref, ca_ref, wp_ref, ps_ref, c_ref):
        i = pl.program_id(0)
        uv = u_ref[...]
        halo = jnp.where(i > 0, uh_ref[...], 0.0)
        diffs = _pool_diff(uv, halo, i, tm)
        c_ref[:, 0:ATTN_W] = ca_ref[...]
        for g in range(len(POOL_WINDOWS)):
            cols = slice(g * gw, (g + 1) * gw)
            mm = _nn(diffs[g].astype(BF16), _pool_weight(wp_ref, g))
            gpv = gp_ref[:, cols].astype(F32)
            b = (mm * ps_ref[:, cols]) * (gpv * _sigmoid(gpv))
            c_ref[:, ATTN_W + g * gw:ATTN_W + (g + 1) * gw] = b.astype(BF16)

    row = pl.BlockSpec((tm, POOL_W), lambda i: (i, 0))
    return _call(
        body, name="pool_fwd", grid=(S // tm,),
        in_specs=[row, pl.BlockSpec((POOL_HALO, POOL_W), lambda i: (jnp.maximum(i * hb - 1, 0), 0)), row, row,
                  pl.BlockSpec(w_pool.shape, lambda i: (0, 0, 0)),
                  pl.BlockSpec((None, 1, POOL_W), lambda i: (l, 0, 0))],
        out_specs=[pl.BlockSpec((tm, ATTN_W + POOL_W), lambda i: (i, 0))],
        out_shape=[jax.ShapeDtypeStruct((S, ATTN_W + POOL_W), BF16)],
        args=(u, u, gp, ca, w_pool, ps), sem=("parallel",), job=job, store=store)[0]


def _ple_embed(p_ref, wple_ref):
    pb = p_ref[...].astype(BF16)
    return jnp.concatenate([_nn(pb, wple_ref[j]) for j in range(N_CHIPS)], axis=1)


def _outproj_ln(c, xh_in, gain_in, bias_in, l_in, gl, p, w_out, w_ple, gain, bias, l, alpha, job, store):
    S, D = xh_in.shape
    tm = min(S, 256)
    kq = D // N_CHIPS

    def body(c_ref, x_ref, gi_ref, bi_ref, gl_ref, p_ref, wo_ref, wp_ref, gain_ref, bias_ref, yb_ref, xh_ref, rs_ref):
        mix = _nn(c_ref[:, 0:kq], wo_ref[0])
        for j in range(1, N_CHIPS):
            mix = mix + _nn(c_ref[:, j * kq:(j + 1) * kq], wo_ref[j])
        ple = _sigmoid(gl_ref[...].astype(F32)) * _ple_embed(p_ref, wp_ref)
        x = x_ref[...] * gi_ref[...] + bi_ref[...]
        z = (alpha * x + mix) + ple
        mu = jnp.mean(z, axis=1, keepdims=True)
        zc = z - mu
        var = jnp.mean(zc * zc, axis=1, keepdims=True)
        rstd = lax.rsqrt(var + LN_EPS)
        xhat = zc * rstd
        yb_ref[...] = (xhat * gain_ref[...] + bias_ref[...]).astype(BF16)
        xh_ref[...] = xhat
        rs_ref[...] = rstd

    row = pl.BlockSpec((tm, D), lambda i: (i, 0))
    vec = pl.BlockSpec((None, 1, D), lambda i: (l, 0, 0))
    vec_in = pl.BlockSpec((None, 1, D), lambda i: (l_in, 0, 0))
    return _call(
        body, name="outproj_ln", grid=(S // tm,),
        in_specs=[row, row, vec_in, vec_in, row, pl.BlockSpec((tm, p.shape[1]), lambda i: (i, 0)),
                  pl.BlockSpec(w_out.shape, lambda i: (0, 0, 0)), pl.BlockSpec(w_ple.shape, lambda i: (0, 0, 0)), vec, vec],
        out_specs=[row, row, pl.BlockSpec((tm, 1), lambda i: (i, 0))],
        out_shape=[jax.ShapeDtypeStruct((S, D), BF16), jax.ShapeDtypeStruct((S, D), F32), jax.ShapeDtypeStruct((S, 1), F32)],
        args=(c, xh_in, gain_in, bias_in, gl, p, w_out, w_ple, gain, bias), sem=("parallel",), job=job, store=store)


def _loss_and_grad(xhat, gain, bias, l, target):
    S, D = xhat.shape
    tm = min(S, 512)

    def body(xh_ref, g_ref, b_ref, t_ref, dy_ref, acc_ref):
        @pl.when(pl.program_id(0) == 0)
        def _():
            acc_ref[...] = jnp.zeros_like(acc_ref)
        d = (xh_ref[...] * g_ref[...] + b_ref[...]) - t_ref[...]
        dy_ref[...] = d * (1.0 / D)
        acc_ref[...] += jnp.sum(jnp.mean(d * d, axis=1, keepdims=True), axis=0, keepdims=True)

    row = pl.BlockSpec((tm, D), lambda i: (i, 0))
    vec = pl.BlockSpec((None, 1, D), lambda i: (l, 0, 0))
    return _pcall(body, name="loss", grid=(S // tm,), in_specs=[row, vec, vec, row],
                  out_specs=[row, pl.BlockSpec((8, LANES), lambda i: (0, 0))],
                  out_shape=[jax.ShapeDtypeStruct((S, D), F32), jax.ShapeDtypeStruct((8, LANES), F32)],
                  compiler_params=_cparams(("arbitrary",)))(xhat, gain, bias, target)


def _ln_bwd(dy, xhat, rstd, gl, p, w_ple, w_out, gain, l, job, store):
    S, D = dy.shape
    tm = min(S, 256)

    nq = w_out.shape[1]

    def body(dy_ref, xh_ref, rs_ref, gl_ref, p_ref, wp_ref, wo_ref, gain_ref, dz_ref, dzb_ref, de_ref, dgl_ref, dc_ref, gg_ref, gb_ref):
        @pl.when(pl.program_id(0) == 0)
        def _():
            gg_ref[...] = jnp.zeros_like(gg_ref)
            gb_ref[...] = jnp.zeros_like(gb_ref)
        dyv = dy_ref[...]
        xh = xh_ref[...]
        dxh = dyv * gain_ref[...]
        m1 = jnp.mean(dxh, axis=1, keepdims=True)
        m2 = jnp.mean(dxh * xh, axis=1, keepdims=True)
        dz = rs_ref[...] * ((dxh - m1) - xh * m2)
        gg_ref[...] += jnp.sum(dyv * xh, axis=0, keepdims=True)
        gb_ref[...] += jnp.sum(dyv, axis=0, keepdims=True)
        sg = _sigmoid(gl_ref[...].astype(F32))
        e = _ple_embed(p_ref, wp_ref)
        dzb = dz.astype(BF16)
        dz_ref[...] = dz
        dzb_ref[...] = dzb
        de_ref[...] = (dz * sg).astype(BF16)
        dgl_ref[...] = ((dz * e) * (sg * (1.0 - sg))).astype(BF16)
        for j in range(N_CHIPS):
            dc_ref[:, j * nq:(j + 1) * nq] = _nt(dzb, wo_ref[j]).astype(ACT)

    row = pl.BlockSpec((tm, D), lambda i: (i, 0))
    vec_in = pl.BlockSpec((None, 1, D), lambda i: (l, 0, 0))
    vec_out = pl.BlockSpec((1, D), lambda i: (0, 0))
    bsd = jax.ShapeDtypeStruct((S, D), BF16)
    fsd = jax.ShapeDtypeStruct((S, D), F32)
    return _call(
        body, name="ln_bwd", grid=(S // tm,),
        in_specs=[row, row, pl.BlockSpec((tm, 1), lambda i: (i, 0)), row, pl.BlockSpec((tm, p.shape[1]), lambda i: (i, 0)),
                  pl.BlockSpec(w_ple.shape, lambda i: (0, 0, 0), pipeline_mode=pl.Buffered(1)),
                  pl.BlockSpec(w_out.shape, lambda i: (0, 0, 0), pipeline_mode=pl.Buffered(1)), vec_in],
        out_specs=[row, row, row, row, row, vec_out, vec_out],
        out_shape=[fsd, bsd, bsd, bsd, jax.ShapeDtypeStruct((S, D), ACT), jax.ShapeDtypeStruct((1, D), F32), jax.ShapeDtypeStruct((1, D), F32)],
        args=(dy, xhat, rstd, gl, p, w_ple, w_out, gain), sem=("arbitrary",), job=job, store=store)


def _matmul_nn_acc(a, b4, name, add, add_scale, job, store):
    S = a.shape[0]
    KS, tk, N = b4.shape
    tm, tn = min(S, 512), min(N, 1024)

    def body(a_ref, b_ref, add_ref, o_ref):
        acc = add_scale * add_ref[...]
        for k in range(KS):
            acc = acc + _nn(a_ref[:, k * tk:(k + 1) * tk], b_ref[k])
        o_ref[...] = acc

    return _call(body, name=name, grid=(N // tn, S // tm),
                 in_specs=[pl.BlockSpec((tm, KS * tk), lambda n, i: (i, 0)),
                           pl.BlockSpec((KS, tk, tn), lambda n, i: (0, 0, n)),
                           pl.BlockSpec((tm, tn), lambda n, i: (i, n))],
                 out_specs=[pl.BlockSpec((tm, tn), lambda n, i: (i, n))], out_shape=[jax.ShapeDtypeStruct((S, N), F32)],
                 args=(a, b4, add), sem=("parallel", "parallel"), job=job, store=store)[0]


def _matmul_nt_rows(a, b4, name, add=None, add_scale=1.0, job=None, store=None):
    S, K = a.shape
    nq = b4.shape[1]
    tm = min(S, 512)
    out_spec = pl.BlockSpec((tm, N_CHIPS * nq), lambda i: (i, 0))
    in_specs = [pl.BlockSpec((tm, K), lambda i: (i, 0)), pl.BlockSpec(b4.shape, lambda i: (0, 0, 0))]
    if add is None:
        def body(a_ref, b_ref, o_ref):
            av = a_ref[...]
            for j in range(N_CHIPS):
                o_ref[:, j * nq:(j + 1) * nq] = _nt(av, b_ref[j])
        args = (a, b4)
    else:
        def body(a_ref, b_ref, add_ref, o_ref):
            av = a_ref[...]
            for j in range(N_CHIPS):
                cols = slice(j * nq, (j + 1) * nq)
                o_ref[:, cols] = _nt(av, b_ref[j]) + add_scale * add_ref[:, cols]
        in_specs.append(out_spec)
        args = (a, b4, add)

    return _call(body, name=name, grid=(S // tm,), in_specs=in_specs,
                 out_specs=[out_spec], out_shape=[jax.ShapeDtypeStruct((S, N_CHIPS * nq), F32)],
                 args=args, sem=("parallel",), job=job, store=store)[0]


def _matmul_tn(a, b, R, C, name, by_rows, job=None, store=None):
    S = a.shape[0]
    tn = min(C, 1024 if R <= 512 else 512)
    nt = C // tn

    def body(a_ref, b_ref, o_ref, at_ref):
        @pl.when(pl.program_id(1) == 0)
        def _():
            at_ref[...] = a_ref[...].T
        o_ref[...] = _nn(at_ref[...], b_ref[...])

    if by_rows:
        a_spec = pl.BlockSpec((S, R), lambda j, n: (0, j))
        b_spec = pl.BlockSpec((S, tn), lambda j, n: (0, n))
    else:
        a_spec = pl.BlockSpec((S, R), lambda j, n: (0, 0))
        b_spec = pl.BlockSpec((S, tn), lambda j, n: (0, j * nt + n))
    return _call(body, name=name, grid=(N_CHIPS, nt), in_specs=[a_spec, b_spec],
                 out_specs=[pl.BlockSpec((None, R, tn), lambda j, n: (j, 0, n))],
                 out_shape=[jax.ShapeDtypeStruct((N_CHIPS, R, C), F32)],
                 scratch_shapes=[pltpu.VMEM((R, S), BF16)],
                 args=(a, b), sem=("parallel", "arbitrary"), job=job, store=store)[0]


def _pool_bwd(u, gp, dc, w_pool, ps, l, job=None, store=None):
    S = u.shape[0]
    tm = min(S, 256)
    hb = tm // POOL_HALO
    ngrp = len(POOL_WINDOWS)
    gw = POOL_W // ngrp
    rr = gw // N_CHIPS

    def body(u_ref, uh_ref, gp_ref, dc_ref, wp_ref, ps_ref, dd_ref, dgp_ref, gps_ref, gwp_ref):
        i = pl.program_id(0)

        @pl.when(i == 0)
        def _():
            gps_ref[...] = jnp.zeros_like(gps_ref)
            gwp_ref[...] = jnp.zeros_like(gwp_ref)
        uv = u_ref[...]
        halo = jnp.where(i > 0, uh_ref[...], 0.0)
        diffs = _pool_diff(uv, halo, i, tm)
        for g in range(ngrp):
            cols = slice(g * gw, (g + 1) * gw)
            w = _pool_weight(wp_ref, g)
            db = diffs[g].astype(BF16)
            mm = _nn(db, w)
            gpv = gp_ref[:, cols].astype(F32)
            sg = _sigmoid(gpv)
            si = gpv * sg
            dsi = sg * (1.0 + gpv * (1.0 - sg))
            dcb = dc_ref[:, cols].astype(F32)
            psv = ps_ref[:, cols]
            d_mm = (dcb * si) * psv
            gps_ref[:, cols] += jnp.sum((dcb * si) * mm, axis=0, keepdims=True)
            dgp_ref[:, cols] = ((dcb * (mm * psv)) * dsi).astype(ACT)
            d_mmb = d_mm.astype(BF16)
            dd_ref[:, cols] = _nt(d_mmb, w)
            gwt = _tn(db, d_mmb)
            for j in range(N_CHIPS):
                gwp_ref[j, g * rr:(g + 1) * rr, :] += gwt[j * rr:(j + 1) * rr, :]

    row = pl.BlockSpec((tm, POOL_W), lambda i: (i, 0))
    wspec = pl.BlockSpec(w_pool.shape, lambda i: (0, 0, 0))
    return _call(
        body, name="pool_bwd", grid=(S // tm,),
        in_specs=[row, pl.BlockSpec((POOL_HALO, POOL_W), lambda i: (jnp.maximum(i * hb - 1, 0), 0)), row,
                  pl.BlockSpec((tm, POOL_W), lambda i: (i, 1)), wspec, pl.BlockSpec((None, 1, POOL_W), lambda i: (l, 0, 0))],
        out_specs=[row, row, pl.BlockSpec((1, POOL_W), lambda i: (0, 0)), wspec],
        out_shape=[jax.ShapeDtypeStruct((S, POOL_W), F32), jax.ShapeDtypeStruct((S, POOL_W), ACT),
                   jax.ShapeDtypeStruct((1, POOL_W), F32), jax.ShapeDtypeStruct(w_pool.shape, F32)],
        args=(u, u, gp, dc, w_pool, ps), sem=("arbitrary",), job=job, store=store)


def _pool_window_t(dd, halo_next, tile_index, tm):
    gw = POOL_W // len(POOL_WINDOWS)
    n = tm + POOL_HALO
    t = tile_index * tm + lax.broadcasted_iota(I32, (n, gw), 0)
    xh = jnp.concatenate([dd, halo_next], axis=0)
    outs = []
    for g, w in enumerate(POOL_WINDOWS):
        cols = slice(g * gw, (g + 1) * gw)
        cnt = jnp.minimum(t + 1, w).astype(F32)
        s = xh[:, cols] / cnt
        step = 1
        while step < w:
            s = s + pltpu.roll(s, n - step, axis=0)
            step *= 2
        outs.append(s[:tm] - dd[:, cols])
    return outs


def _assemble_dh(dq, dk, dv, dga, dd, dgp):
    S = dq.shape[0]
    tm = min(S, 256)
    hb = tm // POOL_HALO
    nt = S // tm

    def body(dq_ref, dk_ref, dv_ref, dga_ref, dd_ref, ddn_ref, dgp_ref, dh_ref, gb_ref):
        i = pl.program_id(0)

        @pl.when(i == 0)
        def _():
            gb_ref[...] = jnp.zeros_like(gb_ref)
        halo = jnp.where(i < nt - 1, ddn_ref[...], 0.0)
        du = jnp.concatenate(_pool_window_t(dd_ref[...], halo, i, tm), axis=1)
        dkv = dk_ref[...]
        dgav = dga_ref[...].astype(F32)
        parts = [(OFF_Q, dq_ref[...].astype(F32)), (OFF_KA, dkv), (OFF_KB, dkv), (OFF_V, dv_ref[...]), (OFF_GA, dgav),
                 (OFF_U, du[:, 0:896]), (OFF_UA, du[:, 896:1024]), (OFF_UB, du[:, 896:1024]), (OFF_GP, dgp_ref[...].astype(F32))]
        for off, val in parts:
            w = val.shape[1]
            dh_ref[:, off:off + w] = val.astype(BF16)
            gb_ref[:, off:off + w] += jnp.sum(val, axis=0, keepdims=True)

    def row(w):
        return pl.BlockSpec((tm, w), lambda i: (i, 0))

    return _pcall(
        body, name="assemble_dh", grid=(nt,),
        in_specs=[row(ATTN_W), row(KV_W), row(KV_W), row(ATTN_W), row(POOL_W),
                  pl.BlockSpec((POOL_HALO, POOL_W), lambda i: (jnp.minimum((i + 1) * hb, S // POOL_HALO - 1), 0)), row(POOL_W)],
        out_specs=[row(EXT), pl.BlockSpec((1, EXT), lambda i: (0, 0))],
        out_shape=[jax.ShapeDtypeStruct((S, EXT), BF16), jax.ShapeDtypeStruct((1, EXT), F32)],
        compiler_params=_cparams(("arbitrary",)),
    )(dq, dk, dv, dga, dd, dd, dgp)


def _attn_bwd(q, k, v, ga, o, dc, lse, bias, sinks, dbias_in, l, job, store):
    S = q.shape[0]
    nblk = S // WINDOW

    def body(sink_ref, q_ref, k_ref, v_ref, ga_ref, o_ref, dc_ref, lse_ref, bias_ref, dbin_ref,
             dq_ref, dga_ref, dk_ref, dv_ref, db_ref, ds_ref, kd, vd, qs, dos, dls, ss, dps, dss, pss, dqs):
        n = pl.program_id(0)

        @pl.when(n == 0)
        def _():
            _dup_heads(k_ref, kd)
            _dup_heads(v_ref, vd)
            dk_ref[...] = jnp.zeros_like(dk_ref)
            dv_ref[...] = jnp.zeros_like(dv_ref)
            db_ref[...] = dbin_ref[...]
            ds_ref[...] = jnp.zeros_like(ds_ref)

        cur = pl.multiple_of(n * WINDOW, WINDOW)
        prev = pl.multiple_of(jnp.maximum(n - 1, 0) * WINDOW, WINDOW)
        lane = lax.broadcasted_iota(I32, (WINDOW, LANES), 1)
        lane8 = lax.broadcasted_iota(I32, (8, LANES), 1)
        lo = lane < HEAD_DIM
        tri = lane > lax.broadcasted_iota(I32, (WINDOW, LANES), 0)
        lse_t = lse_ref[...]
        dk_t, dv_t = [], []
        dsk = jnp.zeros((8, LANES), F32)
        for g in range(N_KV):
            kb = _kv_block(kd, g, prev, cur)
            vb = _kv_block(vd, g, prev, cur)
            for j in range(KV_RATIO // 2):
                pair = g * (KV_RATIO // 2) + j
                sl = slice(LANES * pair, LANES * (pair + 1))
                qp = q_ref[:, sl]
                op = o_ref[:, sl].astype(F32)
                dcp = dc_ref[:, sl].astype(F32)
                gav = ga_ref[:, sl].astype(F32)
                sg = _sigmoid(gav)
                d_o = dcp * (gav * sg)
                dga_ref[:, sl] = ((dcp * op) * (sg * (1.0 + gav * (1.0 - sg)))).astype(ACT)
                prod = d_o * op
                for par in range(2):
                    msk = lo if par == 0 else jnp.logical_not(lo)
                    rows = _rows(2 * j + par)
                    qs[rows, :] = jnp.where(msk, qp, jnp.zeros_like(qp))
                    dos[rows, :] = jnp.where(msk, d_o, 0.0).astype(BF16)
                    delta = jnp.sum(jnp.where(msk, prod, 0.0), axis=1, keepdims=True)
                    dls[rows, :] = jnp.broadcast_to(delta, (WINDOW, LANES))
            ss[...] = _nt(qs[...], kb)
            dps[...] = _nt(dos[...], vb)
            for i in range(KV_RATIO):
                h = g * KV_RATIO + i
                rows = _rows(i)
                lse_h = jnp.sum(jnp.where(lane == h, lse_t, 0.0), axis=1, keepdims=True)
                p = jnp.exp(_fold(ss[rows, :], tri) + bias_ref[h] - lse_h)
                delta = dls[rows, :]
                dsc = p * (_fold(dps[rows, :], tri) - delta)
                db_ref[h] += dsc
                psink = jnp.exp(sink_ref[l, h] - lse_h)
                dsk = dsk + jnp.where(lane8 == h, -jnp.sum(psink * delta, axis=0, keepdims=True), 0.0)
                dss[rows, :] = _unfold(dsc, tri)
                pss[rows, :] = _unfold(p, tri)
            dqs[...] = _nn(dss[...], kb) * QK_SCALE
            for j in range(KV_RATIO // 2):
                pair = g * (KV_RATIO // 2) + j
                dq_ref[:, LANES * pair:LANES * (pair + 1)] = jnp.where(lo, dqs[_rows(2 * j), :], dqs[_rows(2 * j + 1), :]).astype(ACT)
            dk_t.append(_tn(qs[...], dss[...]))
            dv_t.append(_tn(dos[...], pss[...]))

        def untranspose(acc):
            return jnp.concatenate([a[:HEAD_DIM] + a[HEAD_DIM:] for a in acc], axis=0).T

        dkb = untranspose(dk_t)
        dvb = untranspose(dv_t)
        dk_ref[pl.ds(prev, WINDOW), :] += dkb[:WINDOW]
        dk_ref[pl.ds(cur, WINDOW), :] += dkb[WINDOW:]
        dv_ref[pl.ds(prev, WINDOW), :] += dvb[:WINDOW]
        dv_ref[pl.ds(cur, WINDOW), :] += dvb[WINDOW:]
        ds_ref[...] += dsk

    blk = pl.BlockSpec((WINDOW, ATTN_W), lambda n: (n, 0))
    full_kv = pl.BlockSpec((S, KV_W), lambda n: (0, 0))
    full_b = pl.BlockSpec((N_HEADS, WINDOW, WINDOW), lambda n: (0, 0, 0))
    stack = KV_RATIO * WINDOW
    return _call(
        body, name="attn_bwd", grid=(nblk,),
        in_specs=[pl.BlockSpec(memory_space=pltpu.SMEM), blk, full_kv, full_kv, blk, blk, blk,
                  pl.BlockSpec((WINDOW, LANES), lambda n: (n, 0)),
                  pl.BlockSpec((None, N_HEADS, WINDOW, WINDOW), lambda n: (jnp.minimum(n, 1), 0, 0, 0)), full_b],
        out_specs=[blk, blk, full_kv, full_kv, full_b, pl.BlockSpec((8, LANES), lambda n: (0, 0))],
        out_shape=[jax.ShapeDtypeStruct((S, ATTN_W), ACT), jax.ShapeDtypeStruct((S, ATTN_W), ACT), jax.ShapeDtypeStruct((S, KV_W), F32),
                   jax.ShapeDtypeStruct((S, KV_W), F32), jax.ShapeDtypeStruct((N_HEADS, WINDOW, WINDOW), F32),
                   jax.ShapeDtypeStruct((8, LANES), F32)],
        scratch_shapes=[pltpu.VMEM((N_KV, S, LANES), BF16), pltpu.VMEM((N_KV, S, LANES), BF16),
                        pltpu.VMEM((stack, LANES), BF16), pltpu.VMEM((stack, LANES), BF16), pltpu.VMEM((stack, LANES), F32),
                        pltpu.VMEM((stack, 2 * WINDOW), F32), pltpu.VMEM((stack, 2 * WINDOW), F32),
                        pltpu.VMEM((stack, 2 * WINDOW), BF16), pltpu.VMEM((stack, 2 * WINDOW), BF16),
                        pltpu.VMEM((stack, LANES), F32)],
        args=(sinks, q, k, v, ga, o, dc, lse, bias, dbias_in), sem=("arbitrary",), job=job, store=store)


def _pack_small(arrs):
    flat = []
    for a in arrs:
        v = a.reshape(-1)
        flat.append(jnp.pad(v, (0, (-v.shape[0]) % LANES)))
    v = jnp.concatenate(flat)
    v = jnp.pad(v, (0, (-v.shape[0]) % (8 * LANES)))
    return v.reshape(-1, LANES)


def _unpack_small(packed, shapes):
    v = packed.reshape(-1)
    outs, off = [], 0
    for shp in shapes:
        n = math.prod(shp)
        outs.append(v[off:off + n].reshape(shp))
        off += n + (-n) % LANES
    return outs


def _bias_to_ext(b):
    L = b.shape[0]
    z = jnp.zeros((L, HALF_TILE), b.dtype)
    parts = []
    for j in range(N_CHIPS):
        seg = b[:, j * SHARD:(j + 1) * SHARD]
        parts += [z, seg] if j % 2 else [seg, z]
    return jnp.concatenate(parts, axis=1).reshape(L, 1, EXT)


def _bias_from_ext(g):
    parts = []
    for j in range(N_CHIPS):
        o = j * SHARD_P + (HALF_TILE if j % 2 else 0)
        parts.append(g[:, o:o + SHARD])
    return jnp.concatenate(parts, axis=1)


def kernel(x, p, w_in, b_in, w_out, attn_sinks, rel_bias, w_pool, pool_scale, w_ple, w_gate_ple, ln_gain, ln_bias, loss_target, m_w_in, m_b_in, m_w_out, m_attn_sinks, m_rel_bias, m_w_pool, m_pool_scale, m_w_ple, m_w_gate_ple, m_ln_gain, m_ln_bias, v_w_in, v_b_in, v_w_out, v_attn_sinks, v_rel_bias, v_w_pool, v_pool_scale, v_w_ple, v_w_gate_ple, v_ln_gain, v_ln_bias):
    L = w_in.shape[0]
    S, D = x.shape[1], x.shape[2]
    assert D == D_MODEL and w_in.shape[2] == SHARD and S % WINDOW == 0
    alpha = (2.0 * L) ** 0.25
    xc, yc, cc = _mesh_pos()
    idx = jnp.stack([2 * xc + yc, yc, cc]).astype(I32)
    store = {}

    def wkeys(l, names):
        return [("w", l, t) for t in names]

    def weight(l, t):
        return _unhalves(store["w", l, t])

    w_in_t = jnp.swapaxes(w_in, 1, 2)
    w_pool2 = w_pool.reshape(L, w_pool.shape[1] * w_pool.shape[2], w_pool.shape[3])
    for l in range(L):
        store["w", l, "in"] = _halves(_piece_in(w_in_t, l, idx))
        store["w", l, "out"] = _halves(_piece(w_out, l, idx, "piece_out"))
        store["w", l, "gate"] = _halves(_piece(w_gate_ple, l, idx, "piece_gate"))
        store["w", l, "ple"] = _halves(_piece(w_ple, l, idx, "piece_ple"))
        store["w", l, "pool"] = _halves(_piece(w_pool2, l, idx, "piece_pool"))
    _allgather_now(wkeys(0, WEIGHTS), store)

    b_ext = _bias_to_ext(b_in)
    ps3 = pool_scale.reshape(L, 1, POOL_W)
    gain3 = ln_gain.reshape(L, 1, D)
    bias3 = ln_bias.reshape(L, 1, D)
    onehot = _bucket_onehot()
    bias_hqk = _masked_bias(_bias_fwd(rel_bias.T, onehot).reshape(N_HEADS, WINDOW, WINDOW))

    xs = x[0]
    xb = _to_bf16(xs)
    pb = _to_bf16(p.reshape(L * S, p.shape[3])).reshape(L, S, p.shape[3])
    saved = []
    x_norm, x_gain, x_bias, x_l = xs, jnp.ones((1, 1, D), F32), jnp.zeros((1, 1, D), F32), 0
    late = ["out", "ple", "pool"]
    for l in range(L):
        nxt = l + 1 < L
        job = _Job()
        if nxt:
            job.add(_ag_ici, wkeys(l + 1, ["in"]), 3)
        if l >= 1:
            job.add(_ag_fwd, wkeys(l, ["gate"]), 3)
        q, k, v, ga, u, gp = _inproj(xb, weight(l, "in"), b_ext, l, job, store)
        gl = _matmul_nn(xb, weight(l, "gate"), "gate_logits", None, None)
        job = _Job().add(_ag_fwd, wkeys(l + 1, ["in"]), 3).add(_ag_ici, wkeys(l + 1, late), 9) if nxt else None
        o, ca, lse = _attn_fwd(q, k, v, ga, bias_hqk, attn_sinks, l, job, store)
        c = _pool_fwd(u, gp, ca, weight(l, "pool"), ps3, l, None, None)
        job = _Job().add(_ag_fwd, wkeys(l + 1, late), 9).add(_ag_ici, wkeys(l + 1, ["gate"]), 3) if nxt else None
        yb, xhat, rstd = _outproj_ln(c, x_norm, x_gain, x_bias, x_l, gl, pb[l], weight(l, "out"), weight(l, "ple"), gain3, bias3, l,
                                     alpha, job, store)
        saved.append(dict(xb=xb, q=q, k=k, v=v, ga=ga, u=u, gp=gp, gl=gl, o=o, lse=lse, c=c, xhat=xhat, rstd=rstd))
        x_norm, x_gain, x_bias, x_l, xb = xhat, gain3, bias3, l, yb

    dy, loss_acc = _loss_and_grad(x_norm, gain3, bias3, L - 1, loss_target[0])
    loss = lax.psum(0.5 * loss_acc[0, 0], ("x", "y", "c"))

    shapes = {t: store["w", 0, t].shape for t in WEIGHTS}
    for t in WEIGHTS:
        store["full", t] = lax.empty((L,) + shapes[t][1:], F32)

    def rs_keys(kind, l, names):
        return [(kind, l, t) for t in names]

    def rs_pair_job(l, names):
        for t in names:
            store["ra", l, t] = lax.empty((N_CHIPS,) + shapes[t][2:], F32)
        return _Job().add(_rs_pair, rs_keys("g", l, names) + rs_keys("ra", l, names), len(names))

    def rs_pair_add(l, names):
        for t in names:
            p32, p16 = _rs_pair_add(store["g", l, t], store["ra", l, t], idx, "rs_pair_add_" + t)
            store["p32", l, t], store["p16", l, t] = p32, p16
            store["rb", l, t] = lax.empty((3,) + shapes[t][2:], BF16)

    def rs_ici_job(l, names):
        return _Job().add(_rs_ici, rs_keys("p16", l, names) + rs_keys("rb", l, names), 3 * len(names))

    def rs_chip_add(l, names):
        for t in names:
            store["full", t] = _rs_chip_add(store["p32", l, t], store["rb", l, t], store["full", t], l, idx, "rs_chip_add_" + t)

    def rs_share_job(l):
        return _Job().add(functools.partial(_rs_share, layer=l), [("full", t) for t in WEIGHTS], len(WEIGHTS))

    dbias = jnp.zeros((N_HEADS, WINDOW, WINDOW), F32)
    small = [None] * L
    h_in = shapes["in"][2]
    head_rows = (h_in * 5 // 8) // 16 * 16
    early = ["out", "gate", "ple"]

    def in_ici_job(l, rows):
        return _Job().add(functools.partial(_rs_ici, rows=rows), rs_keys("p16", l, ["in"]) + rs_keys("rb", l, ["in"]), 3)

    for l in reversed(range(L)):
        sv = saved[l]
        pl_l = pb[l]
        job = in_ici_job(l + 1, (head_rows, h_in - head_rows)) if l + 1 < L else None
        dz, dzb, d_e, d_gl, dc, ggain, gbias = _ln_bwd(dy, sv["xhat"], sv["rstd"], sv["gl"], pl_l, weight(l, "ple"), weight(l, "out"),
                                                       gain3, l, job, store)
        if l + 1 < L:
            rs_chip_add(l + 1, ["in"])
        g_out = _matmul_tn(sv["c"], dzb, D // N_CHIPS, D, "grad_w_out", by_rows=True,
                           job=rs_share_job(l + 1) if l + 1 < L else None, store=store)
        g_ple = _matmul_tn(pl_l, d_e, pl_l.shape[1], D // N_CHIPS, "grad_w_ple", by_rows=False)
        g_gate = _matmul_tn(sv["xb"], d_gl, D // N_CHIPS, D, "grad_w_gate", by_rows=True)
        for t, g in zip(early, (g_out, g_gate, g_ple)):
            store["g", l, t] = _halves(g)
        dd, dgp, gps, g_pool = _pool_bwd(sv["u"], sv["gp"], dc, weight(l, "pool"), ps3, l, rs_pair_job(l, early), store)
        store["g", l, "pool"] = _halves(g_pool)
        rs_pair_add(l, early)
        job = rs_pair_job(l, ["pool"])
        job.add(_rs_ici, rs_keys("p16", l, ["out"]) + rs_keys("rb", l, ["out"]), 3)
        dq, dga, dk, dv, dbias, dsink = _attn_bwd(sv["q"], sv["k"], sv["v"], sv["ga"], sv["o"], dc, sv["lse"], bias_hqk,
                                                  attn_sinks, dbias, l, job, store)
        rs_pair_add(l, ["pool"])
        dh, gbe = _assemble_dh(dq, dk, dv, dga, dd, dgp)
        g_in = _matmul_tn(dh, sv["xb"], SHARD_P, D, "grad_w_in", by_rows=True, job=rs_ici_job(l, ["gate", "ple", "pool"]), store=store)
        rs_chip_add(l, ["out", "gate", "ple", "pool"])
        store["g", l, "in"] = _halves(g_in)
        dx1 = _matmul_nt_rows(d_gl, weight(l, "gate"), "d_x_gate", dz, alpha, job=rs_pair_job(l, ["in"]), store=store)
        rs_pair_add(l, ["in"])
        dy = _matmul_nn_acc(dh, weight(l, "in"), "d_x", dx1, 1.0, in_ici_job(l, (0, head_rows) if l > 0 else (0, h_in)), store)
        small[l] = dict(b_in=_bias_from_ext(gbe)[0], sinks=dsink[0, :N_HEADS], ps=gps[0], gain=ggain[0], bias=gbias[0])
    rs_chip_add(0, ["in"])
    grad_x = dy[None]

    _run_job("rs_pair_share", rs_share_job(0), store)
    full = {t: _unhalves(store["full", t]) for t in WEIGHTS}

    def t_back(a):
        return jnp.swapaxes(a, 1, 2)

    def pool4(a):
        return a.reshape(w_pool.shape)

    r_in = _adamw(w_in_t, full["in"], jnp.swapaxes(m_w_in, 1, 2), jnp.swapaxes(v_w_in, 1, 2), idx, "adamw_w_in", tr=HALF_TILE)
    gw_in, dw_in, nm_in, nv_in = [t_back(a) for a in r_in]
    gw_out, dw_out, nm_out, nv_out = _adamw(w_out, full["out"], m_w_out, v_w_out, idx, "adamw_w_out")
    gw_gate, dw_gate, nm_gate, nv_gate = _adamw(w_gate_ple, full["gate"], m_w_gate_ple, v_w_gate_ple, idx, "adamw_w_gate")
    gw_ple, dw_ple, nm_ple, nv_ple = _adamw(w_ple, full["ple"], m_w_ple, v_w_ple, idx, "adamw_w_ple")
    r_pool = _adamw(w_pool2, full["pool"], m_w_pool.reshape(w_pool2.shape), v_w_pool.reshape(w_pool2.shape), idx, "adamw_w_pool")
    gw_pool, dw_pool, nm_pool, nv_pool = [pool4(a) for a in r_pool]

    g_rel = _bias_bwd(dbias.reshape(N_HEADS, -1), onehot).T
    small_shapes = [b_in.shape, attn_sinks.shape, rel_bias.shape, pool_scale.shape, ln_gain.shape, ln_bias.shape]
    g_small = [jnp.stack([small[l]["b_in"] for l in range(L)]), jnp.stack([small[l]["sinks"] for l in range(L)]), g_rel,
               jnp.stack([small[l]["ps"] for l in range(L)]), jnp.stack([small[l]["gain"] for l in range(L)]),
               jnp.stack([small[l]["bias"] for l in range(L)])]
    packed = _small_allreduce_adamw(
        _pack_small(g_small),
        _pack_small([b_in, attn_sinks, rel_bias, pool_scale, ln_gain, ln_bias]),
        _pack_small([m_b_in, m_attn_sinks, m_rel_bias, m_pool_scale, m_ln_gain, m_ln_bias]),
        _pack_small([v_b_in, v_attn_sinks, v_rel_bias, v_pool_scale, v_ln_gain, v_ln_bias]))
    sg, sd, sm, sv_ = [_unpack_small(a, small_shapes) for a in packed]

    def order(big, sm_):
        return (big[0], sm_[0], big[1], sm_[1], sm_[2], big[2], sm_[3], big[3], big[4], sm_[4], sm_[5])

    return (loss, grad_x,
            *order((gw_in, gw_out, gw_pool, gw_ple, gw_gate), sg),
            *order((dw_in, dw_out, dw_pool, dw_ple, dw_gate), sd),
            *order((nm_in, nm_out, nm_pool, nm_ple, nm_gate), sm),
            *order((nv_in, nv_out, nv_pool, nv_ple, nv_gate), sv_))
```

```python
import functools
import math

import jax
import jax.numpy as jnp
from jax import lax
from jax.experimental import pallas as pl
from jax.experimental.pallas import tpu as pltpu

F32 = jnp.float32
BF16 = jnp.bfloat16
ACT = jnp.bfloat16
I32 = jnp.int32
MESH = pl.DeviceIdType.MESH

HEAD_DIM = 64
QK_SCALE = HEAD_DIM ** -0.5
WINDOW = 128
KV_RATIO = 8
POOL_WINDOWS = (2, 4, 8, 16)
POOL_HALO = 16
REL_BUCKETS = 32
REL_MAX_DIST = 128
LN_EPS = 1e-5
ADAM_LR, ADAM_B1, ADAM_B2, ADAM_EPS, ADAM_WD, ADAM_STEP = 0.001, 0.9, 0.999, 1e-08, 0.01, 10

LANES = 128
VMEM_LIMIT = 52 * 1024 * 1024
N_CHIPS = 4
N_DEV = 8

D_MODEL = 2048
ATTN_W = 1024
POOL_W = 1024
KV_W = 128
N_HEADS = ATTN_W // HEAD_DIM
N_KV = N_HEADS // KV_RATIO
IN_COLS = 4352
SHARD = IN_COLS // N_CHIPS
SHARD_P = 1152
EXT = N_CHIPS * SHARD_P
HALF_TILE = SHARD_P - SHARD
OFF_Q, OFF_KA, OFF_KB, OFF_V, OFF_GA, OFF_U, OFF_UA, OFF_UB, OFF_GP = 0, 1024, 1152, 1280, 1408, 2432, 3328, 3456, 3584
WEIGHTS = ("in", "out", "gate", "ple", "pool")


def _cparams(sem=None):
    return pltpu.CompilerParams(dimension_semantics=sem, vmem_limit_bytes=VMEM_LIMIT)


def _pcall(body, **kw):
    return pl.pallas_call(body, **kw)


def _sigmoid(x):
    return 1.0 / (1.0 + jnp.exp(-x))


def _nt(a, b):
    return lax.dot_general(a, b, (((1,), (1,)), ((), ())), preferred_element_type=F32)


def _tn(a, b):
    return lax.dot_general(a, b, (((0,), (0,)), ((), ())), preferred_element_type=F32)


def _nn(a, b):
    return jnp.dot(a, b, preferred_element_type=F32)


def _mesh_pos():
    return lax.axis_index("x"), lax.axis_index("y"), lax.axis_index("c")


def _peer_chips(x, y):
    return [(1 - x, y), (x, 1 - y), (1 - x, 1 - y)]


def _row_tile(rows, cap=256):
    t = min(rows, cap)
    while rows % t or t % 16:
        t -= 1
    return t


def _hbm_spec():
    return pl.BlockSpec(memory_space=pltpu.HBM)


def _halves(a):
    return a.reshape(a.shape[:-2] + (2, a.shape[-2] // 2, a.shape[-1]))


def _unhalves(a):
    return a.reshape(a.shape[:-3] + (2 * a.shape[-2], a.shape[-1]))


class _remote:
    def __init__(self, src, dst, send, recv, i, device):
        self.args = dict(src_ref=src, dst_ref=dst, send_sem=send.at[i], recv_sem=recv.at[i], device_id=device, device_id_type=MESH)

    def start(self):
        pltpu.make_async_remote_copy(**self.args).start()

    def wait_recv(self):
        pltpu.make_async_remote_copy(**self.args).wait_recv()

    def wait_send(self):
        pltpu.make_async_remote_copy(**self.args).wait_send()


class _Job:
    def __init__(self):
        self.keys, self.parts, self.n = [], [], 0

    def add(self, fn, keys, n):
        self.parts.append((fn, len(self.keys), len(keys), self.n))
        self.keys += list(keys)
        self.n += n
        return self

    def build(self, refs, send, recv):
        out = []
        for fn, i0, nb, base in self.parts:
            out += fn(refs[i0:i0 + nb], send, recv, base)
        return out


def _ag_ici(refs, send, recv, base):
    x, y, c = _mesh_pos()
    me = 2 * x + y
    out = []
    for t, g in enumerate(refs):
        for k, chip in enumerate(_peer_chips(x, y)):
            i = base + 3 * t + k
            dev = (*chip, c)
            out.append((_remote(g.at[me, c], g.at[me, c], send, recv, i, dev),
                        _remote(g.at[me, c], g.at[2 * chip[0] + chip[1], c], send, recv, i, dev)))
    return out


def _ag_fwd(refs, send, recv, base):
    x, y, c = _mesh_pos()
    out = []
    for t, g in enumerate(refs):
        for k, chip in enumerate(_peer_chips(x, y)):
            i = base + 3 * t + k
            slot = 2 * chip[0] + chip[1]
            dev = (x, y, 1 - c)
            out.append((_remote(g.at[slot, c], g.at[slot, c], send, recv, i, dev),
                        _remote(g.at[slot, c], g.at[slot, 1 - c], send, recv, i, dev)))
    return out


def _rs_pair(refs, send, recv, base):
    x, y, c = _mesh_pos()
    n = len(refs) // 2
    out = []
    for t in range(n):
        cp = _remote(refs[t].at[:, 1 - c], refs[n + t], send, recv, base + t, (x, y, 1 - c))
        out.append((cp, cp))
    return out


def _rs_ici(refs, send, recv, base, rows=None):
    x, y, c = _mesh_pos()
    n = len(refs) // 2
    rsl = slice(None) if rows is None else pl.ds(rows[0], rows[1])
    out = []
    for t in range(n):
        for k, chip in enumerate(_peer_chips(x, y)):
            cp = _remote(refs[t].at[2 * chip[0] + chip[1], rsl], refs[n + t].at[k, rsl], send, recv, base + 3 * t + k, (*chip, c))
            out.append((cp, cp))
    return out


def _rs_share(refs, send, recv, base, layer):
    x, y, c = _mesh_pos()
    out = []
    for t, f in enumerate(refs):
        dev = (x, y, 1 - c)
        out.append((_remote(f.at[layer, c], f.at[layer, c], send, recv, base + t, dev),
                    _remote(f.at[layer, c], f.at[layer, 1 - c], send, recv, base + t, dev)))
    return out


def _call(body, *, name, grid, in_specs, out_specs, out_shape, args, scratch_shapes=(), sem=None, job=None, store=None):
    in_specs, out_specs, out_shape, scratch_shapes = list(in_specs), list(out_specs), list(out_shape), list(scratch_shapes)
    if job is None or job.n == 0:
        return list(_pcall(body, name=name, grid=grid, in_specs=in_specs, out_specs=out_specs, out_shape=out_shape,
                           scratch_shapes=scratch_shapes, compiler_params=_cparams(sem))(*args))
    bufs = [store[k] for k in job.keys]
    nb, n_in, n_out, n_sc = len(bufs), len(args), len(out_shape), len(scratch_shapes)

    def wrapped(*refs):
        ins = refs[:n_in]
        outs = refs[n_in + nb:n_in + nb + n_out]
        cb = refs[n_in + nb + n_out:n_in + 2 * nb + n_out]
        scratch = refs[n_in + 2 * nb + n_out:n_in + 2 * nb + n_out + n_sc]
        send, recv = refs[-2:]
        ids = [pl.program_id(a) for a in range(len(grid))]
        first = functools.reduce(jnp.logical_and, [i == 0 for i in ids])
        last = functools.reduce(jnp.logical_and, [i == g - 1 for i, g in zip(ids, grid)])

        @pl.when(first)
        def _():
            for s, _r in job.build(cb, send, recv):
                s.start()

        body(*ins, *outs, *scratch)

        @pl.when(last)
        def _():
            pairs = job.build(cb, send, recv)
            for _s, r in pairs:
                r.wait_recv()
            for s, _r in pairs:
                s.wait_send()

    res = _pcall(
        wrapped, name=name, grid=grid, in_specs=in_specs + [_hbm_spec()] * nb, out_specs=out_specs + [_hbm_spec()] * nb,
        out_shape=out_shape + [jax.ShapeDtypeStruct(b.shape, b.dtype) for b in bufs],
        scratch_shapes=scratch_shapes + [pltpu.SemaphoreType.DMA((job.n,)), pltpu.SemaphoreType.DMA((job.n,))],
        input_output_aliases={n_in + i: n_out + i for i in range(nb)},
        compiler_params=_cparams(("arbitrary",) * len(grid)))(*args, *bufs)
    for k, v in zip(job.keys, res[n_out:]):
        store[k] = v
    return list(res[:n_out])


def _run_job(name, job, store):
    bufs = [store[k] for k in job.keys]
    nb = len(bufs)

    def body(*refs):
        send, recv = refs[-2:]
        pairs = job.build(refs[nb:2 * nb], send, recv)
        for s, _r in pairs:
            s.start()
        for _s, r in pairs:
            r.wait_recv()
        for s, _r in pairs:
            s.wait_send()

    res = _pcall(body, name=name, in_specs=[_hbm_spec()] * nb, out_specs=[_hbm_spec()] * nb,
                 out_shape=[jax.ShapeDtypeStruct(b.shape, b.dtype) for b in bufs],
                 scratch_shapes=[pltpu.SemaphoreType.DMA((job.n,)), pltpu.SemaphoreType.DMA((job.n,))],
                 input_output_aliases={i: i for i in range(nb)})(*bufs)
    for k, v in zip(job.keys, res):
        store[k] = v


def _allgather_now(keys, store):
    bufs = [store[k] for k in keys]
    nb = len(bufs)

    def body(*refs):
        send, recv = refs[-2:]
        g = refs[nb:2 * nb]
        ici = _ag_ici(g, send, recv, 0)
        fwd = _ag_fwd(g, send, recv, 3 * nb)
        for s, _r in ici:
            s.start()
        for (_s, r), (fs, _fr) in zip(ici, fwd):
            r.wait_recv()
            fs.start()
        for _fs, fr in fwd:
            fr.wait_recv()
        for s, _r in ici + fwd:
            s.wait_send()

    res = _pcall(body, name="allgather_first_layer", in_specs=[_hbm_spec()] * nb, out_specs=[_hbm_spec()] * nb,
                 out_shape=[jax.ShapeDtypeStruct(b.shape, b.dtype) for b in bufs],
                 scratch_shapes=[pltpu.SemaphoreType.DMA((6 * nb,)), pltpu.SemaphoreType.DMA((6 * nb,))],
                 input_output_aliases={i: i for i in range(nb)})(*bufs)
    for k, v in zip(keys, res):
        store[k] = v


def _piece(w, l, idx, name):
    _, R, C = w.shape
    tr = _row_tile(R)

    def body(s_ref, w_ref, o_ref):
        del s_ref
        o_ref[...] = w_ref[...].astype(BF16)

    gs = pltpu.PrefetchScalarGridSpec(
        num_scalar_prefetch=1, grid=(R // tr,),
        in_specs=[pl.BlockSpec((None, tr, C), lambda r, s: (l, r, 0))],
        out_specs=pl.BlockSpec((None, tr, C), lambda r, s: (s[0], r, 0)))
    return _pcall(body, name=name, grid_spec=gs, out_shape=jax.ShapeDtypeStruct((N_CHIPS, R, C), BF16),
                  compiler_params=_cparams(("parallel",)))(idx, w)


def _piece_in(w_in_t, l, idx):
    _, R, D = w_in_t.shape
    tr = HALF_TILE
    nsrc = R // tr

    def body(s_ref, w_ref, o_ref):
        src = pl.program_id(0) - s_ref[1]
        ok = jnp.logical_and(src >= 0, src < nsrc)
        o_ref[...] = jnp.where(ok, w_ref[...], 0.0).astype(BF16)

    gs = pltpu.PrefetchScalarGridSpec(
        num_scalar_prefetch=1, grid=(SHARD_P // tr,),
        in_specs=[pl.BlockSpec((None, tr, D), lambda r, s: (l, jnp.clip(r - s[1], 0, nsrc - 1), 0))],
        out_specs=pl.BlockSpec((None, tr, D), lambda r, s: (s[0], r, 0)))
    return _pcall(body, name="piece_in", grid_spec=gs, out_shape=jax.ShapeDtypeStruct((N_CHIPS, SHARD_P, D), BF16),
                  compiler_params=_cparams(("parallel",)))(idx, w_in_t)


def _rs_pair_add(g5, r4, idx, name):
    J, _, h, C = g5.shape
    th = _row_tile(h)

    def body(s_ref, g_ref, r_ref, o32_ref, o16_ref):
        s = g_ref[...] + r_ref[...]
        o16_ref[...] = s.astype(BF16)

        @pl.when(pl.program_id(1) == s_ref[0])
        def _():
            o32_ref[...] = s

    spec = pl.BlockSpec((None, th, C), lambda r, j, s: (j, r, 0))
    gs = pltpu.PrefetchScalarGridSpec(
        num_scalar_prefetch=1, grid=(h // th, J),
        in_specs=[pl.BlockSpec((None, None, th, C), lambda r, j, s: (j, s[2], r, 0)), spec],
        out_specs=[pl.BlockSpec((th, C), lambda r, j, s: (r, 0)), spec])
    return _pcall(body, name=name, grid_spec=gs,
                  out_shape=[jax.ShapeDtypeStruct((h, C), F32), jax.ShapeDtypeStruct((J, h, C), BF16)],
                  compiler_params=_cparams(("parallel", "arbitrary")))(idx, g5, r4)


def _rs_chip_add(p32, r3, full, l, idx, name):
    h, C = p32.shape
    th = _row_tile(h)

    def body(s_ref, p_ref, r_ref, f_ref, o_ref):
        del s_ref, f_ref
        o_ref[...] = ((p_ref[...] + r_ref[0].astype(F32)) + r_ref[1].astype(F32)) + r_ref[2].astype(F32)

    gs = pltpu.PrefetchScalarGridSpec(
        num_scalar_prefetch=1, grid=(h // th,),
        in_specs=[pl.BlockSpec((th, C), lambda r, s: (r, 0)),
                  pl.BlockSpec((3, th, C), lambda r, s: (0, r, 0)),
                  pl.BlockSpec(memory_space=pl.ANY)],
        out_specs=pl.BlockSpec((None, None, th, C), lambda r, s: (l, s[2], r, 0)))
    return _pcall(body, name=name, grid_spec=gs, out_shape=jax.ShapeDtypeStruct(full.shape, F32),
                  input_output_aliases={3: 0}, compiler_params=_cparams(("parallel",)))(idx, p32, r3, full)


def _adamw_math(w, g, m, v):
    nm = ADAM_B1 * m + (1.0 - ADAM_B1) * g
    nv = ADAM_B2 * v + (1.0 - ADAM_B2) * (g * g)
    m_hat = nm / (1.0 - ADAM_B1 ** ADAM_STEP)
    v_hat = nv / (1.0 - ADAM_B2 ** ADAM_STEP)
    delta = -ADAM_LR * (m_hat / (jnp.sqrt(v_hat) + ADAM_EPS) + ADAM_WD * w)
    return delta, nm, nv


def _adamw(w, g, m, v, idx, name, tr=None):
    L, R, C = w.shape
    Rg = g.shape[1]
    tr = tr or _row_tile(R)
    shift = (Rg - R) // tr
    assert (Rg - R) % tr == 0

    def body(s_ref, w_ref, g_ref, m_ref, v_ref, go_ref, d_ref, nm_ref, nv_ref):
        del s_ref
        gv = g_ref[...]
        d, nm, nv = _adamw_math(w_ref[...], gv, m_ref[...], v_ref[...])
        go_ref[...] = gv
        d_ref[...] = d
        nm_ref[...] = nm
        nv_ref[...] = nv

    wspec = pl.BlockSpec((None, tr, C), lambda l, r, s: (l, r, 0))
    gs = pltpu.PrefetchScalarGridSpec(
        num_scalar_prefetch=1, grid=(L, R // tr),
        in_specs=[wspec, pl.BlockSpec((None, tr, C), lambda l, r, s: (l, r + shift * s[1], 0)), wspec, wspec],
        out_specs=[wspec, wspec, wspec, wspec])
    sds = jax.ShapeDtypeStruct((L, R, C), F32)
    return _pcall(body, name=name, grid_spec=gs, out_shape=[sds, sds, sds, sds],
                  compiler_params=_cparams(("parallel", "parallel")))(idx, w, g, m, v)


def _small_allreduce_adamw(gv, wv, mv, vv):
    NR = gv.shape[0]

    def body(g_ref, w_ref, m_ref, v_ref, go_ref, d_ref, nm_ref, nv_ref, gath, send, recv):
        x, y, c = _mesh_pos()
        rank = 4 * x + 2 * y + c
        gath[rank] = g_ref[...]
        cps = []
        for msk in range(1, N_DEV):
            bx, by, bc = (msk >> 2) & 1, (msk >> 1) & 1, msk & 1
            peer = (1 - x if bx else x, 1 - y if by else y, 1 - c if bc else c)
            cps.append(pltpu.make_async_remote_copy(src_ref=g_ref, dst_ref=gath.at[rank], send_sem=send.at[msk - 1],
                                                    recv_sem=recv.at[msk - 1], device_id=peer, device_id_type=MESH))
        for cp in cps:
            cp.start()
        for msk in range(1, N_DEV):
            bx, by, bc = (msk >> 2) & 1, (msk >> 1) & 1, msk & 1
            peer = (1 - x if bx else x, 1 - y if by else y, 1 - c if bc else c)
            prank = 4 * peer[0] + 2 * peer[1] + peer[2]
            pltpu.make_async_remote_copy(src_ref=g_ref, dst_ref=gath.at[prank], send_sem=send.at[msk - 1],
                                         recv_sem=recv.at[msk - 1], device_id=peer, device_id_type=MESH).wait_recv()
        for cp in cps:
            cp.wait_send()
        tot = gath[0]
        for r in range(1, N_DEV):
            tot = tot + gath[r]
        d, nm, nv = _adamw_math(w_ref[...], tot, m_ref[...], v_ref[...])
        go_ref[...] = tot
        d_ref[...] = d
        nm_ref[...] = nm
        nv_ref[...] = nv

    vm = pl.BlockSpec(memory_space=pltpu.VMEM)
    sds = jax.ShapeDtypeStruct((NR, LANES), F32)
    return _pcall(body, name="small_allreduce_adamw", in_specs=[vm, vm, vm, vm], out_specs=[vm, vm, vm, vm],
                  out_shape=[sds, sds, sds, sds],
                  scratch_shapes=[pltpu.VMEM((N_DEV, NR, LANES), F32), pltpu.SemaphoreType.DMA((N_DEV - 1,)),
                                  pltpu.SemaphoreType.DMA((N_DEV - 1,))],
                  compiler_params=pltpu.CompilerParams(vmem_limit_bytes=VMEM_LIMIT))(gv, wv, mv, vv)


def _split3(a):
    h1 = a.astype(BF16)
    r1 = a - h1.astype(F32)
    h2 = r1.astype(BF16)
    h3 = (r1 - h2.astype(F32)).astype(BF16)
    return h1, h2, h3


def _folded_dist():
    r = jnp.arange(WINDOW)[:, None]
    j = jnp.arange(WINDOW)[None, :]
    return jnp.where(j > r, r + WINDOW - j, r - j)


def _bucket_onehot():
    d = _folded_dist()
    max_exact = REL_BUCKETS // 2
    d_f = jnp.maximum(d, 1).astype(F32)
    large = max_exact + (jnp.log(d_f / max_exact) / math.log(REL_MAX_DIST / max_exact) * (REL_BUCKETS - max_exact)).astype(I32)
    large = jnp.minimum(large, REL_BUCKETS - 1)
    bucket = jnp.where(d < max_exact, d, large)
    oh = bucket[None] == jnp.arange(REL_BUCKETS)[:, None, None]
    return oh.reshape(REL_BUCKETS, WINDOW * WINDOW).astype(BF16)


def _bias_fwd(rel_bias_t, onehot):
    H, N = rel_bias_t.shape[0], onehot.shape[1]
    tn = 4096

    def body(t_ref, oh_ref, o_ref):
        h1, h2, h3 = _split3(t_ref[...])
        oh = oh_ref[...]
        o_ref[...] = (_nn(h1, oh) + _nn(h2, oh)) + _nn(h3, oh)

    return _pcall(body, name="bias_fwd", grid=(N // tn,),
                  in_specs=[pl.BlockSpec((H, REL_BUCKETS), lambda i: (0, 0)), pl.BlockSpec((REL_BUCKETS, tn), lambda i: (0, i))],
                  out_specs=pl.BlockSpec((H, tn), lambda i: (0, i)), out_shape=jax.ShapeDtypeStruct((H, N), F32),
                  compiler_params=_cparams(("parallel",)))(rel_bias_t, onehot)


def _bias_bwd(dbias, onehot):
    H, N = dbias.shape
    tn = 4096

    def body(d_ref, oh_ref, o_ref):
        @pl.when(pl.program_id(0) == 0)
        def _():
            o_ref[...] = jnp.zeros_like(o_ref)
        h1, h2, h3 = _split3(d_ref[...])
        oh = oh_ref[...]
        o_ref[...] += (_nt(h1, oh) + _nt(h2, oh)) + _nt(h3, oh)

    return _pcall(body, name="bias_bwd", grid=(N // tn,),
                  in_specs=[pl.BlockSpec((H, tn), lambda i: (0, i)), pl.BlockSpec((REL_BUCKETS, tn), lambda i: (0, i))],
                  out_specs=pl.BlockSpec((H, REL_BUCKETS), lambda i: (0, 0)), out_shape=jax.ShapeDtypeStruct((H, REL_BUCKETS), F32),
                  compiler_params=_cparams(("arbitrary",)))(dbias, onehot)


def _to_bf16(x):
    S, D = x.shape
    tm = min(S, 512)

    def body(x_ref, o_ref):
        o_ref[...] = x_ref[...].astype(BF16)

    return _pcall(body, name="to_bf16", grid=(S // tm,), in_specs=[pl.BlockSpec((tm, D), lambda i: (i, 0))],
                  out_specs=pl.BlockSpec((tm, D), lambda i: (i, 0)), out_shape=jax.ShapeDtypeStruct((S, D), BF16),
                  compiler_params=_cparams(("parallel",)))(x)


def _inproj(xb, w_t, b_ext, l, job, store):
    S, D = xb.shape
    tm = min(S, 512)

    def body(x_ref, w_ref, b_ref, q_ref, k_ref, v_ref, ga_ref, u_ref, gp_ref):
        xv = x_ref[...]

        def window(j):
            return _nt(xv, w_ref[j]) + b_ref[:, j * SHARD_P:(j + 1) * SHARD_P]

        acc = window(0)
        q_ref[...] = (acc[:, :1024] * QK_SCALE).astype(BF16)
        k_lo = acc[:, 1024:1152]
        acc = window(1)
        k_ref[...] = k_lo + acc[:, 0:128]
        v_ref[...] = acc[:, 128:256]
        ga_ref[:, 0:896] = acc[:, 256:1152].astype(ACT)
        acc = window(2)
        ga_ref[:, 896:1024] = acc[:, 0:128].astype(ACT)
        u_ref[:, 0:896] = acc[:, 128:1024]
        u_lo = acc[:, 1024:1152]
        acc = window(3)
        u_ref[:, 896:1024] = u_lo + acc[:, 0:128]
        gp_ref[...] = acc[:, 128:1152].astype(ACT)

    def ospec(w):
        return pl.BlockSpec((tm, w), lambda i: (i, 0))

    return _call(
        body, name="inproj", grid=(S // tm,),
        in_specs=[pl.BlockSpec((tm, D), lambda i: (i, 0)),
                  pl.BlockSpec((N_CHIPS, SHARD_P, D), lambda i: (0, 0, 0), pipeline_mode=pl.Buffered(1)),
                  pl.BlockSpec((None, 1, EXT), lambda i: (l, 0, 0))],
        out_specs=[ospec(ATTN_W), ospec(KV_W), ospec(KV_W), ospec(ATTN_W), ospec(POOL_W), ospec(POOL_W)],
        out_shape=[jax.ShapeDtypeStruct((S, ATTN_W), BF16), jax.ShapeDtypeStruct((S, KV_W), F32), jax.ShapeDtypeStruct((S, KV_W), F32),
                   jax.ShapeDtypeStruct((S, ATTN_W), ACT), jax.ShapeDtypeStruct((S, POOL_W), F32), jax.ShapeDtypeStruct((S, POOL_W), ACT)],
        args=(xb, w_t, b_ext), sem=("parallel",), job=job, store=store)


def _matmul_nn(a, b4, name, job, store):
    S, K = a.shape
    N = b4.shape[2]
    kq = b4.shape[1]
    tm, tn = min(S, 512), min(N, 1024)

    def body(a_ref, b_ref, o_ref):
        acc = _nn(a_ref[:, 0:kq], b_ref[0])
        for j in range(1, N_CHIPS):
            acc = acc + _nn(a_ref[:, j * kq:(j + 1) * kq], b_ref[j])
        o_ref[...] = acc.astype(ACT)

    return _call(body, name=name, grid=(N // tn, S // tm),
                 in_specs=[pl.BlockSpec((tm, K), lambda n, i: (i, 0)), pl.BlockSpec((N_CHIPS, kq, tn), lambda n, i: (0, 0, n))],
                 out_specs=[pl.BlockSpec((tm, tn), lambda n, i: (i, n))], out_shape=[jax.ShapeDtypeStruct((S, N), ACT)],
                 args=(a, b4), sem=("parallel", "parallel"), job=job, store=store)[0]


def _masked_bias(bias):
    r = jnp.arange(WINDOW)[:, None]
    j = jnp.arange(WINDOW)[None, :]
    return jnp.stack([jnp.where(j > r, -1e30, bias), bias])


def _fold(full, tri):
    return jnp.where(tri, full[:, :WINDOW], full[:, WINDOW:])


def _unfold(folded, tri):
    return jnp.concatenate([jnp.where(tri, folded, 0.0).astype(BF16), jnp.where(tri, 0.0, folded).astype(BF16)], axis=1)


def _dup_heads(src_ref, dst_ref):
    a = src_ref[...]
    r = pltpu.roll(a, HEAD_DIM, axis=1)
    lo = lax.broadcasted_iota(I32, a.shape, 1) < HEAD_DIM
    dst_ref[0] = jnp.where(lo, a, r).astype(BF16)
    dst_ref[1] = jnp.where(lo, r, a).astype(BF16)


def _kv_block(ref, h, prev, cur):
    return jnp.concatenate([ref[h, pl.ds(prev, WINDOW), :], ref[h, pl.ds(cur, WINDOW), :]], axis=0)


def _rows(i):
    return slice(i * WINDOW, (i + 1) * WINDOW)


def _attn_fwd(q, k, v, ga, bias, sinks, l, job, store):
    S = q.shape[0]
    nblk = S // WINDOW

    def body(sink_ref, q_ref, k_ref, v_ref, ga_ref, bias_ref, o_ref, ca_ref, lse_ref, kd, vd, qs, ss, ps, os_):
        n = pl.program_id(0)

        @pl.when(n == 0)
        def _():
            _dup_heads(k_ref, kd)
            _dup_heads(v_ref, vd)

        cur = pl.multiple_of(n * WINDOW, WINDOW)
        prev = pl.multiple_of(jnp.maximum(n - 1, 0) * WINDOW, WINDOW)
        lane = lax.broadcasted_iota(I32, (WINDOW, LANES), 1)
        lo = lane < HEAD_DIM
        tri = lane > lax.broadcasted_iota(I32, (WINDOW, LANES), 0)
        lse_mat = jnp.zeros((WINDOW, LANES), F32)
        for g in range(N_KV):
            for i in range(KV_RATIO):
                h = g * KV_RATIO + i
                qp = q_ref[:, LANES * (h // 2):LANES * (h // 2 + 1)]
                qs[_rows(i), :] = jnp.where(lo if h % 2 == 0 else jnp.logical_not(lo), qp, jnp.zeros_like(qp))
            ss[...] = _nt(qs[...], _kv_block(kd, g, prev, cur))
            for i in range(KV_RATIO):
                h = g * KV_RATIO + i
                s = _fold(ss[_rows(i), :], tri) + bias_ref[h]
                sink = sink_ref[l, h]
                m = jnp.maximum(jnp.max(s, axis=1, keepdims=True), sink)
                e = jnp.exp(s - m)
                den = jnp.sum(e, axis=1, keepdims=True) + jnp.exp(sink - m)
                ps[_rows(i), :] = _unfold(e * (1.0 / den), tri)
                lse_mat = jnp.where(lane == h, m + jnp.log(den), lse_mat)
            os_[...] = _nn(ps[...], _kv_block(vd, g, prev, cur))
            for j in range(KV_RATIO // 2):
                pair = g * (KV_RATIO // 2) + j
                sl = slice(LANES * pair, LANES * (pair + 1))
                o_pair = jnp.where(lo, os_[_rows(2 * j), :], os_[_rows(2 * j + 1), :])
                o_ref[:, sl] = o_pair.astype(ACT)
                gav = ga_ref[:, sl].astype(F32)
                ca_ref[:, sl] = (o_pair * (gav * _sigmoid(gav))).astype(BF16)
        lse_ref[...] = lse_mat

    blk = pl.BlockSpec((WINDOW, ATTN_W), lambda n: (n, 0))
    full_kv = pl.BlockSpec((S, KV_W), lambda n: (0, 0))
    stack = KV_RATIO * WINDOW
    return _call(
        body, name="attn_fwd", grid=(nblk,),
        in_specs=[pl.BlockSpec(memory_space=pltpu.SMEM), blk, full_kv, full_kv, blk,
                  pl.BlockSpec((None, N_HEADS, WINDOW, WINDOW), lambda n: (jnp.minimum(n, 1), 0, 0, 0))],
        out_specs=[blk, blk, pl.BlockSpec((WINDOW, LANES), lambda n: (n, 0))],
        out_shape=[jax.ShapeDtypeStruct((S, ATTN_W), ACT), jax.ShapeDtypeStruct((S, ATTN_W), BF16), jax.ShapeDtypeStruct((S, LANES), F32)],
        scratch_shapes=[pltpu.VMEM((N_KV, S, LANES), BF16), pltpu.VMEM((N_KV, S, LANES), BF16),
                        pltpu.VMEM((stack, LANES), BF16), pltpu.VMEM((stack, 2 * WINDOW), F32),
                        pltpu.VMEM((stack, 2 * WINDOW), BF16), pltpu.VMEM((stack, LANES), F32)],
        args=(sinks, q, k, v, ga, bias), sem=("arbitrary",), job=job, store=store)


def _pool_diff(u, halo, tile_index, tm):
    gw = POOL_W // len(POOL_WINDOWS)
    xh = jnp.concatenate([halo, u], axis=0)
    sums = []
    s = xh
    for step in (1, 2, 4, 8):
        s = s + pltpu.roll(s, step, axis=0)
        sums.append(s)
    t = tile_index * tm + lax.broadcasted_iota(I32, (tm, gw), 0)
    diffs = []
    for g, w in enumerate(POOL_WINDOWS):
        cols = slice(g * gw, (g + 1) * gw)
        cnt = jnp.minimum(t + 1, w).astype(F32)
        diffs.append(sums[g][POOL_HALO:, cols] / cnt - u[:, cols])
    return diffs


def _pool_weight(wp_ref, g):
    r = wp_ref.shape[1] // len(POOL_WINDOWS)
    return jnp.concatenate([wp_ref[j, g * r:(g + 1) * r, :] for j in range(N_CHIPS)], axis=0)


def _pool_fwd(u, gp, ca, w_pool, ps, l, job, store):
    S = u.shape[0]
    tm = min(S, 256)
    hb = tm // POOL_HALO
    gw = POOL_W // len(POOL_WINDOWS)

    def body(u_ref, uh_ref, gp_ref, ca_ref, wp_ref, ps_ref, c_ref):
        i = pl.program_id(0)
        uv = u_ref[...]
        halo = jnp.where(i > 0, uh_ref[...], 0.0)
        diffs = _pool_diff(uv, halo, i, tm)
        c_ref[:, 0:ATTN_W] = ca_ref[...]
        for g in range(len(POOL_WINDOWS)):
            cols = slice(g * gw, (g + 1) * gw)
            mm = _nn(diffs[g].astype(BF16), _pool_weight(wp_ref, g))
            gpv = gp_ref[:, cols].astype(F32)
            b = (mm * ps_ref[:, cols]) * (gpv * _sigmoid(gpv))
            c_ref[:, ATTN_W + g * gw:ATTN_W + (g + 1) * gw] = b.astype(BF16)

    row = pl.BlockSpec((tm, POOL_W), lambda i: (i, 0))
    return _call(
        body, name="pool_fwd", grid=(S // tm,),
        in_specs=[row, pl.BlockSpec((POOL_HALO, POOL_W), lambda i: (jnp.maximum(i * hb - 1, 0), 0)), row, row,
                  pl.BlockSpec(w_pool.shape, lambda i: (0, 0, 0)),
                  pl.BlockSpec((None, 1, POOL_W), lambda i: (l, 0, 0))],
        out_specs=[pl.BlockSpec((tm, ATTN_W + POOL_W), lambda i: (i, 0))],
        out_shape=[jax.ShapeDtypeStruct((S, ATTN_W + POOL_W), BF16)],
        args=(u, u, gp, ca, w_pool, ps), sem=("parallel",), job=job, store=store)[0]


def _ple_embed(p_ref, wple_ref):
    pb = p_ref[...].astype(BF16)
    return jnp.concatenate([_nn(pb, wple_ref[j]) for j in range(N_CHIPS)], axis=1)


def _outproj_ln(c, xh_in, gain_in, bias_in, l_in, gl, p, w_out, w_ple, gain, bias, l, alpha, job, store):
    S, D = xh_in.shape
    tm = min(S, 256)
    kq = D // N_CHIPS

    def body(c_ref, x_ref, gi_ref, bi_ref, gl_ref, p_ref, wo_ref, wp_ref, gain_ref, bias_ref, yb_ref, xh_ref, rs_ref):
        mix = _nn(c_ref[:, 0:kq], wo_ref[0])
        for j in range(1, N_CHIPS):
            mix = mix + _nn(c_ref[:, j * kq:(j + 1) * kq], wo_ref[j])
        ple = _sigmoid(gl_ref[...].astype(F32)) * _ple_embed(p_ref, wp_ref)
        x = x_ref[...] * gi_ref[...] + bi_ref[...]
        z = (alpha * x + mix) + ple
        mu = jnp.mean(z, axis=1, keepdims=True)
        zc = z - mu
        var = jnp.mean(zc * zc, axis=1, keepdims=True)
        rstd = lax.rsqrt(var + LN_EPS)
        xhat = zc * rstd
        yb_ref[...] = (xhat * gain_ref[...] + bias_ref[...]).astype(BF16)
        xh_ref[...] = xhat
        rs_ref[...] = rstd

    row = pl.BlockSpec((tm, D), lambda i: (i, 0))
    vec = pl.BlockSpec((None, 1, D), lambda i: (l, 0, 0))
    vec_in = pl.BlockSpec((None, 1, D), lambda i: (l_in, 0, 0))
    return _call(
        body, name="outproj_ln", grid=(S // tm,),
        in_specs=[row, row, vec_in, vec_in, row, pl.BlockSpec((tm, p.shape[1]), lambda i: (i, 0)),
                  pl.BlockSpec(w_out.shape, lambda i: (0, 0, 0)), pl.BlockSpec(w_ple.shape, lambda i: (0, 0, 0)), vec, vec],
        out_specs=[row, row, pl.BlockSpec((tm, 1), lambda i: (i, 0))],
        out_shape=[jax.ShapeDtypeStruct((S, D), BF16), jax.ShapeDtypeStruct((S, D), F32), jax.ShapeDtypeStruct((S, 1), F32)],
        args=(c, xh_in, gain_in, bias_in, gl, p, w_out, w_ple, gain, bias), sem=("parallel",), job=job, store=store)


def _loss_and_grad(xhat, gain, bias, l, target):
    S, D = xhat.shape
    tm = min(S, 512)

    def body(xh_ref, g_ref, b_ref, t_ref, dy_ref, acc_ref):
        @pl.when(pl.program_id(0) == 0)
        def _():
            acc_ref[...] = jnp.zeros_like(acc_ref)
        d = (xh_ref[...] * g_ref[...] + b_ref[...]) - t_ref[...]
        dy_ref[...] = d * (1.0 / D)
        acc_ref[...] += jnp.sum(jnp.mean(d * d, axis=1, keepdims=True), axis=0, keepdims=True)

    row = pl.BlockSpec((tm, D), lambda i: (i, 0))
    vec = pl.BlockSpec((None, 1, D), lambda i: (l, 0, 0))
    return _pcall(body, name="loss", grid=(S // tm,), in_specs=[row, vec, vec, row],
                  out_specs=[row, pl.BlockSpec((8, LANES), lambda i: (0, 0))],
                  out_shape=[jax.ShapeDtypeStruct((S, D), F32), jax.ShapeDtypeStruct((8, LANES), F32)],
                  compiler_params=_cparams(("arbitrary",)))(xhat, gain, bias, target)


def _ln_bwd(dy, xhat, rstd, gl, p, w_ple, w_out, gain, l, job, store):
    S, D = dy.shape
    tm = min(S, 256)

    nq = w_out.shape[1]

    def body(dy_ref, xh_ref, rs_ref, gl_ref, p_ref, wp_ref, wo_ref, gain_ref, dz_ref, dzb_ref, de_ref, dgl_ref, dc_ref, gg_ref, gb_ref):
        @pl.when(pl.program_id(0) == 0)
        def _():
            gg_ref[...] = jnp.zeros_like(gg_ref)
            gb_ref[...] = jnp.zeros_like(gb_ref)
        dyv = dy_ref[...]
        xh = xh_ref[...]
        dxh = dyv * gain_ref[...]
        m1 = jnp.mean(dxh, axis=1, keepdims=True)
        m2 = jnp.mean(dxh * xh, axis=1, keepdims=True)
        dz = rs_ref[...] * ((dxh - m1) - xh * m2)
        gg_ref[...] += jnp.sum(dyv * xh, axis=0, keepdims=True)
        gb_ref[...] += jnp.sum(dyv, axis=0, keepdims=True)
        sg = _sigmoid(gl_ref[...].astype(F32))
        e = _ple_embed(p_ref, wp_ref)
        dzb = dz.astype(BF16)
        dz_ref[...] = dz
        dzb_ref[...] = dzb
        de_ref[...] = (dz * sg).astype(BF16)
        dgl_ref[...] = ((dz * e) * (sg * (1.0 - sg))).astype(BF16)
        for j in range(N_CHIPS):
            dc_ref[:, j * nq:(j + 1) * nq] = _nt(dzb, wo_ref[j]).astype(ACT)

    row = pl.BlockSpec((tm, D), lambda i: (i, 0))
    vec_in = pl.BlockSpec((None, 1, D), lambda i: (l, 0, 0))
    vec_out = pl.BlockSpec((1, D), lambda i: (0, 0))
    bsd = jax.ShapeDtypeStruct((S, D), BF16)
    fsd = jax.ShapeDtypeStruct((S, D), F32)
    return _call(
        body, name="ln_bwd", grid=(S // tm,),
        in_specs=[row, row, pl.BlockSpec((tm, 1), lambda i: (i, 0)), row, pl.BlockSpec((tm, p.shape[1]), lambda i: (i, 0)),
                  pl.BlockSpec(w_ple.shape, lambda i: (0, 0, 0), pipeline_mode=pl.Buffered(1)),
                  pl.BlockSpec(w_out.shape, lambda i: (0, 0, 0), pipeline_mode=pl.Buffered(1)), vec_in],
        out_specs=[row, row, row, row, row, vec_out, vec_out],
        out_shape=[fsd, bsd, bsd, bsd, jax.ShapeDtypeStruct((S, D), ACT), jax.ShapeDtypeStruct((1, D), F32), jax.ShapeDtypeStruct((1, D), F32)],
        args=(dy, xhat, rstd, gl, p, w_ple, w_out, gain), sem=("arbitrary",), job=job, store=store)


def _matmul_nn_acc(a, b4, name, add, add_scale, job, store):
    S = a.shape[0]
    KS, tk, N = b4.shape
    tm, tn = min(S, 512), min(N, 1024)

    def body(a_ref, b_ref, add_ref, o_ref):
        acc = add_scale * add_ref[...]
        for k in range(KS):
            acc = acc + _nn(a_ref[:, k * tk:(k + 1) * tk], b_ref[k])
        o_ref[...] = acc

    return _call(body, name=name, grid=(N // tn, S // tm),
                 in_specs=[pl.BlockSpec((tm, KS * tk), lambda n, i: (i, 0)),
                           pl.BlockSpec((KS, tk, tn), lambda n, i: (0, 0, n)),
                           pl.BlockSpec((tm, tn), lambda n, i: (i, n))],
                 out_specs=[pl.BlockSpec((tm, tn), lambda n, i: (i, n))], out_shape=[jax.ShapeDtypeStruct((S, N), F32)],
                 args=(a, b4, add), sem=("parallel", "parallel"), job=job, store=store)[0]


def _matmul_nt_rows(a, b4, name, add=None, add_scale=1.0, job=None, store=None):
    S, K = a.shape
    nq = b4.shape[1]
    tm = min(S, 512)
    out_spec = pl.BlockSpec((tm, N_CHIPS * nq), lambda i: (i, 0))
    in_specs = [pl.BlockSpec((tm, K), lambda i: (i, 0)), pl.BlockSpec(b4.shape, lambda i: (0, 0, 0))]
    if add is None:
        def body(a_ref, b_ref, o_ref):
            av = a_ref[...]
            for j in range(N_CHIPS):
                o_ref[:, j * nq:(j + 1) * nq] = _nt(av, b_ref[j])
        args = (a, b4)
    else:
        def body(a_ref, b_ref, add_ref, o_ref):
            av = a_ref[...]
            for j in range(N_CHIPS):
                cols = slice(j * nq, (j + 1) * nq)
                o_ref[:, cols] = _nt(av, b_ref[j]) + add_scale * add_ref[:, cols]
        in_specs.append(out_spec)
        args = (a, b4, add)

    return _call(body, name=name, grid=(S // tm,), in_specs=in_specs,
                 out_specs=[out_spec], out_shape=[jax.ShapeDtypeStruct((S, N_CHIPS * nq), F32)],
                 args=args, sem=("parallel",), job=job, store=store)[0]


def _matmul_tn(a, b, R, C, name, by_rows, job=None, store=None):
    S = a.shape[0]
    tn = min(C, 1024 if R <= 512 else 512)
    nt = C // tn

    def body(a_ref, b_ref, o_ref, at_ref):
        @pl.when(pl.program_id(1) == 0)
        def _():
            at_ref[...] = a_ref[...].T
        o_ref[...] = _nn(at_ref[...], b_ref[...])

    if by_rows:
        a_spec = pl.BlockSpec((S, R), lambda j, n: (0, j))
        b_spec = pl.BlockSpec((S, tn), lambda j, n: (0, n))
    else:
        a_spec = pl.BlockSpec((S, R), lambda j, n: (0, 0))
        b_spec = pl.BlockSpec((S, tn), lambda j, n: (0, j * nt + n))
    return _call(body, name=name, grid=(N_CHIPS, nt), in_specs=[a_spec, b_spec],
                 out_specs=[pl.BlockSpec((None, R, tn), lambda j, n: (j, 0, n))],
                 out_shape=[jax.ShapeDtypeStruct((N_CHIPS, R, C), F32)],
                 scratch_shapes=[pltpu.VMEM((R, S), BF16)],
                 args=(a, b), sem=("parallel", "arbitrary"), job=job, store=store)[0]


def _pool_bwd(u, gp, dc, w_pool, ps, l, job=None, store=None):
    S = u.shape[0]
    tm = min(S, 256)
    hb = tm // POOL_HALO
    ngrp = len(POOL_WINDOWS)
    gw = POOL_W // ngrp
    rr = gw // N_CHIPS

    def body(u_ref, uh_ref, gp_ref, dc_ref, wp_ref, ps_ref, dd_ref, dgp_ref, gps_ref, gwp_ref):
        i = pl.program_id(0)

        @pl.when(i == 0)
        def _():
            gps_ref[...] = jnp.zeros_like(gps_ref)
            gwp_ref[...] = jnp.zeros_like(gwp_ref)
        uv = u_ref[...]
        halo = jnp.where(i > 0, uh_ref[...], 0.0)
        diffs = _pool_diff(uv, halo, i, tm)
        for g in range(ngrp):
            cols = slice(g * gw, (g + 1) * gw)
            w = _pool_weight(wp_ref, g)
            db = diffs[g].astype(BF16)
            mm = _nn(db, w)
            gpv = gp_ref[:, cols].astype(F32)
            sg = _sigmoid(gpv)
            si = gpv * sg
            dsi = sg * (1.0 + gpv * (1.0 - sg))
            dcb = dc_ref[:, cols].astype(F32)
            psv = ps_ref[:, cols]
            d_mm = (dcb * si) * psv
            gps_ref[:, cols] += jnp.sum((dcb * si) * mm, axis=0, keepdims=True)
            dgp_ref[:, cols] = ((dcb * (mm * psv)) * dsi).astype(ACT)
            d_mmb = d_mm.astype(BF16)
            dd_ref[:, cols] = _nt(d_mmb, w).astype(ACT)
            gwt = _tn(db, d_mmb)
            for j in range(N_CHIPS):
                gwp_ref[j, g * rr:(g + 1) * rr, :] += gwt[j * rr:(j + 1) * rr, :]

    row = pl.BlockSpec((tm, POOL_W), lambda i: (i, 0))
    wspec = pl.BlockSpec(w_pool.shape, lambda i: (0, 0, 0))
    return _call(
        body, name="pool_bwd", grid=(S // tm,),
        in_specs=[row, pl.BlockSpec((POOL_HALO, POOL_W), lambda i: (jnp.maximum(i * hb - 1, 0), 0)), row,
                  pl.BlockSpec((tm, POOL_W), lambda i: (i, 1)), wspec, pl.BlockSpec((None, 1, POOL_W), lambda i: (l, 0, 0))],
        out_specs=[row, row, pl.BlockSpec((1, POOL_W), lambda i: (0, 0)), wspec],
        out_shape=[jax.ShapeDtypeStruct((S, POOL_W), ACT), jax.ShapeDtypeStruct((S, POOL_W), ACT),
                   jax.ShapeDtypeStruct((1, POOL_W), F32), jax.ShapeDtypeStruct(w_pool.shape, F32)],
        args=(u, u, gp, dc, w_pool, ps), sem=("arbitrary",), job=job, store=store)


def _pool_window_t(dd, halo_next, tile_index, tm):
    gw = POOL_W // len(POOL_WINDOWS)
    n = tm + POOL_HALO
    t = tile_index * tm + lax.broadcasted_iota(I32, (n, gw), 0)
    xh = jnp.concatenate([dd, halo_next], axis=0)
    outs = []
    for g, w in enumerate(POOL_WINDOWS):
        cols = slice(g * gw, (g + 1) * gw)
        cnt = jnp.minimum(t + 1, w).astype(F32)
        s = xh[:, cols] / cnt
        step = 1
        while step < w:
            s = s + pltpu.roll(s, n - step, axis=0)
            step *= 2
        outs.append(s[:tm] - dd[:, cols])
    return outs


def _assemble_dh(dq, dk, dv, dga, dd, dgp):
    S = dq.shape[0]
    tm = min(S, 256)
    hb = tm // POOL_HALO
    nt = S // tm

    def body(dq_ref, dk_ref, dv_ref, dga_ref, dd_ref, ddn_ref, dgp_ref, dh_ref, gb_ref):
        i = pl.program_id(0)

        @pl.when(i == 0)
        def _():
            gb_ref[...] = jnp.zeros_like(gb_ref)
        halo = jnp.where(i < nt - 1, ddn_ref[...].astype(F32), 0.0)
        du = jnp.concatenate(_pool_window_t(dd_ref[...].astype(F32), halo, i, tm), axis=1)
        dkv = dk_ref[...]
        dgav = dga_ref[...].astype(F32)
        parts = [(OFF_Q, dq_ref[...].astype(F32)), (OFF_KA, dkv), (OFF_KB, dkv), (OFF_V, dv_ref[...]), (OFF_GA, dgav),
                 (OFF_U, du[:, 0:896]), (OFF_UA, du[:, 896:1024]), (OFF_UB, du[:, 896:1024]), (OFF_GP, dgp_ref[...].astype(F32))]
        for off, val in parts:
            w = val.shape[1]
            dh_ref[:, off:off + w] = val.astype(BF16)
            gb_ref[:, off:off + w] += jnp.sum(val, axis=0, keepdims=True)

    def row(w):
        return pl.BlockSpec((tm, w), lambda i: (i, 0))

    return _pcall(
        body, name="assemble_dh", grid=(nt,),
        in_specs=[row(ATTN_W), row(KV_W), row(KV_W), row(ATTN_W), row(POOL_W),
                  pl.BlockSpec((POOL_HALO, POOL_W), lambda i: (jnp.minimum((i + 1) * hb, S // POOL_HALO - 1), 0)), row(POOL_W)],
        out_specs=[row(EXT), pl.BlockSpec((1, EXT), lambda i: (0, 0))],
        out_shape=[jax.ShapeDtypeStruct((S, EXT), BF16), jax.ShapeDtypeStruct((1, EXT), F32)],
        compiler_params=_cparams(("arbitrary",)),
    )(dq, dk, dv, dga, dd, dd, dgp)


def _attn_bwd(q, k, v, ga, o, dc, lse, bias, sinks, dbias_in, l, job, store):
    S = q.shape[0]
    nblk = S // WINDOW

    def body(sink_ref, q_ref, k_ref, v_ref, ga_ref, o_ref, dc_ref, lse_ref, bias_ref, dbin_ref,
             dq_ref, dga_ref, dk_ref, dv_ref, db_ref, ds_ref, kd, vd, qs, dos, dls, ss, dps, dss, pss, dqs):
        n = pl.program_id(0)

        @pl.when(n == 0)
        def _():
            _dup_heads(k_ref, kd)
            _dup_heads(v_ref, vd)
            dk_ref[...] = jnp.zeros_like(dk_ref)
            dv_ref[...] = jnp.zeros_like(dv_ref)
            db_ref[...] = dbin_ref[...]
            ds_ref[...] = jnp.zeros_like(ds_ref)

        cur = pl.multiple_of(n * WINDOW, WINDOW)
        prev = pl.multiple_of(jnp.maximum(n - 1, 0) * WINDOW, WINDOW)
        lane = lax.broadcasted_iota(I32, (WINDOW, LANES), 1)
        lane8 = lax.broadcasted_iota(I32, (8, LANES), 1)
        lo = lane < HEAD_DIM
        tri = lane > lax.broadcasted_iota(I32, (WINDOW, LANES), 0)
        lse_t = lse_ref[...]
        dk_t, dv_t = [], []
        dsk = jnp.zeros((8, LANES), F32)
        for g in range(N_KV):
            kb = _kv_block(kd, g, prev, cur)
            vb = _kv_block(vd, g, prev, cur)
            for j in range(KV_RATIO // 2):
                pair = g * (KV_RATIO // 2) + j
                sl = slice(LANES * pair, LANES * (pair + 1))
                qp = q_ref[:, sl]
                op = o_ref[:, sl].astype(F32)
                dcp = dc_ref[:, sl].astype(F32)
                gav = ga_ref[:, sl].astype(F32)
                sg = _sigmoid(gav)
                d_o = dcp * (gav * sg)
                dga_ref[:, sl] = ((dcp * op) * (sg * (1.0 + gav * (1.0 - sg)))).astype(ACT)
                prod = d_o * op
                for par in range(2):
                    msk = lo if par == 0 else jnp.logical_not(lo)
                    rows = _rows(2 * j + par)
                    qs[rows, :] = jnp.where(msk, qp, jnp.zeros_like(qp))
                    dos[rows, :] = jnp.where(msk, d_o, 0.0).astype(BF16)
                    delta = jnp.sum(jnp.where(msk, prod, 0.0), axis=1, keepdims=True)
                    dls[rows, :] = jnp.broadcast_to(delta, (WINDOW, LANES))
            ss[...] = _nt(qs[...], kb)
            dps[...] = _nt(dos[...], vb)
            for i in range(KV_RATIO):
                h = g * KV_RATIO + i
                rows = _rows(i)
                lse_h = jnp.sum(jnp.where(lane == h, lse_t, 0.0), axis=1, keepdims=True)
                p = jnp.exp(_fold(ss[rows, :], tri) + bias_ref[h] - lse_h)
                delta = dls[rows, :]
                dsc = p * (_fold(dps[rows, :], tri) - delta)
                db_ref[h] += dsc
                psink = jnp.exp(sink_ref[l, h] - lse_h)
                dsk = dsk + jnp.where(lane8 == h, -jnp.sum(psink * delta, axis=0, keepdims=True), 0.0)
                dss[rows, :] = _unfold(dsc, tri)
                pss[rows, :] = _unfold(p, tri)
            dqs[...] = _nn(dss[...], kb) * QK_SCALE
            for j in range(KV_RATIO // 2):
                pair = g * (KV_RATIO // 2) + j
                dq_ref[:, LANES * pair:LANES * (pair + 1)] = jnp.where(lo, dqs[_rows(2 * j), :], dqs[_rows(2 * j + 1), :]).astype(ACT)
            dk_t.append(_tn(qs[...], dss[...]))
            dv_t.append(_tn(dos[...], pss[...]))

        def untranspose(acc):
            return jnp.concatenate([a[:HEAD_DIM] + a[HEAD_DIM:] for a in acc], axis=0).T

        dkb = untranspose(dk_t)
        dvb = untranspose(dv_t)
        dk_ref[pl.ds(prev, WINDOW), :] += dkb[:WINDOW]
        dk_ref[pl.ds(cur, WINDOW), :] += dkb[WINDOW:]
        dv_ref[pl.ds(prev, WINDOW), :] += dvb[:WINDOW]
        dv_ref[pl.ds(cur, WINDOW), :] += dvb[WINDOW:]
        ds_ref[...] += dsk

    blk = pl.BlockSpec((WINDOW, ATTN_W), lambda n: (n, 0))
    full_kv = pl.BlockSpec((S, KV_W), lambda n: (0, 0))
    full_b = pl.BlockSpec((N_HEADS, WINDOW, WINDOW), lambda n: (0, 0, 0))
    stack = KV_RATIO * WINDOW
    return _call(
        body, name="attn_bwd", grid=(nblk,),
        in_specs=[pl.BlockSpec(memory_space=pltpu.SMEM), blk, full_kv, full_kv, blk, blk, blk,
                  pl.BlockSpec((WINDOW, LANES), lambda n: (n, 0)),
                  pl.BlockSpec((None, N_HEADS, WINDOW, WINDOW), lambda n: (jnp.minimum(n, 1), 0, 0, 0)), full_b],
        out_specs=[blk, blk, full_kv, full_kv, full_b, pl.BlockSpec((8, LANES), lambda n: (0, 0))],
        out_shape=[jax.ShapeDtypeStruct((S, ATTN_W), ACT), jax.ShapeDtypeStruct((S, ATTN_W), ACT), jax.ShapeDtypeStruct((S, KV_W), F32),
                   jax.ShapeDtypeStruct((S, KV_W), F32), jax.ShapeDtypeStruct((N_HEADS, WINDOW, WINDOW), F32),
                   jax.ShapeDtypeStruct((8, LANES), F32)],
        scratch_shapes=[pltpu.VMEM((N_KV, S, LANES), BF16), pltpu.VMEM((N_KV, S, LANES), BF16),
                        pltpu.VMEM((stack, LANES), BF16), pltpu.VMEM((stack, LANES), BF16), pltpu.VMEM((stack, LANES), F32),
                        pltpu.VMEM((stack, 2 * WINDOW), F32), pltpu.VMEM((stack, 2 * WINDOW), F32),
                        pltpu.VMEM((stack, 2 * WINDOW), BF16), pltpu.VMEM((stack, 2 * WINDOW), BF16),
                        pltpu.VMEM((stack, LANES), F32)],
        args=(sinks, q, k, v, ga, o, dc, lse, bias, dbias_in), sem=("arbitrary",), job=job, store=store)


def _pack_small(arrs):
    flat = []
    for a in arrs:
        v = a.reshape(-1)
        flat.append(jnp.pad(v, (0, (-v.shape[0]) % LANES)))
    v = jnp.concatenate(flat)
    v = jnp.pad(v, (0, (-v.shape[0]) % (8 * LANES)))
    return v.reshape(-1, LANES)


def _unpack_small(packed, shapes):
    v = packed.reshape(-1)
    outs, off = [], 0
    for shp in shapes:
        n = math.prod(shp)
        outs.append(v[off:off + n].reshape(shp))
        off += n + (-n) % LANES
    return outs


def _bias_to_ext(b):
    L = b.shape[0]
    z = jnp.zeros((L, HALF_TILE), b.dtype)
    parts = []
    for j in range(N_CHIPS):
        seg = b[:, j * SHARD:(j + 1) * SHARD]
        parts += [z, seg] if j % 2 else [seg, z]
    return jnp.concatenate(parts, axis=1).reshape(L, 1, EXT)


def _bias_from_ext(g):
    parts = []
    for j in range(N_CHIPS):
        o = j * SHARD_P + (HALF_TILE if j % 2 else 0)
        parts.append(g[:, o:o + SHARD])
    return jnp.concatenate(parts, axis=1)


def kernel(x, p, w_in, b_in, w_out, attn_sinks, rel_bias, w_pool, pool_scale, w_ple, w_gate_ple, ln_gain, ln_bias, loss_target, m_w_in, m_b_in, m_w_out, m_attn_sinks, m_rel_bias, m_w_pool, m_pool_scale, m_w_ple, m_w_gate_ple, m_ln_gain, m_ln_bias, v_w_in, v_b_in, v_w_out, v_attn_sinks, v_rel_bias, v_w_pool, v_pool_scale, v_w_ple, v_w_gate_ple, v_ln_gain, v_ln_bias):
    L = w_in.shape[0]
    S, D = x.shape[1], x.shape[2]
    assert D == D_MODEL and w_in.shape[2] == SHARD and S % WINDOW == 0
    alpha = (2.0 * L) ** 0.25
    xc, yc, cc = _mesh_pos()
    idx = jnp.stack([2 * xc + yc, yc, cc]).astype(I32)
    store = {}

    def wkeys(l, names):
        return [("w", l, t) for t in names]

    def weight(l, t):
        return _unhalves(store["w", l, t])

    w_in_t = jnp.swapaxes(w_in, 1, 2)
    w_pool2 = w_pool.reshape(L, w_pool.shape[1] * w_pool.shape[2], w_pool.shape[3])
    for l in range(L):
        store["w", l, "in"] = _halves(_piece_in(w_in_t, l, idx))
        store["w", l, "out"] = _halves(_piece(w_out, l, idx, "piece_out"))
        store["w", l, "gate"] = _halves(_piece(w_gate_ple, l, idx, "piece_gate"))
        store["w", l, "ple"] = _halves(_piece(w_ple, l, idx, "piece_ple"))
        store["w", l, "pool"] = _halves(_piece(w_pool2, l, idx, "piece_pool"))
    _allgather_now(wkeys(0, WEIGHTS), store)

    b_ext = _bias_to_ext(b_in)
    ps3 = pool_scale.reshape(L, 1, POOL_W)
    gain3 = ln_gain.reshape(L, 1, D)
    bias3 = ln_bias.reshape(L, 1, D)
    onehot = _bucket_onehot()
    bias_hqk = _masked_bias(_bias_fwd(rel_bias.T, onehot).reshape(N_HEADS, WINDOW, WINDOW))

    xs = x[0]
    xb = _to_bf16(xs)
    pb = _to_bf16(p.reshape(L * S, p.shape[3])).reshape(L, S, p.shape[3])
    saved = []
    x_norm, x_gain, x_bias, x_l = xs, jnp.ones((1, 1, D), F32), jnp.zeros((1, 1, D), F32), 0
    late = ["out", "ple", "pool"]
    for l in range(L):
        nxt = l + 1 < L
        job = _Job()
        if nxt:
            job.add(_ag_ici, wkeys(l + 1, ["in"]), 3)
        if l >= 1:
            job.add(_ag_fwd, wkeys(l, ["gate"]), 3)
        q, k, v, ga, u, gp = _inproj(xb, weight(l, "in"), b_ext, l, job, store)
        gl = _matmul_nn(xb, weight(l, "gate"), "gate_logits", None, None)
        job = _Job().add(_ag_fwd, wkeys(l + 1, ["in"]), 3).add(_ag_ici, wkeys(l + 1, late), 9) if nxt else None
        o, ca, lse = _attn_fwd(q, k, v, ga, bias_hqk, attn_sinks, l, job, store)
        c = _pool_fwd(u, gp, ca, weight(l, "pool"), ps3, l, None, None)
        job = _Job().add(_ag_fwd, wkeys(l + 1, late), 9).add(_ag_ici, wkeys(l + 1, ["gate"]), 3) if nxt else None
        yb, xhat, rstd = _outproj_ln(c, x_norm, x_gain, x_bias, x_l, gl, pb[l], weight(l, "out"), weight(l, "ple"), gain3, bias3, l,
                                     alpha, job, store)
        saved.append(dict(xb=xb, q=q, k=k, v=v, ga=ga, u=u, gp=gp, gl=gl, o=o, lse=lse, c=c, xhat=xhat, rstd=rstd))
        x_norm, x_gain, x_bias, x_l, xb = xhat, gain3, bias3, l, yb

    dy, loss_acc = _loss_and_grad(x_norm, gain3, bias3, L - 1, loss_target[0])
    loss = lax.psum(0.5 * loss_acc[0, 0], ("x", "y", "c"))

    shapes = {t: store["w", 0, t].shape for t in WEIGHTS}
    for t in WEIGHTS:
        store["full", t] = lax.empty((L,) + shapes[t][1:], F32)

    def rs_keys(kind, l, names):
        return [(kind, l, t) for t in names]

    def rs_pair_job(l, names):
        for t in names:
            store["ra", l, t] = lax.empty((N_CHIPS,) + shapes[t][2:], F32)
        return _Job().add(_rs_pair, rs_keys("g", l, names) + rs_keys("ra", l, names), len(names))

    def rs_pair_add(l, names):
        for t in names:
            p32, p16 = _rs_pair_add(store["g", l, t], store["ra", l, t], idx, "rs_pair_add_" + t)
            store["p32", l, t], store["p16", l, t] = p32, p16
            store["rb", l, t] = lax.empty((3,) + shapes[t][2:], BF16)

    def rs_ici_job(l, names):
        return _Job().add(_rs_ici, rs_keys("p16", l, names) + rs_keys("rb", l, names), 3 * len(names))

    def rs_chip_add(l, names):
        for t in names:
            store["full", t] = _rs_chip_add(store["p32", l, t], store["rb", l, t], store["full", t], l, idx, "rs_chip_add_" + t)

    def rs_share_job(l):
        return _Job().add(functools.partial(_rs_share, layer=l), [("full", t) for t in WEIGHTS], len(WEIGHTS))

    dbias = jnp.zeros((N_HEADS, WINDOW, WINDOW), F32)
    small = [None] * L
    h_in = shapes["in"][2]
    head_rows = (h_in * 5 // 8) // 16 * 16
    early = ["out", "gate", "ple"]

    def in_ici_job(l, rows):
        return _Job().add(functools.partial(_rs_ici, rows=rows), rs_keys("p16", l, ["in"]) + rs_keys("rb", l, ["in"]), 3)

    for l in reversed(range(L)):
        sv = saved[l]
        pl_l = pb[l]
        job = in_ici_job(l + 1, (head_rows, h_in - head_rows)) if l + 1 < L else None
        dz, dzb, d_e, d_gl, dc, ggain, gbias = _ln_bwd(dy, sv["xhat"], sv["rstd"], sv["gl"], pl_l, weight(l, "ple"), weight(l, "out"),
                                                       gain3, l, job, store)
        if l + 1 < L:
            rs_chip_add(l + 1, ["in"])
        g_out = _matmul_tn(sv["c"], dzb, D // N_CHIPS, D, "grad_w_out", by_rows=True,
                           job=rs_share_job(l + 1) if l + 1 < L else None, store=store)
        g_ple = _matmul_tn(pl_l, d_e, pl_l.shape[1], D // N_CHIPS, "grad_w_ple", by_rows=False)
        g_gate = _matmul_tn(sv["xb"], d_gl, D // N_CHIPS, D, "grad_w_gate", by_rows=True)
        for t, g in zip(early, (g_out, g_gate, g_ple)):
            store["g", l, t] = _halves(g)
        dd, dgp, gps, g_pool = _pool_bwd(sv["u"], sv["gp"], dc, weight(l, "pool"), ps3, l, rs_pair_job(l, early), store)
        store["g", l, "pool"] = _halves(g_pool)
        rs_pair_add(l, early)
        job = rs_pair_job(l, ["pool"])
        job.add(_rs_ici, rs_keys("p16", l, ["out"]) + rs_keys("rb", l, ["out"]), 3)
        dq, dga, dk, dv, dbias, dsink = _attn_bwd(sv["q"], sv["k"], sv["v"], sv["ga"], sv["o"], dc, sv["lse"], bias_hqk,
                                                  attn_sinks, dbias, l, job, store)
        rs_pair_add(l, ["pool"])
        dh, gbe = _assemble_dh(dq, dk, dv, dga, dd, dgp)
        g_in = _matmul_tn(dh, sv["xb"], SHARD_P, D, "grad_w_in", by_rows=True, job=rs_ici_job(l, ["gate", "ple", "pool"]), store=store)
        rs_chip_add(l, ["out", "gate", "ple", "pool"])
        store["g", l, "in"] = _halves(g_in)
        dx1 = _matmul_nt_rows(d_gl, weight(l, "gate"), "d_x_gate", dz, alpha, job=rs_pair_job(l, ["in"]), store=store)
        rs_pair_add(l, ["in"])
        dy = _matmul_nn_acc(dh, weight(l, "in"), "d_x", dx1, 1.0, in_ici_job(l, (0, head_rows) if l > 0 else (0, h_in)), store)
        small[l] = dict(b_in=_bias_from_ext(gbe)[0], sinks=dsink[0, :N_HEADS], ps=gps[0], gain=ggain[0], bias=gbias[0])
    rs_chip_add(0, ["in"])
    grad_x = dy[None]

    _run_job("rs_pair_share", rs_share_job(0), store)
    full = {t: _unhalves(store["full", t]) for t in WEIGHTS}

    def t_back(a):
        return jnp.swapaxes(a, 1, 2)

    def pool4(a):
        return a.reshape(w_pool.shape)

    r_in = _adamw(w_in_t, full["in"], jnp.swapaxes(m_w_in, 1, 2), jnp.swapaxes(v_w_in, 1, 2), idx, "adamw_w_in", tr=HALF_TILE)
    gw_in, dw_in, nm_in, nv_in = [t_back(a) for a in r_in]
    gw_out, dw_out, nm_out, nv_out = _adamw(w_out, full["out"], m_w_out, v_w_out, idx, "adamw_w_out")
    gw_gate, dw_gate, nm_gate, nv_gate = _adamw(w_gate_ple, full["gate"], m_w_gate_ple, v_w_gate_ple, idx, "adamw_w_gate")
    gw_ple, dw_ple, nm_ple, nv_ple = _adamw(w_ple, full["ple"], m_w_ple, v_w_ple, idx, "adamw_w_ple")
    r_pool = _adamw(w_pool2, full["pool"], m_w_pool.reshape(w_pool2.shape), v_w_pool.reshape(w_pool2.shape), idx, "adamw_w_pool")
    gw_pool, dw_pool, nm_pool, nv_pool = [pool4(a) for a in r_pool]

    g_rel = _bias_bwd(dbias.reshape(N_HEADS, -1), onehot).T
    small_shapes = [b_in.shape, attn_sinks.shape, rel_bias.shape, pool_scale.shape, ln_gain.shape, ln_bias.shape]
    g_small = [jnp.stack([small[l]["b_in"] for l in range(L)]), jnp.stack([small[l]["sinks"] for l in range(L)]), g_rel,
               jnp.stack([small[l]["ps"] for l in range(L)]), jnp.stack([small[l]["gain"] for l in range(L)]),
               jnp.stack([small[l]["bias"] for l in range(L)])]
    packed = _small_allreduce_adamw(
        _pack_small(g_small),
        _pack_small([b_in, attn_sinks, rel_bias, pool_scale, ln_gain, ln_bias]),
        _pack_small([m_b_in, m_attn_sinks, m_rel_bias, m_pool_scale, m_ln_gain, m_ln_bias]),
        _pack_small([v_b_in, v_attn_sinks, v_rel_bias, v_pool_scale, v_ln_gain, v_ln_bias]))
    sg, sd, sm, sv_ = [_unpack_small(a, small_shapes) for a in packed]

    def order(big, sm_):
        return (big[0], sm_[0], big[1], sm_[1], sm_[2], big[2], sm_[3], big[3], big[4], sm_[4], sm_[5])

    return (loss, grad_x,
            *order((gw_in, gw_out, gw_pool, gw_ple, gw_gate), sg),
            *order((dw_in, dw_out, dw_pool, dw_ple, dw_gate), sd),
            *order((nm_in, nm_out, nm_pool, nm_ple, nm_gate), sm),
            *order((nv_in, nv_out, nv_pool, nv_ple, nv_gate), sv_))
```

```python
import functools
import math

import jax
import jax.numpy as jnp
from jax import lax
from jax.experimental import pallas as pl
from jax.experimental.pallas import tpu as pltpu

F32 = jnp.float32
BF16 = jnp.bfloat16
ACT = jnp.bfloat16
I32 = jnp.int32
MESH = pl.DeviceIdType.MESH

HEAD_DIM = 64
QK_SCALE = HEAD_DIM ** -0.5
WINDOW = 128
KV_RATIO = 8
POOL_WINDOWS = (2, 4, 8, 16)
POOL_HALO = 16
REL_BUCKETS = 32
REL_MAX_DIST = 128
LN_EPS = 1e-5
ADAM_LR, ADAM_B1, ADAM_B2, ADAM_EPS, ADAM_WD, ADAM_STEP = 0.001, 0.9, 0.999, 1e-08, 0.01, 10

LANES = 128
VMEM_LIMIT = 52 * 1024 * 1024
N_CHIPS = 4
N_DEV = 8

D_MODEL = 2048
ATTN_W = 1024
POOL_W = 1024
KV_W = 128
N_HEADS = ATTN_W // HEAD_DIM
N_KV = N_HEADS // KV_RATIO
IN_COLS = 4352
SHARD = IN_COLS // N_CHIPS
SHARD_P = 1152
EXT = N_CHIPS * SHARD_P
HALF_TILE = SHARD_P - SHARD
OFF_Q, OFF_KA, OFF_KB, OFF_V, OFF_GA, OFF_U, OFF_UA, OFF_UB, OFF_GP = 0, 1024, 1152, 1280, 1408, 2432, 3328, 3456, 3584
WEIGHTS = ("in", "out", "gate", "ple", "pool")


def _cparams(sem=None):
    return pltpu.CompilerParams(dimension_semantics=sem, vmem_limit_bytes=VMEM_LIMIT)


def _pcall(body, **kw):
    return pl.pallas_call(body, **kw)


def _sigmoid(x):
    return 1.0 / (1.0 + jnp.exp(-x))


def _nt(a, b):
    return lax.dot_general(a, b, (((1,), (1,)), ((), ())), preferred_element_type=F32)


def _tn(a, b):
    return lax.dot_general(a, b, (((0,), (0,)), ((), ())), preferred_element_type=F32)


def _nn(a, b):
    return jnp.dot(a, b, preferred_element_type=F32)


def _mesh_pos():
    return lax.axis_index("x"), lax.axis_index("y"), lax.axis_index("c")


def _peer_chips(x, y):
    return [(1 - x, y), (x, 1 - y), (1 - x, 1 - y)]


def _row_tile(rows, cap=256):
    t = min(rows, cap)
    while rows % t or t % 16:
        t -= 1
    return t


def _hbm_spec():
    return pl.BlockSpec(memory_space=pltpu.HBM)


def _halves(a):
    return a.reshape(a.shape[:-2] + (2, a.shape[-2] // 2, a.shape[-1]))


def _unhalves(a):
    return a.reshape(a.shape[:-3] + (2 * a.shape[-2], a.shape[-1]))


class _remote:
    def __init__(self, src, dst, send, recv, i, device):
        self.args = dict(src_ref=src, dst_ref=dst, send_sem=send.at[i], recv_sem=recv.at[i], device_id=device, device_id_type=MESH)

    def start(self):
        pltpu.make_async_remote_copy(**self.args).start()

    def wait_recv(self):
        pltpu.make_async_remote_copy(**self.args).wait_recv()

    def wait_send(self):
        pltpu.make_async_remote_copy(**self.args).wait_send()


class _Job:
    def __init__(self):
        self.keys, self.parts, self.n = [], [], 0

    def add(self, fn, keys, n):
        self.parts.append((fn, len(self.keys), len(keys), self.n))
        self.keys += list(keys)
        self.n += n
        return self

    def build(self, refs, send, recv):
        out = []
        for fn, i0, nb, base in self.parts:
            out += fn(refs[i0:i0 + nb], send, recv, base)
        return out


def _ag_ici(refs, send, recv, base):
    x, y, c = _mesh_pos()
    me = 2 * x + y
    out = []
    for t, g in enumerate(refs):
        for k, chip in enumerate(_peer_chips(x, y)):
            i = base + 3 * t + k
            dev = (*chip, c)
            out.append((_remote(g.at[me, c], g.at[me, c], send, recv, i, dev),
                        _remote(g.at[me, c], g.at[2 * chip[0] + chip[1], c], send, recv, i, dev)))
    return out


def _ag_fwd(refs, send, recv, base):
    x, y, c = _mesh_pos()
    out = []
    for t, g in enumerate(refs):
        for k, chip in enumerate(_peer_chips(x, y)):
            i = base + 3 * t + k
            slot = 2 * chip[0] + chip[1]
            dev = (x, y, 1 - c)
            out.append((_remote(g.at[slot, c], g.at[slot, c], send, recv, i, dev),
                        _remote(g.at[slot, c], g.at[slot, 1 - c], send, recv, i, dev)))
    return out


def _rs_pair(refs, send, recv, base):
    x, y, c = _mesh_pos()
    n = len(refs) // 2
    out = []
    for t in range(n):
        cp = _remote(refs[t].at[:, 1 - c], refs[n + t], send, recv, base + t, (x, y, 1 - c))
        out.append((cp, cp))
    return out


def _rs_ici(refs, send, recv, base, rows=None):
    x, y, c = _mesh_pos()
    n = len(refs) // 2
    rsl = slice(None) if rows is None else pl.ds(rows[0], rows[1])
    out = []
    for t in range(n):
        for k, chip in enumerate(_peer_chips(x, y)):
            cp = _remote(refs[t].at[2 * chip[0] + chip[1], rsl], refs[n + t].at[k, rsl], send, recv, base + 3 * t + k, (*chip, c))
            out.append((cp, cp))
    return out


def _rs_share(refs, send, recv, base, layer):
    x, y, c = _mesh_pos()
    out = []
    for t, f in enumerate(refs):
        dev = (x, y, 1 - c)
        out.append((_remote(f.at[layer, c], f.at[layer, c], send, recv, base + t, dev),
                    _remote(f.at[layer, c], f.at[layer, 1 - c], send, recv, base + t, dev)))
    return out


def _call(body, *, name, grid, in_specs, out_specs, out_shape, args, scratch_shapes=(), sem=None, job=None, store=None):
    in_specs, out_specs, out_shape, scratch_shapes = list(in_specs), list(out_specs), list(out_shape), list(scratch_shapes)
    if job is None or job.n == 0:
        return list(_pcall(body, name=name, grid=grid, in_specs=in_specs, out_specs=out_specs, out_shape=out_shape,
                           scratch_shapes=scratch_shapes, compiler_params=_cparams(sem))(*args))
    bufs = [store[k] for k in job.keys]
    nb, n_in, n_out, n_sc = len(bufs), len(args), len(out_shape), len(scratch_shapes)

    def wrapped(*refs):
        ins = refs[:n_in]
        outs = refs[n_in + nb:n_in + nb + n_out]
        cb = refs[n_in + nb + n_out:n_in + 2 * nb + n_out]
        scratch = refs[n_in + 2 * nb + n_out:n_in + 2 * nb + n_out + n_sc]
        send, recv = refs[-2:]
        ids = [pl.program_id(a) for a in range(len(grid))]
        first = functools.reduce(jnp.logical_and, [i == 0 for i in ids])
        last = functools.reduce(jnp.logical_and, [i == g - 1 for i, g in zip(ids, grid)])

        @pl.when(first)
        def _():
            for s, _r in job.build(cb, send, recv):
                s.start()

        body(*ins, *outs, *scratch)

        @pl.when(last)
        def _():
            pairs = job.build(cb, send, recv)
            for _s, r in pairs:
                r.wait_recv()
            for s, _r in pairs:
                s.wait_send()

    res = _pcall(
        wrapped, name=name, grid=grid, in_specs=in_specs + [_hbm_spec()] * nb, out_specs=out_specs + [_hbm_spec()] * nb,
        out_shape=out_shape + [jax.ShapeDtypeStruct(b.shape, b.dtype) for b in bufs],
        scratch_shapes=scratch_shapes + [pltpu.SemaphoreType.DMA((job.n,)), pltpu.SemaphoreType.DMA((job.n,))],
        input_output_aliases={n_in + i: n_out + i for i in range(nb)},
        compiler_params=_cparams(("arbitrary",) * len(grid)))(*args, *bufs)
    for k, v in zip(job.keys, res[n_out:]):
        store[k] = v
    return list(res[:n_out])


def _run_job(name, job, store):
    bufs = [store[k] for k in job.keys]
    nb = len(bufs)

    def body(*refs):
        send, recv = refs[-2:]
        pairs = job.build(refs[nb:2 * nb], send, recv)
        for s, _r in pairs:
            s.start()
        for _s, r in pairs:
            r.wait_recv()
        for s, _r in pairs:
            s.wait_send()

    res = _pcall(body, name=name, in_specs=[_hbm_spec()] * nb, out_specs=[_hbm_spec()] * nb,
                 out_shape=[jax.ShapeDtypeStruct(b.shape, b.dtype) for b in bufs],
                 scratch_shapes=[pltpu.SemaphoreType.DMA((job.n,)), pltpu.SemaphoreType.DMA((job.n,))],
                 input_output_aliases={i: i for i in range(nb)})(*bufs)
    for k, v in zip(job.keys, res):
        store[k] = v


def _allgather_now(keys, store):
    bufs = [store[k] for k in keys]
    nb = len(bufs)

    def body(*refs):
        send, recv = refs[-2:]
        g = refs[nb:2 * nb]
        ici = _ag_ici(g, send, recv, 0)
        fwd = _ag_fwd(g, send, recv, 3 * nb)
        for s, _r in ici:
            s.start()
        for (_s, r), (fs, _fr) in zip(ici, fwd):
            r.wait_recv()
            fs.start()
        for _fs, fr in fwd:
            fr.wait_recv()
        for s, _r in ici + fwd:
            s.wait_send()

    res = _pcall(body, name="allgather_first_layer", in_specs=[_hbm_spec()] * nb, out_specs=[_hbm_spec()] * nb,
                 out_shape=[jax.ShapeDtypeStruct(b.shape, b.dtype) for b in bufs],
                 scratch_shapes=[pltpu.SemaphoreType.DMA((6 * nb,)), pltpu.SemaphoreType.DMA((6 * nb,))],
                 input_output_aliases={i: i for i in range(nb)})(*bufs)
    for k, v in zip(keys, res):
        store[k] = v


def _piece(w, l, idx, name):
    _, R, C = w.shape
    tr = _row_tile(R)

    def body(s_ref, w_ref, o_ref):
        del s_ref
        o_ref[...] = w_ref[...].astype(BF16)

    gs = pltpu.PrefetchScalarGridSpec(
        num_scalar_prefetch=1, grid=(R // tr,),
        in_specs=[pl.BlockSpec((None, tr, C), lambda r, s: (l, r, 0))],
        out_specs=pl.BlockSpec((None, tr, C), lambda r, s: (s[0], r, 0)))
    return _pcall(body, name=name, grid_spec=gs, out_shape=jax.ShapeDtypeStruct((N_CHIPS, R, C), BF16),
                  compiler_params=_cparams(("parallel",)))(idx, w)


def _piece_in(w_in_t, l, idx):
    _, R, D = w_in_t.shape
    tr = HALF_TILE
    nsrc = R // tr

    def body(s_ref, w_ref, o_ref):
        src = pl.program_id(0) - s_ref[1]
        ok = jnp.logical_and(src >= 0, src < nsrc)
        o_ref[...] = jnp.where(ok, w_ref[...], 0.0).astype(BF16)

    gs = pltpu.PrefetchScalarGridSpec(
        num_scalar_prefetch=1, grid=(SHARD_P // tr,),
        in_specs=[pl.BlockSpec((None, tr, D), lambda r, s: (l, jnp.clip(r - s[1], 0, nsrc - 1), 0))],
        out_specs=pl.BlockSpec((None, tr, D), lambda r, s: (s[0], r, 0)))
    return _pcall(body, name="piece_in", grid_spec=gs, out_shape=jax.ShapeDtypeStruct((N_CHIPS, SHARD_P, D), BF16),
                  compiler_params=_cparams(("parallel",)))(idx, w_in_t)


def _rs_pair_add(g5, r4, idx, name):
    J, _, h, C = g5.shape
    th = _row_tile(h)

    def body(s_ref, g_ref, r_ref, o32_ref, o16_ref):
        s = g_ref[...] + r_ref[...]
        o16_ref[...] = s.astype(BF16)

        @pl.when(pl.program_id(1) == s_ref[0])
        def _():
            o32_ref[...] = s

    spec = pl.BlockSpec((None, th, C), lambda r, j, s: (j, r, 0))
    gs = pltpu.PrefetchScalarGridSpec(
        num_scalar_prefetch=1, grid=(h // th, J),
        in_specs=[pl.BlockSpec((None, None, th, C), lambda r, j, s: (j, s[2], r, 0)), spec],
        out_specs=[pl.BlockSpec((th, C), lambda r, j, s: (r, 0)), spec])
    return _pcall(body, name=name, grid_spec=gs,
                  out_shape=[jax.ShapeDtypeStruct((h, C), F32), jax.ShapeDtypeStruct((J, h, C), BF16)],
                  compiler_params=_cparams(("parallel", "arbitrary")))(idx, g5, r4)


def _rs_chip_add(p32, r3, full, l, idx, name):
    h, C = p32.shape
    th = _row_tile(h)

    def body(s_ref, p_ref, r_ref, f_ref, o_ref):
        del s_ref, f_ref
        o_ref[...] = ((p_ref[...] + r_ref[0].astype(F32)) + r_ref[1].astype(F32)) + r_ref[2].astype(F32)

    gs = pltpu.PrefetchScalarGridSpec(
        num_scalar_prefetch=1, grid=(h // th,),
        in_specs=[pl.BlockSpec((th, C), lambda r, s: (r, 0)),
                  pl.BlockSpec((3, th, C), lambda r, s: (0, r, 0)),
                  pl.BlockSpec(memory_space=pl.ANY)],
        out_specs=pl.BlockSpec((None, None, th, C), lambda r, s: (l, s[2], r, 0)))
    return _pcall(body, name=name, grid_spec=gs, out_shape=jax.ShapeDtypeStruct(full.shape, F32),
                  input_output_aliases={3: 0}, compiler_params=_cparams(("parallel",)))(idx, p32, r3, full)


def _adamw_math(w, g, m, v):
    nm = ADAM_B1 * m + (1.0 - ADAM_B1) * g
    nv = ADAM_B2 * v + (1.0 - ADAM_B2) * (g * g)
    m_hat = nm / (1.0 - ADAM_B1 ** ADAM_STEP)
    v_hat = nv / (1.0 - ADAM_B2 ** ADAM_STEP)
    delta = -ADAM_LR * (m_hat / (jnp.sqrt(v_hat) + ADAM_EPS) + ADAM_WD * w)
    return delta, nm, nv


def _adamw(w, g, m, v, idx, name, tr=None):
    L, R, C = w.shape
    Rg = g.shape[1]
    tr = tr or _row_tile(R)
    shift = (Rg - R) // tr
    assert (Rg - R) % tr == 0

    def body(s_ref, w_ref, g_ref, m_ref, v_ref, go_ref, d_ref, nm_ref, nv_ref):
        del s_ref
        gv = g_ref[...]
        d, nm, nv = _adamw_math(w_ref[...], gv, m_ref[...], v_ref[...])
        go_ref[...] = gv
        d_ref[...] = d
        nm_ref[...] = nm
        nv_ref[...] = nv

    wspec = pl.BlockSpec((None, tr, C), lambda l, r, s: (l, r, 0))
    gs = pltpu.PrefetchScalarGridSpec(
        num_scalar_prefetch=1, grid=(L, R // tr),
        in_specs=[wspec, pl.BlockSpec((None, tr, C), lambda l, r, s: (l, r + shift * s[1], 0)), wspec, wspec],
        out_specs=[wspec, wspec, wspec, wspec])
    sds = jax.ShapeDtypeStruct((L, R, C), F32)
    return _pcall(body, name=name, grid_spec=gs, out_shape=[sds, sds, sds, sds],
                  compiler_params=_cparams(("parallel", "parallel")))(idx, w, g, m, v)


def _small_allreduce_adamw(gv, wv, mv, vv):
    NR = gv.shape[0]

    def body(g_ref, w_ref, m_ref, v_ref, go_ref, d_ref, nm_ref, nv_ref, gath, send, recv):
        x, y, c = _mesh_pos()
        rank = 4 * x + 2 * y + c
        gath[rank] = g_ref[...]
        cps = []
        for msk in range(1, N_DEV):
            bx, by, bc = (msk >> 2) & 1, (msk >> 1) & 1, msk & 1
            peer = (1 - x if bx else x, 1 - y if by else y, 1 - c if bc else c)
            cps.append(pltpu.make_async_remote_copy(src_ref=g_ref, dst_ref=gath.at[rank], send_sem=send.at[msk - 1],
                                                    recv_sem=recv.at[msk - 1], device_id=peer, device_id_type=MESH))
        for cp in cps:
            cp.start()
        for msk in range(1, N_DEV):
            bx, by, bc = (msk >> 2) & 1, (msk >> 1) & 1, msk & 1
            peer = (1 - x if bx else x, 1 - y if by else y, 1 - c if bc else c)
            prank = 4 * peer[0] + 2 * peer[1] + peer[2]
            pltpu.make_async_remote_copy(src_ref=g_ref, dst_ref=gath.at[prank], send_sem=send.at[msk - 1],
                                         recv_sem=recv.at[msk - 1], device_id=peer, device_id_type=MESH).wait_recv()
        for cp in cps:
            cp.wait_send()
        tot = gath[0]
        for r in range(1, N_DEV):
            tot = tot + gath[r]
        d, nm, nv = _adamw_math(w_ref[...], tot, m_ref[...], v_ref[...])
        go_ref[...] = tot
        d_ref[...] = d
        nm_ref[...] = nm
        nv_ref[...] = nv

    vm = pl.BlockSpec(memory_space=pltpu.VMEM)
    sds = jax.ShapeDtypeStruct((NR, LANES), F32)
    return _pcall(body, name="small_allreduce_adamw", in_specs=[vm, vm, vm, vm], out_specs=[vm, vm, vm, vm],
                  out_shape=[sds, sds, sds, sds],
                  scratch_shapes=[pltpu.VMEM((N_DEV, NR, LANES), F32), pltpu.SemaphoreType.DMA((N_DEV - 1,)),
                                  pltpu.SemaphoreType.DMA((N_DEV - 1,))],
                  compiler_params=pltpu.CompilerParams(vmem_limit_bytes=VMEM_LIMIT))(gv, wv, mv, vv)


def _split3(a):
    h1 = a.astype(BF16)
    r1 = a - h1.astype(F32)
    h2 = r1.astype(BF16)
    h3 = (r1 - h2.astype(F32)).astype(BF16)
    return h1, h2, h3


def _folded_dist():
    r = jnp.arange(WINDOW)[:, None]
    j = jnp.arange(WINDOW)[None, :]
    return jnp.where(j > r, r + WINDOW - j, r - j)


def _bucket_onehot():
    d = _folded_dist()
    max_exact = REL_BUCKETS // 2
    d_f = jnp.maximum(d, 1).astype(F32)
    large = max_exact + (jnp.log(d_f / max_exact) / math.log(REL_MAX_DIST / max_exact) * (REL_BUCKETS - max_exact)).astype(I32)
    large = jnp.minimum(large, REL_BUCKETS - 1)
    bucket = jnp.where(d < max_exact, d, large)
    oh = bucket[None] == jnp.arange(REL_BUCKETS)[:, None, None]
    return oh.reshape(REL_BUCKETS, WINDOW * WINDOW).astype(BF16)


def _bias_fwd(rel_bias_t, onehot):
    H, N = rel_bias_t.shape[0], onehot.shape[1]
    tn = 4096

    def body(t_ref, oh_ref, o_ref):
        h1, h2, h3 = _split3(t_ref[...])
        oh = oh_ref[...]
        o_ref[...] = (_nn(h1, oh) + _nn(h2, oh)) + _nn(h3, oh)

    return _pcall(body, name="bias_fwd", grid=(N // tn,),
                  in_specs=[pl.BlockSpec((H, REL_BUCKETS), lambda i: (0, 0)), pl.BlockSpec((REL_BUCKETS, tn), lambda i: (0, i))],
                  out_specs=pl.BlockSpec((H, tn), lambda i: (0, i)), out_shape=jax.ShapeDtypeStruct((H, N), F32),
                  compiler_params=_cparams(("parallel",)))(rel_bias_t, onehot)


def _bias_bwd(dbias, onehot):
    H, N = dbias.shape
    tn = 4096

    def body(d_ref, oh_ref, o_ref):
        @pl.when(pl.program_id(0) == 0)
        def _():
            o_ref[...] = jnp.zeros_like(o_ref)
        h1, h2, h3 = _split3(d_ref[...])
        oh = oh_ref[...]
        o_ref[...] += (_nt(h1, oh) + _nt(h2, oh)) + _nt(h3, oh)

    return _pcall(body, name="bias_bwd", grid=(N // tn,),
                  in_specs=[pl.BlockSpec((H, tn), lambda i: (0, i)), pl.BlockSpec((REL_BUCKETS, tn), lambda i: (0, i))],
                  out_specs=pl.BlockSpec((H, REL_BUCKETS), lambda i: (0, 0)), out_shape=jax.ShapeDtypeStruct((H, REL_BUCKETS), F32),
                  compiler_params=_cparams(("arbitrary",)))(dbias, onehot)


def _to_bf16(x):
    S, D = x.shape
    tm = min(S, 512)

    def body(x_ref, o_ref):
        o_ref[...] = x_ref[...].astype(BF16)

    return _pcall(body, name="to_bf16", grid=(S // tm,), in_specs=[pl.BlockSpec((tm, D), lambda i: (i, 0))],
                  out_specs=pl.BlockSpec((tm, D), lambda i: (i, 0)), out_shape=jax.ShapeDtypeStruct((S, D), BF16),
                  compiler_params=_cparams(("parallel",)))(x)


def _inproj(xb, w_t, b_ext, l, job, store):
    S, D = xb.shape
    tm = min(S, 512)

    def body(x_ref, w_ref, b_ref, q_ref, k_ref, v_ref, ga_ref, u_ref, gp_ref):
        xv = x_ref[...]

        def window(j):
            return _nt(xv, w_ref[j]) + b_ref[:, j * SHARD_P:(j + 1) * SHARD_P]

        acc = window(0)
        q_ref[...] = (acc[:, :1024] * QK_SCALE).astype(BF16)
        k_lo = acc[:, 1024:1152]
        acc = window(1)
        k_ref[...] = k_lo + acc[:, 0:128]
        v_ref[...] = acc[:, 128:256]
        ga_ref[:, 0:896] = acc[:, 256:1152].astype(ACT)
        acc = window(2)
        ga_ref[:, 896:1024] = acc[:, 0:128].astype(ACT)
        u_ref[:, 0:896] = acc[:, 128:1024]
        u_lo = acc[:, 1024:1152]
        acc = window(3)
        u_ref[:, 896:1024] = u_lo + acc[:, 0:128]
        gp_ref[...] = acc[:, 128:1152].astype(ACT)

    def ospec(w):
        return pl.BlockSpec((tm, w), lambda i: (i, 0))

    return _call(
        body, name="inproj", grid=(S // tm,),
        in_specs=[pl.BlockSpec((tm, D), lambda i: (i, 0)),
                  pl.BlockSpec((N_CHIPS, SHARD_P, D), lambda i: (0, 0, 0), pipeline_mode=pl.Buffered(1)),
                  pl.BlockSpec((None, 1, EXT), lambda i: (l, 0, 0))],
        out_specs=[ospec(ATTN_W), ospec(KV_W), ospec(KV_W), ospec(ATTN_W), ospec(POOL_W), ospec(POOL_W)],
        out_shape=[jax.ShapeDtypeStruct((S, ATTN_W), BF16), jax.ShapeDtypeStruct((S, KV_W), F32), jax.ShapeDtypeStruct((S, KV_W), F32),
                   jax.ShapeDtypeStruct((S, ATTN_W), ACT), jax.ShapeDtypeStruct((S, POOL_W), F32), jax.ShapeDtypeStruct((S, POOL_W), ACT)],
        args=(xb, w_t, b_ext), sem=("parallel",), job=job, store=store)


def _matmul_nn(a, b4, name, job, store):
    S, K = a.shape
    N = b4.shape[2]
    kq = b4.shape[1]
    tm, tn = min(S, 512), min(N, 1024)

    def body(a_ref, b_ref, o_ref):
        acc = _nn(a_ref[:, 0:kq], b_ref[0])
        for j in range(1, N_CHIPS):
            acc = acc + _nn(a_ref[:, j * kq:(j + 1) * kq], b_ref[j])
        o_ref[...] = acc.astype(ACT)

    return _call(body, name=name, grid=(N // tn, S // tm),
                 in_specs=[pl.BlockSpec((tm, K), lambda n, i: (i, 0)), pl.BlockSpec((N_CHIPS, kq, tn), lambda n, i: (0, 0, n))],
                 out_specs=[pl.BlockSpec((tm, tn), lambda n, i: (i, n))], out_shape=[jax.ShapeDtypeStruct((S, N), ACT)],
                 args=(a, b4), sem=("parallel", "parallel"), job=job, store=store)[0]


def _masked_bias(bias):
    r = jnp.arange(WINDOW)[:, None]
    j = jnp.arange(WINDOW)[None, :]
    return jnp.stack([jnp.where(j > r, -1e30, bias), bias])


def _fold(full, tri):
    return jnp.where(tri, full[:, :WINDOW], full[:, WINDOW:])


def _unfold(folded, tri):
    return jnp.concatenate([jnp.where(tri, folded, 0.0).astype(BF16), jnp.where(tri, 0.0, folded).astype(BF16)], axis=1)


def _dup_heads(src_ref, dst_ref):
    a = src_ref[...]
    r = pltpu.roll(a, HEAD_DIM, axis=1)
    lo = lax.broadcasted_iota(I32, a.shape, 1) < HEAD_DIM
    dst_ref[0] = jnp.where(lo, a, r).astype(BF16)
    dst_ref[1] = jnp.where(lo, r, a).astype(BF16)


def _kv_block(ref, h, prev, cur):
    return jnp.concatenate([ref[h, pl.ds(prev, WINDOW), :], ref[h, pl.ds(cur, WINDOW), :]], axis=0)


def _rows(i):
    return slice(i * WINDOW, (i + 1) * WINDOW)


def _attn_fwd(q, k, v, ga, bias, sinks, l, job, store):
    S = q.shape[0]
    nblk = S // WINDOW

    def body(sink_ref, q_ref, k_ref, v_ref, ga_ref, bias_ref, o_ref, ca_ref, lse_ref, kd, vd, qs, ss, ps, os_):
        n = pl.program_id(0)

        @pl.when(n == 0)
        def _():
            _dup_heads(k_ref, kd)
            _dup_heads(v_ref, vd)

        cur = pl.multiple_of(n * WINDOW, WINDOW)
        prev = pl.multiple_of(jnp.maximum(n - 1, 0) * WINDOW, WINDOW)
        lane = lax.broadcasted_iota(I32, (WINDOW, LANES), 1)
        lo = lane < HEAD_DIM
        tri = lane > lax.broadcasted_iota(I32, (WINDOW, LANES), 0)
        lse_mat = jnp.zeros((WINDOW, LANES), F32)
        for g in range(N_KV):
            for i in range(KV_RATIO):
                h = g * KV_RATIO + i
                qp = q_ref[:, LANES * (h // 2):LANES * (h // 2 + 1)]
                qs[_rows(i), :] = jnp.where(lo if h % 2 == 0 else jnp.logical_not(lo), qp, jnp.zeros_like(qp))
            ss[...] = _nt(qs[...], _kv_block(kd, g, prev, cur))
            for i in range(KV_RATIO):
                h = g * KV_RATIO + i
                s = _fold(ss[_rows(i), :], tri) + bias_ref[h]
                sink = sink_ref[l, h]
                m = jnp.maximum(jnp.max(s, axis=1, keepdims=True), sink)
                e = jnp.exp(s - m)
                den = jnp.sum(e, axis=1, keepdims=True) + jnp.exp(sink - m)
                ps[_rows(i), :] = _unfold(e * (1.0 / den), tri)
                lse_mat = jnp.where(lane == h, m + jnp.log(den), lse_mat)
            os_[...] = _nn(ps[...], _kv_block(vd, g, prev, cur))
            for j in range(KV_RATIO // 2):
                pair = g * (KV_RATIO // 2) + j
                sl = slice(LANES * pair, LANES * (pair + 1))
                o_pair = jnp.where(lo, os_[_rows(2 * j), :], os_[_rows(2 * j + 1), :])
                o_ref[:, sl] = o_pair.astype(ACT)
                gav = ga_ref[:, sl].astype(F32)
                ca_ref[:, sl] = (o_pair * (gav * _sigmoid(gav))).astype(BF16)
        lse_ref[...] = lse_mat

    blk = pl.BlockSpec((WINDOW, ATTN_W), lambda n: (n, 0))
    full_kv = pl.BlockSpec((S, KV_W), lambda n: (0, 0))
    stack = KV_RATIO * WINDOW
    return _call(
        body, name="attn_fwd", grid=(nblk,),
        in_specs=[pl.BlockSpec(memory_space=pltpu.SMEM), blk, full_kv, full_kv, blk,
                  pl.BlockSpec((None, N_HEADS, WINDOW, WINDOW), lambda n: (jnp.minimum(n, 1), 0, 0, 0))],
        out_specs=[blk, blk, pl.BlockSpec((WINDOW, LANES), lambda n: (n, 0))],
        out_shape=[jax.ShapeDtypeStruct((S, ATTN_W), ACT), jax.ShapeDtypeStruct((S, ATTN_W), BF16), jax.ShapeDtypeStruct((S, LANES), F32)],
        scratch_shapes=[pltpu.VMEM((N_KV, S, LANES), BF16), pltpu.VMEM((N_KV, S, LANES), BF16),
                        pltpu.VMEM((stack, LANES), BF16), pltpu.VMEM((stack, 2 * WINDOW), F32),
                        pltpu.VMEM((stack, 2 * WINDOW), BF16), pltpu.VMEM((stack, LANES), F32)],
        args=(sinks, q, k, v, ga, bias), sem=("arbitrary",), job=job, store=store)


def _pool_diff(u, halo, tile_index, tm):
    gw = POOL_W // len(POOL_WINDOWS)
    xh = jnp.concatenate([halo, u], axis=0)
    sums = []
    s = xh
    for step in (1, 2, 4, 8):
        s = s + pltpu.roll(s, step, axis=0)
        sums.append(s)
    t = tile_index * tm + lax.broadcasted_iota(I32, (tm, gw), 0)
    diffs = []
    for g, w in enumerate(POOL_WINDOWS):
        cols = slice(g * gw, (g + 1) * gw)
        cnt = jnp.minimum(t + 1, w).astype(F32)
        diffs.append(sums[g][POOL_HALO:, cols] / cnt - u[:, cols])
    return diffs


def _pool_weight(wp_ref, g):
    r = wp_ref.shape[1] // len(POOL_WINDOWS)
    return jnp.concatenate([wp_ref[j, g * r:(g + 1) * r, :] for j in range(N_CHIPS)], axis=0)


def _pool_fwd(u, gp, ca, w_pool, ps, l, job, store):
    S = u.shape[0]
    tm = min(S, 512)
    hb = tm // POOL_HALO
    gw = POOL_W // len(POOL_WINDOWS)

    def body(u_ref, uh_ref, gp_ref, ca_ref, wp_ref, ps_ref, c_ref):
        i = pl.program_id(0)
        uv = u_ref[...]
        halo = jnp.where(i > 0, uh_ref[...], 0.0)
        diffs = _pool_diff(uv, halo, i, tm)
        c_ref[:, 0:ATTN_W] = ca_ref[...]
        for g in range(len(POOL_WINDOWS)):
            cols = slice(g * gw, (g + 1) * gw)
            mm = _nn(diffs[g].astype(BF16), _pool_weight(wp_ref, g))
            gpv = gp_ref[:, cols].astype(F32)
            b = (mm * ps_ref[:, cols]) * (gpv * _sigmoid(gpv))
            c_ref[:, ATTN_W + g * gw:ATTN_W + (g + 1) * gw] = b.astype(BF16)

    row = pl.BlockSpec((tm, POOL_W), lambda i: (i, 0))
    return _call(
        body, name="pool_fwd", grid=(S // tm,),
        in_specs=[row, pl.BlockSpec((POOL_HALO, POOL_W), lambda i: (jnp.maximum(i * hb - 1, 0), 0)), row, row,
                  pl.BlockSpec(w_pool.shape, lambda i: (0, 0, 0)),
                  pl.BlockSpec((None, 1, POOL_W), lambda i: (l, 0, 0))],
        out_specs=[pl.BlockSpec((tm, ATTN_W + POOL_W), lambda i: (i, 0))],
        out_shape=[jax.ShapeDtypeStruct((S, ATTN_W + POOL_W), BF16)],
        args=(u, u, gp, ca, w_pool, ps), sem=("parallel",), job=job, store=store)[0]


def _ple_embed(p_ref, wple_ref):
    pb = p_ref[...].astype(BF16)
    return jnp.concatenate([_nn(pb, wple_ref[j]) for j in range(N_CHIPS)], axis=1)


def _outproj_ln(c, xh_in, gain_in, bias_in, l_in, gl, p, w_out, w_ple, gain, bias, l, alpha, job, store):
    S, D = xh_in.shape
    tm = min(S, 256)
    kq = D // N_CHIPS

    def body(c_ref, x_ref, gi_ref, bi_ref, gl_ref, p_ref, wo_ref, wp_ref, gain_ref, bias_ref, yb_ref, xh_ref, rs_ref):
        mix = _nn(c_ref[:, 0:kq], wo_ref[0])
        for j in range(1, N_CHIPS):
            mix = mix + _nn(c_ref[:, j * kq:(j + 1) * kq], wo_ref[j])
        ple = _sigmoid(gl_ref[...].astype(F32)) * _ple_embed(p_ref, wp_ref)
        x = x_ref[...] * gi_ref[...] + bi_ref[...]
        z = (alpha * x + mix) + ple
        mu = jnp.mean(z, axis=1, keepdims=True)
        zc = z - mu
        var = jnp.mean(zc * zc, axis=1, keepdims=True)
        rstd = lax.rsqrt(var + LN_EPS)
        xhat = zc * rstd
        yb_ref[...] = (xhat * gain_ref[...] + bias_ref[...]).astype(BF16)
        xh_ref[...] = xhat
        rs_ref[...] = rstd

    row = pl.BlockSpec((tm, D), lambda i: (i, 0))
    vec = pl.BlockSpec((None, 1, D), lambda i: (l, 0, 0))
    vec_in = pl.BlockSpec((None, 1, D), lambda i: (l_in, 0, 0))
    return _call(
        body, name="outproj_ln", grid=(S // tm,),
        in_specs=[row, row, vec_in, vec_in, row, pl.BlockSpec((tm, p.shape[1]), lambda i: (i, 0)),
                  pl.BlockSpec(w_out.shape, lambda i: (0, 0, 0)), pl.BlockSpec(w_ple.shape, lambda i: (0, 0, 0)), vec, vec],
        out_specs=[row, row, pl.BlockSpec((tm, 1), lambda i: (i, 0))],
        out_shape=[jax.ShapeDtypeStruct((S, D), BF16), jax.ShapeDtypeStruct((S, D), F32), jax.ShapeDtypeStruct((S, 1), F32)],
        args=(c, xh_in, gain_in, bias_in, gl, p, w_out, w_ple, gain, bias), sem=("parallel",), job=job, store=store)


def _loss_and_grad(xhat, gain, bias, l, target):
    S, D = xhat.shape
    tm = min(S, 512)

    def body(xh_ref, g_ref, b_ref, t_ref, dy_ref, acc_ref):
        @pl.when(pl.program_id(0) == 0)
        def _():
            acc_ref[...] = jnp.zeros_like(acc_ref)
        d = (xh_ref[...] * g_ref[...] + b_ref[...]) - t_ref[...]
        dy_ref[...] = d * (1.0 / D)
        acc_ref[...] += jnp.sum(jnp.mean(d * d, axis=1, keepdims=True), axis=0, keepdims=True)

    row = pl.BlockSpec((tm, D), lambda i: (i, 0))
    vec = pl.BlockSpec((None, 1, D), lambda i: (l, 0, 0))
    return _pcall(body, name="loss", grid=(S // tm,), in_specs=[row, vec, vec, row],
                  out_specs=[row, pl.BlockSpec((8, LANES), lambda i: (0, 0))],
                  out_shape=[jax.ShapeDtypeStruct((S, D), F32), jax.ShapeDtypeStruct((8, LANES), F32)],
                  compiler_params=_cparams(("arbitrary",)))(xhat, gain, bias, target)


def _ln_bwd(dy, xhat, rstd, gl, p, w_ple, w_out, gain, l, job, store):
    S, D = dy.shape
    tm = min(S, 256)

    nq = w_out.shape[1]

    def body(dy_ref, xh_ref, rs_ref, gl_ref, p_ref, wp_ref, wo_ref, gain_ref, dz_ref, dzb_ref, de_ref, dgl_ref, dc_ref, gg_ref, gb_ref):
        @pl.when(pl.program_id(0) == 0)
        def _():
            gg_ref[...] = jnp.zeros_like(gg_ref)
            gb_ref[...] = jnp.zeros_like(gb_ref)
        dyv = dy_ref[...]
        xh = xh_ref[...]
        dxh = dyv * gain_ref[...]
        m1 = jnp.mean(dxh, axis=1, keepdims=True)
        m2 = jnp.mean(dxh * xh, axis=1, keepdims=True)
        dz = rs_ref[...] * ((dxh - m1) - xh * m2)
        gg_ref[...] += jnp.sum(dyv * xh, axis=0, keepdims=True)
        gb_ref[...] += jnp.sum(dyv, axis=0, keepdims=True)
        sg = _sigmoid(gl_ref[...].astype(F32))
        e = _ple_embed(p_ref, wp_ref)
        dzb = dz.astype(BF16)
        dz_ref[...] = dz
        dzb_ref[...] = dzb
        de_ref[...] = (dz * sg).astype(BF16)
        dgl_ref[...] = ((dz * e) * (sg * (1.0 - sg))).astype(BF16)
        for j in range(N_CHIPS):
            dc_ref[:, j * nq:(j + 1) * nq] = _nt(dzb, wo_ref[j]).astype(ACT)

    row = pl.BlockSpec((tm, D), lambda i: (i, 0))
    vec_in = pl.BlockSpec((None, 1, D), lambda i: (l, 0, 0))
    vec_out = pl.BlockSpec((1, D), lambda i: (0, 0))
    bsd = jax.ShapeDtypeStruct((S, D), BF16)
    fsd = jax.ShapeDtypeStruct((S, D), F32)
    return _call(
        body, name="ln_bwd", grid=(S // tm,),
        in_specs=[row, row, pl.BlockSpec((tm, 1), lambda i: (i, 0)), row, pl.BlockSpec((tm, p.shape[1]), lambda i: (i, 0)),
                  pl.BlockSpec(w_ple.shape, lambda i: (0, 0, 0), pipeline_mode=pl.Buffered(1)),
                  pl.BlockSpec(w_out.shape, lambda i: (0, 0, 0), pipeline_mode=pl.Buffered(1)), vec_in],
        out_specs=[row, row, row, row, row, vec_out, vec_out],
        out_shape=[fsd, bsd, bsd, bsd, jax.ShapeDtypeStruct((S, D), ACT), jax.ShapeDtypeStruct((1, D), F32), jax.ShapeDtypeStruct((1, D), F32)],
        args=(dy, xhat, rstd, gl, p, w_ple, w_out, gain), sem=("arbitrary",), job=job, store=store)


def _matmul_nn_acc(a, b4, name, add, add_scale, job, store):
    S = a.shape[0]
    KS, tk, N = b4.shape
    tm, tn = min(S, 512), min(N, 1024)

    def body(a_ref, b_ref, add_ref, o_ref):
        acc = add_scale * add_ref[...]
        for k in range(KS):
            acc = acc + _nn(a_ref[:, k * tk:(k + 1) * tk], b_ref[k])
        o_ref[...] = acc

    return _call(body, name=name, grid=(N // tn, S // tm),
                 in_specs=[pl.BlockSpec((tm, KS * tk), lambda n, i: (i, 0)),
                           pl.BlockSpec((KS, tk, tn), lambda n, i: (0, 0, n)),
                           pl.BlockSpec((tm, tn), lambda n, i: (i, n))],
                 out_specs=[pl.BlockSpec((tm, tn), lambda n, i: (i, n))], out_shape=[jax.ShapeDtypeStruct((S, N), F32)],
                 args=(a, b4, add), sem=("parallel", "parallel"), job=job, store=store)[0]


def _matmul_nt_rows(a, b4, name, add=None, add_scale=1.0, job=None, store=None):
    S, K = a.shape
    nq = b4.shape[1]
    tm = min(S, 512)
    out_spec = pl.BlockSpec((tm, N_CHIPS * nq), lambda i: (i, 0))
    in_specs = [pl.BlockSpec((tm, K), lambda i: (i, 0)), pl.BlockSpec(b4.shape, lambda i: (0, 0, 0))]
    if add is None:
        def body(a_ref, b_ref, o_ref):
            av = a_ref[...]
            for j in range(N_CHIPS):
                o_ref[:, j * nq:(j + 1) * nq] = _nt(av, b_ref[j])
        args = (a, b4)
    else:
        def body(a_ref, b_ref, add_ref, o_ref):
            av = a_ref[...]
            for j in range(N_CHIPS):
                cols = slice(j * nq, (j + 1) * nq)
                o_ref[:, cols] = _nt(av, b_ref[j]) + add_scale * add_ref[:, cols]
        in_specs.append(out_spec)
        args = (a, b4, add)

    return _call(body, name=name, grid=(S // tm,), in_specs=in_specs,
                 out_specs=[out_spec], out_shape=[jax.ShapeDtypeStruct((S, N_CHIPS * nq), F32)],
                 args=args, sem=("parallel",), job=job, store=store)[0]


def _matmul_tn(a, b, R, C, name, by_rows, job=None, store=None):
    S = a.shape[0]
    tn = min(C, 1024 if R <= 512 else 512)
    nt = C // tn

    def body(a_ref, b_ref, o_ref, at_ref):
        @pl.when(pl.program_id(1) == 0)
        def _():
            at_ref[...] = a_ref[...].T
        o_ref[...] = _nn(at_ref[...], b_ref[...])

    if by_rows:
        a_spec = pl.BlockSpec((S, R), lambda j, n: (0, j))
        b_spec = pl.BlockSpec((S, tn), lambda j, n: (0, n))
    else:
        a_spec = pl.BlockSpec((S, R), lambda j, n: (0, 0))
        b_spec = pl.BlockSpec((S, tn), lambda j, n: (0, j * nt + n))
    return _call(body, name=name, grid=(N_CHIPS, nt), in_specs=[a_spec, b_spec],
                 out_specs=[pl.BlockSpec((None, R, tn), lambda j, n: (j, 0, n))],
                 out_shape=[jax.ShapeDtypeStruct((N_CHIPS, R, C), F32)],
                 scratch_shapes=[pltpu.VMEM((R, S), BF16)],
                 args=(a, b), sem=("parallel", "arbitrary"), job=job, store=store)[0]


def _pool_bwd(u, gp, dc, w_pool, ps, l, job=None, store=None):
    S = u.shape[0]
    tm = min(S, 512)
    hb = tm // POOL_HALO
    ngrp = len(POOL_WINDOWS)
    gw = POOL_W // ngrp
    rr = gw // N_CHIPS

    def body(u_ref, uh_ref, gp_ref, dc_ref, wp_ref, ps_ref, dd_ref, dgp_ref, gps_ref, gwp_ref):
        i = pl.program_id(0)

        @pl.when(i == 0)
        def _():
            gps_ref[...] = jnp.zeros_like(gps_ref)
            gwp_ref[...] = jnp.zeros_like(gwp_ref)
        uv = u_ref[...]
        halo = jnp.where(i > 0, uh_ref[...], 0.0)
        diffs = _pool_diff(uv, halo, i, tm)
        for g in range(ngrp):
            cols = slice(g * gw, (g + 1) * gw)
            w = _pool_weight(wp_ref, g)
            db = diffs[g].astype(BF16)
            mm = _nn(db, w)
            gpv = gp_ref[:, cols].astype(F32)
            sg = _sigmoid(gpv)
            si = gpv * sg
            dsi = sg * (1.0 + gpv * (1.0 - sg))
            dcb = dc_ref[:, cols].astype(F32)
            psv = ps_ref[:, cols]
            d_mm = (dcb * si) * psv
            gps_ref[:, cols] += jnp.sum((dcb * si) * mm, axis=0, keepdims=True)
            dgp_ref[:, cols] = ((dcb * (mm * psv)) * dsi).astype(ACT)
            d_mmb = d_mm.astype(BF16)
            dd_ref[:, cols] = _nt(d_mmb, w).astype(ACT)
            gwt = _tn(db, d_mmb)
            for j in range(N_CHIPS):
                gwp_ref[j, g * rr:(g + 1) * rr, :] += gwt[j * rr:(j + 1) * rr, :]

    row = pl.BlockSpec((tm, POOL_W), lambda i: (i, 0))
    wspec = pl.BlockSpec(w_pool.shape, lambda i: (0, 0, 0))
    return _call(
        body, name="pool_bwd", grid=(S // tm,),
        in_specs=[row, pl.BlockSpec((POOL_HALO, POOL_W), lambda i: (jnp.maximum(i * hb - 1, 0), 0)), row,
                  pl.BlockSpec((tm, POOL_W), lambda i: (i, 1)), wspec, pl.BlockSpec((None, 1, POOL_W), lambda i: (l, 0, 0))],
        out_specs=[row, row, pl.BlockSpec((1, POOL_W), lambda i: (0, 0)), wspec],
        out_shape=[jax.ShapeDtypeStruct((S, POOL_W), ACT), jax.ShapeDtypeStruct((S, POOL_W), ACT),
                   jax.ShapeDtypeStruct((1, POOL_W), F32), jax.ShapeDtypeStruct(w_pool.shape, F32)],
        args=(u, u, gp, dc, w_pool, ps), sem=("arbitrary",), job=job, store=store)


def _pool_window_t(dd, halo_next, tile_index, tm):
    gw = POOL_W // len(POOL_WINDOWS)
    n = tm + POOL_HALO
    t = tile_index * tm + lax.broadcasted_iota(I32, (n, gw), 0)
    xh = jnp.concatenate([dd, halo_next], axis=0)
    outs = []
    for g, w in enumerate(POOL_WINDOWS):
        cols = slice(g * gw, (g + 1) * gw)
        cnt = jnp.minimum(t + 1, w).astype(F32)
        s = xh[:, cols] / cnt
        step = 1
        while step < w:
            s = s + pltpu.roll(s, n - step, axis=0)
            step *= 2
        outs.append(s[:tm] - dd[:, cols])
    return outs


def _assemble_dh(dq, dk, dv, dga, dd, dgp):
    S = dq.shape[0]
    tm = min(S, 512)
    hb = tm // POOL_HALO
    nt = S // tm

    def body(dq_ref, dk_ref, dv_ref, dga_ref, dd_ref, ddn_ref, dgp_ref, dh_ref, gb_ref):
        i = pl.program_id(0)

        @pl.when(i == 0)
        def _():
            gb_ref[...] = jnp.zeros_like(gb_ref)
        halo = jnp.where(i < nt - 1, ddn_ref[...].astype(F32), 0.0)
        du = jnp.concatenate(_pool_window_t(dd_ref[...].astype(F32), halo, i, tm), axis=1)
        dkv = dk_ref[...]
        dgav = dga_ref[...].astype(F32)
        parts = [(OFF_Q, dq_ref[...].astype(F32)), (OFF_KA, dkv), (OFF_KB, dkv), (OFF_V, dv_ref[...]), (OFF_GA, dgav),
                 (OFF_U, du[:, 0:896]), (OFF_UA, du[:, 896:1024]), (OFF_UB, du[:, 896:1024]), (OFF_GP, dgp_ref[...].astype(F32))]
        for off, val in parts:
            w = val.shape[1]
            dh_ref[:, off:off + w] = val.astype(BF16)
            gb_ref[:, off:off + w] += jnp.sum(val, axis=0, keepdims=True)

    def row(w):
        return pl.BlockSpec((tm, w), lambda i: (i, 0))

    return _pcall(
        body, name="assemble_dh", grid=(nt,),
        in_specs=[row(ATTN_W), row(KV_W), row(KV_W), row(ATTN_W), row(POOL_W),
                  pl.BlockSpec((POOL_HALO, POOL_W), lambda i: (jnp.minimum((i + 1) * hb, S // POOL_HALO - 1), 0)), row(POOL_W)],
        out_specs=[row(EXT), pl.BlockSpec((1, EXT), lambda i: (0, 0))],
        out_shape=[jax.ShapeDtypeStruct((S, EXT), BF16), jax.ShapeDtypeStruct((1, EXT), F32)],
        compiler_params=_cparams(("arbitrary",)),
    )(dq, dk, dv, dga, dd, dd, dgp)


def _attn_bwd(q, k, v, ga, o, dc, lse, bias, sinks, dbias_in, l, job, store):
    S = q.shape[0]
    nblk = S // WINDOW

    def body(sink_ref, q_ref, k_ref, v_ref, ga_ref, o_ref, dc_ref, lse_ref, bias_ref, dbin_ref,
             dq_ref, dga_ref, dk_ref, dv_ref, db_ref, ds_ref, kd, vd, qs, dos, dls, ss, dps, dss, pss, dqs):
        n = pl.program_id(0)

        @pl.when(n == 0)
        def _():
            _dup_heads(k_ref, kd)
            _dup_heads(v_ref, vd)
            dk_ref[...] = jnp.zeros_like(dk_ref)
            dv_ref[...] = jnp.zeros_like(dv_ref)
            db_ref[...] = dbin_ref[...]
            ds_ref[...] = jnp.zeros_like(ds_ref)

        cur = pl.multiple_of(n * WINDOW, WINDOW)
        prev = pl.multiple_of(jnp.maximum(n - 1, 0) * WINDOW, WINDOW)
        lane = lax.broadcasted_iota(I32, (WINDOW, LANES), 1)
        lane8 = lax.broadcasted_iota(I32, (8, LANES), 1)
        lo = lane < HEAD_DIM
        tri = lane > lax.broadcasted_iota(I32, (WINDOW, LANES), 0)
        lse_t = lse_ref[...]
        dk_t, dv_t = [], []
        dsk = jnp.zeros((8, LANES), F32)
        for g in range(N_KV):
            kb = _kv_block(kd, g, prev, cur)
            vb = _kv_block(vd, g, prev, cur)
            for j in range(KV_RATIO // 2):
                pair = g * (KV_RATIO // 2) + j
                sl = slice(LANES * pair, LANES * (pair + 1))
                qp = q_ref[:, sl]
                op = o_ref[:, sl].astype(F32)
                dcp = dc_ref[:, sl].astype(F32)
                gav = ga_ref[:, sl].astype(F32)
                sg = _sigmoid(gav)
                d_o = dcp * (gav * sg)
                dga_ref[:, sl] = ((dcp * op) * (sg * (1.0 + gav * (1.0 - sg)))).astype(ACT)
                prod = d_o * op
                for par in range(2):
                    msk = lo if par == 0 else jnp.logical_not(lo)
                    rows = _rows(2 * j + par)
                    qs[rows, :] = jnp.where(msk, qp, jnp.zeros_like(qp))
                    dos[rows, :] = jnp.where(msk, d_o, 0.0).astype(BF16)
                    delta = jnp.sum(jnp.where(msk, prod, 0.0), axis=1, keepdims=True)
                    dls[rows, :] = jnp.broadcast_to(delta, (WINDOW, LANES))
            ss[...] = _nt(qs[...], kb)
            dps[...] = _nt(dos[...], vb)
            for i in range(KV_RATIO):
                h = g * KV_RATIO + i
                rows = _rows(i)
                lse_h = jnp.sum(jnp.where(lane == h, lse_t, 0.0), axis=1, keepdims=True)
                p = jnp.exp(_fold(ss[rows, :], tri) + bias_ref[h] - lse_h)
                delta = dls[rows, :]
                dsc = p * (_fold(dps[rows, :], tri) - delta)
                db_ref[h] += dsc
                psink = jnp.exp(sink_ref[l, h] - lse_h)
                dsk = dsk + jnp.where(lane8 == h, -jnp.sum(psink * delta, axis=0, keepdims=True), 0.0)
                dss[rows, :] = _unfold(dsc, tri)
                pss[rows, :] = _unfold(p, tri)
            dqs[...] = _nn(dss[...], kb) * QK_SCALE
            for j in range(KV_RATIO // 2):
                pair = g * (KV_RATIO // 2) + j
                dq_ref[:, LANES * pair:LANES * (pair + 1)] = jnp.where(lo, dqs[_rows(2 * j), :], dqs[_rows(2 * j + 1), :]).astype(ACT)
            dk_t.append(_tn(qs[...], dss[...]))
            dv_t.append(_tn(dos[...], pss[...]))

        def untranspose(acc):
            return jnp.concatenate([a[:HEAD_DIM] + a[HEAD_DIM:] for a in acc], axis=0).T

        dkb = untranspose(dk_t)
        dvb = untranspose(dv_t)
        dk_ref[pl.ds(prev, WINDOW), :] += dkb[:WINDOW]
        dk_ref[pl.ds(cur, WINDOW), :] += dkb[WINDOW:]
        dv_ref[pl.ds(prev, WINDOW), :] += dvb[:WINDOW]
        dv_ref[pl.ds(cur, WINDOW), :] += dvb[WINDOW:]
        ds_ref[...] += dsk

    blk = pl.BlockSpec((WINDOW, ATTN_W), lambda n: (n, 0))
    full_kv = pl.BlockSpec((S, KV_W), lambda n: (0, 0))
    full_b = pl.BlockSpec((N_HEADS, WINDOW, WINDOW), lambda n: (0, 0, 0))
    stack = KV_RATIO * WINDOW
    return _call(
        body, name="attn_bwd", grid=(nblk,),
        in_specs=[pl.BlockSpec(memory_space=pltpu.SMEM), blk, full_kv, full_kv, blk, blk, blk,
                  pl.BlockSpec((WINDOW, LANES), lambda n: (n, 0)),
                  pl.BlockSpec((None, N_HEADS, WINDOW, WINDOW), lambda n: (jnp.minimum(n, 1), 0, 0, 0)), full_b],
        out_specs=[blk, blk, full_kv, full_kv, full_b, pl.BlockSpec((8, LANES), lambda n: (0, 0))],
        out_shape=[jax.ShapeDtypeStruct((S, ATTN_W), ACT), jax.ShapeDtypeStruct((S, ATTN_W), ACT), jax.ShapeDtypeStruct((S, KV_W), F32),
                   jax.ShapeDtypeStruct((S, KV_W), F32), jax.ShapeDtypeStruct((N_HEADS, WINDOW, WINDOW), F32),
                   jax.ShapeDtypeStruct((8, LANES), F32)],
        scratch_shapes=[pltpu.VMEM((N_KV, S, LANES), BF16), pltpu.VMEM((N_KV, S, LANES), BF16),
                        pltpu.VMEM((stack, LANES), BF16), pltpu.VMEM((stack, LANES), BF16), pltpu.VMEM((stack, LANES), F32),
                        pltpu.VMEM((stack, 2 * WINDOW), F32), pltpu.VMEM((stack, 2 * WINDOW), F32),
                        pltpu.VMEM((stack, 2 * WINDOW), BF16), pltpu.VMEM((stack, 2 * WINDOW), BF16),
                        pltpu.VMEM((stack, LANES), F32)],
        args=(sinks, q, k, v, ga, o, dc, lse, bias, dbias_in), sem=("arbitrary",), job=job, store=store)


def _pack_small(arrs):
    flat = []
    for a in arrs:
        v = a.reshape(-1)
        flat.append(jnp.pad(v, (0, (-v.shape[0]) % LANES)))
    v = jnp.concatenate(flat)
    v = jnp.pad(v, (0, (-v.shape[0]) % (8 * LANES)))
    return v.reshape(-1, LANES)


def _unpack_small(packed, shapes):
    v = packed.reshape(-1)
    outs, off = [], 0
    for shp in shapes:
        n = math.prod(shp)
        outs.append(v[off:off + n].reshape(shp))
        off += n + (-n) % LANES
    return outs


def _bias_to_ext(b):
    L = b.shape[0]
    z = jnp.zeros((L, HALF_TILE), b.dtype)
    parts = []
    for j in range(N_CHIPS):
        seg = b[:, j * SHARD:(j + 1) * SHARD]
        parts += [z, seg] if j % 2 else [seg, z]
    return jnp.concatenate(parts, axis=1).reshape(L, 1, EXT)


def _bias_from_ext(g):
    parts = []
    for j in range(N_CHIPS):
        o = j * SHARD_P + (HALF_TILE if j % 2 else 0)
        parts.append(g[:, o:o + SHARD])
    return jnp.concatenate(parts, axis=1)


def kernel(x, p, w_in, b_in, w_out, attn_sinks, rel_bias, w_pool, pool_scale, w_ple, w_gate_ple, ln_gain, ln_bias, loss_target, m_w_in, m_b_in, m_w_out, m_attn_sinks, m_rel_bias, m_w_pool, m_pool_scale, m_w_ple, m_w_gate_ple, m_ln_gain, m_ln_bias, v_w_in, v_b_in, v_w_out, v_attn_sinks, v_rel_bias, v_w_pool, v_pool_scale, v_w_ple, v_w_gate_ple, v_ln_gain, v_ln_bias):
    L = w_in.shape[0]
    S, D = x.shape[1], x.shape[2]
    assert D == D_MODEL and w_in.shape[2] == SHARD and S % WINDOW == 0
    alpha = (2.0 * L) ** 0.25
    xc, yc, cc = _mesh_pos()
    idx = jnp.stack([2 * xc + yc, yc, cc]).astype(I32)
    store = {}

    def wkeys(l, names):
        return [("w", l, t) for t in names]

    def weight(l, t):
        return _unhalves(store["w", l, t])

    w_in_t = jnp.swapaxes(w_in, 1, 2)
    w_pool2 = w_pool.reshape(L, w_pool.shape[1] * w_pool.shape[2], w_pool.shape[3])
    for l in range(L):
        store["w", l, "in"] = _halves(_piece_in(w_in_t, l, idx))
        store["w", l, "out"] = _halves(_piece(w_out, l, idx, "piece_out"))
        store["w", l, "gate"] = _halves(_piece(w_gate_ple, l, idx, "piece_gate"))
        store["w", l, "ple"] = _halves(_piece(w_ple, l, idx, "piece_ple"))
        store["w", l, "pool"] = _halves(_piece(w_pool2, l, idx, "piece_pool"))
    _allgather_now(wkeys(0, WEIGHTS), store)

    b_ext = _bias_to_ext(b_in)
    ps3 = pool_scale.reshape(L, 1, POOL_W)
    gain3 = ln_gain.reshape(L, 1, D)
    bias3 = ln_bias.reshape(L, 1, D)
    onehot = _bucket_onehot()
    bias_hqk = _masked_bias(_bias_fwd(rel_bias.T, onehot).reshape(N_HEADS, WINDOW, WINDOW))

    xs = x[0]
    xb = _to_bf16(xs)
    pb = _to_bf16(p.reshape(L * S, p.shape[3])).reshape(L, S, p.shape[3])
    saved = []
    x_norm, x_gain, x_bias, x_l = xs, jnp.ones((1, 1, D), F32), jnp.zeros((1, 1, D), F32), 0
    late = ["out", "ple", "pool"]
    for l in range(L):
        nxt = l + 1 < L
        job = _Job()
        if nxt:
            job.add(_ag_ici, wkeys(l + 1, ["in"]), 3)
        if l >= 1:
            job.add(_ag_fwd, wkeys(l, ["gate"]), 3)
        q, k, v, ga, u, gp = _inproj(xb, weight(l, "in"), b_ext, l, job, store)
        gl = _matmul_nn(xb, weight(l, "gate"), "gate_logits", None, None)
        job = _Job().add(_ag_fwd, wkeys(l + 1, ["in"]), 3).add(_ag_ici, wkeys(l + 1, late), 9) if nxt else None
        o, ca, lse = _attn_fwd(q, k, v, ga, bias_hqk, attn_sinks, l, job, store)
        c = _pool_fwd(u, gp, ca, weight(l, "pool"), ps3, l, None, None)
        job = _Job().add(_ag_fwd, wkeys(l + 1, late), 9).add(_ag_ici, wkeys(l + 1, ["gate"]), 3) if nxt else None
        yb, xhat, rstd = _outproj_ln(c, x_norm, x_gain, x_bias, x_l, gl, pb[l], weight(l, "out"), weight(l, "ple"), gain3, bias3, l,
                                     alpha, job, store)
        saved.append(dict(xb=xb, q=q, k=k, v=v, ga=ga, u=u, gp=gp, gl=gl, o=o, lse=lse, c=c, xhat=xhat, rstd=rstd))
        x_norm, x_gain, x_bias, x_l, xb = xhat, gain3, bias3, l, yb

    dy, loss_acc = _loss_and_grad(x_norm, gain3, bias3, L - 1, loss_target[0])
    loss = lax.psum(0.5 * loss_acc[0, 0], ("x", "y", "c"))

    shapes = {t: store["w", 0, t].shape for t in WEIGHTS}
    for t in WEIGHTS:
        store["full", t] = lax.empty((L,) + shapes[t][1:], F32)

    def rs_keys(kind, l, names):
        return [(kind, l, t) for t in names]

    def rs_pair_job(l, names):
        for t in names:
            store["ra", l, t] = lax.empty((N_CHIPS,) + shapes[t][2:], F32)
        return _Job().add(_rs_pair, rs_keys("g", l, names) + rs_keys("ra", l, names), len(names))

    def rs_pair_add(l, names):
        for t in names:
            p32, p16 = _rs_pair_add(store["g", l, t], store["ra", l, t], idx, "rs_pair_add_" + t)
            store["p32", l, t], store["p16", l, t] = p32, p16
            store["rb", l, t] = lax.empty((3,) + shapes[t][2:], BF16)

    def rs_ici_job(l, names):
        return _Job().add(_rs_ici, rs_keys("p16", l, names) + rs_keys("rb", l, names), 3 * len(names))

    def rs_chip_add(l, names):
        for t in names:
            store["full", t] = _rs_chip_add(store["p32", l, t], store["rb", l, t], store["full", t], l, idx, "rs_chip_add_" + t)

    def rs_share_job(l):
        return _Job().add(functools.partial(_rs_share, layer=l), [("full", t) for t in WEIGHTS], len(WEIGHTS))

    dbias = jnp.zeros((N_HEADS, WINDOW, WINDOW), F32)
    small = [None] * L
    h_in = shapes["in"][2]
    head_rows = (h_in * 5 // 8) // 16 * 16
    early = ["out", "gate", "ple"]

    def in_ici_job(l, rows):
        return _Job().add(functools.partial(_rs_ici, rows=rows), rs_keys("p16", l, ["in"]) + rs_keys("rb", l, ["in"]), 3)

    for l in reversed(range(L)):
        sv = saved[l]
        pl_l = pb[l]
        job = in_ici_job(l + 1, (head_rows, h_in - head_rows)) if l + 1 < L else None
        dz, dzb, d_e, d_gl, dc, ggain, gbias = _ln_bwd(dy, sv["xhat"], sv["rstd"], sv["gl"], pl_l, weight(l, "ple"), weight(l, "out"),
                                                       gain3, l, job, store)
        if l + 1 < L:
            rs_chip_add(l + 1, ["in"])
        g_out = _matmul_tn(sv["c"], dzb, D // N_CHIPS, D, "grad_w_out", by_rows=True,
                           job=rs_share_job(l + 1) if l + 1 < L else None, store=store)
        g_ple = _matmul_tn(pl_l, d_e, pl_l.shape[1], D // N_CHIPS, "grad_w_ple", by_rows=False)
        g_gate = _matmul_tn(sv["xb"], d_gl, D // N_CHIPS, D, "grad_w_gate", by_rows=True)
        for t, g in zip(early, (g_out, g_gate, g_ple)):
            store["g", l, t] = _halves(g)
        dd, dgp, gps, g_pool = _pool_bwd(sv["u"], sv["gp"], dc, weight(l, "pool"), ps3, l, rs_pair_job(l, early), store)
        store["g", l, "pool"] = _halves(g_pool)
        rs_pair_add(l, early)
        job = rs_pair_job(l, ["pool"])
        job.add(_rs_ici, rs_keys("p16", l, ["out"]) + rs_keys("rb", l, ["out"]), 3)
        dq, dga, dk, dv, dbias, dsink = _attn_bwd(sv["q"], sv["k"], sv["v"], sv["ga"], sv["o"], dc, sv["lse"], bias_hqk,
                                                  attn_sinks, dbias, l, job, store)
        rs_pair_add(l, ["pool"])
        dh, gbe = _assemble_dh(dq, dk, dv, dga, dd, dgp)
        g_in = _matmul_tn(dh, sv["xb"], SHARD_P, D, "grad_w_in", by_rows=True, job=rs_ici_job(l, ["gate", "ple", "pool"]), store=store)
        rs_chip_add(l, ["out", "gate", "ple", "pool"])
        store["g", l, "in"] = _halves(g_in)
        dx1 = _matmul_nt_rows(d_gl, weight(l, "gate"), "d_x_gate", dz, alpha, job=rs_pair_job(l, ["in"]), store=store)
        rs_pair_add(l, ["in"])
        dy = _matmul_nn_acc(dh, weight(l, "in"), "d_x", dx1, 1.0, in_ici_job(l, (0, head_rows) if l > 0 else (0, h_in)), store)
        small[l] = dict(b_in=_bias_from_ext(gbe)[0], sinks=dsink[0, :N_HEADS], ps=gps[0], gain=ggain[0], bias=gbias[0])
    rs_chip_add(0, ["in"])
    grad_x = dy[None]

    _run_job("rs_pair_share", rs_share_job(0), store)
    full = {t: _unhalves(store["full", t]) for t in WEIGHTS}

    def t_back(a):
        return jnp.swapaxes(a, 1, 2)

    def pool4(a):
        return a.reshape(w_pool.shape)

    r_in = _adamw(w_in_t, full["in"], jnp.swapaxes(m_w_in, 1, 2), jnp.swapaxes(v_w_in, 1, 2), idx, "adamw_w_in", tr=HALF_TILE)
    gw_in, dw_in, nm_in, nv_in = [t_back(a) for a in r_in]
    gw_out, dw_out, nm_out, nv_out = _adamw(w_out, full["out"], m_w_out, v_w_out, idx, "adamw_w_out")
    gw_gate, dw_gate, nm_gate, nv_gate = _adamw(w_gate_ple, full["gate"], m_w_gate_ple, v_w_gate_ple, idx, "adamw_w_gate")
    gw_ple, dw_ple, nm_ple, nv_ple = _adamw(w_ple, full["ple"], m_w_ple, v_w_ple, idx, "adamw_w_ple")
    r_pool = _adamw(w_pool2, full["pool"], m_w_pool.reshape(w_pool2.shape), v_w_pool.reshape(w_pool2.shape), idx, "adamw_w_pool")
    gw_pool, dw_pool, nm_pool, nv_pool = [pool4(a) for a in r_pool]

    g_rel = _bias_bwd(dbias.reshape(N_HEADS, -1), onehot).T
    small_shapes = [b_in.shape, attn_sinks.shape, rel_bias.shape, pool_scale.shape, ln_gain.shape, ln_bias.shape]
    g_small = [jnp.stack([small[l]["b_in"] for l in range(L)]), jnp.stack([small[l]["sinks"] for l in range(L)]), g_rel,
               jnp.stack([small[l]["ps"] for l in range(L)]), jnp.stack([small[l]["gain"] for l in range(L)]),
               jnp.stack([small[l]["bias"] for l in range(L)])]
    packed = _small_allreduce_adamw(
        _pack_small(g_small),
        _pack_small([b_in, attn_sinks, rel_bias, pool_scale, ln_gain, ln_bias]),
        _pack_small([m_b_in, m_attn_sinks, m_rel_bias, m_pool_scale, m_ln_gain, m_ln_bias]),
        _pack_small([v_b_in, v_attn_sinks, v_rel_bias, v_pool_scale, v_ln_gain, v_ln_bias]))
    sg, sd, sm, sv_ = [_unpack_small(a, small_shapes) for a in packed]

    def order(big, sm_):
        return (big[0], sm_[0], big[1], sm_[1], sm_[2], big[2], sm_[3], big[3], big[4], sm_[4], sm_[5])

    return (loss, grad_x,
            *order((gw_in, gw_out, gw_pool, gw_ple, gw_gate), sg),
            *order((dw_in, dw_out, dw_pool, dw_ple, dw_gate), sd),
            *order((nm_in, nm_out, nm_pool, nm_ple, nm_gate), sm),
            *order((nv_in, nv_out, nv_pool, nv_ple, nv_gate), sv_))
```
